```python
import jax, jax.numpy as jnp
from jax import lax
import numpy as np

D_MODEL = 1024
BATCH = 8
SEQ = 2048
DEPTH = 4
DEC_BATCH = 128
DEC_SEQ = 1
PAST_LEN = 2048
PAGE_SIZE = 128

N_EVEN = (DEPTH + 1) // 2
N_ODD = DEPTH // 2
H_RET = 4
DK_RET = 128
DV_RET = 128
RET_CHUNK = 128
ROPE_BASE = 10000.0
H_SB = 4
HD_SB = 128
SB_BLOCK = 128
CONV_W = 31
D_FF = 2816
FFN_CONV_W = 3
N_MEM = 256
H_X = 4
HD_X = D_MODEL // H_X
NORM_EPS = 1e-6
W_RK = H_RET * DK_RET
W_RV = H_RET * DV_RET
W_SB = H_SB * HD_SB
D_IN_AB = 2 * W_RK + 2 * W_RV + 3 * W_SB
D_OUT_AB = W_RV + W_SB
SPLIT_AB = [int(s) for s in np.cumsum([W_RK, W_RK, W_RV, W_RV, W_SB, W_SB])]

kernel_name = "retnet_stickbreak_conformer_hybrid_step"


def rmsnorm(x, g):
    xf = x.astype(jnp.float32)
    return (xf * lax.rsqrt(jnp.mean(xf * xf, -1, keepdims=True) + NORM_EPS) * g).astype(x.dtype)


def layernorm(x, g, b):
    xf = x.astype(jnp.float32)
    mu = jnp.mean(xf, -1, keepdims=True)
    var = jnp.mean(jnp.square(xf - mu), -1, keepdims=True)
    return ((xf - mu) * lax.rsqrt(var + NORM_EPS) * g + b).astype(x.dtype)


def rotary(x, pos):
    half = x.shape[-1] // 2
    inv = ROPE_BASE ** (-jnp.arange(half, dtype=jnp.float32) / half)
    ang = pos.astype(jnp.float32)[:, None] * inv[None, :]
    cos = jnp.cos(ang)[None, :, None, :]
    sin = jnp.sin(ang)[None, :, None, :]
    xf = x.astype(jnp.float32)
    x1, x2 = xf[..., :half], xf[..., half:]
    return jnp.concatenate([x1 * cos - x2 * sin, x1 * sin + x2 * cos], -1)


def retention_chunk(q, k, v, s0, log_gamma):
    c = q.shape[1]
    idx = jnp.arange(c)
    diff = idx[:, None] - idx[None, :]
    expo = jnp.maximum(diff, 0).astype(jnp.float32)[None] * log_gamma[:, None, None]
    dmask = jnp.where(diff[None] >= 0, jnp.exp(expo), 0.0)
    scores = jnp.einsum('bchd,bshd->bhcs', q, k) * dmask[None]
    intra = jnp.einsum('bhcs,bshe->bche', scores, v)
    q_decay = jnp.exp((idx + 1).astype(jnp.float32)[:, None] * log_gamma[None, :])
    inter = jnp.einsum('bchd,bhde->bche', q, s0) * q_decay[None, :, :, None]
    k_decay = jnp.exp((c - 1 - idx).astype(jnp.float32)[:, None] * log_gamma[None, :])
    s1 = (jnp.exp(c * log_gamma)[None, :, None, None] * s0
          + jnp.einsum('bchd,bche->bhde', k * k_decay[None, :, :, None], v))
    return intra + inter, s1


def retention_prompt(q, k, v, log_gamma):
    b, t, h, dk = q.shape
    nc = t // RET_CHUNK
    def chunks(a):
        return a.reshape(b, nc, RET_CHUNK, h, a.shape[-1]).swapaxes(0, 1)
    s0 = jnp.zeros((b, h, dk, v.shape[-1]), jnp.float32)
    def step(s, qkv):
        o, s_new = retention_chunk(qkv[0], qkv[1], qkv[2], s, log_gamma)
        return s_new, o
    s_end, o = lax.scan(step, s0, (chunks(q), chunks(k), chunks(v)))
    return o.swapaxes(0, 1).reshape(b, t, h, v.shape[-1]), s_end


def head_layernorm(o, g):
    mu = jnp.mean(o, -1, keepdims=True)
    var = jnp.mean(jnp.square(o - mu), -1, keepdims=True)
    on = (o - mu) * lax.rsqrt(var + NORM_EPS)
    return on.reshape(o.shape[0], o.shape[1], -1) * g


def sb_attend(q, qpos, k, v, kpos, bias):
    z = (jnp.einsum('bqhd,bkhd->bhqk', q, k).astype(jnp.float32) * (HD_SB ** -0.5)
         + bias.astype(jnp.float32)[None, :, None, None])
    mask = kpos[None, :] < qpos[:, None]
    log_1m = jnp.where(mask, jax.nn.log_sigmoid(-z), 0.0)
    after = lax.cumsum(log_1m, axis=3, reverse=True) - log_1m
    a = jnp.where(mask, jnp.exp(jax.nn.log_sigmoid(z) + after), 0.0)
    return jnp.einsum('bhqk,bkhd->bqhd', a.astype(v.dtype), v)


def sb_prompt(q, k, v, bias):
    b, t, h, d = q.shape
    nb = t // SB_BLOCK
    qb = q.reshape(b, nb, SB_BLOCK, h, d).swapaxes(0, 1)
    kpos = jnp.arange(t)
    def block(args):
        qi, i = args
        return sb_attend(qi, i * SB_BLOCK + jnp.arange(SB_BLOCK), k, v, kpos, bias)
    o = lax.map(block, (qb, jnp.arange(nb)))
    return o.swapaxes(0, 1).reshape(b, t, h, d)


def gather_pages(cache_l, page_table):
    g = cache_l[page_table]
    return g.reshape(page_table.shape[0], -1, cache_l.shape[2], cache_l.shape[3])


def mixer_ab(h, pos, s_ret, past_k, past_v, w_in, gn_g, w_out, sb_bias, log_gamma):
    b, t, _ = h.shape
    rq, rk, rv, rg, sq, sk, sv = jnp.split(h @ w_in, SPLIT_AB, axis=-1)
    q = rotary(rq.reshape(b, t, H_RET, DK_RET), pos)
    k = rotary(rk.reshape(b, t, H_RET, DK_RET), pos) * (DK_RET ** -0.5)
    v = rv.reshape(b, t, H_RET, DV_RET).astype(jnp.float32)
    if s_ret is None:
        o, s_new = retention_prompt(q, k, v, log_gamma)
    else:
        o, s_new = retention_chunk(q, k, v, s_ret.astype(jnp.float32), log_gamma)
    ret_out = jax.nn.silu(rg.astype(jnp.float32)) * head_layernorm(o, gn_g)
    qs = sq.reshape(b, t, H_SB, HD_SB)
    ks = sk.reshape(b, t, H_SB, HD_SB)
    vs = sv.reshape(b, t, H_SB, HD_SB)
    if past_k is None:
        sb = sb_prompt(qs, ks, vs, sb_bias)
    else:
        k_all = jnp.concatenate([past_k.astype(ks.dtype), ks], axis=1)
        v_all = jnp.concatenate([past_v.astype(vs.dtype), vs], axis=1)
        sb = sb_attend(qs, pos, k_all, v_all, jnp.arange(k_all.shape[1]), sb_bias)
    y = jnp.concatenate([ret_out.astype(h.dtype), sb.reshape(b, t, W_SB).astype(h.dtype)], -1) @ w_out
    return y, s_new, ks, vs


def causal_dwconv(full, w):
    return lax.conv_general_dilated(full, w[:, None, :].astype(full.dtype), (1,), 'VALID',
                                    dimension_numbers=('NWC', 'WIO', 'NWC'),
                                    feature_group_count=full.shape[-1])


def conformer_conv(h, buf, w1, b1, dw, dwb, ln_g, ln_b, w2, b2):
    a, gt = jnp.split(h @ w1 + b1, 2, axis=-1)
    u = a * jax.nn.sigmoid(gt)
    full = jnp.concatenate([buf.astype(u.dtype), u], axis=1)
    c = layernorm(causal_dwconv(full, dw) + dwb, ln_g, ln_b)
    return jax.nn.silu(c) @ w2 + b2, full[:, -(CONV_W - 1):]


def cross_attn(h, mk, mv, wq, wo):
    b, t, _ = h.shape
    q = (h @ wq).reshape(b, t, H_X, HD_X)
    s = jnp.einsum('bthd,bmhd->bhtm', q, mk.astype(q.dtype)).astype(jnp.float32) * (HD_X ** -0.5)
    p = jax.nn.softmax(s, axis=-1)
    o = jnp.einsum('bhtm,bmhd->bthd', p.astype(h.dtype), mv.astype(h.dtype))
    return o.reshape(b, t, D_MODEL) @ wo


def conv_ffn(h, buf, w_up, dw, dwb, w_down):
    u = h @ w_up
    full = jnp.concatenate([buf.astype(u.dtype), u], axis=1)
    gt, val = jnp.split(causal_dwconv(full, dw) + dwb, 2, axis=-1)
    return (jax.nn.silu(gt) * val) @ w_down, full[:, -(FFN_CONV_W - 1):]


def setup_inputs(seed: int = 0) -> dict:
    key = jax.random.key(seed)
    ks = iter(jax.random.split(key, 40))
    def nrm(shape, scale):
        return jax.random.normal(next(ks), shape, jnp.float32) * scale
    def gain(shape):
        return 1.0 + nrm(shape, 0.05)
    n_pages = PAST_LEN // PAGE_SIZE
    n_used = DEC_BATCH * n_pages
    n_phys = n_used + (n_used + 3) // 4
    page_table = jax.random.permutation(next(ks), n_phys)[:n_used].reshape(DEC_BATCH, n_pages).astype(jnp.int32)
    d = D_MODEL
    return {
        "x_prompt": nrm((BATCH, SEQ, d), 1.0),
        "x_sample": nrm((DEC_BATCH, DEC_SEQ, d), 1.0),
        "cache_sb_k": nrm((N_EVEN, n_phys, PAGE_SIZE, H_SB, HD_SB), 1.0),
        "cache_sb_v": nrm((N_EVEN, n_phys, PAGE_SIZE, H_SB, HD_SB), 1.0),
        "state_ret": nrm((N_EVEN, DEC_BATCH, H_RET, DK_RET, DV_RET), 0.3),
        "state_conv": nrm((N_ODD, DEC_BATCH, CONV_W - 1, d), 0.5),
        "state_ffn_conv": nrm((DEPTH, DEC_BATCH, FFN_CONV_W - 1, 2 * D_FF), 1.0),
        "cache_mem_k": nrm((DEPTH, DEC_BATCH, N_MEM, H_X, HD_X), 1.0),
        "cache_mem_v": nrm((DEPTH, DEC_BATCH, N_MEM, H_X, HD_X), 1.0),
        "page_table": page_table,
        "mem_prompt": nrm((BATCH, N_MEM, d), 1.0),
        "g_mix": gain((DEPTH, d)),
        "w_in_ab": nrm((N_EVEN, d, D_IN_AB), d ** -0.5),
        "ret_gn_g": gain((N_EVEN, W_RV)),
        "w_out_ab": nrm((N_EVEN, D_OUT_AB, d), D_OUT_AB ** -0.5),
        "sb_bias": (-6.0 - jnp.arange(H_SB, dtype=jnp.float32))[None, :] + nrm((N_EVEN, H_SB), 0.1),
        "cv_w1": nrm((N_ODD, d, 2 * d), d ** -0.5),
        "cv_b1": nrm((N_ODD, 2 * d), 0.02),
        "cv_dw": nrm((N_ODD, CONV_W, d), CONV_W ** -0.5),
        "cv_dwb": nrm((N_ODD, d), 0.02),
        "cv_ln_g": gain((N_ODD, d)),
        "cv_ln_b": nrm((N_ODD, d), 0.02),
        "cv_w2": nrm((N_ODD, d, d), d ** -0.5),
        "cv_b2": nrm((N_ODD, d), 0.02),
        "g_cross": gain((DEPTH, d)),
        "xa_wq": nrm((DEPTH, d, d), d ** -0.5),
        "xa_wk": nrm((DEPTH, d, d), d ** -0.5),
        "xa_wv": nrm((DEPTH, d, d), d ** -0.5),
        "xa_wo": nrm((DEPTH, d, d), d ** -0.5),
        "g_ffn": gain((DEPTH, d)),
        "ffn_w_up": nrm((DEPTH, d, 2 * D_FF), d ** -0.5),
        "ffn_dw": nrm((DEPTH, FFN_CONV_W, 2 * D_FF), FFN_CONV_W ** -0.5),
        "ffn_dwb": nrm((DEPTH, 2 * D_FF), 0.02),
        "ffn_w_down": nrm((DEPTH, D_FF, d), D_FF ** -0.5),
        "g_final": gain((d,)),
    }


def reference(x_prompt, x_sample, cache_sb_k, cache_sb_v, state_ret, state_conv, state_ffn_conv,
              cache_mem_k, cache_mem_v, page_table, mem_prompt,
              g_mix, w_in_ab, ret_gn_g, w_out_ab, sb_bias,
              cv_w1, cv_b1, cv_dw, cv_dwb, cv_ln_g, cv_ln_b, cv_w2, cv_b2,
              g_cross, xa_wq, xa_wk, xa_wv, xa_wo,
              g_ffn, ffn_w_up, ffn_dw, ffn_dwb, ffn_w_down, g_final):
    log_gamma = jnp.log1p(-jnp.exp2(-5.0 - jnp.arange(H_RET, dtype=jnp.float32)))
    pos_p = jnp.arange(x_prompt.shape[1])
    pos_s = PAST_LEN + jnp.arange(x_sample.shape[1])
    bp = x_prompt.shape[0]
    xp, xs = x_prompt, x_sample
    sbk_p, sbv_p, sbk_s, sbv_s, ret_p, ret_s = [], [], [], [], [], []
    cv_p, cv_s, ff_p, ff_s, mk_p, mv_p = [], [], [], [], [], []
    for l in range(DEPTH):
        hp = rmsnorm(xp, g_mix[l])
        hs = rmsnorm(xs, g_mix[l])
        if l % 2 == 0:
            e = l // 2
            yp, sp, kp, vp = mixer_ab(hp, pos_p, None, None, None, w_in_ab[e], ret_gn_g[e], w_out_ab[e], sb_bias[e], log_gamma)
            past_k = gather_pages(cache_sb_k[e], page_table)
            past_v = gather_pages(cache_sb_v[e], page_table)
            ys, ss, ksn, vsn = mixer_ab(hs, pos_s, state_ret[e], past_k, past_v, w_in_ab[e], ret_gn_g[e], w_out_ab[e], sb_bias[e], log_gamma)
            sbk_p.append(kp); sbv_p.append(vp); sbk_s.append(ksn); sbv_s.append(vsn)
            ret_p.append(sp); ret_s.append(ss)
        else:
            o = l // 2
            zero_buf = jnp.zeros((bp, CONV_W - 1, D_MODEL), xp.dtype)
            yp, bufp = conformer_conv(hp, zero_buf, cv_w1[o], cv_b1[o], cv_dw[o], cv_dwb[o], cv_ln_g[o], cv_ln_b[o], cv_w2[o], cv_b2[o])
            ys, bufs = conformer_conv(hs, state_conv[o], cv_w1[o], cv_b1[o], cv_dw[o], cv_dwb[o], cv_ln_g[o], cv_ln_b[o], cv_w2[o], cv_b2[o])
            cv_p.append(bufp); cv_s.append(bufs)
        xp = xp + yp
        xs = xs + ys
        mk = (mem_prompt @ xa_wk[l]).reshape(bp, N_MEM, H_X, HD_X)
        mv = (mem_prompt @ xa_wv[l]).reshape(bp, N_MEM, H_X, HD_X)
        mk_p.append(mk); mv_p.append(mv)
        xp = xp + cross_attn(rmsnorm(xp, g_cross[l]), mk, mv, xa_wq[l], xa_wo[l])
        xs = xs + cross_attn(rmsnorm(xs, g_cross[l]), cache_mem_k[l], cache_mem_v[l], xa_wq[l], xa_wo[l])
        zero_ff = jnp.zeros((bp, FFN_CONV_W - 1, 2 * D_FF), xp.dtype)
        fp, fbp = conv_ffn(rmsnorm(xp, g_ffn[l]), zero_ff, ffn_w_up[l], ffn_dw[l], ffn_dwb[l], ffn_w_down[l])
        fs, fbs = conv_ffn(rmsnorm(xs, g_ffn[l]), state_ffn_conv[l], ffn_w_up[l], ffn_dw[l], ffn_dwb[l], ffn_w_down[l])
        ff_p.append(fbp); ff_s.append(fbs)
        xp = xp + fp
        xs = xs + fs
    y_prompt = rmsnorm(xp, g_final)
    y_sample = rmsnorm(xs, g_final)
    sb_k_prompt = jnp.stack(sbk_p); sb_v_prompt = jnp.stack(sbv_p)
    sb_k_sample = jnp.stack(sbk_s); sb_v_sample = jnp.stack(sbv_s)
    ret_prompt = jnp.stack(ret_p); ret_sample = jnp.stack(ret_s)
    conv_prompt = jnp.stack(cv_p); conv_sample = jnp.stack(cv_s)
    ffn_prompt = jnp.stack(ff_p); ffn_sample = jnp.stack(ff_s)
    mem_k_prompt = jnp.stack(mk_p); mem_v_prompt = jnp.stack(mv_p)
    return (y_prompt, y_sample, sb_k_prompt, sb_v_prompt, sb_k_sample, sb_v_sample,
            ret_prompt, ret_sample, conv_prompt, conv_sample, ffn_prompt, ffn_sample,
            mem_k_prompt, mem_v_prompt)
```

```python
import functools

import numpy as np
import jax
import jax.numpy as jnp
from jax import lax
from jax.experimental import pallas as pl
from jax.experimental.pallas import tpu as pltpu

F32 = jnp.float32
BF16 = jnp.bfloat16

D_MODEL = 1024
DEPTH = 4
PAST_LEN = 2048
PAGE_SIZE = 128
N_PAGES = PAST_LEN // PAGE_SIZE
H_RET = 4
DK_RET = 128
DV_RET = 128
RET_CHUNK = 128
ROPE_BASE = 10000.0
H_SB = 4
HD_SB = 128
SB_BLOCK = 128
CONV_W = 31
D_FF = 2816
FFN_CONV_W = 3
N_MEM = 256
H_X = 4
HD_X = D_MODEL // H_X
NORM_EPS = 1e-6
W_RK = H_RET * DK_RET
W_SB = H_SB * HD_SB

SUBLANES = 8
LANES = 128
VMEM_LIMIT_BYTES = 56 * 1024 * 1024


def _params(*sem):
    return pltpu.CompilerParams(dimension_semantics=sem, vmem_limit_bytes=VMEM_LIMIT_BYTES)


def _dot(a, b):
    return jnp.dot(a, b, preferred_element_type=F32)


def _dot_nt(a, b):
    return lax.dot_general(a, b, (((1,), (1,)), ((), ())), preferred_element_type=F32)


def _rms(x, g):
    return x * lax.rsqrt(jnp.mean(x * x, axis=-1, keepdims=True) + NORM_EPS) * g


def _silu(x):
    return x * jax.nn.sigmoid(x)


def _softplus(z):
    return jnp.maximum(z, 0.0) + jnp.log1p(jnp.exp(-jnp.abs(z)))


def _split_hi_lo(x):
    hi = x.astype(BF16)
    lo = (x - hi.astype(F32)).astype(BF16)
    return hi, lo


def _mm_kernel(*refs, n_lhs, prologue, has_bias, has_res, glu):
    it = iter(refs)
    x_refs = [next(it) for _ in range(n_lhs)]
    w_refs = [next(it) for _ in range(n_lhs)]
    wg_refs = [next(it) for _ in range(n_lhs)] if glu else []
    n_pro = {None: 0, "rms": 1, "ln_silu": 2, "swiglu": 0}[prologue]
    p_refs = [next(it) for _ in range(n_pro)]
    b_ref = next(it) if has_bias else None
    bg_ref = next(it) if (has_bias and glu) else None
    r_ref = next(it) if has_res else None
    o_ref = next(it)
    xn_ref = next(it) if prologue else None

    if prologue:
        @pl.when(pl.program_id(1) == 0)
        def _():
            x = x_refs[0][...].astype(F32)
            if prologue == "rms":
                y = _rms(x, p_refs[0][...])
            elif prologue == "ln_silu":
                mu = jnp.mean(x, axis=-1, keepdims=True)
                xc = x - mu
                var = jnp.mean(xc * xc, axis=-1, keepdims=True)
                y = _silu(xc * lax.rsqrt(var + NORM_EPS) * p_refs[0][...] + p_refs[1][...])
            else:
                k = x.shape[-1] // 2
                y = _silu(x[:, :k]) * x[:, k:]
            xn_ref[...] = y.astype(BF16)
        lhs = [xn_ref[...]]
    else:
        lhs = [r[...].astype(BF16) for r in x_refs]

    acc = _dot(lhs[0], w_refs[0][...])
    for a, w in zip(lhs[1:], w_refs[1:]):
        acc += _dot(a, w[...])
    if has_bias:
        acc += b_ref[...]
    if glu:
        gate = _dot(lhs[0], wg_refs[0][...])
        for a, w in zip(lhs[1:], wg_refs[1:]):
            gate += _dot(a, w[...])
        if has_bias:
            gate += bg_ref[...]
        acc = acc * jax.nn.sigmoid(gate)
    if has_res:
        acc += r_ref[...]
    o_ref[...] = acc.astype(o_ref.dtype)


def _mm(xs, ws, *, prologue=None, pro=(), bias=None, res=None, glu=False, tm=512, tn=512,
        out_dtype=F32, name="mm"):
    m = xs[0].shape[0]
    n = ws[0].shape[1] // (2 if glu else 1)
    tm = min(tm, m)
    tn = min(tn, n)
    assert m % tm == 0 and n % tn == 0, (m, tm, n, tn)
    nj = n // tn
    args, specs = [], []
    for x in xs:
        args.append(x)
        specs.append(pl.BlockSpec((tm, x.shape[1]), lambda i, j: (i, 0)))
    for w in ws:
        args.append(w)
        specs.append(pl.BlockSpec((w.shape[0], tn), lambda i, j: (0, j)))
    if glu:
        for w in ws:
            args.append(w)
            specs.append(pl.BlockSpec((w.shape[0], tn), lambda i, j: (0, j + nj)))
    for p in pro:
        args.append(p.reshape(1, -1))
        specs.append(pl.BlockSpec((1, p.size), lambda i, j: (0, 0)))
    if bias is not None:
        b2 = bias.reshape(1, -1)
        args.append(b2)
        specs.append(pl.BlockSpec((1, tn), lambda i, j: (0, j)))
        if glu:
            args.append(b2)
            specs.append(pl.BlockSpec((1, tn), lambda i, j: (0, j + nj)))
    if res is not None:
        args.append(res)
        specs.append(pl.BlockSpec((tm, tn), lambda i, j: (i, j)))
    scratch = []
    if prologue:
        k_eff = xs[0].shape[1] // (2 if prologue == "swiglu" else 1)
        scratch.append(pltpu.VMEM((tm, k_eff), BF16))
    kern = functools.partial(_mm_kernel, n_lhs=len(xs), prologue=prologue,
                             has_bias=bias is not None, has_res=res is not None, glu=glu)
    return pl.pallas_call(
        kern,
        grid=(m // tm, nj),
        in_specs=specs,
        out_specs=pl.BlockSpec((tm, tn), lambda i, j: (i, j)),
        out_shape=jax.ShapeDtypeStruct((m, n), out_dtype),
        scratch_shapes=scratch,
        compiler_params=_params("parallel", "arbitrary"),
        name=name,
    )(*args)


def _rmsnorm_kernel(x_ref, g_ref, o_ref):
    o_ref[...] = _rms(x_ref[...], g_ref[...])


def _rmsnorm(x, g, tm=1024):
    m, d = x.shape
    tm = min(tm, m)
    return pl.pallas_call(
        _rmsnorm_kernel,
        grid=(m // tm,),
        in_specs=[pl.BlockSpec((tm, d), lambda i: (i, 0)), pl.BlockSpec((1, d), lambda i: (0, 0))],
        out_specs=pl.BlockSpec((tm, d), lambda i: (i, 0)),
        out_shape=jax.ShapeDtypeStruct((m, d), F32),
        compiler_params=_params("parallel"),
        name="rmsnorm",
    )(x, g.reshape(1, d))


def _rotate(x, cos2, sin2):
    return x * cos2 + pltpu.roll(x, DK_RET // 2, axis=1) * sin2


def _head_ln_gate(o, gate, gn):
    mu = jnp.mean(o, axis=-1, keepdims=True)
    oc = o - mu
    var = jnp.mean(oc * oc, axis=-1, keepdims=True)
    return _silu(gate) * (oc * lax.rsqrt(var + NORM_EPS) * gn)


def _ret_prompt_kernel(rq_ref, rk_ref, rv_ref, rg_ref, cos_ref, sin_ref, dmask_ref, qdec_ref,
                       kdec_ref, cdec_ref, gn_ref, o_ref, s_ref):
    c = pl.program_id(1)

    @pl.when(c == 0)
    def _():
        s_ref[...] = jnp.zeros_like(s_ref)

    cos2 = cos_ref[...]
    sin2 = sin_ref[...]
    for h in range(H_RET):
        sl = slice(h * DK_RET, (h + 1) * DK_RET)
        q = _rotate(rq_ref[0, :, sl], cos2, sin2)
        k = _rotate(rk_ref[0, :, sl], cos2, sin2) * (DK_RET ** -0.5)
        vb = rv_ref[0, :, sl].astype(BF16)
        qb = q.astype(BF16)
        s0 = s_ref[0, h]
        scores = _dot_nt(qb, k.astype(BF16)) * dmask_ref[h]
        o = _dot(scores.astype(BF16), vb) + _dot(qb, s0.astype(BF16)) * qdec_ref[h]
        kd_t = (k * kdec_ref[h]).T.astype(BF16)
        s_ref[0, h] = cdec_ref[h] * s0 + _dot(kd_t, vb)
        o_ref[0, :, sl] = _head_ln_gate(o, rg_ref[0, :, sl], gn_ref[:, sl]).astype(o_ref.dtype)


def _ret_tables(log_gamma):
    idx = jnp.arange(RET_CHUNK)
    diff = idx[:, None] - idx[None, :]
    expo = jnp.maximum(diff, 0).astype(F32)[None] * log_gamma[:, None, None]
    dmask = jnp.where(diff[None] >= 0, jnp.exp(expo), 0.0)
    q_decay = jnp.exp((idx + 1).astype(F32)[:, None] * log_gamma[None, :])
    k_decay = jnp.exp((RET_CHUNK - 1 - idx).astype(F32)[:, None] * log_gamma[None, :])
    ones = jnp.ones((1, 1, DK_RET), F32)
    qdec = q_decay.T[:, :, None] * ones
    kdec = k_decay.T[:, :, None] * ones
    cdec = jnp.exp(RET_CHUNK * log_gamma)[:, None, None] * jnp.ones((1, DK_RET, DV_RET), F32)
    return dmask, qdec, kdec, cdec


def _rope_tables(pos):
    half = DK_RET // 2
    inv = ROPE_BASE ** (-jnp.arange(half, dtype=F32) / half)
    ang = pos.astype(F32)[:, None] * inv[None, :]
    cos, sin = jnp.cos(ang), jnp.sin(ang)
    return jnp.concatenate([cos, cos], -1), jnp.concatenate([-sin, sin], -1)


def _ret_prompt(proj3, cos2, sin2, tabs, gn):
    b, t, _ = proj3.shape
    nc = t // RET_CHUNK
    dmask, qdec, kdec, cdec = tabs

    def col(cb):
        return pl.BlockSpec((1, RET_CHUNK, W_RK), lambda i, c: (i, c, cb))

    tab = pl.BlockSpec((H_RET, RET_CHUNK, RET_CHUNK), lambda i, c: (0, 0, 0))
    return pl.pallas_call(
        _ret_prompt_kernel,
        grid=(b, nc),
        in_specs=[col(0), col(1), col(2), col(3),
                  pl.BlockSpec((RET_CHUNK, DK_RET), lambda i, c: (c, 0)),
                  pl.BlockSpec((RET_CHUNK, DK_RET), lambda i, c: (c, 0)),
                  tab, tab, tab, tab,
                  pl.BlockSpec((1, W_RK), lambda i, c: (0, 0))],
        out_specs=[pl.BlockSpec((1, RET_CHUNK, W_RK), lambda i, c: (i, c, 0)),
                   pl.BlockSpec((1, H_RET, DK_RET, DV_RET), lambda i, c: (i, 0, 0, 0))],
        out_shape=[jax.ShapeDtypeStruct((b, t, W_RK), BF16),
                   jax.ShapeDtypeStruct((b, H_RET, DK_RET, DV_RET), F32)],
        compiler_params=_params("parallel", "arbitrary"),
        name="ret_prompt",
    )(proj3, proj3, proj3, proj3, cos2, sin2, dmask, qdec, kdec, cdec, gn.reshape(1, W_RK))


RET_BB = 8
MXU_ROWS = 16


def _ret_sample_kernel(rq_ref, rk_ref, rv_ref, rg_ref, cos_ref, sin_ref, gn_ref, st_ref,
                       o_ref, so_ref, inter_ref, *, g1):
    cos2 = cos_ref[...]
    sin2 = sin_ref[...]
    eye = (lax.broadcasted_iota(jnp.int32, (DK_RET, DK_RET), 0)
           == lax.broadcasted_iota(jnp.int32, (DK_RET, DK_RET), 1))
    for h in range(H_RET):
        sl = slice(h * DK_RET, (h + 1) * DK_RET)
        q = _rotate(rq_ref[:, sl], cos2, sin2)
        k = _rotate(rk_ref[:, sl], cos2, sin2) * (DK_RET ** -0.5)
        v = rv_ref[:, sl]
        for r in range(RET_BB):
            s0 = st_ref[r, h]
            qr = jnp.broadcast_to(q[r:r + 1], (MXU_ROWS, DK_RET)).astype(BF16)
            inter_ref[r:r + 1, :] = _dot(qr, s0.astype(BF16))[0:1] * g1[h]
            diag_k = jnp.where(eye, jnp.broadcast_to(k[r:r + 1], (DK_RET, DK_RET)), 0.0).astype(BF16)
            v_rows = jnp.broadcast_to(v[r:r + 1], (DK_RET, DV_RET)).astype(BF16)
            so_ref[r, h] = g1[h] * s0 + _dot(diag_k, v_rows)
        o = jnp.sum(q * k, axis=-1, keepdims=True) * v + inter_ref[...]
        o_ref[:, sl] = _head_ln_gate(o, rg_ref[:, sl], gn_ref[:, sl]).astype(o_ref.dtype)


def _ret_sample(proj, cos2, sin2, gn, state, e):
    bs = proj.shape[0]
    log_gamma = np.log1p(-np.exp2(-5.0 - np.arange(H_RET, dtype=np.float32)))
    g1 = tuple(float(x) for x in np.exp(log_gamma).astype(np.float32))

    def col(cb):
        return pl.BlockSpec((RET_BB, W_RK), lambda i: (i, cb))

    row = pl.BlockSpec((1, DK_RET), lambda i: (0, 0))
    return pl.pallas_call(
        functools.partial(_ret_sample_kernel, g1=g1),
        grid=(bs // RET_BB,),
        in_specs=[col(0), col(1), col(2), col(3), row, row,
                  pl.BlockSpec((1, W_RK), lambda i: (0, 0)),
                  pl.BlockSpec((None, RET_BB, H_RET, DK_RET, DV_RET), lambda i: (e, i, 0, 0, 0))],
        out_specs=[pl.BlockSpec((RET_BB, W_RK), lambda i: (i, 0)),
                   pl.BlockSpec((RET_BB, H_RET, DK_RET, DV_RET), lambda i: (i, 0, 0, 0))],
        out_shape=[jax.ShapeDtypeStruct((bs, W_RK), BF16),
                   jax.ShapeDtypeStruct((bs, H_RET, DK_RET, DV_RET), F32)],
        scratch_shapes=[pltpu.VMEM((RET_BB, DV_RET), F32)],
        compiler_params=_params("parallel"),
        name="ret_sample",
    )(proj, proj, proj, proj, cos2, sin2, gn.reshape(1, W_RK), state)


def _suffix_matrix():
    j = jnp.arange(SB_BLOCK)
    u = (j[:, None] > j[None, :]).astype(BF16)
    return jnp.concatenate([u, jnp.ones((SB_BLOCK, SB_BLOCK), BF16)], axis=1)


def _sb_block(z, sp, log_1m, carry, uo):
    rows = z.shape[0]
    hi, lo = _split_hi_lo(log_1m)
    r = _dot(jnp.concatenate([hi, lo], axis=0), uo)
    after = r[:rows, :SB_BLOCK] + r[rows:, :SB_BLOCK] + carry
    total = r[:rows, SB_BLOCK:] + r[rows:, SB_BLOCK:]
    return z - sp + after, total


def _sb_prompt_kernel(q_ref, k_ref, v_ref, bias_ref, uo_ref, o_ref, acc_ref, carry_ref):
    i = pl.program_id(1)
    uo = uo_ref[...]
    row = lax.broadcasted_iota(jnp.int32, (SB_BLOCK, SB_BLOCK), 0)
    colm = lax.broadcasted_iota(jnp.int32, (SB_BLOCK, SB_BLOCK), 1)
    causal = colm < row

    def block(j, diag):
        start = pl.multiple_of(j * SB_BLOCK, SB_BLOCK)
        for h in range(H_SB):
            sl = slice(h * HD_SB, (h + 1) * HD_SB)
            qb = q_ref[0, :, sl].astype(BF16)
            kb = k_ref[0, pl.ds(start, SB_BLOCK), sl].astype(BF16)
            vb = v_ref[0, pl.ds(start, SB_BLOCK), sl].astype(BF16)
            z = _dot_nt(qb, kb) * (HD_SB ** -0.5) + bias_ref[:, sl]
            sp = _softplus(z)
            log_1m = jnp.where(causal, -sp, 0.0) if diag else -sp
            if diag:
                loga, total = _sb_block(z, sp, log_1m, 0.0, uo)
                a = jnp.where(causal, jnp.exp(loga), 0.0)
                acc_ref[h] = _dot(a.astype(BF16), vb)
                carry_ref[h] = total
            else:
                loga, total = _sb_block(z, sp, log_1m, carry_ref[h], uo)
                acc_ref[h] += _dot(jnp.exp(loga).astype(BF16), vb)
                carry_ref[h] += total

    block(i, True)

    def body(t, _):
        block(i - 1 - t, False)
        return 0

    lax.fori_loop(0, i, body, 0)
    for h in range(H_SB):
        o_ref[0, :, h * HD_SB:(h + 1) * HD_SB] = acc_ref[h].astype(o_ref.dtype)


def _sb_prompt(proj3, bias_row, uo):
    b, t, _ = proj3.shape
    nq = t // SB_BLOCK
    return pl.pallas_call(
        _sb_prompt_kernel,
        grid=(b, nq),
        in_specs=[pl.BlockSpec((1, SB_BLOCK, W_SB), lambda bi, i: (bi, i, 4)),
                  pl.BlockSpec((1, t, W_SB), lambda bi, i: (bi, 0, 5)),
                  pl.BlockSpec((1, t, W_SB), lambda bi, i: (bi, 0, 6)),
                  pl.BlockSpec((1, W_SB), lambda bi, i: (0, 0)),
                  pl.BlockSpec((SB_BLOCK, 2 * SB_BLOCK), lambda bi, i: (0, 0))],
        out_specs=pl.BlockSpec((1, SB_BLOCK, W_SB), lambda bi, i: (bi, i, 0)),
        out_shape=jax.ShapeDtypeStruct((b, t, W_SB), BF16),
        scratch_shapes=[pltpu.VMEM((H_SB, SB_BLOCK, HD_SB), F32),
                        pltpu.VMEM((H_SB, SB_BLOCK, SB_BLOCK), F32)],
        compiler_params=_params("parallel", "arbitrary"),
        name="sb_prompt",
    )(proj3, proj3, proj3, bias_row, uo)


def _head_rows(x_row, width):
    n = x_row.shape[-1]
    r = lax.broadcasted_iota(jnp.int32, (MXU_ROWS, n), 0)
    c = lax.broadcasted_iota(jnp.int32, (MXU_ROWS, n), 1)
    lo = r * width
    return jnp.where((c >= lo) & (c < lo + width), jnp.broadcast_to(x_row, (MXU_ROWS, n)), 0.0)


def _sb_sample_kernel(pt_ref, q_ref, bias_ref, uo_ref, *refs):
    del pt_ref
    k_refs = refs[:N_PAGES]
    v_refs = refs[N_PAGES:2 * N_PAGES]
    o_ref = refs[2 * N_PAGES]
    uo = uo_ref[...]
    q_bd = _head_rows(q_ref[0], HD_SB).astype(BF16)
    bias = bias_ref[...]
    carry = jnp.zeros((MXU_ROWS, PAGE_SIZE), F32)
    acc = jnp.zeros((MXU_ROWS, W_SB), F32)
    for p in reversed(range(N_PAGES)):
        kb = k_refs[p][...].astype(BF16)
        vb = v_refs[p][...].astype(BF16)
        z = _dot_nt(q_bd, kb) * (HD_SB ** -0.5) + bias
        sp = _softplus(z)
        loga, total = _sb_block(z, sp, -sp, carry, uo)
        acc += _dot(jnp.exp(loga).astype(BF16), vb)
        carry += total
    r = lax.broadcasted_iota(jnp.int32, acc.shape, 0)
    c = lax.broadcasted_iota(jnp.int32, acc.shape, 1)
    pick = (c >= r * HD_SB) & (c < (r + 1) * HD_SB)
    o_ref[0] = jnp.sum(jnp.where(pick, acc, 0.0), axis=0, keepdims=True).astype(o_ref.dtype)


def _sb_sample(proj3, bias_rows, uo, cache_k, cache_v, page_table, e):
    bs = proj3.shape[0]

    def page(p):
        return pl.BlockSpec((None, None, PAGE_SIZE, W_SB), lambda b, pt: (e, pt[b, p], 0, 0))

    grid_spec = pltpu.PrefetchScalarGridSpec(
        num_scalar_prefetch=1,
        grid=(bs,),
        in_specs=[pl.BlockSpec((1, 1, W_SB), lambda b, pt: (b, 0, 4)),
                  pl.BlockSpec((MXU_ROWS, PAGE_SIZE), lambda b, pt: (0, 0)),
                  pl.BlockSpec((SB_BLOCK, 2 * SB_BLOCK), lambda b, pt: (0, 0))]
                 + [page(p) for p in range(N_PAGES)] + [page(p) for p in range(N_PAGES)],
        out_specs=pl.BlockSpec((1, 1, W_SB), lambda b, pt: (b, 0, 0)),
    )
    return pl.pallas_call(
        _sb_sample_kernel,
        grid_spec=grid_spec,
        out_shape=jax.ShapeDtypeStruct((bs, 1, W_SB), BF16),
        compiler_params=_params("arbitrary"),
        name="sb_sample",
    )(page_table, proj3, bias_rows, uo, *([cache_k] * N_PAGES), *([cache_v] * N_PAGES))


def _softmax_rows(s):
    m = jnp.max(s, axis=-1, keepdims=True)
    p = jnp.exp(s - m)
    return p / jnp.sum(p, axis=-1, keepdims=True)


def _xattn_prompt_kernel(x_ref, g_ref, wq_ref, mk_ref, mv_ref, wo_ref, o_ref, att_ref):
    x = x_ref[0]
    hn = _rms(x, g_ref[...]).astype(BF16)
    q = _dot(hn, wq_ref[...]).astype(BF16)
    for h in range(H_X):
        sl = slice(h * HD_X, (h + 1) * HD_X)
        s = _dot_nt(q[:, sl], mk_ref[0, :, sl].astype(BF16)) * (HD_X ** -0.5)
        p = _softmax_rows(s)
        att_ref[:, sl] = _dot(p.astype(BF16), mv_ref[0, :, sl].astype(BF16)).astype(BF16)
    o_ref[0] = x + _dot(att_ref[...], wo_ref[...])


def _xattn_prompt(x3, g, wq, mk3, mv3, wo, tm=512):
    b, t, d = x3.shape
    tm = min(tm, t)
    xs = pl.BlockSpec((1, tm, d), lambda bi, i: (bi, i, 0))
    ws = pl.BlockSpec((d, d), lambda bi, i: (0, 0))
    ms = pl.BlockSpec((1, N_MEM, d), lambda bi, i: (bi, 0, 0))
    return pl.pallas_call(
        _xattn_prompt_kernel,
        grid=(b, t // tm),
        in_specs=[xs, pl.BlockSpec((1, d), lambda bi, i: (0, 0)), ws, ms, ms, ws],
        out_specs=xs,
        out_shape=jax.ShapeDtypeStruct((b, t, d), F32),
        scratch_shapes=[pltpu.VMEM((tm, d), BF16)],
        compiler_params=_params("parallel", "arbitrary"),
        name="xattn_prompt",
    )(x3, g.reshape(1, d), wq, mk3, mv3, wo)


def _xattn_sample_kernel(q_ref, mk_ref, mv_ref, o_ref):
    q_bd = _head_rows(q_ref[0], HD_X).astype(BF16)
    s = _dot_nt(q_bd, mk_ref[...].astype(BF16)) * (HD_X ** -0.5)
    p = _softmax_rows(s)
    o = _dot(p.astype(BF16), mv_ref[...].astype(BF16))
    r = lax.broadcasted_iota(jnp.int32, o.shape, 0)
    c = lax.broadcasted_iota(jnp.int32, o.shape, 1)
    pick = (c >= r * HD_X) & (c < (r + 1) * HD_X)
    o_ref[0] = jnp.sum(jnp.where(pick, o, 0.0), axis=0, keepdims=True).astype(o_ref.dtype)


def _xattn_sample(q3, mem_k, mem_v, l):
    bs, _, d = q3.shape
    ms = pl.BlockSpec((None, None, N_MEM, d), lambda b: (l, b, 0, 0))
    return pl.pallas_call(
        _xattn_sample_kernel,
        grid=(bs,),
        in_specs=[pl.BlockSpec((1, 1, d), lambda b: (b, 0, 0)), ms, ms],
        out_specs=pl.BlockSpec((1, 1, d), lambda b: (b, 0, 0)),
        out_shape=jax.ShapeDtypeStruct((bs, 1, d), BF16),
        compiler_params=_params("parallel"),
        name="xattn_sample",
    )(q3, mem_k, mem_v)


FFN_TF = 256
HALO = SUBLANES


def _ffn_prompt_kernel(x_ref, g_ref, wg_ref, wv_ref, dwg_ref, dwv_ref, bg_ref, bv_ref, wd_ref,
                       o_ref, hn_ref, acc_ref, u_ref, tail_ref):
    t = pl.program_id(1)
    f = pl.program_id(2)
    tm = x_ref.shape[1]
    tf = wg_ref.shape[1]

    @pl.when(f == 0)
    def _():
        hn_ref[...] = _rms(x_ref[0], g_ref[...]).astype(BF16)
        acc_ref[...] = jnp.zeros_like(acc_ref)

    hn = hn_ref[...]
    u_ref[HALO:, :tf] = _dot(hn, wg_ref[...])
    u_ref[HALO:, tf:] = _dot(hn, wv_ref[...])
    u_ref[:HALO, :] = jnp.where(t == 0, 0.0, tail_ref[f])
    tail_ref[f] = u_ref[tm:, :]

    def conv(dw_ref, b_ref, sl):
        c = b_ref[...] + dw_ref[FFN_CONV_W - 1:FFN_CONV_W, :] * u_ref[HALO:, sl]
        for w in range(FFN_CONV_W - 1):
            off = HALO - (FFN_CONV_W - 1) + w
            c += dw_ref[w:w + 1, :] * u_ref[off:off + tm, sl]
        return c

    gated = _silu(conv(dwg_ref, bg_ref, slice(0, tf))) * conv(dwv_ref, bv_ref, slice(tf, 2 * tf))
    acc_ref[...] += _dot(gated.astype(BF16), wd_ref[...])

    @pl.when(f == pl.num_programs(2) - 1)
    def _():
        o_ref[0] = x_ref[0] + acc_ref[...]


def _ffn_prompt(x3, g, w_up, dw, dwb, w_down, tm=1024):
    b, t, d = x3.shape
    tm = min(tm, t)
    nf = D_FF // FFN_TF
    assert D_FF % FFN_TF == 0 and t % tm == 0
    xs = pl.BlockSpec((1, tm, d), lambda bi, i, f: (bi, i, 0))
    dwb2 = dwb.reshape(1, -1)
    return pl.pallas_call(
        _ffn_prompt_kernel,
        grid=(b, t // tm, nf),
        in_specs=[xs, pl.BlockSpec((1, d), lambda bi, i, f: (0, 0)),
                  pl.BlockSpec((d, FFN_TF), lambda bi, i, f: (0, f)),
                  pl.BlockSpec((d, FFN_TF), lambda bi, i, f: (0, f + nf)),
                  pl.BlockSpec((FFN_CONV_W, FFN_TF), lambda bi, i, f: (0, f)),
                  pl.BlockSpec((FFN_CONV_W, FFN_TF), lambda bi, i, f: (0, f + nf)),
                  pl.BlockSpec((1, FFN_TF), lambda bi, i, f: (0, f)),
                  pl.BlockSpec((1, FFN_TF), lambda bi, i, f: (0, f + nf)),
                  pl.BlockSpec((FFN_TF, d), lambda bi, i, f: (f, 0))],
        out_specs=xs,
        out_shape=jax.ShapeDtypeStruct((b, t, d), F32),
        scratch_shapes=[pltpu.VMEM((tm, d), BF16), pltpu.VMEM((tm, d), F32),
                        pltpu.VMEM((HALO + tm, 2 * FFN_TF), F32),
                        pltpu.VMEM((nf, HALO, 2 * FFN_TF), F32)],
        compiler_params=_params("parallel", "arbitrary", "arbitrary"),
        name="ffn_prompt",
    )(x3, g.reshape(1, d), w_up, w_up, dw, dw, dwb2, dwb2, w_down)


CONV_HALO = 32
CONV_RB = 16


def _conv_prompt_kernel(u_ref, x_ref, dw_ref, dwb_ref, lg_ref, lb_ref, w2_ref, b2_ref,
                        o_ref, s_ref, c_ref):
    t = pl.program_id(1)
    tm = u_ref.shape[1]

    @pl.when(t == 0)
    def _():
        s_ref[:CONV_HALO, :] = jnp.zeros((CONV_HALO, D_MODEL), F32)

    @pl.when(t > 0)
    def _():
        s_ref[:CONV_HALO, :] = s_ref[tm:, :]

    s_ref[CONV_HALO:, :] = u_ref[0]
    base = CONV_HALO - (CONV_W - 1)
    for rb in range(tm // CONV_RB):
        r0 = rb * CONV_RB
        acc = jnp.broadcast_to(dwb_ref[...], (CONV_RB, D_MODEL))
        for w in range(CONV_W):
            acc += dw_ref[w:w + 1, :] * s_ref[base + r0 + w:base + r0 + w + CONV_RB, :]
        c_ref[r0:r0 + CONV_RB, :] = acc
    c = c_ref[...]
    mu = jnp.mean(c, axis=-1, keepdims=True)
    cc = c - mu
    var = jnp.mean(cc * cc, axis=-1, keepdims=True)
    y = _silu(cc * lax.rsqrt(var + NORM_EPS) * lg_ref[...] + lb_ref[...])
    o_ref[0] = x_ref[0] + _dot(y.astype(BF16), w2_ref[...]) + b2_ref[...]


def _conv_prompt(u3, x3, dw, dwb, ln_g, ln_b, w2, b2, tm=256):
    b, t, d = x3.shape
    xs = pl.BlockSpec((1, tm, d), lambda bi, i: (bi, i, 0))
    vec = pl.BlockSpec((1, d), lambda bi, i: (0, 0))
    return pl.pallas_call(
        _conv_prompt_kernel,
        grid=(b, t // tm),
        in_specs=[xs, xs, pl.BlockSpec((CONV_W, d), lambda bi, i: (0, 0)), vec, vec, vec,
                  pl.BlockSpec((d, d), lambda bi, i: (0, 0)), vec],
        out_specs=xs,
        out_shape=jax.ShapeDtypeStruct((b, t, d), F32),
        scratch_shapes=[pltpu.VMEM((CONV_HALO + tm, d), F32), pltpu.VMEM((tm, d), F32)],
        compiler_params=_params("parallel", "arbitrary"),
        name="conv_prompt",
    )(u3, x3, dw, dwb.reshape(1, d), ln_g.reshape(1, d), ln_b.reshape(1, d), w2, b2.reshape(1, d))


STEP_BB = 8


def _conv_step_kernel(st_ref, u_ref, dw_ref, dwb_ref, o_ref):
    w_taps = dw_ref.shape[0]
    for r in range(STEP_BB):
        acc = dwb_ref[...] + dw_ref[w_taps - 1:w_taps, :] * u_ref[r:r + 1, :]
        for w in range(w_taps - 1):
            acc += dw_ref[w:w + 1, :] * st_ref[r, w:w + 1, :]
        o_ref[r:r + 1, :] = acc


def _conv_step(state, l, u, dw, dwb):
    bs, c = u.shape
    w_taps = dw.shape[0]
    return pl.pallas_call(
        _conv_step_kernel,
        grid=(bs // STEP_BB,),
        in_specs=[pl.BlockSpec((None, STEP_BB, w_taps - 1, c), lambda i: (l, i, 0, 0)),
                  pl.BlockSpec((STEP_BB, c), lambda i: (i, 0)),
                  pl.BlockSpec((w_taps, c), lambda i: (0, 0)),
                  pl.BlockSpec((1, c), lambda i: (0, 0))],
        out_specs=pl.BlockSpec((STEP_BB, c), lambda i: (i, 0)),
        out_shape=jax.ShapeDtypeStruct((bs, c), F32),
        compiler_params=_params("parallel"),
        name="conv_step",
    )(state, u, dw, dwb.reshape(1, c))


def kernel(x_prompt, x_sample, cache_sb_k, cache_sb_v, state_ret, state_conv, state_ffn_conv, cache_mem_k, cache_mem_v, page_table, mem_prompt, g_mix, w_in_ab, ret_gn_g, w_out_ab, sb_bias, cv_w1, cv_b1, cv_dw, cv_dwb, cv_ln_g, cv_ln_b, cv_w2, cv_b2, g_cross, xa_wq, xa_wk, xa_wv, xa_wo, g_ffn, ffn_w_up, ffn_dw, ffn_dwb, ffn_w_down, g_final):
    bp, t, d = x_prompt.shape
    bs = x_sample.shape[0]
    n_phys = cache_sb_k.shape[1]

    log_gamma = jnp.log1p(-jnp.exp2(-5.0 - jnp.arange(H_RET, dtype=F32)))
    ret_tabs = _ret_tables(log_gamma)
    cos_p, sin_p = _rope_tables(jnp.arange(t))
    cos_s, sin_s = _rope_tables(PAST_LEN + jnp.arange(1))
    uo = _suffix_matrix()
    cache_k = cache_sb_k.reshape(cache_sb_k.shape[0], n_phys, PAGE_SIZE, W_SB)
    cache_v = cache_sb_v.reshape(cache_sb_v.shape[0], n_phys, PAGE_SIZE, W_SB)
    mem_k = cache_mem_k.reshape(DEPTH, bs, N_MEM, d)
    mem_v = cache_mem_v.reshape(DEPTH, bs, N_MEM, d)
    mem2 = mem_prompt.reshape(bp * N_MEM, d)

    xp = x_prompt.reshape(bp * t, d)
    xs = x_sample.reshape(bs, d)
    sbk_p, sbv_p, sbk_s, sbv_s, ret_p, ret_s = [], [], [], [], [], []
    cv_p, cv_s, ff_p, ff_s, mk_p, mv_p = [], [], [], [], [], []

    for l in range(DEPTH):
        if l % 2 == 0:
            e = l // 2
            w_in = w_in_ab[e].astype(BF16)
            w_out = w_out_ab[e].astype(BF16)
            w_out_parts = [w_out[:W_RK], w_out[W_RK:]]
            bias = sb_bias[e].astype(F32)
            bias_row = jnp.repeat(bias, HD_SB).reshape(1, W_SB)
            bias_rows = jnp.zeros((MXU_ROWS, PAGE_SIZE), F32).at[:H_SB].set(
                jnp.broadcast_to(bias[:, None], (H_SB, PAGE_SIZE)))
            proj = _mm([xp], [w_in], prologue="rms", pro=(g_mix[l],), tm=1024, name="proj_in")
            proj3 = proj.reshape(bp, t, -1)
            ret_o, s_p = _ret_prompt(proj3, cos_p, sin_p, ret_tabs, ret_gn_g[e])
            sb_o = _sb_prompt(proj3, bias_row, uo)
            xp = _mm([ret_o.reshape(bp * t, W_RK), sb_o.reshape(bp * t, W_SB)], w_out_parts,
                     res=xp, tm=1024, tn=d, name="proj_out")
            sbk_p.append(proj3[:, :, 5 * W_SB:6 * W_SB].reshape(bp, t, H_SB, HD_SB))
            sbv_p.append(proj3[:, :, 6 * W_SB:7 * W_SB].reshape(bp, t, H_SB, HD_SB))
            ret_p.append(s_p)
            proj_s = _mm([xs], [w_in], prologue="rms", pro=(g_mix[l],), name="proj_in_s")
            ret_os, s_s = _ret_sample(proj_s, cos_s, sin_s, ret_gn_g[e], state_ret, e)
            sb_os = _sb_sample(proj_s.reshape(bs, 1, -1), bias_rows, uo, cache_k, cache_v,
                               page_table, e)
            xs = _mm([ret_os, sb_os.reshape(bs, W_SB)], w_out_parts, res=xs, tn=d,
                     name="proj_out_s")
            sbk_s.append(proj_s[:, 5 * W_SB:6 * W_SB].reshape(bs, 1, H_SB, HD_SB))
            sbv_s.append(proj_s[:, 6 * W_SB:7 * W_SB].reshape(bs, 1, H_SB, HD_SB))
            ret_s.append(s_s)
        else:
            o = l // 2
            w1 = cv_w1[o].astype(BF16)
            w2 = cv_w2[o].astype(BF16)
            u = _mm([xp], [w1], prologue="rms", pro=(g_mix[l],), bias=cv_b1[o], glu=True,
                    tm=1024, name="conv_glu")
            u3 = u.reshape(bp, t, d)
            xp = _conv_prompt(u3, xp.reshape(bp, t, d), cv_dw[o], cv_dwb[o], cv_ln_g[o],
                              cv_ln_b[o], w2, cv_b2[o]).reshape(bp * t, d)
            cv_p.append(u3[:, t - (CONV_W - 1):, :])
            u_s = _mm([xs], [w1], prologue="rms", pro=(g_mix[l],), bias=cv_b1[o], glu=True,
                      name="conv_glu_s")
            c_s = _conv_step(state_conv, o, u_s, cv_dw[o], cv_dwb[o])
            xs = _mm([c_s], [w2], prologue="ln_silu", pro=(cv_ln_g[o], cv_ln_b[o]), bias=cv_b2[o],
                     res=xs, tn=d, name="conv_out_s")
            cv_s.append(jnp.concatenate([state_conv[o][:, 1:], u_s[:, None, :]], axis=1))

        wq = xa_wq[l].astype(BF16)
        wo = xa_wo[l].astype(BF16)
        mk = _mm([mem2], [xa_wk[l].astype(BF16)], tn=d, name="mem_k")
        mv = _mm([mem2], [xa_wv[l].astype(BF16)], tn=d, name="mem_v")
        mk_p.append(mk.reshape(bp, N_MEM, H_X, HD_X))
        mv_p.append(mv.reshape(bp, N_MEM, H_X, HD_X))
        xp = _xattn_prompt(xp.reshape(bp, t, d), g_cross[l], wq, mk.reshape(bp, N_MEM, d),
                           mv.reshape(bp, N_MEM, d), wo)
        q_s = _mm([xs], [wq], prologue="rms", pro=(g_cross[l],), tn=d, name="xattn_q_s")
        att_s = _xattn_sample(q_s.reshape(bs, 1, d), mem_k, mem_v, l)
        xs = _mm([att_s.reshape(bs, d)], [wo], res=xs, tn=d, name="xattn_out_s")

        w_up = ffn_w_up[l].astype(BF16)
        w_down = ffn_w_down[l].astype(BF16)
        tail_rows = xp[:, t - (FFN_CONV_W - 1):, :].reshape(bp * (FFN_CONV_W - 1), d)
        ff_p.append(_mm([tail_rows], [w_up], prologue="rms", pro=(g_ffn[l],),
                        name="ffn_tail").reshape(bp, FFN_CONV_W - 1, 2 * D_FF))
        xp = _ffn_prompt(xp, g_ffn[l], w_up, ffn_dw[l], ffn_dwb[l], w_down).reshape(bp * t, d)
        u_s = _mm([xs], [w_up], prologue="rms", pro=(g_ffn[l],), name="ffn_up_s")
        c_s = _conv_step(state_ffn_conv, l, u_s, ffn_dw[l], ffn_dwb[l])
        xs = _mm([c_s], [w_down], prologue="swiglu", res=xs, tn=d, name="ffn_down_s")
        ff_s.append(jnp.concatenate([state_ffn_conv[l][:, 1:], u_s[:, None, :]], axis=1))

    y_prompt = _rmsnorm(xp, g_final).reshape(bp, t, d)
    y_sample = _rmsnorm(xs, g_final).reshape(bs, 1, d)
    return (y_prompt, y_sample, jnp.stack(sbk_p), jnp.stack(sbv_p), jnp.stack(sbk_s),
            jnp.stack(sbv_s), jnp.stack(ret_p), jnp.stack(ret_s), jnp.stack(cv_p),
            jnp.stack(cv_s), jnp.stack(ff_p), jnp.stack(ff_s), jnp.stack(mk_p), jnp.stack(mv_p))
```

```python
import functools

import numpy as np
import jax
import jax.numpy as jnp
from jax import lax
from jax.experimental import pallas as pl
from jax.experimental.pallas import tpu as pltpu

F32 = jnp.float32
BF16 = jnp.bfloat16

D_MODEL = 1024
DEPTH = 4
PAST_LEN = 2048
PAGE_SIZE = 128
N_PAGES = PAST_LEN // PAGE_SIZE
H_RET = 4
DK_RET = 128
DV_RET = 128
RET_CHUNK = 128
ROPE_BASE = 10000.0
H_SB = 4
HD_SB = 128
SB_BLOCK = 128
CONV_W = 31
D_FF = 2816
FFN_CONV_W = 3
N_MEM = 256
H_X = 4
HD_X = D_MODEL // H_X
NORM_EPS = 1e-6
W_RK = H_RET * DK_RET
W_SB = H_SB * HD_SB

SUBLANES = 8
LANES = 128
VMEM_LIMIT_BYTES = 56 * 1024 * 1024


def _params(*sem):
    return pltpu.CompilerParams(dimension_semantics=sem, vmem_limit_bytes=VMEM_LIMIT_BYTES)


def _dot(a, b):
    return jnp.dot(a, b, preferred_element_type=F32)


def _dot_nt(a, b):
    return lax.dot_general(a, b, (((1,), (1,)), ((), ())), preferred_element_type=F32)


def _rms(x, g):
    return x * lax.rsqrt(jnp.mean(x * x, axis=-1, keepdims=True) + NORM_EPS) * g


def _silu(x):
    return x * jax.nn.sigmoid(x)


def _softplus(z):
    return jnp.maximum(z, 0.0) + jnp.log1p(jnp.exp(-jnp.abs(z)))


def _split_hi_lo(x):
    hi = x.astype(BF16)
    lo = (x - hi.astype(F32)).astype(BF16)
    return hi, lo


def _mm_kernel(*refs, n_lhs, prologue, has_bias, has_res, glu):
    it = iter(refs)
    x_refs = [next(it) for _ in range(n_lhs)]
    w_refs = [next(it) for _ in range(n_lhs)]
    wg_refs = [next(it) for _ in range(n_lhs)] if glu else []
    n_pro = {None: 0, "rms": 1, "ln_silu": 2, "swiglu": 0}[prologue]
    p_refs = [next(it) for _ in range(n_pro)]
    b_ref = next(it) if has_bias else None
    bg_ref = next(it) if (has_bias and glu) else None
    r_ref = next(it) if has_res else None
    o_ref = next(it)
    xn_ref = next(it) if prologue else None

    if prologue:
        @pl.when(pl.program_id(1) == 0)
        def _():
            x = x_refs[0][...].astype(F32)
            if prologue == "rms":
                y = _rms(x, p_refs[0][...])
            elif prologue == "ln_silu":
                mu = jnp.mean(x, axis=-1, keepdims=True)
                xc = x - mu
                var = jnp.mean(xc * xc, axis=-1, keepdims=True)
                y = _silu(xc * lax.rsqrt(var + NORM_EPS) * p_refs[0][...] + p_refs[1][...])
            else:
                k = x.shape[-1] // 2
                y = _silu(x[:, :k]) * x[:, k:]
            xn_ref[...] = y.astype(BF16)
        lhs = [xn_ref[...]]
    else:
        lhs = [r[...].astype(BF16) for r in x_refs]

    acc = _dot(lhs[0], w_refs[0][...])
    for a, w in zip(lhs[1:], w_refs[1:]):
        acc += _dot(a, w[...])
    if has_bias:
        acc += b_ref[...]
    if glu:
        gate = _dot(lhs[0], wg_refs[0][...])
        for a, w in zip(lhs[1:], wg_refs[1:]):
            gate += _dot(a, w[...])
        if has_bias:
            gate += bg_ref[...]
        acc = acc * jax.nn.sigmoid(gate)
    if has_res:
        acc += r_ref[...]
    o_ref[...] = acc.astype(o_ref.dtype)


def _mm(xs, ws, *, prologue=None, pro=(), bias=None, res=None, glu=False, tm=512, tn=512,
        out_dtype=F32, name="mm"):
    m = xs[0].shape[0]
    n = ws[0].shape[1] // (2 if glu else 1)
    tm = min(tm, m)
    tn = min(tn, n)
    assert m % tm == 0 and n % tn == 0, (m, tm, n, tn)
    nj = n // tn
    args, specs = [], []
    for x in xs:
        args.append(x)
        specs.append(pl.BlockSpec((tm, x.shape[1]), lambda i, j: (i, 0)))
    for w in ws:
        args.append(w)
        specs.append(pl.BlockSpec((w.shape[0], tn), lambda i, j: (0, j)))
    if glu:
        for w in ws:
            args.append(w)
            specs.append(pl.BlockSpec((w.shape[0], tn), lambda i, j: (0, j + nj)))
    for p in pro:
        args.append(p.reshape(1, -1))
        specs.append(pl.BlockSpec((1, p.size), lambda i, j: (0, 0)))
    if bias is not None:
        b2 = bias.reshape(1, -1)
        args.append(b2)
        specs.append(pl.BlockSpec((1, tn), lambda i, j: (0, j)))
        if glu:
            args.append(b2)
            specs.append(pl.BlockSpec((1, tn), lambda i, j: (0, j + nj)))
    if res is not None:
        args.append(res)
        specs.append(pl.BlockSpec((tm, tn), lambda i, j: (i, j)))
    scratch = []
    if prologue:
        k_eff = xs[0].shape[1] // (2 if prologue == "swiglu" else 1)
        scratch.append(pltpu.VMEM((tm, k_eff), BF16))
    kern = functools.partial(_mm_kernel, n_lhs=len(xs), prologue=prologue,
                             has_bias=bias is not None, has_res=res is not None, glu=glu)
    return pl.pallas_call(
        kern,
        grid=(m // tm, nj),
        in_specs=specs,
        out_specs=pl.BlockSpec((tm, tn), lambda i, j: (i, j)),
        out_shape=jax.ShapeDtypeStruct((m, n), out_dtype),
        scratch_shapes=scratch,
        compiler_params=_params("parallel", "arbitrary"),
        name=name,
    )(*args)


def _rmsnorm_kernel(x_ref, g_ref, o_ref):
    o_ref[...] = _rms(x_ref[...], g_ref[...])


def _rmsnorm(x, g, tm=1024):
    m, d = x.shape
    tm = min(tm, m)
    return pl.pallas_call(
        _rmsnorm_kernel,
        grid=(m // tm,),
        in_specs=[pl.BlockSpec((tm, d), lambda i: (i, 0)), pl.BlockSpec((1, d), lambda i: (0, 0))],
        out_specs=pl.BlockSpec((tm, d), lambda i: (i, 0)),
        out_shape=jax.ShapeDtypeStruct((m, d), F32),
        compiler_params=_params("parallel"),
        name="rmsnorm",
    )(x, g.reshape(1, d))


def _rotate(x, cos2, sin2):
    return x * cos2 + pltpu.roll(x, DK_RET // 2, axis=1) * sin2


def _head_ln_gate(o, gate, gn):
    mu = jnp.mean(o, axis=-1, keepdims=True)
    oc = o - mu
    var = jnp.mean(oc * oc, axis=-1, keepdims=True)
    return _silu(gate) * (oc * lax.rsqrt(var + NORM_EPS) * gn)


def _ret_prompt_kernel(rq_ref, rk_ref, rv_ref, rg_ref, cos_ref, sin_ref, dmask_ref, qdec_ref,
                       kdec_ref, cdec_ref, gn_ref, o_ref, s_ref):
    c = pl.program_id(1)

    @pl.when(c == 0)
    def _():
        s_ref[...] = jnp.zeros_like(s_ref)

    cos2 = cos_ref[...]
    sin2 = sin_ref[...]
    for h in range(H_RET):
        sl = slice(h * DK_RET, (h + 1) * DK_RET)
        q = _rotate(rq_ref[0, :, sl], cos2, sin2)
        k = _rotate(rk_ref[0, :, sl], cos2, sin2) * (DK_RET ** -0.5)
        vb = rv_ref[0, :, sl].astype(BF16)
        qb = q.astype(BF16)
        s0 = s_ref[0, h]
        scores = _dot_nt(qb, k.astype(BF16)) * dmask_ref[h]
        o = _dot(scores.astype(BF16), vb) + _dot(qb, s0.astype(BF16)) * qdec_ref[h]
        kd_t = (k * kdec_ref[h]).T.astype(BF16)
        s_ref[0, h] = cdec_ref[h] * s0 + _dot(kd_t, vb)
        o_ref[0, :, sl] = _head_ln_gate(o, rg_ref[0, :, sl], gn_ref[:, sl]).astype(o_ref.dtype)


def _ret_tables(log_gamma):
    idx = jnp.arange(RET_CHUNK)
    diff = idx[:, None] - idx[None, :]
    expo = jnp.maximum(diff, 0).astype(F32)[None] * log_gamma[:, None, None]
    dmask = jnp.where(diff[None] >= 0, jnp.exp(expo), 0.0)
    q_decay = jnp.exp((idx + 1).astype(F32)[:, None] * log_gamma[None, :])
    k_decay = jnp.exp((RET_CHUNK - 1 - idx).astype(F32)[:, None] * log_gamma[None, :])
    ones = jnp.ones((1, 1, DK_RET), F32)
    qdec = q_decay.T[:, :, None] * ones
    kdec = k_decay.T[:, :, None] * ones
    cdec = jnp.exp(RET_CHUNK * log_gamma)[:, None, None] * jnp.ones((1, DK_RET, DV_RET), F32)
    return dmask, qdec, kdec, cdec


def _rope_tables(pos):
    half = DK_RET // 2
    inv = ROPE_BASE ** (-jnp.arange(half, dtype=F32) / half)
    ang = pos.astype(F32)[:, None] * inv[None, :]
    cos, sin = jnp.cos(ang), jnp.sin(ang)
    return jnp.concatenate([cos, cos], -1), jnp.concatenate([-sin, sin], -1)


def _ret_prompt(proj3, cos2, sin2, tabs, gn):
    b, t, _ = proj3.shape
    nc = t // RET_CHUNK
    dmask, qdec, kdec, cdec = tabs

    def col(cb):
        return pl.BlockSpec((1, RET_CHUNK, W_RK), lambda i, c: (i, c, cb))

    tab = pl.BlockSpec((H_RET, RET_CHUNK, RET_CHUNK), lambda i, c: (0, 0, 0))
    return pl.pallas_call(
        _ret_prompt_kernel,
        grid=(b, nc),
        in_specs=[col(0), col(1), col(2), col(3),
                  pl.BlockSpec((RET_CHUNK, DK_RET), lambda i, c: (c, 0)),
                  pl.BlockSpec((RET_CHUNK, DK_RET), lambda i, c: (c, 0)),
                  tab, tab, tab, tab,
                  pl.BlockSpec((1, W_RK), lambda i, c: (0, 0))],
        out_specs=[pl.BlockSpec((1, RET_CHUNK, W_RK), lambda i, c: (i, c, 0)),
                   pl.BlockSpec((1, H_RET, DK_RET, DV_RET), lambda i, c: (i, 0, 0, 0))],
        out_shape=[jax.ShapeDtypeStruct((b, t, W_RK), BF16),
                   jax.ShapeDtypeStruct((b, H_RET, DK_RET, DV_RET), F32)],
        compiler_params=_params("parallel", "arbitrary"),
        name="ret_prompt",
    )(proj3, proj3, proj3, proj3, cos2, sin2, dmask, qdec, kdec, cdec, gn.reshape(1, W_RK))


RET_BB = 8
MXU_ROWS = 16


def _ret_sample_kernel(rq_ref, rk_ref, rv_ref, rg_ref, cos_ref, sin_ref, gn_ref, st_ref,
                       o_ref, so_ref, inter_ref, *, g1):
    cos2 = cos_ref[...]
    sin2 = sin_ref[...]
    eye = (lax.broadcasted_iota(jnp.int32, (DK_RET, DK_RET), 0)
           == lax.broadcasted_iota(jnp.int32, (DK_RET, DK_RET), 1))
    for h in range(H_RET):
        sl = slice(h * DK_RET, (h + 1) * DK_RET)
        q = _rotate(rq_ref[:, sl], cos2, sin2)
        k = _rotate(rk_ref[:, sl], cos2, sin2) * (DK_RET ** -0.5)
        v = rv_ref[:, sl]
        for r in range(RET_BB):
            s0 = st_ref[r, h]
            qr = jnp.broadcast_to(q[r:r + 1], (MXU_ROWS, DK_RET)).astype(BF16)
            inter_ref[r:r + 1, :] = _dot(qr, s0.astype(BF16))[0:1] * g1[h]
            diag_k = jnp.where(eye, jnp.broadcast_to(k[r:r + 1], (DK_RET, DK_RET)), 0.0).astype(BF16)
            v_rows = jnp.broadcast_to(v[r:r + 1], (DK_RET, DV_RET)).astype(BF16)
            so_ref[r, h] = g1[h] * s0 + _dot(diag_k, v_rows)
        o = jnp.sum(q * k, axis=-1, keepdims=True) * v + inter_ref[...]
        o_ref[:, sl] = _head_ln_gate(o, rg_ref[:, sl], gn_ref[:, sl]).astype(o_ref.dtype)


def _ret_sample(proj, cos2, sin2, gn, state, e):
    bs = proj.shape[0]
    log_gamma = np.log1p(-np.exp2(-5.0 - np.arange(H_RET, dtype=np.float32)))
    g1 = tuple(float(x) for x in np.exp(log_gamma).astype(np.float32))

    def col(cb):
        return pl.BlockSpec((RET_BB, W_RK), lambda i: (i, cb))

    row = pl.BlockSpec((1, DK_RET), lambda i: (0, 0))
    return pl.pallas_call(
        functools.partial(_ret_sample_kernel, g1=g1),
        grid=(bs // RET_BB,),
        in_specs=[col(0), col(1), col(2), col(3), row, row,
                  pl.BlockSpec((1, W_RK), lambda i: (0, 0)),
                  pl.BlockSpec((None, RET_BB, H_RET, DK_RET, DV_RET), lambda i: (e, i, 0, 0, 0))],
        out_specs=[pl.BlockSpec((RET_BB, W_RK), lambda i: (i, 0)),
                   pl.BlockSpec((RET_BB, H_RET, DK_RET, DV_RET), lambda i: (i, 0, 0, 0))],
        out_shape=[jax.ShapeDtypeStruct((bs, W_RK), BF16),
                   jax.ShapeDtypeStruct((bs, H_RET, DK_RET, DV_RET), F32)],
        scratch_shapes=[pltpu.VMEM((RET_BB, DV_RET), F32)],
        compiler_params=_params("parallel"),
        name="ret_sample",
    )(proj, proj, proj, proj, cos2, sin2, gn.reshape(1, W_RK), state)


def _suffix_matrix():
    j = jnp.arange(SB_BLOCK)
    u = (j[:, None] > j[None, :]).astype(BF16)
    return jnp.concatenate([u, jnp.ones((SB_BLOCK, SB_BLOCK), BF16)], axis=1)


def _sb_block(z, sp, log_1m, carry, uo):
    rows = z.shape[0]
    hi, lo = _split_hi_lo(log_1m)
    r = _dot(jnp.concatenate([hi, lo], axis=0), uo)
    after = r[:rows, :SB_BLOCK] + r[rows:, :SB_BLOCK] + carry
    total = r[:rows, SB_BLOCK:] + r[rows:, SB_BLOCK:]
    return z - sp + after, total


SB_TQ = 256


def _sb_prompt_kernel(q_ref, k_ref, v_ref, bias_ref, uo_ref, o_ref, acc_ref, carry_ref):
    i = pl.program_id(1)
    uo = uo_ref[...]
    rows = H_SB * SB_TQ
    heads = [slice(h * HD_SB, (h + 1) * HD_SB) for h in range(H_SB)]

    def block(j, masked, first):
        start = pl.multiple_of(j * SB_BLOCK, SB_BLOCK)
        z = jnp.concatenate(
            [_dot_nt(q_ref[0, :, sl].astype(BF16), k_ref[0, pl.ds(start, SB_BLOCK), sl].astype(BF16))
             * (HD_SB ** -0.5) + bias_ref[:, sl] for sl in heads], axis=0)
        sp = _softplus(z)
        if masked:
            qpos = i * SB_TQ + (lax.broadcasted_iota(jnp.int32, z.shape, 0) & (SB_TQ - 1))
            kpos = start + lax.broadcasted_iota(jnp.int32, z.shape, 1)
            valid = kpos < qpos
            log_1m = jnp.where(valid, -sp, 0.0)
        else:
            log_1m = -sp
        carry = 0.0 if first else carry_ref[...]
        loga, total = _sb_block(z, sp, log_1m, carry, uo)
        a = jnp.exp(loga)
        if masked:
            a = jnp.where(valid, a, 0.0)
        ab = a.astype(BF16)
        pv = jnp.concatenate(
            [_dot(ab[h * SB_TQ:(h + 1) * SB_TQ], v_ref[0, pl.ds(start, SB_BLOCK), sl].astype(BF16))
             for h, sl in enumerate(heads)], axis=0)
        if first:
            acc_ref[...] = pv
            carry_ref[...] = total
        else:
            acc_ref[...] += pv
            carry_ref[...] = carry + total

    n_diag = SB_TQ // SB_BLOCK
    for d in reversed(range(n_diag)):
        block(i * n_diag + d, True, d == n_diag - 1)

    def body(t, _):
        block(i * n_diag - 1 - t, False, False)
        return 0

    lax.fori_loop(0, i * n_diag, body, 0)
    for h, sl in enumerate(heads):
        o_ref[0, :, sl] = acc_ref[h * SB_TQ:(h + 1) * SB_TQ, :].astype(o_ref.dtype)


def _sb_prompt(proj3, bias_row, uo):
    b, t, _ = proj3.shape
    tq = min(SB_TQ, t)
    assert tq == SB_TQ and t % tq == 0
    return pl.pallas_call(
        _sb_prompt_kernel,
        grid=(b, t // tq),
        in_specs=[pl.BlockSpec((1, tq, W_SB), lambda bi, i: (bi, i, 4)),
                  pl.BlockSpec((1, t, W_SB), lambda bi, i: (bi, 0, 5)),
                  pl.BlockSpec((1, t, W_SB), lambda bi, i: (bi, 0, 6)),
                  pl.BlockSpec((1, W_SB), lambda bi, i: (0, 0)),
                  pl.BlockSpec((SB_BLOCK, 2 * SB_BLOCK), lambda bi, i: (0, 0))],
        out_specs=pl.BlockSpec((1, tq, W_SB), lambda bi, i: (bi, i, 0)),
        out_shape=jax.ShapeDtypeStruct((b, t, W_SB), BF16),
        scratch_shapes=[pltpu.VMEM((H_SB * tq, HD_SB), F32),
                        pltpu.VMEM((H_SB * tq, SB_BLOCK), F32)],
        compiler_params=_params("parallel", "arbitrary"),
        name="sb_prompt",
    )(proj3, proj3, proj3, bias_row, uo)


def _row_select(parts):
    r = lax.broadcasted_iota(jnp.int32, parts[0].shape, 0)
    out = jnp.zeros_like(parts[0])
    for h, p in enumerate(parts):
        out = jnp.where(r == h, p, out)
    return out


def _bcast_rows_b16(row):
    return jnp.broadcast_to(row, (MXU_ROWS, row.shape[-1])).astype(BF16)


def _sb_sample_kernel(pt_ref, q_ref, bias_ref, uo_ref, *refs):
    del pt_ref
    k_refs = refs[:N_PAGES]
    v_refs = refs[N_PAGES:2 * N_PAGES]
    o_ref = refs[2 * N_PAGES]
    uo = uo_ref[...]
    n_col = PAGE_SIZE * H_SB
    n_grp = n_col // SB_BLOCK
    q4 = _row_select([jnp.broadcast_to(q_ref[0, :, h * HD_SB:(h + 1) * HD_SB], (SUBLANES, HD_SB))
                      for h in range(H_SB)])
    qb = jnp.concatenate([q4, q4], axis=0).astype(BF16)
    row = lax.broadcasted_iota(jnp.int32, (SUBLANES, n_col), 0)
    col = lax.broadcasted_iota(jnp.int32, (SUBLANES, n_col), 1)
    valid = (col & (H_SB - 1)) == row
    bias = bias_ref[...]
    zs, sps, his, los = [], [], [], []
    for p in range(N_PAGES):
        z = _dot_nt(qb, k_refs[p][...].astype(BF16))[:SUBLANES] * (HD_SB ** -0.5) + bias
        sp = _softplus(z)
        hi, lo = _split_hi_lo(jnp.where(valid, -sp, 0.0))
        zs.append(z)
        sps.append(sp)
        his += [hi[:, g * SB_BLOCK:(g + 1) * SB_BLOCK] for g in range(n_grp)]
        los += [lo[:, g * SB_BLOCK:(g + 1) * SB_BLOCK] for g in range(n_grp)]
    n_all = N_PAGES * n_grp
    r = _dot(jnp.concatenate(his + los, axis=0), uo)
    r = r[:n_all * SUBLANES] + r[n_all * SUBLANES:]
    carry = jnp.zeros((SUBLANES, SB_BLOCK), F32)
    after = [None] * n_all
    for g in reversed(range(n_all)):
        rows = slice(g * SUBLANES, (g + 1) * SUBLANES)
        after[g] = r[rows, :SB_BLOCK] + carry
        carry = carry + r[rows, SB_BLOCK:]
    acc = jnp.zeros((MXU_ROWS, HD_SB), F32)
    for p in range(N_PAGES):
        aft = jnp.concatenate(after[p * n_grp:(p + 1) * n_grp], axis=1)
        a = jnp.where(valid, jnp.exp(zs[p] - sps[p] + aft), 0.0)
        ab = jnp.concatenate([a, a], axis=0).astype(BF16)
        acc += _dot(ab, v_refs[p][...].astype(BF16))
    o_ref[0] = acc[:H_SB].astype(o_ref.dtype)


def _sb_sample(proj3, bias_rows, uo, cache_k, cache_v, page_table, e):
    bs = proj3.shape[0]

    def page(p):
        return pl.BlockSpec((None, None, PAGE_SIZE * H_SB, HD_SB),
                            lambda b, pt: (e, pt[b, p], 0, 0))

    grid_spec = pltpu.PrefetchScalarGridSpec(
        num_scalar_prefetch=1,
        grid=(bs,),
        in_specs=[pl.BlockSpec((1, 1, W_SB), lambda b, pt: (b, 0, 4)),
                  pl.BlockSpec((SUBLANES, PAGE_SIZE * H_SB), lambda b, pt: (0, 0)),
                  pl.BlockSpec((SB_BLOCK, 2 * SB_BLOCK), lambda b, pt: (0, 0))]
                 + [page(p) for p in range(N_PAGES)] + [page(p) for p in range(N_PAGES)],
        out_specs=pl.BlockSpec((1, H_SB, HD_SB), lambda b, pt: (b, 0, 0)),
    )
    return pl.pallas_call(
        _sb_sample_kernel,
        grid_spec=grid_spec,
        out_shape=jax.ShapeDtypeStruct((bs, H_SB, HD_SB), F32),
        compiler_params=_params("arbitrary"),
        name="sb_sample",
    )(page_table, proj3, bias_rows, uo, *([cache_k] * N_PAGES), *([cache_v] * N_PAGES))


def _softmax_rows(s):
    m = jnp.max(s, axis=-1, keepdims=True)
    p = jnp.exp(s - m)
    return p / jnp.sum(p, axis=-1, keepdims=True)


def _xattn_prompt_kernel(x_ref, g_ref, wq_ref, mk_ref, mv_ref, wo_ref, o_ref, att_ref):
    x = x_ref[0]
    hn = _rms(x, g_ref[...]).astype(BF16)
    q = _dot(hn, wq_ref[...]).astype(BF16)
    for h in range(H_X):
        sl = slice(h * HD_X, (h + 1) * HD_X)
        s = _dot_nt(q[:, sl], mk_ref[0, :, sl].astype(BF16)) * (HD_X ** -0.5)
        p = _softmax_rows(s)
        att_ref[:, sl] = _dot(p.astype(BF16), mv_ref[0, :, sl].astype(BF16)).astype(BF16)
    o_ref[0] = x + _dot(att_ref[...], wo_ref[...])


def _xattn_prompt(x3, g, wq, mk3, mv3, wo, tm=512):
    b, t, d = x3.shape
    tm = min(tm, t)
    xs = pl.BlockSpec((1, tm, d), lambda bi, i: (bi, i, 0))
    ws = pl.BlockSpec((d, d), lambda bi, i: (0, 0))
    ms = pl.BlockSpec((1, N_MEM, d), lambda bi, i: (bi, 0, 0))
    return pl.pallas_call(
        _xattn_prompt_kernel,
        grid=(b, t // tm),
        in_specs=[xs, pl.BlockSpec((1, d), lambda bi, i: (0, 0)), ws, ms, ms, ws],
        out_specs=xs,
        out_shape=jax.ShapeDtypeStruct((b, t, d), F32),
        scratch_shapes=[pltpu.VMEM((tm, d), BF16)],
        compiler_params=_params("parallel", "arbitrary"),
        name="xattn_prompt",
    )(x3, g.reshape(1, d), wq, mk3, mv3, wo)


XS_HALVES = HD_X // LANES
XS_ROWS = XS_HALVES * H_X


def _xattn_sample_kernel(q_ref, mk_ref, mv_ref, o_ref):
    n_col = N_MEM * XS_ROWS
    q8 = _row_select([jnp.broadcast_to(q_ref[0, :, (a % H_X) * HD_X + (a // H_X) * LANES:
                                             (a % H_X) * HD_X + (a // H_X + 1) * LANES],
                                       (SUBLANES, LANES)) for a in range(XS_ROWS)])
    qb = jnp.concatenate([q8, q8], axis=0).astype(BF16)
    row = lax.broadcasted_iota(jnp.int32, (SUBLANES, n_col), 0)
    col = lax.broadcasted_iota(jnp.int32, (SUBLANES, n_col), 1)
    valid = (col & (XS_ROWS - 1)) == row
    z = jnp.where(valid, _dot_nt(qb, mk_ref[...].astype(BF16))[:SUBLANES], 0.0)
    zr = pltpu.roll(z, H_X, axis=0)
    other = jnp.where(row < H_X, pltpu.roll(zr, n_col - H_X, axis=1), pltpu.roll(zr, H_X, axis=1))
    s = (z + other) * (HD_X ** -0.5)
    m = jnp.max(jnp.where(valid, s, -jnp.inf), axis=-1, keepdims=True)
    p = jnp.where(valid, jnp.exp(s - m), 0.0)
    p = p / jnp.sum(p, axis=-1, keepdims=True)
    pb = jnp.concatenate([p, p], axis=0).astype(BF16)
    o_ref[0] = _dot(pb, mv_ref[...].astype(BF16))[:SUBLANES].astype(o_ref.dtype)


def _xattn_cache_view(cache):
    dp, b = cache.shape[:2]
    c = cache.reshape(dp, b, N_MEM, H_X, XS_HALVES, LANES)
    return jnp.swapaxes(c, 3, 4).reshape(dp, b, N_MEM * XS_ROWS, LANES)


def _xattn_sample(q3, mem_k, mem_v, l):
    bs, _, d = q3.shape
    ms = pl.BlockSpec((None, None, N_MEM * XS_ROWS, LANES), lambda b: (l, b, 0, 0))
    o = pl.pallas_call(
        _xattn_sample_kernel,
        grid=(bs,),
        in_specs=[pl.BlockSpec((1, 1, d), lambda b: (b, 0, 0)), ms, ms],
        out_specs=pl.BlockSpec((1, XS_ROWS, LANES), lambda b: (b, 0, 0)),
        out_shape=jax.ShapeDtypeStruct((bs, XS_ROWS, LANES), F32),
        compiler_params=_params("parallel"),
        name="xattn_sample",
    )(q3, mem_k, mem_v)
    return jnp.swapaxes(o.reshape(bs, XS_HALVES, H_X, LANES), 1, 2).reshape(bs, d)


FFN_TF = 256
HALO = SUBLANES


def _ffn_prompt_kernel(x_ref, g_ref, wg_ref, wv_ref, dwg_ref, dwv_ref, bg_ref, bv_ref, wd_ref,
                       o_ref, hn_ref, acc_ref, u_ref, tail_ref):
    t = pl.program_id(1)
    f = pl.program_id(2)
    tm = x_ref.shape[1]
    tf = wg_ref.shape[1]

    @pl.when(f == 0)
    def _():
        hn_ref[...] = _rms(x_ref[0], g_ref[...]).astype(BF16)
        acc_ref[...] = jnp.zeros_like(acc_ref)

    hn = hn_ref[...]
    u_ref[HALO:, :tf] = _dot(hn, wg_ref[...])
    u_ref[HALO:, tf:] = _dot(hn, wv_ref[...])
    u_ref[:HALO, :] = jnp.where(t == 0, 0.0, tail_ref[f])
    tail_ref[f] = u_ref[tm:, :]

    def conv(dw_ref, b_ref, sl):
        c = b_ref[...] + dw_ref[FFN_CONV_W - 1:FFN_CONV_W, :] * u_ref[HALO:, sl]
        for w in range(FFN_CONV_W - 1):
            off = HALO - (FFN_CONV_W - 1) + w
            c += dw_ref[w:w + 1, :] * u_ref[off:off + tm, sl]
        return c

    gated = _silu(conv(dwg_ref, bg_ref, slice(0, tf))) * conv(dwv_ref, bv_ref, slice(tf, 2 * tf))
    acc_ref[...] += _dot(gated.astype(BF16), wd_ref[...])

    @pl.when(f == pl.num_programs(2) - 1)
    def _():
        o_ref[0] = x_ref[0] + acc_ref[...]


def _ffn_prompt(x3, g, w_up, dw, dwb, w_down, tm=1024):
    b, t, d = x3.shape
    tm = min(tm, t)
    nf = D_FF // FFN_TF
    assert D_FF % FFN_TF == 0 and t % tm == 0
    xs = pl.BlockSpec((1, tm, d), lambda bi, i, f: (bi, i, 0))
    dwb2 = dwb.reshape(1, -1)
    return pl.pallas_call(
        _ffn_prompt_kernel,
        grid=(b, t // tm, nf),
        in_specs=[xs, pl.BlockSpec((1, d), lambda bi, i, f: (0, 0)),
                  pl.BlockSpec((d, FFN_TF), lambda bi, i, f: (0, f)),
                  pl.BlockSpec((d, FFN_TF), lambda bi, i, f: (0, f + nf)),
                  pl.BlockSpec((FFN_CONV_W, FFN_TF), lambda bi, i, f: (0, f)),
                  pl.BlockSpec((FFN_CONV_W, FFN_TF), lambda bi, i, f: (0, f + nf)),
                  pl.BlockSpec((1, FFN_TF), lambda bi, i, f: (0, f)),
                  pl.BlockSpec((1, FFN_TF), lambda bi, i, f: (0, f + nf)),
                  pl.BlockSpec((FFN_TF, d), lambda bi, i, f: (f, 0))],
        out_specs=xs,
        out_shape=jax.ShapeDtypeStruct((b, t, d), F32),
        scratch_shapes=[pltpu.VMEM((tm, d), BF16), pltpu.VMEM((tm, d), F32),
                        pltpu.VMEM((HALO + tm, 2 * FFN_TF), F32),
                        pltpu.VMEM((nf, HALO, 2 * FFN_TF), F32)],
        compiler_params=_params("parallel", "arbitrary", "arbitrary"),
        name="ffn_prompt",
    )(x3, g.reshape(1, d), w_up, w_up, dw, dw, dwb2, dwb2, w_down)


CONV_HALO = 32
CONV_RB = 32


def _conv_prompt_kernel(u_ref, x_ref, dw_ref, dwb_ref, lg_ref, lb_ref, w2_ref, b2_ref,
                        o_ref, s_ref, c_ref, dwt_ref):
    t = pl.program_id(1)
    tm = u_ref.shape[1]
    n = CONV_HALO + tm

    @pl.when(t == 0)
    def _():
        s_ref[0, :CONV_HALO, :] = jnp.zeros((CONV_HALO, D_MODEL), F32)

    @pl.when(t > 0)
    def _():
        s_ref[0, :CONV_HALO, :] = s_ref[0, tm:, :]

    s_ref[0, CONV_HALO:, :] = u_ref[0]
    for r in range(1, SUBLANES):
        s_ref[r] = pltpu.roll(s_ref[0], n - r, axis=0)
    base = CONV_HALO - (CONV_W - 1)

    @pl.when(t == 0)
    def _():
        for w in range(CONV_W):
            dwt_ref[w] = jnp.broadcast_to(dw_ref[w:w + 1, :], (SUBLANES, D_MODEL))

    n_sub = CONV_RB // SUBLANES

    def row_block(rb, carry):
        r0 = rb * CONV_RB
        bias = jnp.broadcast_to(dwb_ref[...], (SUBLANES, D_MODEL))
        accs = [bias] * n_sub
        for w in range(CONV_W):
            r = (base + w) % SUBLANES
            dwt = dwt_ref[w]
            for k in range(n_sub):
                start = pl.multiple_of(r0 + (base + w - r) + k * SUBLANES, SUBLANES)
                accs[k] = accs[k] + dwt * s_ref[r, pl.ds(start, SUBLANES), :]
        for k in range(n_sub):
            c_ref[pl.ds(pl.multiple_of(r0 + k * SUBLANES, SUBLANES), SUBLANES), :] = accs[k]
        return carry

    lax.fori_loop(0, tm // CONV_RB, row_block, 0)
    c = c_ref[...]
    mu = jnp.mean(c, axis=-1, keepdims=True)
    cc = c - mu
    var = jnp.mean(cc * cc, axis=-1, keepdims=True)
    y = _silu(cc * lax.rsqrt(var + NORM_EPS) * lg_ref[...] + lb_ref[...])
    o_ref[0] = x_ref[0] + _dot(y.astype(BF16), w2_ref[...]) + b2_ref[...]


def _conv_prompt(u3, x3, dw, dwb, ln_g, ln_b, w2, b2, tm=512):
    b, t, d = x3.shape
    tm = min(tm, t)
    xs = pl.BlockSpec((1, tm, d), lambda bi, i: (bi, i, 0))
    vec = pl.BlockSpec((1, d), lambda bi, i: (0, 0))
    return pl.pallas_call(
        _conv_prompt_kernel,
        grid=(b, t // tm),
        in_specs=[xs, xs, pl.BlockSpec((CONV_W, d), lambda bi, i: (0, 0)), vec, vec, vec,
                  pl.BlockSpec((d, d), lambda bi, i: (0, 0)), vec],
        out_specs=xs,
        out_shape=jax.ShapeDtypeStruct((b, t, d), F32),
        scratch_shapes=[pltpu.VMEM((SUBLANES, CONV_HALO + tm, d), F32), pltpu.VMEM((tm, d), F32),
                        pltpu.VMEM((CONV_W, SUBLANES, d), F32)],
        compiler_params=_params("parallel", "arbitrary"),
        name="conv_prompt",
    )(u3, x3, dw, dwb.reshape(1, d), ln_g.reshape(1, d), ln_b.reshape(1, d), w2, b2.reshape(1, d))


STEP_BB = 8


def _conv_step_kernel(st_ref, u_ref, dw_ref, dwb_ref, o_ref):
    w_taps = dw_ref.shape[0]
    for r in range(STEP_BB):
        acc = dwb_ref[...] + dw_ref[w_taps - 1:w_taps, :] * u_ref[r:r + 1, :]
        for w in range(w_taps - 1):
            acc += dw_ref[w:w + 1, :] * st_ref[r, w:w + 1, :]
        o_ref[r:r + 1, :] = acc


def _conv_step(state, l, u, dw, dwb):
    bs, c = u.shape
    w_taps = dw.shape[0]
    return pl.pallas_call(
        _conv_step_kernel,
        grid=(bs // STEP_BB,),
        in_specs=[pl.BlockSpec((None, STEP_BB, w_taps - 1, c), lambda i: (l, i, 0, 0)),
                  pl.BlockSpec((STEP_BB, c), lambda i: (i, 0)),
                  pl.BlockSpec((w_taps, c), lambda i: (0, 0)),
                  pl.BlockSpec((1, c), lambda i: (0, 0))],
        out_specs=pl.BlockSpec((STEP_BB, c), lambda i: (i, 0)),
        out_shape=jax.ShapeDtypeStruct((bs, c), F32),
        compiler_params=_params("parallel"),
        name="conv_step",
    )(state, u, dw, dwb.reshape(1, c))


def kernel(x_prompt, x_sample, cache_sb_k, cache_sb_v, state_ret, state_conv, state_ffn_conv, cache_mem_k, cache_mem_v, page_table, mem_prompt, g_mix, w_in_ab, ret_gn_g, w_out_ab, sb_bias, cv_w1, cv_b1, cv_dw, cv_dwb, cv_ln_g, cv_ln_b, cv_w2, cv_b2, g_cross, xa_wq, xa_wk, xa_wv, xa_wo, g_ffn, ffn_w_up, ffn_dw, ffn_dwb, ffn_w_down, g_final):
    bp, t, d = x_prompt.shape
    bs = x_sample.shape[0]
    n_phys = cache_sb_k.shape[1]

    log_gamma = jnp.log1p(-jnp.exp2(-5.0 - jnp.arange(H_RET, dtype=F32)))
    ret_tabs = _ret_tables(log_gamma)
    cos_p, sin_p = _rope_tables(jnp.arange(t))
    cos_s, sin_s = _rope_tables(PAST_LEN + jnp.arange(1))
    uo = _suffix_matrix()
    cache_k = cache_sb_k.reshape(cache_sb_k.shape[0], n_phys, PAGE_SIZE * H_SB, HD_SB)
    cache_v = cache_sb_v.reshape(cache_sb_v.shape[0], n_phys, PAGE_SIZE * H_SB, HD_SB)
    mem_k = _xattn_cache_view(cache_mem_k)
    mem_v = _xattn_cache_view(cache_mem_v)
    mem2 = mem_prompt.reshape(bp * N_MEM, d)

    xp = x_prompt.reshape(bp * t, d)
    xs = x_sample.reshape(bs, d)
    sbk_p, sbv_p, sbk_s, sbv_s, ret_p, ret_s = [], [], [], [], [], []
    cv_p, cv_s, ff_p, ff_s, mk_p, mv_p = [], [], [], [], [], []

    for l in range(DEPTH):
        if l % 2 == 0:
            e = l // 2
            w_in = w_in_ab[e].astype(BF16)
            w_out = w_out_ab[e].astype(BF16)
            w_out_parts = [w_out[:W_RK], w_out[W_RK:]]
            bias = sb_bias[e].astype(F32)
            bias_row = jnp.repeat(bias, HD_SB).reshape(1, W_SB)
            bias_rows = jnp.zeros((SUBLANES, PAGE_SIZE * H_SB), F32).at[:H_SB].set(
                jnp.broadcast_to(bias[:, None], (H_SB, PAGE_SIZE * H_SB)))
            proj = _mm([xp], [w_in], prologue="rms", pro=(g_mix[l],), tm=1024, name="proj_in")
            proj3 = proj.reshape(bp, t, -1)
            ret_o, s_p = _ret_prompt(proj3, cos_p, sin_p, ret_tabs, ret_gn_g[e])
            sb_o = _sb_prompt(proj3, bias_row, uo)
            xp = _mm([ret_o.reshape(bp * t, W_RK), sb_o.reshape(bp * t, W_SB)], w_out_parts,
                     res=xp, tm=1024, tn=d, name="proj_out")
            sbk_p.append(proj3[:, :, 5 * W_SB:6 * W_SB].reshape(bp, t, H_SB, HD_SB))
            sbv_p.append(proj3[:, :, 6 * W_SB:7 * W_SB].reshape(bp, t, H_SB, HD_SB))
            ret_p.append(s_p)
            proj_s = _mm([xs], [w_in], prologue="rms", pro=(g_mix[l],), name="proj_in_s")
            ret_os, s_s = _ret_sample(proj_s, cos_s, sin_s, ret_gn_g[e], state_ret, e)
            sb_os = _sb_sample(proj_s.reshape(bs, 1, -1), bias_rows, uo, cache_k, cache_v,
                               page_table, e)
            xs = _mm([ret_os, sb_os.reshape(bs, W_SB)], w_out_parts, res=xs, tn=d,
                     name="proj_out_s")
            sbk_s.append(proj_s[:, 5 * W_SB:6 * W_SB].reshape(bs, 1, H_SB, HD_SB))
            sbv_s.append(proj_s[:, 6 * W_SB:7 * W_SB].reshape(bs, 1, H_SB, HD_SB))
            ret_s.append(s_s)
        else:
            o = l // 2
            w1 = cv_w1[o].astype(BF16)
            w2 = cv_w2[o].astype(BF16)
            u = _mm([xp], [w1], prologue="rms", pro=(g_mix[l],), bias=cv_b1[o], glu=True,
                    tm=1024, name="conv_glu")
            u3 = u.reshape(bp, t, d)
            xp = _conv_prompt(u3, xp.reshape(bp, t, d), cv_dw[o], cv_dwb[o], cv_ln_g[o],
                              cv_ln_b[o], w2, cv_b2[o]).reshape(bp * t, d)
            cv_p.append(u3[:, t - (CONV_W - 1):, :])
            u_s = _mm([xs], [w1], prologue="rms", pro=(g_mix[l],), bias=cv_b1[o], glu=True,
                      name="conv_glu_s")
            c_s = _conv_step(state_conv, o, u_s, cv_dw[o], cv_dwb[o])
            xs = _mm([c_s], [w2], prologue="ln_silu", pro=(cv_ln_g[o], cv_ln_b[o]), bias=cv_b2[o],
                     res=xs, tn=d, name="conv_out_s")
            cv_s.append(jnp.concatenate([state_conv[o][:, 1:], u_s[:, None, :]], axis=1))

        wq = xa_wq[l].astype(BF16)
        wo = xa_wo[l].astype(BF16)
        mk = _mm([mem2], [xa_wk[l].astype(BF16)], tn=d, name="mem_k")
        mv = _mm([mem2], [xa_wv[l].astype(BF16)], tn=d, name="mem_v")
        mk_p.append(mk.reshape(bp, N_MEM, H_X, HD_X))
        mv_p.append(mv.reshape(bp, N_MEM, H_X, HD_X))
        xp = _xattn_prompt(xp.reshape(bp, t, d), g_cross[l], wq, mk.reshape(bp, N_MEM, d),
                           mv.reshape(bp, N_MEM, d), wo)
        q_s = _mm([xs], [wq], prologue="rms", pro=(g_cross[l],), tn=d, name="xattn_q_s")
        att_s = _xattn_sample(q_s.reshape(bs, 1, d), mem_k, mem_v, l)
        xs = _mm([att_s], [wo], res=xs, tn=d, name="xattn_out_s")

        w_up = ffn_w_up[l].astype(BF16)
        w_down = ffn_w_down[l].astype(BF16)
        tail_rows = xp[:, t - (FFN_CONV_W - 1):, :].reshape(bp * (FFN_CONV_W - 1), d)
        ff_p.append(_mm([tail_rows], [w_up], prologue="rms", pro=(g_ffn[l],),
                        name="ffn_tail").reshape(bp, FFN_CONV_W - 1, 2 * D_FF))
        xp = _ffn_prompt(xp, g_ffn[l], w_up, ffn_dw[l], ffn_dwb[l], w_down).reshape(bp * t, d)
        u_s = _mm([xs], [w_up], prologue="rms", pro=(g_ffn[l],), name="ffn_up_s")
        c_s = _conv_step(state_ffn_conv, l, u_s, ffn_dw[l], ffn_dwb[l])
        xs = _mm([c_s], [w_down], prologue="swiglu", res=xs, tn=d, name="ffn_down_s")
        ff_s.append(jnp.concatenate([state_ffn_conv[l][:, 1:], u_s[:, None, :]], axis=1))

    y_prompt = _rmsnorm(xp, g_final).reshape(bp, t, d)
    y_sample = _rmsnorm(xs, g_final).reshape(bs, 1, d)
    return (y_prompt, y_sample, jnp.stack(sbk_p), jnp.stack(sbv_p), jnp.stack(sbk_s),
            jnp.stack(sbv_s), jnp.stack(ret_p), jnp.stack(ret_s), jnp.stack(cv_p),
            jnp.stack(cv_s), jnp.stack(ff_p), jnp.stack(ff_s), jnp.stack(mk_p), jnp.stack(mv_p))
```

```python
import functools

import numpy as np
import jax
import jax.numpy as jnp
from jax import lax
from jax.experimental import pallas as pl
from jax.experimental.pallas import tpu as pltpu

F32 = jnp.float32
BF16 = jnp.bfloat16

D_MODEL = 1024
DEPTH = 4
PAST_LEN = 2048
PAGE_SIZE = 128
N_PAGES = PAST_LEN // PAGE_SIZE
H_RET = 4
DK_RET = 128
DV_RET = 128
RET_CHUNK = 128
ROPE_BASE = 10000.0
H_SB = 4
HD_SB = 128
SB_BLOCK = 128
CONV_W = 31
D_FF = 2816
FFN_CONV_W = 3
N_MEM = 256
H_X = 4
HD_X = D_MODEL // H_X
NORM_EPS = 1e-6
W_RK = H_RET * DK_RET
W_SB = H_SB * HD_SB

SUBLANES = 8
LANES = 128
VMEM_LIMIT_BYTES = 56 * 1024 * 1024


def _params(*sem):
    return pltpu.CompilerParams(dimension_semantics=sem, vmem_limit_bytes=VMEM_LIMIT_BYTES)


def _dot(a, b):
    return jnp.dot(a, b, preferred_element_type=F32)


def _dot_nt(a, b):
    return lax.dot_general(a, b, (((1,), (1,)), ((), ())), preferred_element_type=F32)


def _rms(x, g):
    return x * lax.rsqrt(jnp.mean(x * x, axis=-1, keepdims=True) + NORM_EPS) * g


def _silu(x):
    return x * jax.nn.sigmoid(x)


def _softplus(z):
    return jnp.maximum(z, 0.0) + jnp.log(1.0 + jnp.exp(-jnp.abs(z)))


def _split_hi_lo(x):
    hi = x.astype(BF16)
    lo = (x - hi.astype(F32)).astype(BF16)
    return hi, lo


def _mm_kernel(*refs, n_lhs, prologue, has_bias, has_res, glu):
    it = iter(refs)
    x_refs = [next(it) for _ in range(n_lhs)]
    w_refs = [next(it) for _ in range(n_lhs)]
    wg_refs = [next(it) for _ in range(n_lhs)] if glu else []
    n_pro = {None: 0, "rms": 1, "ln_silu": 2, "swiglu": 0}[prologue]
    p_refs = [next(it) for _ in range(n_pro)]
    b_ref = next(it) if has_bias else None
    bg_ref = next(it) if (has_bias and glu) else None
    r_ref = next(it) if has_res else None
    o_ref = next(it)
    xn_ref = next(it) if prologue else None

    if prologue:
        @pl.when(pl.program_id(1) == 0)
        def _():
            x = x_refs[0][...].astype(F32)
            if prologue == "rms":
                y = _rms(x, p_refs[0][...])
            elif prologue == "ln_silu":
                mu = jnp.mean(x, axis=-1, keepdims=True)
                xc = x - mu
                var = jnp.mean(xc * xc, axis=-1, keepdims=True)
                y = _silu(xc * lax.rsqrt(var + NORM_EPS) * p_refs[0][...] + p_refs[1][...])
            else:
                k = x.shape[-1] // 2
                y = _silu(x[:, :k]) * x[:, k:]
            xn_ref[...] = y.astype(BF16)
        lhs = [xn_ref[...]]
    else:
        lhs = [r[...].astype(BF16) for r in x_refs]

    acc = _dot(lhs[0], w_refs[0][...])
    for a, w in zip(lhs[1:], w_refs[1:]):
        acc += _dot(a, w[...])
    if has_bias:
        acc += b_ref[...]
    if glu:
        gate = _dot(lhs[0], wg_refs[0][...])
        for a, w in zip(lhs[1:], wg_refs[1:]):
            gate += _dot(a, w[...])
        if has_bias:
            gate += bg_ref[...]
        acc = acc * jax.nn.sigmoid(gate)
    if has_res:
        acc += r_ref[...]
    o_ref[...] = acc.astype(o_ref.dtype)


def _mm(xs, ws, *, prologue=None, pro=(), bias=None, res=None, glu=False, tm=512, tn=512,
        out_dtype=F32, name="mm"):
    m = xs[0].shape[0]
    n = ws[0].shape[1] // (2 if glu else 1)
    tm = min(tm, m)
    tn = min(tn, n)
    assert m % tm == 0 and n % tn == 0, (m, tm, n, tn)
    nj = n // tn
    args, specs = [], []
    for x in xs:
        args.append(x)
        specs.append(pl.BlockSpec((tm, x.shape[1]), lambda i, j: (i, 0)))
    for w in ws:
        args.append(w)
        specs.append(pl.BlockSpec((w.shape[0], tn), lambda i, j: (0, j)))
    if glu:
        for w in ws:
            args.append(w)
            specs.append(pl.BlockSpec((w.shape[0], tn), lambda i, j: (0, j + nj)))
    for p in pro:
        args.append(p.reshape(1, -1))
        specs.append(pl.BlockSpec((1, p.size), lambda i, j: (0, 0)))
    if bias is not None:
        b2 = bias.reshape(1, -1)
        args.append(b2)
        specs.append(pl.BlockSpec((1, tn), lambda i, j: (0, j)))
        if glu:
            args.append(b2)
            specs.append(pl.BlockSpec((1, tn), lambda i, j: (0, j + nj)))
    if res is not None:
        args.append(res)
        specs.append(pl.BlockSpec((tm, tn), lambda i, j: (i, j)))
    scratch = []
    if prologue:
        k_eff = xs[0].shape[1] // (2 if prologue == "swiglu" else 1)
        scratch.append(pltpu.VMEM((tm, k_eff), BF16))
    kern = functools.partial(_mm_kernel, n_lhs=len(xs), prologue=prologue,
                             has_bias=bias is not None, has_res=res is not None, glu=glu)
    return pl.pallas_call(
        kern,
        grid=(m // tm, nj),
        in_specs=specs,
        out_specs=pl.BlockSpec((tm, tn), lambda i, j: (i, j)),
        out_shape=jax.ShapeDtypeStruct((m, n), out_dtype),
        scratch_shapes=scratch,
        compiler_params=_params("parallel", "arbitrary"),
        name=name,
    )(*args)


def _rmsnorm_kernel(x_ref, g_ref, o_ref):
    o_ref[...] = _rms(x_ref[...], g_ref[...])


def _rmsnorm(x, g, tm=1024):
    m, d = x.shape
    tm = min(tm, m)
    return pl.pallas_call(
        _rmsnorm_kernel,
        grid=(m // tm,),
        in_specs=[pl.BlockSpec((tm, d), lambda i: (i, 0)), pl.BlockSpec((1, d), lambda i: (0, 0))],
        out_specs=pl.BlockSpec((tm, d), lambda i: (i, 0)),
        out_shape=jax.ShapeDtypeStruct((m, d), F32),
        compiler_params=_params("parallel"),
        name="rmsnorm",
    )(x, g.reshape(1, d))


def _rotate(x, cos2, sin2):
    return x * cos2 + pltpu.roll(x, DK_RET // 2, axis=1) * sin2


def _head_ln_gate(o, gate, gn):
    mu = jnp.mean(o, axis=-1, keepdims=True)
    oc = o - mu
    var = jnp.mean(oc * oc, axis=-1, keepdims=True)
    return _silu(gate) * (oc * lax.rsqrt(var + NORM_EPS) * gn)


def _ret_prompt_kernel(rq_ref, rk_ref, rv_ref, rg_ref, cos_ref, sin_ref, dmask_ref, qdec_ref,
                       kdec_ref, cdec_ref, gn_ref, o_ref, s_ref):
    c = pl.program_id(1)

    @pl.when(c == 0)
    def _():
        s_ref[...] = jnp.zeros_like(s_ref)

    cos2 = cos_ref[...]
    sin2 = sin_ref[...]
    for h in range(H_RET):
        sl = slice(h * DK_RET, (h + 1) * DK_RET)
        q = _rotate(rq_ref[0, :, sl], cos2, sin2)
        k = _rotate(rk_ref[0, :, sl], cos2, sin2) * (DK_RET ** -0.5)
        vb = rv_ref[0, :, sl].astype(BF16)
        qb = q.astype(BF16)
        s0 = s_ref[0, h]
        scores = _dot_nt(qb, k.astype(BF16)) * dmask_ref[h]
        o = _dot(scores.astype(BF16), vb) + _dot(qb, s0.astype(BF16)) * qdec_ref[h]
        kd_t = (k * kdec_ref[h]).T.astype(BF16)
        s_ref[0, h] = cdec_ref[h] * s0 + _dot(kd_t, vb)
        o_ref[0, :, sl] = _head_ln_gate(o, rg_ref[0, :, sl], gn_ref[:, sl]).astype(o_ref.dtype)


def _ret_tables(log_gamma):
    idx = jnp.arange(RET_CHUNK)
    diff = idx[:, None] - idx[None, :]
    expo = jnp.maximum(diff, 0).astype(F32)[None] * log_gamma[:, None, None]
    dmask = jnp.where(diff[None] >= 0, jnp.exp(expo), 0.0)
    q_decay = jnp.exp((idx + 1).astype(F32)[:, None] * log_gamma[None, :])
    k_decay = jnp.exp((RET_CHUNK - 1 - idx).astype(F32)[:, None] * log_gamma[None, :])
    ones = jnp.ones((1, 1, DK_RET), F32)
    qdec = q_decay.T[:, :, None] * ones
    kdec = k_decay.T[:, :, None] * ones
    cdec = jnp.exp(RET_CHUNK * log_gamma)[:, None, None] * jnp.ones((1, DK_RET, DV_RET), F32)
    return dmask, qdec, kdec, cdec


def _rope_tables(pos):
    half = DK_RET // 2
    inv = ROPE_BASE ** (-jnp.arange(half, dtype=F32) / half)
    ang = pos.astype(F32)[:, None] * inv[None, :]
    cos, sin = jnp.cos(ang), jnp.sin(ang)
    return jnp.concatenate([cos, cos], -1), jnp.concatenate([-sin, sin], -1)


def _ret_prompt(proj3, cos2, sin2, tabs, gn):
    b, t, _ = proj3.shape
    nc = t // RET_CHUNK
    dmask, qdec, kdec, cdec = tabs

    def col(cb):
        return pl.BlockSpec((1, RET_CHUNK, W_RK), lambda i, c: (i, c, cb))

    tab = pl.BlockSpec((H_RET, RET_CHUNK, RET_CHUNK), lambda i, c: (0, 0, 0))
    return pl.pallas_call(
        _ret_prompt_kernel,
        grid=(b, nc),
        in_specs=[col(0), col(1), col(2), col(3),
                  pl.BlockSpec((RET_CHUNK, DK_RET), lambda i, c: (c, 0)),
                  pl.BlockSpec((RET_CHUNK, DK_RET), lambda i, c: (c, 0)),
                  tab, tab, tab, tab,
                  pl.BlockSpec((1, W_RK), lambda i, c: (0, 0))],
        out_specs=[pl.BlockSpec((1, RET_CHUNK, W_RK), lambda i, c: (i, c, 0)),
                   pl.BlockSpec((1, H_RET, DK_RET, DV_RET), lambda i, c: (i, 0, 0, 0))],
        out_shape=[jax.ShapeDtypeStruct((b, t, W_RK), BF16),
                   jax.ShapeDtypeStruct((b, H_RET, DK_RET, DV_RET), F32)],
        compiler_params=_params("parallel", "arbitrary"),
        name="ret_prompt",
    )(proj3, proj3, proj3, proj3, cos2, sin2, dmask, qdec, kdec, cdec, gn.reshape(1, W_RK))


RET_BB = 8
MXU_ROWS = 16


def _ret_sample_kernel(rq_ref, rk_ref, rv_ref, rg_ref, cos_ref, sin_ref, gn_ref, st_ref,
                       o_ref, so_ref, inter_ref, *, g1):
    cos2 = cos_ref[...]
    sin2 = sin_ref[...]
    eye = (lax.broadcasted_iota(jnp.int32, (DK_RET, DK_RET), 0)
           == lax.broadcasted_iota(jnp.int32, (DK_RET, DK_RET), 1))
    for h in range(H_RET):
        sl = slice(h * DK_RET, (h + 1) * DK_RET)
        q = _rotate(rq_ref[:, sl], cos2, sin2)
        k = _rotate(rk_ref[:, sl], cos2, sin2) * (DK_RET ** -0.5)
        v = rv_ref[:, sl]
        for r in range(RET_BB):
            s0 = st_ref[r, h]
            qr = jnp.broadcast_to(q[r:r + 1], (MXU_ROWS, DK_RET)).astype(BF16)
            inter_ref[r:r + 1, :] = _dot(qr, s0.astype(BF16))[0:1] * g1[h]
            diag_k = jnp.where(eye, jnp.broadcast_to(k[r:r + 1], (DK_RET, DK_RET)), 0.0).astype(BF16)
            v_rows = jnp.broadcast_to(v[r:r + 1], (DK_RET, DV_RET)).astype(BF16)
            so_ref[r, h] = g1[h] * s0 + _dot(diag_k, v_rows)
        o = jnp.sum(q * k, axis=-1, keepdims=True) * v + inter_ref[...]
        o_ref[:, sl] = _head_ln_gate(o, rg_ref[:, sl], gn_ref[:, sl]).astype(o_ref.dtype)


def _ret_sample(proj, cos2, sin2, gn, state, e):
    bs = proj.shape[0]
    log_gamma = np.log1p(-np.exp2(-5.0 - np.arange(H_RET, dtype=np.float32)))
    g1 = tuple(float(x) for x in np.exp(log_gamma).astype(np.float32))

    def col(cb):
        return pl.BlockSpec((RET_BB, W_RK), lambda i: (i, cb))

    row = pl.BlockSpec((1, DK_RET), lambda i: (0, 0))
    return pl.pallas_call(
        functools.partial(_ret_sample_kernel, g1=g1),
        grid=(bs // RET_BB,),
        in_specs=[col(0), col(1), col(2), col(3), row, row,
                  pl.BlockSpec((1, W_RK), lambda i: (0, 0)),
                  pl.BlockSpec((None, RET_BB, H_RET, DK_RET, DV_RET), lambda i: (e, i, 0, 0, 0))],
        out_specs=[pl.BlockSpec((RET_BB, W_RK), lambda i: (i, 0)),
                   pl.BlockSpec((RET_BB, H_RET, DK_RET, DV_RET), lambda i: (i, 0, 0, 0))],
        out_shape=[jax.ShapeDtypeStruct((bs, W_RK), BF16),
                   jax.ShapeDtypeStruct((bs, H_RET, DK_RET, DV_RET), F32)],
        scratch_shapes=[pltpu.VMEM((RET_BB, DV_RET), F32)],
        compiler_params=_params("parallel"),
        name="ret_sample",
    )(proj, proj, proj, proj, cos2, sin2, gn.reshape(1, W_RK), state)


def _suffix_matrix():
    j = jnp.arange(SB_BLOCK)
    u = (j[:, None] > j[None, :]).astype(BF16)
    half = jnp.concatenate([u, jnp.ones((SB_BLOCK, SB_BLOCK), BF16)], axis=1)
    return jnp.concatenate([half, half], axis=0)


def _suffix_sums(log_1m, uo):
    hi, lo = _split_hi_lo(log_1m)
    return _dot(jnp.concatenate([hi, lo], axis=1), uo)


SB_TQ = 256


def _sb_prompt_kernel(q_ref, k_ref, v_ref, bias_ref, uo_ref, o_ref, acc_ref, carry_ref):
    i = pl.program_id(1)
    uo = uo_ref[...]
    heads = [slice(h * HD_SB, (h + 1) * HD_SB) for h in range(H_SB)]

    def blocks(js, masked, first):
        starts = [pl.multiple_of(j * SB_BLOCK, SB_BLOCK) for j in js]
        zs = [jnp.concatenate(
            [_dot_nt(q_ref[0, :, sl].astype(BF16), k_ref[0, pl.ds(st, SB_BLOCK), sl].astype(BF16))
             * (HD_SB ** -0.5) + bias_ref[:, sl] for sl in heads], axis=0) for st in starts]
        sps = [_softplus(z) for z in zs]
        valids = []
        if masked:
            for z, st in zip(zs, starts):
                qpos = i * SB_TQ + (lax.broadcasted_iota(jnp.int32, z.shape, 0) & (SB_TQ - 1))
                kpos = st + lax.broadcasted_iota(jnp.int32, z.shape, 1)
                valids.append(kpos < qpos)
            rs = [_suffix_sums(jnp.where(v, -sp, 0.0), uo) for v, sp in zip(valids, sps)]
        else:
            rs = [_suffix_sums(-sp, uo) for sp in sps]
        carry = None if first else carry_ref[...]
        pvs = []
        for n, st in enumerate(starts):
            after = rs[n][:, :SB_BLOCK]
            if carry is not None:
                after = after + carry
            a = jnp.exp(zs[n] - sps[n] + after)
            if masked:
                a = jnp.where(valids[n], a, 0.0)
            ab = a.astype(BF16)
            pvs.append(jnp.concatenate(
                [_dot(ab[h * SB_TQ:(h + 1) * SB_TQ], v_ref[0, pl.ds(st, SB_BLOCK), sl].astype(BF16))
                 for h, sl in enumerate(heads)], axis=0))
            total = rs[n][:, SB_BLOCK:]
            carry = total if carry is None else carry + total
        pv = pvs[0]
        for p in pvs[1:]:
            pv = pv + p
        acc_ref[...] = pv if first else acc_ref[...] + pv
        carry_ref[...] = carry

    n_diag = SB_TQ // SB_BLOCK
    blocks([i * n_diag + d for d in reversed(range(n_diag))], True, True)

    def body(t, _):
        j = (i - t) * n_diag - 1
        blocks([j - d for d in range(n_diag)], False, False)
        return 0

    lax.fori_loop(0, i, body, 0)
    for h, sl in enumerate(heads):
        o_ref[0, :, sl] = acc_ref[h * SB_TQ:(h + 1) * SB_TQ, :].astype(o_ref.dtype)


def _sb_prompt(proj3, bias_row, uo):
    b, t, _ = proj3.shape
    tq = min(SB_TQ, t)
    assert tq == SB_TQ and t % tq == 0
    return pl.pallas_call(
        _sb_prompt_kernel,
        grid=(b, t // tq),
        in_specs=[pl.BlockSpec((1, tq, W_SB), lambda bi, i: (bi, i, 4)),
                  pl.BlockSpec((1, t, W_SB), lambda bi, i: (bi, 0, 5)),
                  pl.BlockSpec((1, t, W_SB), lambda bi, i: (bi, 0, 6)),
                  pl.BlockSpec((1, W_SB), lambda bi, i: (0, 0)),
                  pl.BlockSpec((2 * SB_BLOCK, 2 * SB_BLOCK), lambda bi, i: (0, 0))],
        out_specs=pl.BlockSpec((1, tq, W_SB), lambda bi, i: (bi, i, 0)),
        out_shape=jax.ShapeDtypeStruct((b, t, W_SB), BF16),
        scratch_shapes=[pltpu.VMEM((H_SB * tq, HD_SB), F32),
                        pltpu.VMEM((H_SB * tq, SB_BLOCK), F32)],
        compiler_params=_params("parallel", "arbitrary"),
        name="sb_prompt",
    )(proj3, proj3, proj3, bias_row, uo)


def _row_select(parts):
    r = lax.broadcasted_iota(jnp.int32, parts[0].shape, 0)
    out = jnp.zeros_like(parts[0])
    for h, p in enumerate(parts):
        out = jnp.where(r == h, p, out)
    return out


def _bcast_rows_b16(row):
    return jnp.broadcast_to(row, (MXU_ROWS, row.shape[-1])).astype(BF16)


def _sb_sample_kernel(pt_ref, q_ref, bias_ref, uo_ref, *refs):
    del pt_ref
    k_refs = refs[:N_PAGES]
    v_refs = refs[N_PAGES:2 * N_PAGES]
    o_ref = refs[2 * N_PAGES]
    uo = uo_ref[...]
    n_col = PAGE_SIZE * H_SB
    n_grp = n_col // SB_BLOCK
    q4 = _row_select([jnp.broadcast_to(q_ref[0, :, h * HD_SB:(h + 1) * HD_SB], (SUBLANES, HD_SB))
                      for h in range(H_SB)])
    qb = jnp.concatenate([q4, q4], axis=0).astype(BF16)
    row = lax.broadcasted_iota(jnp.int32, (SUBLANES, n_col), 0)
    col = lax.broadcasted_iota(jnp.int32, (SUBLANES, n_col), 1)
    valid = (col & (H_SB - 1)) == row
    bias = bias_ref[...]
    zs, sps, his, los = [], [], [], []
    for p in range(N_PAGES):
        z = _dot_nt(qb, k_refs[p][...].astype(BF16))[:SUBLANES] * (HD_SB ** -0.5) + bias
        sp = _softplus(z)
        hi, lo = _split_hi_lo(jnp.where(valid, -sp, 0.0))
        zs.append(z)
        sps.append(sp)
        his += [hi[:, g * SB_BLOCK:(g + 1) * SB_BLOCK] for g in range(n_grp)]
        los += [lo[:, g * SB_BLOCK:(g + 1) * SB_BLOCK] for g in range(n_grp)]
    n_all = N_PAGES * n_grp
    r = _dot(jnp.concatenate([jnp.concatenate(his, axis=0), jnp.concatenate(los, axis=0)],
                             axis=1), uo)
    carry = jnp.zeros((SUBLANES, SB_BLOCK), F32)
    after = [None] * n_all
    for g in reversed(range(n_all)):
        rows = slice(g * SUBLANES, (g + 1) * SUBLANES)
        after[g] = r[rows, :SB_BLOCK] + carry
        carry = carry + r[rows, SB_BLOCK:]
    acc = jnp.zeros((MXU_ROWS, HD_SB), F32)
    for p in range(N_PAGES):
        aft = jnp.concatenate(after[p * n_grp:(p + 1) * n_grp], axis=1)
        a = jnp.where(valid, jnp.exp(zs[p] - sps[p] + aft), 0.0)
        ab = jnp.concatenate([a, a], axis=0).astype(BF16)
        acc += _dot(ab, v_refs[p][...].astype(BF16))
    o_ref[0] = acc[:H_SB].astype(o_ref.dtype)


def _sb_sample(proj3, bias_rows, uo, cache_k, cache_v, page_table, e):
    bs = proj3.shape[0]

    def page(p):
        return pl.BlockSpec((None, None, PAGE_SIZE * H_SB, HD_SB),
                            lambda b, pt: (e, pt[b, p], 0, 0))

    grid_spec = pltpu.PrefetchScalarGridSpec(
        num_scalar_prefetch=1,
        grid=(bs,),
        in_specs=[pl.BlockSpec((1, 1, W_SB), lambda b, pt: (b, 0, 4)),
                  pl.BlockSpec((SUBLANES, PAGE_SIZE * H_SB), lambda b, pt: (0, 0)),
                  pl.BlockSpec((2 * SB_BLOCK, 2 * SB_BLOCK), lambda b, pt: (0, 0))]
                 + [page(p) for p in range(N_PAGES)] + [page(p) for p in range(N_PAGES)],
        out_specs=pl.BlockSpec((1, H_SB, HD_SB), lambda b, pt: (b, 0, 0)),
    )
    return pl.pallas_call(
        _sb_sample_kernel,
        grid_spec=grid_spec,
        out_shape=jax.ShapeDtypeStruct((bs, H_SB, HD_SB), F32),
        compiler_params=_params("arbitrary"),
        name="sb_sample",
    )(page_table, proj3, bias_rows, uo, *([cache_k] * N_PAGES), *([cache_v] * N_PAGES))


def _softmax_rows(s):
    m = jnp.max(s, axis=-1, keepdims=True)
    p = jnp.exp(s - m)
    return p / jnp.sum(p, axis=-1, keepdims=True)


def _xattn_prompt_kernel(x_ref, g_ref, wq_ref, mk_ref, mv_ref, wo_ref, o_ref, att_ref):
    x = x_ref[0]
    hn = _rms(x, g_ref[...]).astype(BF16)
    q = _dot(hn, wq_ref[...]).astype(BF16)
    for h in range(H_X):
        sl = slice(h * HD_X, (h + 1) * HD_X)
        s = _dot_nt(q[:, sl], mk_ref[0, :, sl].astype(BF16)) * (HD_X ** -0.5)
        p = _softmax_rows(s)
        att_ref[:, sl] = _dot(p.astype(BF16), mv_ref[0, :, sl].astype(BF16)).astype(BF16)
    o_ref[0] = x + _dot(att_ref[...], wo_ref[...])


def _xattn_prompt(x3, g, wq, mk3, mv3, wo, tm=512):
    b, t, d = x3.shape
    tm = min(tm, t)
    xs = pl.BlockSpec((1, tm, d), lambda bi, i: (bi, i, 0))
    ws = pl.BlockSpec((d, d), lambda bi, i: (0, 0))
    ms = pl.BlockSpec((1, N_MEM, d), lambda bi, i: (bi, 0, 0))
    return pl.pallas_call(
        _xattn_prompt_kernel,
        grid=(b, t // tm),
        in_specs=[xs, pl.BlockSpec((1, d), lambda bi, i: (0, 0)), ws, ms, ms, ws],
        out_specs=xs,
        out_shape=jax.ShapeDtypeStruct((b, t, d), F32),
        scratch_shapes=[pltpu.VMEM((tm, d), BF16)],
        compiler_params=_params("parallel", "arbitrary"),
        name="xattn_prompt",
    )(x3, g.reshape(1, d), wq, mk3, mv3, wo)


XS_HALVES = HD_X // LANES
XS_ROWS = XS_HALVES * H_X
XS_BB = 4


def _xattn_sample_kernel(q_ref, mk_ref, mv_ref, o_ref):
    n_col = N_MEM * XS_ROWS
    row = lax.broadcasted_iota(jnp.int32, (SUBLANES, n_col), 0)
    col = lax.broadcasted_iota(jnp.int32, (SUBLANES, n_col), 1)
    valid = (col & (XS_ROWS - 1)) == row
    for i in range(q_ref.shape[0]):
        q8 = _row_select([jnp.broadcast_to(q_ref[i, :, (a % H_X) * HD_X + (a // H_X) * LANES:
                                                 (a % H_X) * HD_X + (a // H_X + 1) * LANES],
                                           (SUBLANES, LANES)) for a in range(XS_ROWS)])
        qb = jnp.concatenate([q8, q8], axis=0).astype(BF16)
        z = jnp.where(valid, _dot_nt(qb, mk_ref[i].astype(BF16))[:SUBLANES], 0.0)
        zr = pltpu.roll(z, H_X, axis=0)
        other = jnp.where(row < H_X, pltpu.roll(zr, n_col - H_X, axis=1),
                          pltpu.roll(zr, H_X, axis=1))
        s = (z + other) * (HD_X ** -0.5)
        m = jnp.max(jnp.where(valid, s, -jnp.inf), axis=-1, keepdims=True)
        p = jnp.where(valid, jnp.exp(s - m), 0.0)
        p = p / jnp.sum(p, axis=-1, keepdims=True)
        pb = jnp.concatenate([p, p], axis=0).astype(BF16)
        o_ref[i] = _dot(pb, mv_ref[i].astype(BF16))[:SUBLANES].astype(o_ref.dtype)


def _xattn_cache_view(cache):
    dp, b = cache.shape[:2]
    c = cache.reshape(dp, b, N_MEM, H_X, XS_HALVES, LANES)
    return jnp.swapaxes(c, 3, 4).reshape(dp, b, N_MEM * XS_ROWS, LANES)


def _xattn_sample(q3, mem_k, mem_v, l):
    bs, _, d = q3.shape
    bb = XS_BB
    ms = pl.BlockSpec((None, bb, N_MEM * XS_ROWS, LANES), lambda b: (l, b, 0, 0))
    o = pl.pallas_call(
        _xattn_sample_kernel,
        grid=(bs // bb,),
        in_specs=[pl.BlockSpec((bb, 1, d), lambda b: (b, 0, 0)), ms, ms],
        out_specs=pl.BlockSpec((bb, XS_ROWS, LANES), lambda b: (b, 0, 0)),
        out_shape=jax.ShapeDtypeStruct((bs, XS_ROWS, LANES), F32),
        compiler_params=_params("parallel"),
        name="xattn_sample",
    )(q3, mem_k, mem_v)
    return jnp.swapaxes(o.reshape(bs, XS_HALVES, H_X, LANES), 1, 2).reshape(bs, d)


FFN_TF = 256
HALO = SUBLANES


FFN_ROW_CHUNKS = 4


def _ffn_prompt_kernel(x_ref, g_ref, wg_ref, wv_ref, dwg_ref, dwv_ref, bg_ref, bv_ref, wd_ref,
                       o_ref, hn_ref, tail_ref, *u_refs):
    t = pl.program_id(1)
    f = pl.program_id(2)
    tm = x_ref.shape[1]
    tf = wg_ref.shape[1]
    rc = tm // len(u_refs)

    @pl.when(f == 0)
    def _():
        hn_ref[...] = _rms(x_ref[0], g_ref[...]).astype(BF16)
        o_ref[0] = x_ref[0]

    def conv(u_ref, dw_ref, b_ref, sl):
        c = b_ref[...] + dw_ref[FFN_CONV_W - 1:FFN_CONV_W, :] * u_ref[HALO:, sl]
        for w in range(FFN_CONV_W - 1):
            off = HALO - (FFN_CONV_W - 1) + w
            c += dw_ref[w:w + 1, :] * u_ref[off:off + rc, sl]
        return c

    def up(k):
        hn = hn_ref[k * rc:(k + 1) * rc, :]
        u_refs[k][HALO:, :tf] = _dot(hn, wg_ref[...])
        u_refs[k][HALO:, tf:] = _dot(hn, wv_ref[...])

    up(0)
    for k, u_ref in enumerate(u_refs):
        if k + 1 < len(u_refs):
            up(k + 1)
        if k == 0:
            u_ref[:HALO, :] = jnp.where(t == 0, 0.0, tail_ref[f])
        else:
            u_ref[:HALO, :] = u_refs[k - 1][rc:, :]
        gated = (_silu(conv(u_ref, dwg_ref, bg_ref, slice(0, tf)))
                 * conv(u_ref, dwv_ref, bv_ref, slice(tf, 2 * tf)))
        o_ref[0, k * rc:(k + 1) * rc, :] += _dot(gated.astype(BF16), wd_ref[...])
    tail_ref[f] = u_refs[-1][rc:, :]


def _ffn_prompt(x3, g, w_up, dw, dwb, w_down, tm=1024):
    b, t, d = x3.shape
    tm = min(tm, t)
    nf = D_FF // FFN_TF
    assert D_FF % FFN_TF == 0 and t % tm == 0
    xs = pl.BlockSpec((1, tm, d), lambda bi, i, f: (bi, i, 0))
    dwb2 = dwb.reshape(1, -1)
    return pl.pallas_call(
        _ffn_prompt_kernel,
        grid=(b, t // tm, nf),
        in_specs=[xs, pl.BlockSpec((1, d), lambda bi, i, f: (0, 0)),
                  pl.BlockSpec((d, FFN_TF), lambda bi, i, f: (0, f)),
                  pl.BlockSpec((d, FFN_TF), lambda bi, i, f: (0, f + nf)),
                  pl.BlockSpec((FFN_CONV_W, FFN_TF), lambda bi, i, f: (0, f)),
                  pl.BlockSpec((FFN_CONV_W, FFN_TF), lambda bi, i, f: (0, f + nf)),
                  pl.BlockSpec((1, FFN_TF), lambda bi, i, f: (0, f)),
                  pl.BlockSpec((1, FFN_TF), lambda bi, i, f: (0, f + nf)),
                  pl.BlockSpec((FFN_TF, d), lambda bi, i, f: (f, 0))],
        out_specs=xs,
        out_shape=jax.ShapeDtypeStruct((b, t, d), F32),
        scratch_shapes=[pltpu.VMEM((tm, d), BF16), pltpu.VMEM((nf, HALO, 2 * FFN_TF), F32)]
                       + [pltpu.VMEM((HALO + tm // FFN_ROW_CHUNKS, 2 * FFN_TF), F32)] * FFN_ROW_CHUNKS,
        compiler_params=_params("parallel", "arbitrary", "arbitrary"),
        name="ffn_prompt",
    )(x3, g.reshape(1, d), w_up, w_up, dw, dw, dwb2, dwb2, w_down)


CONV_HALO = 32
CONV_RB = 32


def _conv_prompt_kernel(u_ref, x_ref, dw_ref, dwb_ref, lg_ref, lb_ref, w2_ref, b2_ref,
                        o_ref, s_ref, c_ref, dwt_ref):
    t = pl.program_id(1)
    tm = u_ref.shape[1]
    n = CONV_HALO + tm

    @pl.when(t == 0)
    def _():
        s_ref[0, :CONV_HALO, :] = jnp.zeros((CONV_HALO, D_MODEL), F32)

    @pl.when(t > 0)
    def _():
        s_ref[0, :CONV_HALO, :] = s_ref[0, tm:, :]

    s_ref[0, CONV_HALO:, :] = u_ref[0]
    for r in range(1, SUBLANES):
        s_ref[r] = pltpu.roll(s_ref[0], n - r, axis=0)
    base = CONV_HALO - (CONV_W - 1)

    @pl.when(t == 0)
    def _():
        for w in range(CONV_W):
            dwt_ref[w] = jnp.broadcast_to(dw_ref[w:w + 1, :], (SUBLANES, D_MODEL))

    n_sub = CONV_RB // SUBLANES

    def row_block(rb, carry):
        r0 = rb * CONV_RB
        bias = jnp.broadcast_to(dwb_ref[...], (SUBLANES, D_MODEL))
        accs = [bias] * n_sub
        for w in range(CONV_W):
            r = (base + w) % SUBLANES
            dwt = dwt_ref[w]
            for k in range(n_sub):
                start = pl.multiple_of(r0 + (base + w - r) + k * SUBLANES, SUBLANES)
                accs[k] = accs[k] + dwt * s_ref[r, pl.ds(start, SUBLANES), :]
        for k in range(n_sub):
            c_ref[pl.ds(pl.multiple_of(r0 + k * SUBLANES, SUBLANES), SUBLANES), :] = accs[k]
        return carry

    lax.fori_loop(0, tm // CONV_RB, row_block, 0)
    c = c_ref[...]
    mu = jnp.mean(c, axis=-1, keepdims=True)
    cc = c - mu
    var = jnp.mean(cc * cc, axis=-1, keepdims=True)
    y = _silu(cc * lax.rsqrt(var + NORM_EPS) * lg_ref[...] + lb_ref[...])
    o_ref[0] = x_ref[0] + _dot(y.astype(BF16), w2_ref[...]) + b2_ref[...]


def _conv_prompt(u3, x3, dw, dwb, ln_g, ln_b, w2, b2, tm=512):
    b, t, d = x3.shape
    tm = min(tm, t)
    xs = pl.BlockSpec((1, tm, d), lambda bi, i: (bi, i, 0))
    vec = pl.BlockSpec((1, d), lambda bi, i: (0, 0))
    return pl.pallas_call(
        _conv_prompt_kernel,
        grid=(b, t // tm),
        in_specs=[xs, xs, pl.BlockSpec((CONV_W, d), lambda bi, i: (0, 0)), vec, vec, vec,
                  pl.BlockSpec((d, d), lambda bi, i: (0, 0)), vec],
        out_specs=xs,
        out_shape=jax.ShapeDtypeStruct((b, t, d), F32),
        scratch_shapes=[pltpu.VMEM((SUBLANES, CONV_HALO + tm, d), F32), pltpu.VMEM((tm, d), F32),
                        pltpu.VMEM((CONV_W, SUBLANES, d), F32)],
        compiler_params=_params("parallel", "arbitrary"),
        name="conv_prompt",
    )(u3, x3, dw, dwb.reshape(1, d), ln_g.reshape(1, d), ln_b.reshape(1, d), w2, b2.reshape(1, d))


STEP_BB = 8


def _conv_step_kernel(st_ref, u_ref, dw_ref, dwb_ref, o_ref):
    w_taps = dw_ref.shape[0]
    for r in range(STEP_BB):
        acc = dwb_ref[...] + dw_ref[w_taps - 1:w_taps, :] * u_ref[r:r + 1, :]
        for w in range(w_taps - 1):
            acc += dw_ref[w:w + 1, :] * st_ref[r, w:w + 1, :]
        o_ref[r:r + 1, :] = acc


def _conv_step(state, l, u, dw, dwb):
    bs, c = u.shape
    w_taps = dw.shape[0]
    return pl.pallas_call(
        _conv_step_kernel,
        grid=(bs // STEP_BB,),
        in_specs=[pl.BlockSpec((None, STEP_BB, w_taps - 1, c), lambda i: (l, i, 0, 0)),
                  pl.BlockSpec((STEP_BB, c), lambda i: (i, 0)),
                  pl.BlockSpec((w_taps, c), lambda i: (0, 0)),
                  pl.BlockSpec((1, c), lambda i: (0, 0))],
        out_specs=pl.BlockSpec((STEP_BB, c), lambda i: (i, 0)),
        out_shape=jax.ShapeDtypeStruct((bs, c), F32),
        compiler_params=_params("parallel"),
        name="conv_step",
    )(state, u, dw, dwb.reshape(1, c))


def kernel(x_prompt, x_sample, cache_sb_k, cache_sb_v, state_ret, state_conv, state_ffn_conv, cache_mem_k, cache_mem_v, page_table, mem_prompt, g_mix, w_in_ab, ret_gn_g, w_out_ab, sb_bias, cv_w1, cv_b1, cv_dw, cv_dwb, cv_ln_g, cv_ln_b, cv_w2, cv_b2, g_cross, xa_wq, xa_wk, xa_wv, xa_wo, g_ffn, ffn_w_up, ffn_dw, ffn_dwb, ffn_w_down, g_final):
    bp, t, d = x_prompt.shape
    bs = x_sample.shape[0]
    n_phys = cache_sb_k.shape[1]

    log_gamma = jnp.log1p(-jnp.exp2(-5.0 - jnp.arange(H_RET, dtype=F32)))
    ret_tabs = _ret_tables(log_gamma)
    cos_p, sin_p = _rope_tables(jnp.arange(t))
    cos_s, sin_s = _rope_tables(PAST_LEN + jnp.arange(1))
    uo = _suffix_matrix()
    cache_k = cache_sb_k.reshape(cache_sb_k.shape[0], n_phys, PAGE_SIZE * H_SB, HD_SB)
    cache_v = cache_sb_v.reshape(cache_sb_v.shape[0], n_phys, PAGE_SIZE * H_SB, HD_SB)
    mem_k = _xattn_cache_view(cache_mem_k)
    mem_v = _xattn_cache_view(cache_mem_v)
    mem2 = mem_prompt.reshape(bp * N_MEM, d)

    xp = x_prompt.reshape(bp * t, d)
    xs = x_sample.reshape(bs, d)
    sbk_p, sbv_p, sbk_s, sbv_s, ret_p, ret_s = [], [], [], [], [], []
    cv_p, cv_s, ff_p, ff_s, mk_p, mv_p = [], [], [], [], [], []

    for l in range(DEPTH):
        if l % 2 == 0:
            e = l // 2
            w_in = w_in_ab[e].astype(BF16)
            w_out = w_out_ab[e].astype(BF16)
            w_out_parts = [w_out[:W_RK], w_out[W_RK:]]
            bias = sb_bias[e].astype(F32)
            bias_row = jnp.repeat(bias, HD_SB).reshape(1, W_SB)
            bias_rows = jnp.zeros((SUBLANES, PAGE_SIZE * H_SB), F32).at[:H_SB].set(
                jnp.broadcast_to(bias[:, None], (H_SB, PAGE_SIZE * H_SB)))
            proj = _mm([xp], [w_in], prologue="rms", pro=(g_mix[l],), tm=1024, name="proj_in")
            proj3 = proj.reshape(bp, t, -1)
            ret_o, s_p = _ret_prompt(proj3, cos_p, sin_p, ret_tabs, ret_gn_g[e])
            sb_o = _sb_prompt(proj3, bias_row, uo)
            xp = _mm([ret_o.reshape(bp * t, W_RK), sb_o.reshape(bp * t, W_SB)], w_out_parts,
                     res=xp, tm=1024, tn=d, name="proj_out")
            sbk_p.append(proj3[:, :, 5 * W_SB:6 * W_SB].reshape(bp, t, H_SB, HD_SB))
            sbv_p.append(proj3[:, :, 6 * W_SB:7 * W_SB].reshape(bp, t, H_SB, HD_SB))
            ret_p.append(s_p)
            proj_s = _mm([xs], [w_in], prologue="rms", pro=(g_mix[l],), name="proj_in_s")
            ret_os, s_s = _ret_sample(proj_s, cos_s, sin_s, ret_gn_g[e], state_ret, e)
            sb_os = _sb_sample(proj_s.reshape(bs, 1, -1), bias_rows, uo, cache_k, cache_v,
                               page_table, e)
            xs = _mm([ret_os, sb_os.reshape(bs, W_SB)], w_out_parts, res=xs, tn=d,
                     name="proj_out_s")
            sbk_s.append(proj_s[:, 5 * W_SB:6 * W_SB].reshape(bs, 1, H_SB, HD_SB))
            sbv_s.append(proj_s[:, 6 * W_SB:7 * W_SB].reshape(bs, 1, H_SB, HD_SB))
            ret_s.append(s_s)
        else:
            o = l // 2
            w1 = cv_w1[o].astype(BF16)
            w2 = cv_w2[o].astype(BF16)
            u = _mm([xp], [w1], prologue="rms", pro=(g_mix[l],), bias=cv_b1[o], glu=True,
                    tm=1024, name="conv_glu")
            u3 = u.reshape(bp, t, d)
            xp = _conv_prompt(u3, xp.reshape(bp, t, d), cv_dw[o], cv_dwb[o], cv_ln_g[o],
                              cv_ln_b[o], w2, cv_b2[o]).reshape(bp * t, d)
            cv_p.append(u3[:, t - (CONV_W - 1):, :])
            u_s = _mm([xs], [w1], prologue="rms", pro=(g_mix[l],), bias=cv_b1[o], glu=True,
                      name="conv_glu_s")
            c_s = _conv_step(state_conv, o, u_s, cv_dw[o], cv_dwb[o])
            xs = _mm([c_s], [w2], prologue="ln_silu", pro=(cv_ln_g[o], cv_ln_b[o]), bias=cv_b2[o],
                     res=xs, tn=d, name="conv_out_s")
            cv_s.append(jnp.concatenate([state_conv[o][:, 1:], u_s[:, None, :]], axis=1))

        wq = xa_wq[l].astype(BF16)
        wo = xa_wo[l].astype(BF16)
        mk = _mm([mem2], [xa_wk[l].astype(BF16)], tn=d, name="mem_k")
        mv = _mm([mem2], [xa_wv[l].astype(BF16)], tn=d, name="mem_v")
        mk_p.append(mk.reshape(bp, N_MEM, H_X, HD_X))
        mv_p.append(mv.reshape(bp, N_MEM, H_X, HD_X))
        xp = _xattn_prompt(xp.reshape(bp, t, d), g_cross[l], wq, mk.reshape(bp, N_MEM, d),
                           mv.reshape(bp, N_MEM, d), wo)
        q_s = _mm([xs], [wq], prologue="rms", pro=(g_cross[l],), tn=d, name="xattn_q_s")
        att_s = _xattn_sample(q_s.reshape(bs, 1, d), mem_k, mem_v, l)
        xs = _mm([att_s], [wo], res=xs, tn=d, name="xattn_out_s")

        w_up = ffn_w_up[l].astype(BF16)
        w_down = ffn_w_down[l].astype(BF16)
        tail_rows = xp[:, t - (FFN_CONV_W - 1):, :].reshape(bp * (FFN_CONV_W - 1), d)
        ff_p.append(_mm([tail_rows], [w_up], prologue="rms", pro=(g_ffn[l],),
                        name="ffn_tail").reshape(bp, FFN_CONV_W - 1, 2 * D_FF))
        xp = _ffn_prompt(xp, g_ffn[l], w_up, ffn_dw[l], ffn_dwb[l], w_down).reshape(bp * t, d)
        u_s = _mm([xs], [w_up], prologue="rms", pro=(g_ffn[l],), name="ffn_up_s")
        c_s = _conv_step(state_ffn_conv, l, u_s, ffn_dw[l], ffn_dwb[l])
        xs = _mm([c_s], [w_down], prologue="swiglu", res=xs, tn=d, name="ffn_down_s")
        ff_s.append(jnp.concatenate([state_ffn_conv[l][:, 1:], u_s[:, None, :]], axis=1))

    y_prompt = _rmsnorm(xp, g_final).reshape(bp, t, d)
    y_sample = _rmsnorm(xs, g_final).reshape(bs, 1, d)
    return (y_prompt, y_sample, jnp.stack(sbk_p), jnp.stack(sbv_p), jnp.stack(sbk_s),
            jnp.stack(sbv_s), jnp.stack(ret_p), jnp.stack(ret_s), jnp.stack(cv_p),
            jnp.stack(cv_s), jnp.stack(ff_p), jnp.stack(ff_s), jnp.stack(mk_p), jnp.stack(mv_p))
```

```python
import functools

import numpy as np
import jax
import jax.numpy as jnp
from jax import lax
from jax.experimental import pallas as pl
from jax.experimental.pallas import tpu as pltpu

F32 = jnp.float32
BF16 = jnp.bfloat16

D_MODEL = 1024
DEPTH = 4
PAST_LEN = 2048
PAGE_SIZE = 128
N_PAGES = PAST_LEN // PAGE_SIZE
H_RET = 4
DK_RET = 128
DV_RET = 128
RET_CHUNK = 128
ROPE_BASE = 10000.0
H_SB = 4
HD_SB = 128
SB_BLOCK = 128
CONV_W = 31
D_FF = 2816
FFN_CONV_W = 3
N_MEM = 256
H_X = 4
HD_X = D_MODEL // H_X
NORM_EPS = 1e-6
W_RK = H_RET * DK_RET
W_SB = H_SB * HD_SB

SUBLANES = 8
LANES = 128
VMEM_LIMIT_BYTES = 56 * 1024 * 1024


def _params(*sem):
    return pltpu.CompilerParams(dimension_semantics=sem, vmem_limit_bytes=VMEM_LIMIT_BYTES)


def _dot(a, b):
    return jnp.dot(a, b, preferred_element_type=F32)


def _dot_nt(a, b):
    return lax.dot_general(a, b, (((1,), (1,)), ((), ())), preferred_element_type=F32)


def _rms(x, g):
    return x * lax.rsqrt(jnp.mean(x * x, axis=-1, keepdims=True) + NORM_EPS) * g


def _silu(x):
    return x * jax.nn.sigmoid(x)


def _softplus(z):
    return jnp.maximum(z, 0.0) + jnp.log(1.0 + jnp.exp(-jnp.abs(z)))


def _split_hi_lo(x):
    hi = x.astype(BF16)
    lo = (x - hi.astype(F32)).astype(BF16)
    return hi, lo


def _mm_kernel(*refs, n_lhs, prologue, has_bias, has_res, glu):
    it = iter(refs)
    x_refs = [next(it) for _ in range(n_lhs)]
    w_refs = [next(it) for _ in range(n_lhs)]
    wg_refs = [next(it) for _ in range(n_lhs)] if glu else []
    n_pro = {None: 0, "rms": 1, "ln_silu": 2, "swiglu": 0}[prologue]
    p_refs = [next(it) for _ in range(n_pro)]
    b_ref = next(it) if has_bias else None
    bg_ref = next(it) if (has_bias and glu) else None
    r_ref = next(it) if has_res else None
    o_ref = next(it)
    xn_ref = next(it) if prologue else None

    if prologue:
        @pl.when(pl.program_id(1) == 0)
        def _():
            x = x_refs[0][...].astype(F32)
            if prologue == "rms":
                y = _rms(x, p_refs[0][...])
            elif prologue == "ln_silu":
                mu = jnp.mean(x, axis=-1, keepdims=True)
                xc = x - mu
                var = jnp.mean(xc * xc, axis=-1, keepdims=True)
                y = _silu(xc * lax.rsqrt(var + NORM_EPS) * p_refs[0][...] + p_refs[1][...])
            else:
                k = x.shape[-1] // 2
                y = _silu(x[:, :k]) * x[:, k:]
            xn_ref[...] = y.astype(BF16)
        lhs = [xn_ref[...]]
    else:
        lhs = [r[...].astype(BF16) for r in x_refs]

    acc = _dot(lhs[0], w_refs[0][...])
    for a, w in zip(lhs[1:], w_refs[1:]):
        acc += _dot(a, w[...])
    if has_bias:
        acc += b_ref[...]
    if glu:
        gate = _dot(lhs[0], wg_refs[0][...])
        for a, w in zip(lhs[1:], wg_refs[1:]):
            gate += _dot(a, w[...])
        if has_bias:
            gate += bg_ref[...]
        acc = acc * jax.nn.sigmoid(gate)
    if has_res:
        acc += r_ref[...]
    o_ref[...] = acc.astype(o_ref.dtype)


def _mm(xs, ws, *, prologue=None, pro=(), bias=None, res=None, glu=False, tm=512, tn=512,
        out_dtype=F32, name="mm"):
    m = xs[0].shape[0]
    n = ws[0].shape[1] // (2 if glu else 1)
    tm = min(tm, m)
    tn = min(tn, n)
    assert m % tm == 0 and n % tn == 0, (m, tm, n, tn)
    nj = n // tn
    args, specs = [], []
    for x in xs:
        args.append(x)
        specs.append(pl.BlockSpec((tm, x.shape[1]), lambda i, j: (i, 0)))
    for w in ws:
        args.append(w)
        specs.append(pl.BlockSpec((w.shape[0], tn), lambda i, j: (0, j)))
    if glu:
        for w in ws:
            args.append(w)
            specs.append(pl.BlockSpec((w.shape[0], tn), lambda i, j: (0, j + nj)))
    for p in pro:
        args.append(p.reshape(1, -1))
        specs.append(pl.BlockSpec((1, p.size), lambda i, j: (0, 0)))
    if bias is not None:
        b2 = bias.reshape(1, -1)
        args.append(b2)
        specs.append(pl.BlockSpec((1, tn), lambda i, j: (0, j)))
        if glu:
            args.append(b2)
            specs.append(pl.BlockSpec((1, tn), lambda i, j: (0, j + nj)))
    if res is not None:
        args.append(res)
        specs.append(pl.BlockSpec((tm, tn), lambda i, j: (i, j)))
    scratch = []
    if prologue:
        k_eff = xs[0].shape[1] // (2 if prologue == "swiglu" else 1)
        scratch.append(pltpu.VMEM((tm, k_eff), BF16))
    kern = functools.partial(_mm_kernel, n_lhs=len(xs), prologue=prologue,
                             has_bias=bias is not None, has_res=res is not None, glu=glu)
    return pl.pallas_call(
        kern,
        grid=(m // tm, nj),
        in_specs=specs,
        out_specs=pl.BlockSpec((tm, tn), lambda i, j: (i, j)),
        out_shape=jax.ShapeDtypeStruct((m, n), out_dtype),
        scratch_shapes=scratch,
        compiler_params=_params("parallel", "arbitrary"),
        name=name,
    )(*args)


def _rmsnorm_kernel(x_ref, g_ref, o_ref):
    o_ref[...] = _rms(x_ref[...], g_ref[...])


def _rmsnorm(x, g, tm=1024):
    m, d = x.shape
    tm = min(tm, m)
    return pl.pallas_call(
        _rmsnorm_kernel,
        grid=(m // tm,),
        in_specs=[pl.BlockSpec((tm, d), lambda i: (i, 0)), pl.BlockSpec((1, d), lambda i: (0, 0))],
        out_specs=pl.BlockSpec((tm, d), lambda i: (i, 0)),
        out_shape=jax.ShapeDtypeStruct((m, d), F32),
        compiler_params=_params("parallel"),
        name="rmsnorm",
    )(x, g.reshape(1, d))


def _rotate(x, cos2, sin2):
    return x * cos2 + pltpu.roll(x, DK_RET // 2, axis=1) * sin2


def _head_ln_gate(o, gate, gn):
    mu = jnp.mean(o, axis=-1, keepdims=True)
    oc = o - mu
    var = jnp.mean(oc * oc, axis=-1, keepdims=True)
    return _silu(gate) * (oc * lax.rsqrt(var + NORM_EPS) * gn)


def _ret_prompt_kernel(rq_ref, rk_ref, rv_ref, rg_ref, cos_ref, sin_ref, dmask_ref, qdec_ref,
                       kdec_ref, cdec_ref, gn_ref, o_ref, s_ref):
    c = pl.program_id(1)

    @pl.when(c == 0)
    def _():
        s_ref[...] = jnp.zeros_like(s_ref)

    cos2 = cos_ref[...]
    sin2 = sin_ref[...]
    for h in range(H_RET):
        sl = slice(h * DK_RET, (h + 1) * DK_RET)
        q = _rotate(rq_ref[0, :, sl], cos2, sin2)
        k = _rotate(rk_ref[0, :, sl], cos2, sin2) * (DK_RET ** -0.5)
        vb = rv_ref[0, :, sl].astype(BF16)
        qb = q.astype(BF16)
        s0 = s_ref[0, h]
        scores = _dot_nt(qb, k.astype(BF16)) * dmask_ref[h]
        o = _dot(scores.astype(BF16), vb) + _dot(qb, s0.astype(BF16)) * qdec_ref[h]
        kd_t = (k * kdec_ref[h]).T.astype(BF16)
        s_ref[0, h] = cdec_ref[h] * s0 + _dot(kd_t, vb)
        o_ref[0, :, sl] = _head_ln_gate(o, rg_ref[0, :, sl], gn_ref[:, sl]).astype(o_ref.dtype)


def _ret_tables(log_gamma):
    idx = jnp.arange(RET_CHUNK)
    diff = idx[:, None] - idx[None, :]
    expo = jnp.maximum(diff, 0).astype(F32)[None] * log_gamma[:, None, None]
    dmask = jnp.where(diff[None] >= 0, jnp.exp(expo), 0.0)
    q_decay = jnp.exp((idx + 1).astype(F32)[:, None] * log_gamma[None, :])
    k_decay = jnp.exp((RET_CHUNK - 1 - idx).astype(F32)[:, None] * log_gamma[None, :])
    ones = jnp.ones((1, 1, DK_RET), F32)
    qdec = q_decay.T[:, :, None] * ones
    kdec = k_decay.T[:, :, None] * ones
    cdec = jnp.exp(RET_CHUNK * log_gamma)[:, None, None] * jnp.ones((1, DK_RET, DV_RET), F32)
    return dmask, qdec, kdec, cdec


def _rope_tables(pos):
    half = DK_RET // 2
    inv = ROPE_BASE ** (-jnp.arange(half, dtype=F32) / half)
    ang = pos.astype(F32)[:, None] * inv[None, :]
    cos, sin = jnp.cos(ang), jnp.sin(ang)
    return jnp.concatenate([cos, cos], -1), jnp.concatenate([-sin, sin], -1)


def _ret_prompt(proj3, cos2, sin2, tabs, gn):
    b, t, _ = proj3.shape
    nc = t // RET_CHUNK
    dmask, qdec, kdec, cdec = tabs

    def col(cb):
        return pl.BlockSpec((1, RET_CHUNK, W_RK), lambda i, c: (i, c, cb))

    tab = pl.BlockSpec((H_RET, RET_CHUNK, RET_CHUNK), lambda i, c: (0, 0, 0))
    return pl.pallas_call(
        _ret_prompt_kernel,
        grid=(b, nc),
        in_specs=[col(0), col(1), col(2), col(3),
                  pl.BlockSpec((RET_CHUNK, DK_RET), lambda i, c: (c, 0)),
                  pl.BlockSpec((RET_CHUNK, DK_RET), lambda i, c: (c, 0)),
                  tab, tab, tab, tab,
                  pl.BlockSpec((1, W_RK), lambda i, c: (0, 0))],
        out_specs=[pl.BlockSpec((1, RET_CHUNK, W_RK), lambda i, c: (i, c, 0)),
                   pl.BlockSpec((1, H_RET, DK_RET, DV_RET), lambda i, c: (i, 0, 0, 0))],
        out_shape=[jax.ShapeDtypeStruct((b, t, W_RK), BF16),
                   jax.ShapeDtypeStruct((b, H_RET, DK_RET, DV_RET), F32)],
        compiler_params=_params("parallel", "arbitrary"),
        name="ret_prompt",
    )(proj3, proj3, proj3, proj3, cos2, sin2, dmask, qdec, kdec, cdec, gn.reshape(1, W_RK))


RET_BB = 8
MXU_ROWS = 16


def _ret_sample_kernel(rq_ref, rk_ref, rv_ref, rg_ref, cos_ref, sin_ref, gn_ref, st_ref,
                       o_ref, so_ref, inter_ref, *, g1):
    cos2 = cos_ref[...]
    sin2 = sin_ref[...]
    eye = (lax.broadcasted_iota(jnp.int32, (DK_RET, DK_RET), 0)
           == lax.broadcasted_iota(jnp.int32, (DK_RET, DK_RET), 1))
    for h in range(H_RET):
        sl = slice(h * DK_RET, (h + 1) * DK_RET)
        q = _rotate(rq_ref[:, sl], cos2, sin2)
        k = _rotate(rk_ref[:, sl], cos2, sin2) * (DK_RET ** -0.5)
        v = rv_ref[:, sl]
        for r in range(RET_BB):
            s0 = st_ref[r, h]
            qr = jnp.broadcast_to(q[r:r + 1], (MXU_ROWS, DK_RET)).astype(BF16)
            inter_ref[r:r + 1, :] = _dot(qr, s0.astype(BF16))[0:1] * g1[h]
            diag_k = jnp.where(eye, jnp.broadcast_to(k[r:r + 1], (DK_RET, DK_RET)), 0.0).astype(BF16)
            v_rows = jnp.broadcast_to(v[r:r + 1], (DK_RET, DV_RET)).astype(BF16)
            so_ref[r, h] = g1[h] * s0 + _dot(diag_k, v_rows)
        o = jnp.sum(q * k, axis=-1, keepdims=True) * v + inter_ref[...]
        o_ref[:, sl] = _head_ln_gate(o, rg_ref[:, sl], gn_ref[:, sl]).astype(o_ref.dtype)


def _ret_sample(proj, cos2, sin2, gn, state, e):
    bs = proj.shape[0]
    log_gamma = np.log1p(-np.exp2(-5.0 - np.arange(H_RET, dtype=np.float32)))
    g1 = tuple(float(x) for x in np.exp(log_gamma).astype(np.float32))

    def col(cb):
        return pl.BlockSpec((RET_BB, W_RK), lambda i: (i, cb))

    row = pl.BlockSpec((1, DK_RET), lambda i: (0, 0))
    return pl.pallas_call(
        functools.partial(_ret_sample_kernel, g1=g1),
        grid=(bs // RET_BB,),
        in_specs=[col(0), col(1), col(2), col(3), row, row,
                  pl.BlockSpec((1, W_RK), lambda i: (0, 0)),
                  pl.BlockSpec((None, RET_BB, H_RET, DK_RET, DV_RET), lambda i: (e, i, 0, 0, 0))],
        out_specs=[pl.BlockSpec((RET_BB, W_RK), lambda i: (i, 0)),
                   pl.BlockSpec((RET_BB, H_RET, DK_RET, DV_RET), lambda i: (i, 0, 0, 0))],
        out_shape=[jax.ShapeDtypeStruct((bs, W_RK), BF16),
                   jax.ShapeDtypeStruct((bs, H_RET, DK_RET, DV_RET), F32)],
        scratch_shapes=[pltpu.VMEM((RET_BB, DV_RET), F32)],
        compiler_params=_params("parallel"),
        name="ret_sample",
    )(proj, proj, proj, proj, cos2, sin2, gn.reshape(1, W_RK), state)


def _suffix_matrix():
    j = jnp.arange(SB_BLOCK)
    u = (j[:, None] > j[None, :]).astype(BF16)
    half = jnp.concatenate([u, jnp.ones((SB_BLOCK, SB_BLOCK), BF16)], axis=1)
    return jnp.concatenate([half, half], axis=0)


def _suffix_sums(log_1m, uo):
    hi, lo = _split_hi_lo(log_1m)
    return _dot(jnp.concatenate([hi, lo], axis=1), uo)


SB_TQ = 256


def _sb_prompt_kernel(q_ref, k_ref, v_ref, bias_ref, uo_ref, o_ref, acc_ref, carry_ref):
    i = pl.program_id(1)
    uo = uo_ref[...]
    heads = [slice(h * HD_SB, (h + 1) * HD_SB) for h in range(H_SB)]

    def blocks(js, masked, first):
        starts = [pl.multiple_of(j * SB_BLOCK, SB_BLOCK) for j in js]
        zs = [jnp.concatenate(
            [_dot_nt(q_ref[0, :, sl].astype(BF16), k_ref[0, pl.ds(st, SB_BLOCK), sl].astype(BF16))
             * (HD_SB ** -0.5) + bias_ref[:, sl] for sl in heads], axis=0) for st in starts]
        sps = [_softplus(z) for z in zs]
        valids = []
        if masked:
            for z, st in zip(zs, starts):
                qpos = i * SB_TQ + (lax.broadcasted_iota(jnp.int32, z.shape, 0) & (SB_TQ - 1))
                kpos = st + lax.broadcasted_iota(jnp.int32, z.shape, 1)
                valids.append(kpos < qpos)
            rs = [_suffix_sums(jnp.where(v, -sp, 0.0), uo) for v, sp in zip(valids, sps)]
        else:
            rs = [_suffix_sums(-sp, uo) for sp in sps]
        carry = None if first else carry_ref[...]
        pvs = []
        for n, st in enumerate(starts):
            after = rs[n][:, :SB_BLOCK]
            if carry is not None:
                after = after + carry
            a = jnp.exp(zs[n] - sps[n] + after)
            if masked:
                a = jnp.where(valids[n], a, 0.0)
            ab = a.astype(BF16)
            pvs.append(jnp.concatenate(
                [_dot(ab[h * SB_TQ:(h + 1) * SB_TQ], v_ref[0, pl.ds(st, SB_BLOCK), sl].astype(BF16))
                 for h, sl in enumerate(heads)], axis=0))
            total = rs[n][:, SB_BLOCK:]
            carry = total if carry is None else carry + total
        pv = pvs[0]
        for p in pvs[1:]:
            pv = pv + p
        acc_ref[...] = pv if first else acc_ref[...] + pv
        carry_ref[...] = carry

    n_diag = SB_TQ // SB_BLOCK
    blocks([i * n_diag + d for d in reversed(range(n_diag))], True, True)

    def body(t, _):
        j = (i - t) * n_diag - 1
        blocks([j - d for d in range(n_diag)], False, False)
        return 0

    lax.fori_loop(0, i, body, 0)
    for h, sl in enumerate(heads):
        o_ref[0, :, sl] = acc_ref[h * SB_TQ:(h + 1) * SB_TQ, :].astype(o_ref.dtype)


def _sb_prompt(proj3, bias_row, uo):
    b, t, _ = proj3.shape
    tq = min(SB_TQ, t)
    assert tq == SB_TQ and t % tq == 0
    return pl.pallas_call(
        _sb_prompt_kernel,
        grid=(b, t // tq),
        in_specs=[pl.BlockSpec((1, tq, W_SB), lambda bi, i: (bi, i, 4)),
                  pl.BlockSpec((1, t, W_SB), lambda bi, i: (bi, 0, 5)),
                  pl.BlockSpec((1, t, W_SB), lambda bi, i: (bi, 0, 6)),
                  pl.BlockSpec((1, W_SB), lambda bi, i: (0, 0)),
                  pl.BlockSpec((2 * SB_BLOCK, 2 * SB_BLOCK), lambda bi, i: (0, 0))],
        out_specs=pl.BlockSpec((1, tq, W_SB), lambda bi, i: (bi, i, 0)),
        out_shape=jax.ShapeDtypeStruct((b, t, W_SB), BF16),
        scratch_shapes=[pltpu.VMEM((H_SB * tq, HD_SB), F32),
                        pltpu.VMEM((H_SB * tq, SB_BLOCK), F32)],
        compiler_params=_params("parallel", "arbitrary"),
        name="sb_prompt",
    )(proj3, proj3, proj3, bias_row, uo)


def _row_select(parts):
    r = lax.broadcasted_iota(jnp.int32, parts[0].shape, 0)
    out = jnp.zeros_like(parts[0])
    for h, p in enumerate(parts):
        out = jnp.where(r == h, p, out)
    return out


def _bcast_rows_b16(row):
    return jnp.broadcast_to(row, (MXU_ROWS, row.shape[-1])).astype(BF16)


def _sb_sample_kernel(pt_ref, q_ref, bias_ref, uo_ref, *refs):
    del pt_ref
    k_refs = refs[:N_PAGES]
    v_refs = refs[N_PAGES:2 * N_PAGES]
    o_ref = refs[2 * N_PAGES]
    uo = uo_ref[...]
    n_col = PAGE_SIZE * H_SB
    n_grp = n_col // SB_BLOCK
    q4 = _row_select([jnp.broadcast_to(q_ref[0, :, h * HD_SB:(h + 1) * HD_SB], (SUBLANES, HD_SB))
                      for h in range(H_SB)])
    qb = jnp.concatenate([q4, q4], axis=0).astype(BF16)
    row = lax.broadcasted_iota(jnp.int32, (SUBLANES, n_col), 0)
    col = lax.broadcasted_iota(jnp.int32, (SUBLANES, n_col), 1)
    valid = (col & (H_SB - 1)) == row
    bias = bias_ref[...]
    zs, sps, his, los = [], [], [], []
    for p in range(N_PAGES):
        z = _dot_nt(qb, k_refs[p][...].astype(BF16))[:SUBLANES] * (HD_SB ** -0.5) + bias
        sp = _softplus(z)
        hi, lo = _split_hi_lo(jnp.where(valid, -sp, 0.0))
        zs.append(z)
        sps.append(sp)
        his += [hi[:, g * SB_BLOCK:(g + 1) * SB_BLOCK] for g in range(n_grp)]
        los += [lo[:, g * SB_BLOCK:(g + 1) * SB_BLOCK] for g in range(n_grp)]
    n_all = N_PAGES * n_grp
    r = _dot(jnp.concatenate([jnp.concatenate(his, axis=0), jnp.concatenate(los, axis=0)],
                             axis=1), uo)
    carry = jnp.zeros((SUBLANES, SB_BLOCK), F32)
    after = [None] * n_all
    for g in reversed(range(n_all)):
        rows = slice(g * SUBLANES, (g + 1) * SUBLANES)
        after[g] = r[rows, :SB_BLOCK] + carry
        carry = carry + r[rows, SB_BLOCK:]
    acc = jnp.zeros((MXU_ROWS, HD_SB), F32)
    for p in range(N_PAGES):
        aft = jnp.concatenate(after[p * n_grp:(p + 1) * n_grp], axis=1)
        a = jnp.where(valid, jnp.exp(zs[p] - sps[p] + aft), 0.0)
        ab = jnp.concatenate([a, a], axis=0).astype(BF16)
        acc += _dot(ab, v_refs[p][...].astype(BF16))
    o_ref[0] = acc[:H_SB].astype(o_ref.dtype)


def _sb_sample(proj3, bias_rows, uo, cache_k, cache_v, page_table, e):
    bs = proj3.shape[0]

    def page(p):
        return pl.BlockSpec((None, None, PAGE_SIZE * H_SB, HD_SB),
                            lambda b, pt: (e, pt[b, p], 0, 0))

    grid_spec = pltpu.PrefetchScalarGridSpec(
        num_scalar_prefetch=1,
        grid=(bs,),
        in_specs=[pl.BlockSpec((1, 1, W_SB), lambda b, pt: (b, 0, 4)),
                  pl.BlockSpec((SUBLANES, PAGE_SIZE * H_SB), lambda b, pt: (0, 0)),
                  pl.BlockSpec((2 * SB_BLOCK, 2 * SB_BLOCK), lambda b, pt: (0, 0))]
                 + [page(p) for p in range(N_PAGES)] + [page(p) for p in range(N_PAGES)],
        out_specs=pl.BlockSpec((1, H_SB, HD_SB), lambda b, pt: (b, 0, 0)),
    )
    return pl.pallas_call(
        _sb_sample_kernel,
        grid_spec=grid_spec,
        out_shape=jax.ShapeDtypeStruct((bs, H_SB, HD_SB), F32),
        compiler_params=_params("arbitrary"),
        name="sb_sample",
    )(page_table, proj3, bias_rows, uo, *([cache_k] * N_PAGES), *([cache_v] * N_PAGES))


def _softmax_rows(s):
    m = jnp.max(s, axis=-1, keepdims=True)
    p = jnp.exp(s - m)
    return p / jnp.sum(p, axis=-1, keepdims=True)


def _xattn_prompt_kernel(x_ref, g_ref, wq_ref, mk_ref, mv_ref, wo_ref, o_ref, att_ref):
    x = x_ref[0]
    hn = _rms(x, g_ref[...]).astype(BF16)
    q = _dot(hn, wq_ref[...]).astype(BF16)
    for h in range(H_X):
        sl = slice(h * HD_X, (h + 1) * HD_X)
        s = _dot_nt(q[:, sl], mk_ref[0, :, sl].astype(BF16)) * (HD_X ** -0.5)
        p = _softmax_rows(s)
        att_ref[:, sl] = _dot(p.astype(BF16), mv_ref[0, :, sl].astype(BF16)).astype(BF16)
    o_ref[0] = x + _dot(att_ref[...], wo_ref[...])


def _xattn_prompt(x3, g, wq, mk3, mv3, wo, tm=512):
    b, t, d = x3.shape
    tm = min(tm, t)
    xs = pl.BlockSpec((1, tm, d), lambda bi, i: (bi, i, 0))
    ws = pl.BlockSpec((d, d), lambda bi, i: (0, 0))
    ms = pl.BlockSpec((1, N_MEM, d), lambda bi, i: (bi, 0, 0))
    return pl.pallas_call(
        _xattn_prompt_kernel,
        grid=(b, t // tm),
        in_specs=[xs, pl.BlockSpec((1, d), lambda bi, i: (0, 0)), ws, ms, ms, ws],
        out_specs=xs,
        out_shape=jax.ShapeDtypeStruct((b, t, d), F32),
        scratch_shapes=[pltpu.VMEM((tm, d), BF16)],
        compiler_params=_params("parallel", "arbitrary"),
        name="xattn_prompt",
    )(x3, g.reshape(1, d), wq, mk3, mv3, wo)


XS_HALVES = HD_X // LANES
XS_ROWS = XS_HALVES * H_X
XS_BB = 4


def _xattn_sample_kernel(q_ref, mk_ref, mv_ref, o_ref):
    n_col = N_MEM * XS_ROWS
    row = lax.broadcasted_iota(jnp.int32, (SUBLANES, n_col), 0)
    col = lax.broadcasted_iota(jnp.int32, (SUBLANES, n_col), 1)
    valid = (col & (XS_ROWS - 1)) == row
    for i in range(q_ref.shape[0]):
        q8 = _row_select([jnp.broadcast_to(q_ref[i, :, (a % H_X) * HD_X + (a // H_X) * LANES:
                                                 (a % H_X) * HD_X + (a // H_X + 1) * LANES],
                                           (SUBLANES, LANES)) for a in range(XS_ROWS)])
        qb = jnp.concatenate([q8, q8], axis=0).astype(BF16)
        z = jnp.where(valid, _dot_nt(qb, mk_ref[i].astype(BF16))[:SUBLANES], 0.0)
        zr = pltpu.roll(z, H_X, axis=0)
        other = jnp.where(row < H_X, pltpu.roll(zr, n_col - H_X, axis=1),
                          pltpu.roll(zr, H_X, axis=1))
        s = (z + other) * (HD_X ** -0.5)
        m = jnp.max(jnp.where(valid, s, -jnp.inf), axis=-1, keepdims=True)
        p = jnp.where(valid, jnp.exp(s - m), 0.0)
        p = p / jnp.sum(p, axis=-1, keepdims=True)
        pb = jnp.concatenate([p, p], axis=0).astype(BF16)
        o_ref[i] = _dot(pb, mv_ref[i].astype(BF16))[:SUBLANES].astype(o_ref.dtype)


def _xattn_cache_view(cache):
    dp, b = cache.shape[:2]
    c = cache.reshape(dp, b, N_MEM, H_X, XS_HALVES, LANES)
    return jnp.swapaxes(c, 3, 4).reshape(dp, b, N_MEM * XS_ROWS, LANES)


def _xattn_sample(q3, mem_k, mem_v, l):
    bs, _, d = q3.shape
    bb = XS_BB
    ms = pl.BlockSpec((None, bb, N_MEM * XS_ROWS, LANES), lambda b: (l, b, 0, 0))
    o = pl.pallas_call(
        _xattn_sample_kernel,
        grid=(bs // bb,),
        in_specs=[pl.BlockSpec((bb, 1, d), lambda b: (b, 0, 0)), ms, ms],
        out_specs=pl.BlockSpec((bb, XS_ROWS, LANES), lambda b: (b, 0, 0)),
        out_shape=jax.ShapeDtypeStruct((bs, XS_ROWS, LANES), F32),
        compiler_params=_params("parallel"),
        name="xattn_sample",
    )(q3, mem_k, mem_v)
    return jnp.swapaxes(o.reshape(bs, XS_HALVES, H_X, LANES), 1, 2).reshape(bs, d)


FFN_TF = 256
HALO = SUBLANES


FFN_ROW_CHUNKS = 8


def _ffn_prompt_kernel(x_ref, g_ref, wg_ref, wv_ref, dwg_ref, dwv_ref, bg_ref, bv_ref, wd_ref,
                       o_ref, hn_ref, tail_ref, *u_refs):
    t = pl.program_id(1)
    f = pl.program_id(2)
    tm = x_ref.shape[1]
    tf = wg_ref.shape[1]
    rc = tm // len(u_refs)

    @pl.when(f == 0)
    def _():
        hn_ref[...] = _rms(x_ref[0], g_ref[...]).astype(BF16)
        o_ref[0] = x_ref[0]

    def conv(u_ref, dw_ref, b_ref, sl):
        c = b_ref[...] + dw_ref[FFN_CONV_W - 1:FFN_CONV_W, :] * u_ref[HALO:, sl]
        for w in range(FFN_CONV_W - 1):
            off = HALO - (FFN_CONV_W - 1) + w
            c += dw_ref[w:w + 1, :] * u_ref[off:off + rc, sl]
        return c

    def up(k):
        hn = hn_ref[k * rc:(k + 1) * rc, :]
        u_refs[k][HALO:, :tf] = _dot(hn, wg_ref[...])
        u_refs[k][HALO:, tf:] = _dot(hn, wv_ref[...])

    up(0)
    for k, u_ref in enumerate(u_refs):
        if k + 1 < len(u_refs):
            up(k + 1)
        if k == 0:
            u_ref[:HALO, :] = jnp.where(t == 0, 0.0, tail_ref[f])
        else:
            u_ref[:HALO, :] = u_refs[k - 1][rc:, :]
        gated = (_silu(conv(u_ref, dwg_ref, bg_ref, slice(0, tf)))
                 * conv(u_ref, dwv_ref, bv_ref, slice(tf, 2 * tf)))
        o_ref[0, k * rc:(k + 1) * rc, :] += _dot(gated.astype(BF16), wd_ref[...])
    tail_ref[f] = u_refs[-1][rc:, :]


def _ffn_prompt(x3, g, w_up, dw, dwb, w_down, tm=2048):
    b, t, d = x3.shape
    tm = min(tm, t)
    nf = D_FF // FFN_TF
    assert D_FF % FFN_TF == 0 and t % tm == 0
    xs = pl.BlockSpec((1, tm, d), lambda bi, i, f: (bi, i, 0))
    dwb2 = dwb.reshape(1, -1)
    return pl.pallas_call(
        _ffn_prompt_kernel,
        grid=(b, t // tm, nf),
        in_specs=[xs, pl.BlockSpec((1, d), lambda bi, i, f: (0, 0)),
                  pl.BlockSpec((d, FFN_TF), lambda bi, i, f: (0, f)),
                  pl.BlockSpec((d, FFN_TF), lambda bi, i, f: (0, f + nf)),
                  pl.BlockSpec((FFN_CONV_W, FFN_TF), lambda bi, i, f: (0, f)),
                  pl.BlockSpec((FFN_CONV_W, FFN_TF), lambda bi, i, f: (0, f + nf)),
                  pl.BlockSpec((1, FFN_TF), lambda bi, i, f: (0, f)),
                  pl.BlockSpec((1, FFN_TF), lambda bi, i, f: (0, f + nf)),
                  pl.BlockSpec((FFN_TF, d), lambda bi, i, f: (f, 0))],
        out_specs=xs,
        out_shape=jax.ShapeDtypeStruct((b, t, d), F32),
        scratch_shapes=[pltpu.VMEM((tm, d), BF16), pltpu.VMEM((nf, HALO, 2 * FFN_TF), F32)]
                       + [pltpu.VMEM((HALO + tm // FFN_ROW_CHUNKS, 2 * FFN_TF), F32)] * FFN_ROW_CHUNKS,
        compiler_params=_params("parallel", "arbitrary", "arbitrary"),
        name="ffn_prompt",
    )(x3, g.reshape(1, d), w_up, w_up, dw, dw, dwb2, dwb2, w_down)


CONV_HALO = 32
CONV_RB = 32


def _conv_prompt_kernel(u_ref, x_ref, dw_ref, dwb_ref, lg_ref, lb_ref, w2_ref, b2_ref,
                        o_ref, s_ref, c_ref, dwt_ref):
    t = pl.program_id(1)
    tm = u_ref.shape[1]
    n = CONV_HALO + tm

    @pl.when(t == 0)
    def _():
        s_ref[0, :CONV_HALO, :] = jnp.zeros((CONV_HALO, D_MODEL), F32)

    @pl.when(t > 0)
    def _():
        s_ref[0, :CONV_HALO, :] = s_ref[0, tm:, :]

    s_ref[0, CONV_HALO:, :] = u_ref[0]
    for r in range(1, SUBLANES):
        s_ref[r] = pltpu.roll(s_ref[0], n - r, axis=0)
    base = CONV_HALO - (CONV_W - 1)

    @pl.when(t == 0)
    def _():
        for w in range(CONV_W):
            dwt_ref[w] = jnp.broadcast_to(dw_ref[w:w + 1, :], (SUBLANES, D_MODEL))

    n_sub = CONV_RB // SUBLANES

    def row_block(rb, carry):
        r0 = rb * CONV_RB
        bias = jnp.broadcast_to(dwb_ref[...], (SUBLANES, D_MODEL))
        accs = [bias] * n_sub
        for w in range(CONV_W):
            r = (base + w) % SUBLANES
            dwt = dwt_ref[w]
            for k in range(n_sub):
                start = pl.multiple_of(r0 + (base + w - r) + k * SUBLANES, SUBLANES)
                accs[k] = accs[k] + dwt * s_ref[r, pl.ds(start, SUBLANES), :]
        for k in range(n_sub):
            c_ref[pl.ds(pl.multiple_of(r0 + k * SUBLANES, SUBLANES), SUBLANES), :] = accs[k]
        return carry

    lax.fori_loop(0, tm // CONV_RB, row_block, 0)
    c = c_ref[...]
    mu = jnp.mean(c, axis=-1, keepdims=True)
    cc = c - mu
    var = jnp.mean(cc * cc, axis=-1, keepdims=True)
    y = _silu(cc * lax.rsqrt(var + NORM_EPS) * lg_ref[...] + lb_ref[...])
    o_ref[0] = x_ref[0] + _dot(y.astype(BF16), w2_ref[...]) + b2_ref[...]


def _conv_prompt(u3, x3, dw, dwb, ln_g, ln_b, w2, b2, tm=512):
    b, t, d = x3.shape
    tm = min(tm, t)
    xs = pl.BlockSpec((1, tm, d), lambda bi, i: (bi, i, 0))
    vec = pl.BlockSpec((1, d), lambda bi, i: (0, 0))
    return pl.pallas_call(
        _conv_prompt_kernel,
        grid=(b, t // tm),
        in_specs=[xs, xs, pl.BlockSpec((CONV_W, d), lambda bi, i: (0, 0)), vec, vec, vec,
                  pl.BlockSpec((d, d), lambda bi, i: (0, 0)), vec],
        out_specs=xs,
        out_shape=jax.ShapeDtypeStruct((b, t, d), F32),
        scratch_shapes=[pltpu.VMEM((SUBLANES, CONV_HALO + tm, d), F32), pltpu.VMEM((tm, d), F32),
                        pltpu.VMEM((CONV_W, SUBLANES, d), F32)],
        compiler_params=_params("parallel", "arbitrary"),
        name="conv_prompt",
    )(u3, x3, dw, dwb.reshape(1, d), ln_g.reshape(1, d), ln_b.reshape(1, d), w2, b2.reshape(1, d))


STEP_BB = 8


def _conv_step_kernel(st_ref, u_ref, dw_ref, dwb_ref, o_ref):
    w_taps = dw_ref.shape[0]
    for r in range(STEP_BB):
        acc = dwb_ref[...] + dw_ref[w_taps - 1:w_taps, :] * u_ref[r:r + 1, :]
        for w in range(w_taps - 1):
            acc += dw_ref[w:w + 1, :] * st_ref[r, w:w + 1, :]
        o_ref[r:r + 1, :] = acc


def _conv_step(state, l, u, dw, dwb):
    bs, c = u.shape
    w_taps = dw.shape[0]
    return pl.pallas_call(
        _conv_step_kernel,
        grid=(bs // STEP_BB,),
        in_specs=[pl.BlockSpec((None, STEP_BB, w_taps - 1, c), lambda i: (l, i, 0, 0)),
                  pl.BlockSpec((STEP_BB, c), lambda i: (i, 0)),
                  pl.BlockSpec((w_taps, c), lambda i: (0, 0)),
                  pl.BlockSpec((1, c), lambda i: (0, 0))],
        out_specs=pl.BlockSpec((STEP_BB, c), lambda i: (i, 0)),
        out_shape=jax.ShapeDtypeStruct((bs, c), F32),
        compiler_params=_params("parallel"),
        name="conv_step",
    )(state, u, dw, dwb.reshape(1, c))


def _stack_rows_kernel(*refs, n_layers, col_starts):
    srcs, dst = refs[:n_layers], refs[n_layers]
    rows = srcs[0].shape[1]
    layer = pl.program_id(0)
    for n, src in enumerate(srcs):
        @pl.when(layer == n)
        def _(src=src):
            for a, c0 in enumerate(col_starts):
                dst[pl.ds(a, rows, stride=len(col_starts)), :] = src[0, :, c0:c0 + LANES]


def _stack_rows(srcs, col_block, width, col_starts, tm):
    n_layers = len(srcs)
    b, t, _ = srcs[0].shape
    tm = min(tm, t)
    nt = t // tm
    n_phase = len(col_starts)

    def src_spec(n):
        def idx(l, bi, i):
            on = l == n
            return (jnp.where(on, bi, 0), jnp.where(on, i, 0), col_block)
        return pl.BlockSpec((1, tm, width), idx)

    return pl.pallas_call(
        functools.partial(_stack_rows_kernel, n_layers=n_layers, col_starts=tuple(col_starts)),
        grid=(n_layers, b, nt),
        in_specs=[src_spec(n) for n in range(n_layers)],
        out_specs=pl.BlockSpec((None, None, tm * n_phase, LANES), lambda l, bi, i: (l, bi, i, 0)),
        out_shape=jax.ShapeDtypeStruct((n_layers, b, t * n_phase, LANES), F32),
        compiler_params=_params("arbitrary", "arbitrary", "arbitrary"),
        name="stack_rows",
    )(*srcs)


def kernel(x_prompt, x_sample, cache_sb_k, cache_sb_v, state_ret, state_conv, state_ffn_conv, cache_mem_k, cache_mem_v, page_table, mem_prompt, g_mix, w_in_ab, ret_gn_g, w_out_ab, sb_bias, cv_w1, cv_b1, cv_dw, cv_dwb, cv_ln_g, cv_ln_b, cv_w2, cv_b2, g_cross, xa_wq, xa_wk, xa_wv, xa_wo, g_ffn, ffn_w_up, ffn_dw, ffn_dwb, ffn_w_down, g_final):
    bp, t, d = x_prompt.shape
    bs = x_sample.shape[0]
    n_phys = cache_sb_k.shape[1]

    log_gamma = jnp.log1p(-jnp.exp2(-5.0 - jnp.arange(H_RET, dtype=F32)))
    ret_tabs = _ret_tables(log_gamma)
    cos_p, sin_p = _rope_tables(jnp.arange(t))
    cos_s, sin_s = _rope_tables(PAST_LEN + jnp.arange(1))
    uo = _suffix_matrix()
    cache_k = cache_sb_k.reshape(cache_sb_k.shape[0], n_phys, PAGE_SIZE * H_SB, HD_SB)
    cache_v = cache_sb_v.reshape(cache_sb_v.shape[0], n_phys, PAGE_SIZE * H_SB, HD_SB)
    mem_k = _xattn_cache_view(cache_mem_k)
    mem_v = _xattn_cache_view(cache_mem_v)
    mem2 = mem_prompt.reshape(bp * N_MEM, d)

    xp = x_prompt.reshape(bp * t, d)
    xs = x_sample.reshape(bs, d)
    sbk_p, sbv_p, sbk_s, sbv_s, ret_p, ret_s = [], [], [], [], [], []
    cv_p, cv_s, ff_p, ff_s, mk_p, mv_p = [], [], [], [], [], []

    for l in range(DEPTH):
        if l % 2 == 0:
            e = l // 2
            w_in = w_in_ab[e].astype(BF16)
            w_out = w_out_ab[e].astype(BF16)
            w_out_parts = [w_out[:W_RK], w_out[W_RK:]]
            bias = sb_bias[e].astype(F32)
            bias_row = jnp.repeat(bias, HD_SB).reshape(1, W_SB)
            bias_rows = jnp.zeros((SUBLANES, PAGE_SIZE * H_SB), F32).at[:H_SB].set(
                jnp.broadcast_to(bias[:, None], (H_SB, PAGE_SIZE * H_SB)))
            proj = _mm([xp], [w_in], prologue="rms", pro=(g_mix[l],), tm=1024, name="proj_in")
            proj3 = proj.reshape(bp, t, -1)
            ret_o, s_p = _ret_prompt(proj3, cos_p, sin_p, ret_tabs, ret_gn_g[e])
            sb_o = _sb_prompt(proj3, bias_row, uo)
            xp = _mm([ret_o.reshape(bp * t, W_RK), sb_o.reshape(bp * t, W_SB)], w_out_parts,
                     res=xp, tm=1024, tn=d, name="proj_out")
            sbk_p.append(proj3)
            ret_p.append(s_p)
            proj_s = _mm([xs], [w_in], prologue="rms", pro=(g_mix[l],), name="proj_in_s")
            ret_os, s_s = _ret_sample(proj_s, cos_s, sin_s, ret_gn_g[e], state_ret, e)
            sb_os = _sb_sample(proj_s.reshape(bs, 1, -1), bias_rows, uo, cache_k, cache_v,
                               page_table, e)
            xs = _mm([ret_os, sb_os.reshape(bs, W_SB)], w_out_parts, res=xs, tn=d,
                     name="proj_out_s")
            sbk_s.append(proj_s[:, 5 * W_SB:6 * W_SB].reshape(bs, 1, H_SB, HD_SB))
            sbv_s.append(proj_s[:, 6 * W_SB:7 * W_SB].reshape(bs, 1, H_SB, HD_SB))
            ret_s.append(s_s)
        else:
            o = l // 2
            w1 = cv_w1[o].astype(BF16)
            w2 = cv_w2[o].astype(BF16)
            u = _mm([xp], [w1], prologue="rms", pro=(g_mix[l],), bias=cv_b1[o], glu=True,
                    tm=1024, name="conv_glu")
            u3 = u.reshape(bp, t, d)
            xp = _conv_prompt(u3, xp.reshape(bp, t, d), cv_dw[o], cv_dwb[o], cv_ln_g[o],
                              cv_ln_b[o], w2, cv_b2[o]).reshape(bp * t, d)
            cv_p.append(u3[:, t - (CONV_W - 1):, :])
            u_s = _mm([xs], [w1], prologue="rms", pro=(g_mix[l],), bias=cv_b1[o], glu=True,
                      name="conv_glu_s")
            c_s = _conv_step(state_conv, o, u_s, cv_dw[o], cv_dwb[o])
            xs = _mm([c_s], [w2], prologue="ln_silu", pro=(cv_ln_g[o], cv_ln_b[o]), bias=cv_b2[o],
                     res=xs, tn=d, name="conv_out_s")
            cv_s.append(jnp.concatenate([state_conv[o][:, 1:], u_s[:, None, :]], axis=1))

        wq = xa_wq[l].astype(BF16)
        wo = xa_wo[l].astype(BF16)
        mk = _mm([mem2], [xa_wk[l].astype(BF16)], tn=d, name="mem_k")
        mv = _mm([mem2], [xa_wv[l].astype(BF16)], tn=d, name="mem_v")
        mk_p.append(mk.reshape(bp, N_MEM, d))
        mv_p.append(mv.reshape(bp, N_MEM, d))
        xp = _xattn_prompt(xp.reshape(bp, t, d), g_cross[l], wq, mk.reshape(bp, N_MEM, d),
                           mv.reshape(bp, N_MEM, d), wo)
        q_s = _mm([xs], [wq], prologue="rms", pro=(g_cross[l],), tn=d, name="xattn_q_s")
        att_s = _xattn_sample(q_s.reshape(bs, 1, d), mem_k, mem_v, l)
        xs = _mm([att_s], [wo], res=xs, tn=d, name="xattn_out_s")

        w_up = ffn_w_up[l].astype(BF16)
        w_down = ffn_w_down[l].astype(BF16)
        tail_rows = xp[:, t - (FFN_CONV_W - 1):, :].reshape(bp * (FFN_CONV_W - 1), d)
        ff_p.append(_mm([tail_rows], [w_up], prologue="rms", pro=(g_ffn[l],),
                        name="ffn_tail").reshape(bp, FFN_CONV_W - 1, 2 * D_FF))
        xp = _ffn_prompt(xp, g_ffn[l], w_up, ffn_dw[l], ffn_dwb[l], w_down).reshape(bp * t, d)
        u_s = _mm([xs], [w_up], prologue="rms", pro=(g_ffn[l],), name="ffn_up_s")
        c_s = _conv_step(state_ffn_conv, l, u_s, ffn_dw[l], ffn_dwb[l])
        xs = _mm([c_s], [w_down], prologue="swiglu", res=xs, tn=d, name="ffn_down_s")
        ff_s.append(jnp.concatenate([state_ffn_conv[l][:, 1:], u_s[:, None, :]], axis=1))

    y_prompt = _rmsnorm(xp, g_final).reshape(bp, t, d)
    y_sample = _rmsnorm(xs, g_final).reshape(bs, 1, d)
    sb_cols = [h * HD_SB for h in range(H_SB)]
    sb_shape = (len(sbk_p), bp, t, H_SB, HD_SB)
    sb_k_prompt = _stack_rows(sbk_p, 5, W_SB, sb_cols, 512).reshape(sb_shape)
    sb_v_prompt = _stack_rows(sbk_p, 6, W_SB, sb_cols, 512).reshape(sb_shape)
    mem_cols = [(a % H_X) * HD_X + (a // H_X) * LANES for a in range(XS_ROWS)]

    def mem_out(parts):
        o = _stack_rows(parts, 0, d, mem_cols, N_MEM)
        o = o.reshape(DEPTH, bp, N_MEM, XS_HALVES, H_X, LANES)
        return jnp.swapaxes(o, 3, 4).reshape(DEPTH, bp, N_MEM, H_X, HD_X)

    return (y_prompt, y_sample, sb_k_prompt, sb_v_prompt, jnp.stack(sbk_s),
            jnp.stack(sbv_s), jnp.stack(ret_p), jnp.stack(ret_s), jnp.stack(cv_p),
            jnp.stack(cv_s), jnp.stack(ff_p), jnp.stack(ff_s), mem_out(mk_p), mem_out(mv_p))
```

```python
import functools

import numpy as np
import jax
import jax.numpy as jnp
from jax import lax
from jax.experimental import pallas as pl
from jax.experimental.pallas import tpu as pltpu

F32 = jnp.float32
BF16 = jnp.bfloat16

D_MODEL = 1024
DEPTH = 4
PAST_LEN = 2048
PAGE_SIZE = 128
N_PAGES = PAST_LEN // PAGE_SIZE
H_RET = 4
DK_RET = 128
DV_RET = 128
RET_CHUNK = 128
ROPE_BASE = 10000.0
H_SB = 4
HD_SB = 128
SB_BLOCK = 128
CONV_W = 31
D_FF = 2816
FFN_CONV_W = 3
N_MEM = 256
H_X = 4
HD_X = D_MODEL // H_X
NORM_EPS = 1e-6
W_RK = H_RET * DK_RET
W_SB = H_SB * HD_SB

SUBLANES = 8
LANES = 128
VMEM_LIMIT_BYTES = 56 * 1024 * 1024


def _params(*sem):
    return pltpu.CompilerParams(dimension_semantics=sem, vmem_limit_bytes=VMEM_LIMIT_BYTES)


def _dot(a, b):
    return jnp.dot(a, b, preferred_element_type=F32)


def _dot_nt(a, b):
    return lax.dot_general(a, b, (((1,), (1,)), ((), ())), preferred_element_type=F32)


def _rms(x, g):
    return x * lax.rsqrt(jnp.mean(x * x, axis=-1, keepdims=True) + NORM_EPS) * g


def _silu(x):
    return x * jax.nn.sigmoid(x)


def _softplus(z):
    return jnp.maximum(z, 0.0) + jnp.log(1.0 + jnp.exp(-jnp.abs(z)))


def _split_hi_lo(x):
    hi = x.astype(BF16)
    lo = (x - hi.astype(F32)).astype(BF16)
    return hi, lo


def _mm_kernel(*refs, n_lhs, prologue, has_bias, has_res, glu):
    it = iter(refs)
    x_refs = [next(it) for _ in range(n_lhs)]
    w_refs = [next(it) for _ in range(n_lhs)]
    wg_refs = [next(it) for _ in range(n_lhs)] if glu else []
    n_pro = {None: 0, "rms": 1, "ln_silu": 2, "swiglu": 0}[prologue]
    p_refs = [next(it) for _ in range(n_pro)]
    b_ref = next(it) if has_bias else None
    bg_ref = next(it) if (has_bias and glu) else None
    r_ref = next(it) if has_res else None
    o_ref = next(it)
    xn_ref = next(it) if prologue else None

    if prologue:
        @pl.when(pl.program_id(1) == 0)
        def _():
            x = x_refs[0][...].astype(F32)
            if prologue == "rms":
                y = _rms(x, p_refs[0][...])
            elif prologue == "ln_silu":
                mu = jnp.mean(x, axis=-1, keepdims=True)
                xc = x - mu
                var = jnp.mean(xc * xc, axis=-1, keepdims=True)
                y = _silu(xc * lax.rsqrt(var + NORM_EPS) * p_refs[0][...] + p_refs[1][...])
            else:
                k = x.shape[-1] // 2
                y = _silu(x[:, :k]) * x[:, k:]
            xn_ref[...] = y.astype(BF16)
        lhs = [xn_ref[...]]
    else:
        lhs = [r[...].astype(BF16) for r in x_refs]

    acc = _dot(lhs[0], w_refs[0][...])
    for a, w in zip(lhs[1:], w_refs[1:]):
        acc += _dot(a, w[...])
    if has_bias:
        acc += b_ref[...]
    if glu:
        gate = _dot(lhs[0], wg_refs[0][...])
        for a, w in zip(lhs[1:], wg_refs[1:]):
            gate += _dot(a, w[...])
        if has_bias:
            gate += bg_ref[...]
        acc = acc * jax.nn.sigmoid(gate)
    if has_res:
        acc += r_ref[...]
    o_ref[...] = acc.astype(o_ref.dtype)


def _mm(xs, ws, *, prologue=None, pro=(), bias=None, res=None, glu=False, tm=512, tn=512,
        out_dtype=F32, name="mm"):
    m = xs[0].shape[0]
    n = ws[0].shape[1] // (2 if glu else 1)
    tm = min(tm, m)
    tn = min(tn, n)
    assert m % tm == 0 and n % tn == 0, (m, tm, n, tn)
    nj = n // tn
    args, specs = [], []
    for x in xs:
        args.append(x)
        specs.append(pl.BlockSpec((tm, x.shape[1]), lambda i, j: (i, 0)))
    for w in ws:
        args.append(w)
        specs.append(pl.BlockSpec((w.shape[0], tn), lambda i, j: (0, j)))
    if glu:
        for w in ws:
            args.append(w)
            specs.append(pl.BlockSpec((w.shape[0], tn), lambda i, j: (0, j + nj)))
    for p in pro:
        args.append(p.reshape(1, -1))
        specs.append(pl.BlockSpec((1, p.size), lambda i, j: (0, 0)))
    if bias is not None:
        b2 = bias.reshape(1, -1)
        args.append(b2)
        specs.append(pl.BlockSpec((1, tn), lambda i, j: (0, j)))
        if glu:
            args.append(b2)
            specs.append(pl.BlockSpec((1, tn), lambda i, j: (0, j + nj)))
    if res is not None:
        args.append(res)
        specs.append(pl.BlockSpec((tm, tn), lambda i, j: (i, j)))
    scratch = []
    if prologue:
        k_eff = xs[0].shape[1] // (2 if prologue == "swiglu" else 1)
        scratch.append(pltpu.VMEM((tm, k_eff), BF16))
    kern = functools.partial(_mm_kernel, n_lhs=len(xs), prologue=prologue,
                             has_bias=bias is not None, has_res=res is not None, glu=glu)
    return pl.pallas_call(
        kern,
        grid=(m // tm, nj),
        in_specs=specs,
        out_specs=pl.BlockSpec((tm, tn), lambda i, j: (i, j)),
        out_shape=jax.ShapeDtypeStruct((m, n), out_dtype),
        scratch_shapes=scratch,
        compiler_params=_params("parallel", "arbitrary"),
        name=name,
    )(*args)


def _rmsnorm_kernel(x_ref, g_ref, o_ref):
    o_ref[...] = _rms(x_ref[...], g_ref[...])


def _rmsnorm(x, g, tm=1024):
    m, d = x.shape
    tm = min(tm, m)
    return pl.pallas_call(
        _rmsnorm_kernel,
        grid=(m // tm,),
        in_specs=[pl.BlockSpec((tm, d), lambda i: (i, 0)), pl.BlockSpec((1, d), lambda i: (0, 0))],
        out_specs=pl.BlockSpec((tm, d), lambda i: (i, 0)),
        out_shape=jax.ShapeDtypeStruct((m, d), F32),
        compiler_params=_params("parallel"),
        name="rmsnorm",
    )(x, g.reshape(1, d))


def _rotate(x, cos2, sin2):
    return x * cos2 + pltpu.roll(x, DK_RET // 2, axis=1) * sin2


def _head_ln_gate(o, gate, gn):
    mu = jnp.mean(o, axis=-1, keepdims=True)
    oc = o - mu
    var = jnp.mean(oc * oc, axis=-1, keepdims=True)
    return _silu(gate) * (oc * lax.rsqrt(var + NORM_EPS) * gn)


def _ret_prompt_kernel(rq_ref, rk_ref, rv_ref, rg_ref, cos_ref, sin_ref, dmask_ref, qdec_ref,
                       kdec_ref, cdec_ref, gn_ref, o_ref, s_ref):
    c = pl.program_id(1)

    @pl.when(c == 0)
    def _():
        s_ref[...] = jnp.zeros_like(s_ref)

    cos2 = cos_ref[...]
    sin2 = sin_ref[...]
    pairs = [(i, h, slice(h * DK_RET, (h + 1) * DK_RET))
             for i in range(rq_ref.shape[0]) for h in range(H_RET)]
    ks = [_rotate(rk_ref[i, :, sl], cos2, sin2) * (DK_RET ** -0.5) for i, h, sl in pairs]
    qbs = [_rotate(rq_ref[i, :, sl], cos2, sin2).astype(BF16) for i, h, sl in pairs]
    vbs = [rv_ref[i, :, sl].astype(BF16) for i, h, sl in pairs]
    s0s = [s_ref[i, h] for i, h, sl in pairs]
    scores = [(_dot_nt(qb, k.astype(BF16)) * dmask_ref[h]).astype(BF16)
              for (i, h, sl), qb, k in zip(pairs, qbs, ks)]
    inters = [_dot(qb, s0.astype(BF16)) * qdec_ref[h]
              for (i, h, sl), qb, s0 in zip(pairs, qbs, s0s)]
    for n, (i, h, sl) in enumerate(pairs):
        kd_t = (ks[n] * kdec_ref[h]).T.astype(BF16)
        s_ref[i, h] = cdec_ref[h] * s0s[n] + _dot(kd_t, vbs[n])
    for n, (i, h, sl) in enumerate(pairs):
        o = _dot(scores[n], vbs[n]) + inters[n]
        o_ref[i, :, sl] = _head_ln_gate(o, rg_ref[i, :, sl], gn_ref[:, sl]).astype(o_ref.dtype)


def _ret_tables(log_gamma):
    idx = jnp.arange(RET_CHUNK)
    diff = idx[:, None] - idx[None, :]
    expo = jnp.maximum(diff, 0).astype(F32)[None] * log_gamma[:, None, None]
    dmask = jnp.where(diff[None] >= 0, jnp.exp(expo), 0.0)
    q_decay = jnp.exp((idx + 1).astype(F32)[:, None] * log_gamma[None, :])
    k_decay = jnp.exp((RET_CHUNK - 1 - idx).astype(F32)[:, None] * log_gamma[None, :])
    ones = jnp.ones((1, 1, DK_RET), F32)
    qdec = q_decay.T[:, :, None] * ones
    kdec = k_decay.T[:, :, None] * ones
    cdec = jnp.exp(RET_CHUNK * log_gamma)[:, None, None] * jnp.ones((1, DK_RET, DV_RET), F32)
    return dmask, qdec, kdec, cdec


def _rope_tables(pos):
    half = DK_RET // 2
    inv = ROPE_BASE ** (-jnp.arange(half, dtype=F32) / half)
    ang = pos.astype(F32)[:, None] * inv[None, :]
    cos, sin = jnp.cos(ang), jnp.sin(ang)
    return jnp.concatenate([cos, cos], -1), jnp.concatenate([-sin, sin], -1)


RET_PROMPT_BB = 2


def _ret_prompt(proj3, cos2, sin2, tabs, gn):
    b, t, _ = proj3.shape
    nc = t // RET_CHUNK
    dmask, qdec, kdec, cdec = tabs

    bb = RET_PROMPT_BB

    def col(cb):
        return pl.BlockSpec((bb, RET_CHUNK, W_RK), lambda i, c: (i, c, cb))

    tab = pl.BlockSpec((H_RET, RET_CHUNK, RET_CHUNK), lambda i, c: (0, 0, 0))
    return pl.pallas_call(
        _ret_prompt_kernel,
        grid=(b // bb, nc),
        in_specs=[col(0), col(1), col(2), col(3),
                  pl.BlockSpec((RET_CHUNK, DK_RET), lambda i, c: (c, 0)),
                  pl.BlockSpec((RET_CHUNK, DK_RET), lambda i, c: (c, 0)),
                  tab, tab, tab, tab,
                  pl.BlockSpec((1, W_RK), lambda i, c: (0, 0))],
        out_specs=[pl.BlockSpec((bb, RET_CHUNK, W_RK), lambda i, c: (i, c, 0)),
                   pl.BlockSpec((bb, H_RET, DK_RET, DV_RET), lambda i, c: (i, 0, 0, 0))],
        out_shape=[jax.ShapeDtypeStruct((b, t, W_RK), BF16),
                   jax.ShapeDtypeStruct((b, H_RET, DK_RET, DV_RET), F32)],
        compiler_params=_params("parallel", "arbitrary"),
        name="ret_prompt",
    )(proj3, proj3, proj3, proj3, cos2, sin2, dmask, qdec, kdec, cdec, gn.reshape(1, W_RK))


RET_BB = 8
MXU_ROWS = 16


def _ret_sample_kernel(rq_ref, rk_ref, rv_ref, rg_ref, cos_ref, sin_ref, gn_ref, st_ref,
                       o_ref, so_ref, inter_ref, *, g1):
    cos2 = cos_ref[...]
    sin2 = sin_ref[...]
    eye = (lax.broadcasted_iota(jnp.int32, (DK_RET, DK_RET), 0)
           == lax.broadcasted_iota(jnp.int32, (DK_RET, DK_RET), 1))
    for h in range(H_RET):
        sl = slice(h * DK_RET, (h + 1) * DK_RET)
        q = _rotate(rq_ref[:, sl], cos2, sin2)
        k = _rotate(rk_ref[:, sl], cos2, sin2) * (DK_RET ** -0.5)
        v = rv_ref[:, sl]
        for r in range(RET_BB):
            s0 = st_ref[r, h]
            qr = jnp.broadcast_to(q[r:r + 1], (MXU_ROWS, DK_RET)).astype(BF16)
            inter_ref[r:r + 1, :] = _dot(qr, s0.astype(BF16))[0:1] * g1[h]
            diag_k = jnp.where(eye, jnp.broadcast_to(k[r:r + 1], (DK_RET, DK_RET)), 0.0).astype(BF16)
            v_rows = jnp.broadcast_to(v[r:r + 1], (DK_RET, DV_RET)).astype(BF16)
            so_ref[r, h] = g1[h] * s0 + _dot(diag_k, v_rows)
        o = jnp.sum(q * k, axis=-1, keepdims=True) * v + inter_ref[...]
        o_ref[:, sl] = _head_ln_gate(o, rg_ref[:, sl], gn_ref[:, sl]).astype(o_ref.dtype)


def _ret_sample(proj, cos2, sin2, gn, state, e):
    bs = proj.shape[0]
    log_gamma = np.log1p(-np.exp2(-5.0 - np.arange(H_RET, dtype=np.float32)))
    g1 = tuple(float(x) for x in np.exp(log_gamma).astype(np.float32))

    def col(cb):
        return pl.BlockSpec((RET_BB, W_RK), lambda i: (i, cb))

    row = pl.BlockSpec((1, DK_RET), lambda i: (0, 0))
    return pl.pallas_call(
        functools.partial(_ret_sample_kernel, g1=g1),
        grid=(bs // RET_BB,),
        in_specs=[col(0), col(1), col(2), col(3), row, row,
                  pl.BlockSpec((1, W_RK), lambda i: (0, 0)),
                  pl.BlockSpec((None, RET_BB, H_RET, DK_RET, DV_RET), lambda i: (e, i, 0, 0, 0))],
        out_specs=[pl.BlockSpec((RET_BB, W_RK), lambda i: (i, 0)),
                   pl.BlockSpec((RET_BB, H_RET, DK_RET, DV_RET), lambda i: (i, 0, 0, 0))],
        out_shape=[jax.ShapeDtypeStruct((bs, W_RK), BF16),
                   jax.ShapeDtypeStruct((bs, H_RET, DK_RET, DV_RET), F32)],
        scratch_shapes=[pltpu.VMEM((RET_BB, DV_RET), F32)],
        compiler_params=_params("parallel"),
        name="ret_sample",
    )(proj, proj, proj, proj, cos2, sin2, gn.reshape(1, W_RK), state)


def _suffix_matrix():
    j = jnp.arange(SB_BLOCK)
    u = (j[:, None] > j[None, :]).astype(BF16)
    half = jnp.concatenate([u, jnp.ones((SB_BLOCK, SB_BLOCK), BF16)], axis=1)
    return jnp.concatenate([half, half], axis=0)


def _suffix_sums(log_1m, uo):
    hi, lo = _split_hi_lo(log_1m)
    return _dot(jnp.concatenate([hi, lo], axis=1), uo)


SB_TQ = 256


def _sb_prompt_kernel(q_ref, k_ref, v_ref, bias_ref, uo_ref, o_ref, acc_ref, carry_ref):
    i = pl.program_id(1)
    uo = uo_ref[...]
    heads = [slice(h * HD_SB, (h + 1) * HD_SB) for h in range(H_SB)]

    def blocks(js, masked, first):
        starts = [pl.multiple_of(j * SB_BLOCK, SB_BLOCK) for j in js]
        zs = [jnp.concatenate(
            [_dot_nt(q_ref[0, :, sl].astype(BF16), k_ref[0, pl.ds(st, SB_BLOCK), sl].astype(BF16))
             * (HD_SB ** -0.5) + bias_ref[:, sl] for sl in heads], axis=0) for st in starts]
        sps = [_softplus(z) for z in zs]
        valids = []
        if masked:
            for z, st in zip(zs, starts):
                qpos = i * SB_TQ + (lax.broadcasted_iota(jnp.int32, z.shape, 0) & (SB_TQ - 1))
                kpos = st + lax.broadcasted_iota(jnp.int32, z.shape, 1)
                valids.append(kpos < qpos)
            rs = [_suffix_sums(jnp.where(v, -sp, 0.0), uo) for v, sp in zip(valids, sps)]
        else:
            rs = [_suffix_sums(-sp, uo) for sp in sps]
        carry = None if first else carry_ref[...]
        pvs = []
        for n, st in enumerate(starts):
            after = rs[n][:, :SB_BLOCK]
            if carry is not None:
                after = after + carry
            a = jnp.exp(zs[n] - sps[n] + after)
            if masked:
                a = jnp.where(valids[n], a, 0.0)
            ab = a.astype(BF16)
            pvs.append(jnp.concatenate(
                [_dot(ab[h * SB_TQ:(h + 1) * SB_TQ], v_ref[0, pl.ds(st, SB_BLOCK), sl].astype(BF16))
                 for h, sl in enumerate(heads)], axis=0))
            total = rs[n][:, SB_BLOCK:]
            carry = total if carry is None else carry + total
        pv = pvs[0]
        for p in pvs[1:]:
            pv = pv + p
        acc_ref[...] = pv if first else acc_ref[...] + pv
        carry_ref[...] = carry

    n_diag = SB_TQ // SB_BLOCK
    blocks([i * n_diag + d for d in reversed(range(n_diag))], True, True)

    def body(t, _):
        j = (i - t) * n_diag - 1
        blocks([j - d for d in range(n_diag)], False, False)
        return 0

    lax.fori_loop(0, i, body, 0)
    for h, sl in enumerate(heads):
        o_ref[0, :, sl] = acc_ref[h * SB_TQ:(h + 1) * SB_TQ, :].astype(o_ref.dtype)


def _sb_prompt(proj3, bias_row, uo):
    b, t, _ = proj3.shape
    tq = min(SB_TQ, t)
    assert tq == SB_TQ and t % tq == 0
    return pl.pallas_call(
        _sb_prompt_kernel,
        grid=(b, t // tq),
        in_specs=[pl.BlockSpec((1, tq, W_SB), lambda bi, i: (bi, i, 4)),
                  pl.BlockSpec((1, t, W_SB), lambda bi, i: (bi, 0, 5)),
                  pl.BlockSpec((1, t, W_SB), lambda bi, i: (bi, 0, 6)),
                  pl.BlockSpec((1, W_SB), lambda bi, i: (0, 0)),
                  pl.BlockSpec((2 * SB_BLOCK, 2 * SB_BLOCK), lambda bi, i: (0, 0))],
        out_specs=pl.BlockSpec((1, tq, W_SB), lambda bi, i: (bi, i, 0)),
        out_shape=jax.ShapeDtypeStruct((b, t, W_SB), BF16),
        scratch_shapes=[pltpu.VMEM((H_SB * tq, HD_SB), F32),
                        pltpu.VMEM((H_SB * tq, SB_BLOCK), F32)],
        compiler_params=_params("parallel", "arbitrary"),
        name="sb_prompt",
    )(proj3, proj3, proj3, bias_row, uo)


def _row_select(parts):
    r = lax.broadcasted_iota(jnp.int32, parts[0].shape, 0)
    out = jnp.zeros_like(parts[0])
    for h, p in enumerate(parts):
        out = jnp.where(r == h, p, out)
    return out


def _bcast_rows_b16(row):
    return jnp.broadcast_to(row, (MXU_ROWS, row.shape[-1])).astype(BF16)


def _sb_sample_kernel(pt_ref, q_ref, bias_ref, uo_ref, *refs):
    del pt_ref
    k_refs = refs[:N_PAGES]
    v_refs = refs[N_PAGES:2 * N_PAGES]
    o_ref = refs[2 * N_PAGES]
    uo = uo_ref[...]
    n_col = PAGE_SIZE * H_SB
    n_grp = n_col // SB_BLOCK
    q4 = _row_select([jnp.broadcast_to(q_ref[0, :, h * HD_SB:(h + 1) * HD_SB], (SUBLANES, HD_SB))
                      for h in range(H_SB)])
    qb = jnp.concatenate([q4, q4], axis=0).astype(BF16)
    row = lax.broadcasted_iota(jnp.int32, (SUBLANES, n_col), 0)
    col = lax.broadcasted_iota(jnp.int32, (SUBLANES, n_col), 1)
    valid = (col & (H_SB - 1)) == row
    bias = bias_ref[...]
    zs, sps, his, los = [], [], [], []
    for p in range(N_PAGES):
        z = _dot_nt(qb, k_refs[p][...].astype(BF16))[:SUBLANES] * (HD_SB ** -0.5) + bias
        sp = _softplus(z)
        hi, lo = _split_hi_lo(jnp.where(valid, -sp, 0.0))
        zs.append(z)
        sps.append(sp)
        his += [hi[:, g * SB_BLOCK:(g + 1) * SB_BLOCK] for g in range(n_grp)]
        los += [lo[:, g * SB_BLOCK:(g + 1) * SB_BLOCK] for g in range(n_grp)]
    n_all = N_PAGES * n_grp
    r = _dot(jnp.concatenate([jnp.concatenate(his, axis=0), jnp.concatenate(los, axis=0)],
                             axis=1), uo)
    carry = jnp.zeros((SUBLANES, SB_BLOCK), F32)
    after = [None] * n_all
    for g in reversed(range(n_all)):
        rows = slice(g * SUBLANES, (g + 1) * SUBLANES)
        after[g] = r[rows, :SB_BLOCK] + carry
        carry = carry + r[rows, SB_BLOCK:]
    acc = jnp.zeros((MXU_ROWS, HD_SB), F32)
    for p in range(N_PAGES):
        aft = jnp.concatenate(after[p * n_grp:(p + 1) * n_grp], axis=1)
        a = jnp.where(valid, jnp.exp(zs[p] - sps[p] + aft), 0.0)
        ab = jnp.concatenate([a, a], axis=0).astype(BF16)
        acc += _dot(ab, v_refs[p][...].astype(BF16))
    o_ref[0] = acc[:H_SB].astype(o_ref.dtype)


def _sb_sample(proj3, bias_rows, uo, cache_k, cache_v, page_table, e):
    bs = proj3.shape[0]

    def page(p):
        return pl.BlockSpec((None, None, PAGE_SIZE * H_SB, HD_SB),
                            lambda b, pt: (e, pt[b, p], 0, 0))

    grid_spec = pltpu.PrefetchScalarGridSpec(
        num_scalar_prefetch=1,
        grid=(bs,),
        in_specs=[pl.BlockSpec((1, 1, W_SB), lambda b, pt: (b, 0, 4)),
                  pl.BlockSpec((SUBLANES, PAGE_SIZE * H_SB), lambda b, pt: (0, 0)),
                  pl.BlockSpec((2 * SB_BLOCK, 2 * SB_BLOCK), lambda b, pt: (0, 0))]
                 + [page(p) for p in range(N_PAGES)] + [page(p) for p in range(N_PAGES)],
        out_specs=pl.BlockSpec((1, H_SB, HD_SB), lambda b, pt: (b, 0, 0)),
    )
    return pl.pallas_call(
        _sb_sample_kernel,
        grid_spec=grid_spec,
        out_shape=jax.ShapeDtypeStruct((bs, H_SB, HD_SB), F32),
        compiler_params=_params("arbitrary"),
        name="sb_sample",
    )(page_table, proj3, bias_rows, uo, *([cache_k] * N_PAGES), *([cache_v] * N_PAGES))


def _softmax_rows(s):
    m = jnp.max(s, axis=-1, keepdims=True)
    p = jnp.exp(s - m)
    return p / jnp.sum(p, axis=-1, keepdims=True)


def _xattn_prompt_kernel(x_ref, g_ref, wq_ref, mk_ref, mv_ref, wo_ref, o_ref, att_ref):
    x = x_ref[0]
    hn = _rms(x, g_ref[...]).astype(BF16)
    q = _dot(hn, wq_ref[...]).astype(BF16)
    for h in range(H_X):
        sl = slice(h * HD_X, (h + 1) * HD_X)
        s = _dot_nt(q[:, sl], mk_ref[0, :, sl].astype(BF16)) * (HD_X ** -0.5)
        p = _softmax_rows(s)
        att_ref[:, sl] = _dot(p.astype(BF16), mv_ref[0, :, sl].astype(BF16)).astype(BF16)
    o_ref[0] = x + _dot(att_ref[...], wo_ref[...])


def _xattn_prompt(x3, g, wq, mk3, mv3, wo, tm=512):
    b, t, d = x3.shape
    tm = min(tm, t)
    xs = pl.BlockSpec((1, tm, d), lambda bi, i: (bi, i, 0))
    ws = pl.BlockSpec((d, d), lambda bi, i: (0, 0))
    ms = pl.BlockSpec((1, N_MEM, d), lambda bi, i: (bi, 0, 0))
    return pl.pallas_call(
        _xattn_prompt_kernel,
        grid=(b, t // tm),
        in_specs=[xs, pl.BlockSpec((1, d), lambda bi, i: (0, 0)), ws, ms, ms, ws],
        out_specs=xs,
        out_shape=jax.ShapeDtypeStruct((b, t, d), F32),
        scratch_shapes=[pltpu.VMEM((tm, d), BF16)],
        compiler_params=_params("parallel", "arbitrary"),
        name="xattn_prompt",
    )(x3, g.reshape(1, d), wq, mk3, mv3, wo)


XS_HALVES = HD_X // LANES
XS_ROWS = XS_HALVES * H_X
XS_BB = 4


def _xattn_sample_kernel(q_ref, mk_ref, mv_ref, o_ref):
    n_col = N_MEM * XS_ROWS
    row = lax.broadcasted_iota(jnp.int32, (SUBLANES, n_col), 0)
    col = lax.broadcasted_iota(jnp.int32, (SUBLANES, n_col), 1)
    valid = (col & (XS_ROWS - 1)) == row
    for i in range(q_ref.shape[0]):
        q8 = _row_select([jnp.broadcast_to(q_ref[i, :, (a % H_X) * HD_X + (a // H_X) * LANES:
                                                 (a % H_X) * HD_X + (a // H_X + 1) * LANES],
                                           (SUBLANES, LANES)) for a in range(XS_ROWS)])
        qb = jnp.concatenate([q8, q8], axis=0).astype(BF16)
        z = jnp.where(valid, _dot_nt(qb, mk_ref[i].astype(BF16))[:SUBLANES], 0.0)
        zr = pltpu.roll(z, H_X, axis=0)
        other = jnp.where(row < H_X, pltpu.roll(zr, n_col - H_X, axis=1),
                          pltpu.roll(zr, H_X, axis=1))
        s = (z + other) * (HD_X ** -0.5)
        m = jnp.max(jnp.where(valid, s, -jnp.inf), axis=-1, keepdims=True)
        p = jnp.where(valid, jnp.exp(s - m), 0.0)
        p = p / jnp.sum(p, axis=-1, keepdims=True)
        pb = jnp.concatenate([p, p], axis=0).astype(BF16)
        o_ref[i] = _dot(pb, mv_ref[i].astype(BF16))[:SUBLANES].astype(o_ref.dtype)


def _xattn_cache_view(cache):
    dp, b = cache.shape[:2]
    c = cache.reshape(dp, b, N_MEM, H_X, XS_HALVES, LANES)
    return jnp.swapaxes(c, 3, 4).reshape(dp, b, N_MEM * XS_ROWS, LANES)


def _xattn_sample(q3, mem_k, mem_v, l):
    bs, _, d = q3.shape
    bb = XS_BB
    ms = pl.BlockSpec((None, bb, N_MEM * XS_ROWS, LANES), lambda b: (l, b, 0, 0))
    o = pl.pallas_call(
        _xattn_sample_kernel,
        grid=(bs // bb,),
        in_specs=[pl.BlockSpec((bb, 1, d), lambda b: (b, 0, 0)), ms, ms],
        out_specs=pl.BlockSpec((bb, XS_ROWS, LANES), lambda b: (b, 0, 0)),
        out_shape=jax.ShapeDtypeStruct((bs, XS_ROWS, LANES), F32),
        compiler_params=_params("parallel"),
        name="xattn_sample",
    )(q3, mem_k, mem_v)
    return jnp.swapaxes(o.reshape(bs, XS_HALVES, H_X, LANES), 1, 2).reshape(bs, d)


FFN_TF = 256
HALO = SUBLANES


FFN_ROW_CHUNKS = 8


def _ffn_prompt_kernel(x_ref, g_ref, wg_ref, wv_ref, dwg_ref, dwv_ref, bg_ref, bv_ref, wd_ref,
                       o_ref, hn_ref, tail_ref, *u_refs):
    t = pl.program_id(1)
    f = pl.program_id(2)
    tm = x_ref.shape[1]
    tf = wg_ref.shape[1]
    rc = tm // len(u_refs)

    @pl.when(f == 0)
    def _():
        hn_ref[...] = _rms(x_ref[0], g_ref[...]).astype(BF16)
        o_ref[0] = x_ref[0]

    def conv(u_ref, dw_ref, b_ref, sl):
        cur = u_ref[HALO:, sl]
        above = u_ref[:HALO, sl]
        row = lax.broadcasted_iota(jnp.int32, (SUBLANES, cur.shape[1]), 0)
        c = b_ref[...] + dw_ref[FFN_CONV_W - 1:FFN_CONV_W, :] * cur
        for w in range(FFN_CONV_W - 1):
            d = FFN_CONV_W - 1 - w
            rolled = pltpu.roll(cur, d, axis=0)
            head = jnp.where(row < d, pltpu.roll(above, d, axis=0), rolled[:SUBLANES])
            shifted = jnp.concatenate([head, rolled[SUBLANES:]], axis=0)
            c += dw_ref[w:w + 1, :] * shifted
        return c

    def up(k):
        hn = hn_ref[k * rc:(k + 1) * rc, :]
        u_refs[k][HALO:, :tf] = _dot(hn, wg_ref[...])
        u_refs[k][HALO:, tf:] = _dot(hn, wv_ref[...])

    up(0)
    for k, u_ref in enumerate(u_refs):
        if k + 1 < len(u_refs):
            up(k + 1)
        if k == 0:
            u_ref[:HALO, :] = jnp.where(t == 0, 0.0, tail_ref[f])
        else:
            u_ref[:HALO, :] = u_refs[k - 1][rc:, :]
        gated = (_silu(conv(u_ref, dwg_ref, bg_ref, slice(0, tf)))
                 * conv(u_ref, dwv_ref, bv_ref, slice(tf, 2 * tf)))
        o_ref[0, k * rc:(k + 1) * rc, :] += _dot(gated.astype(BF16), wd_ref[...])
    tail_ref[f] = u_refs[-1][rc:, :]


def _ffn_prompt(x3, g, w_up, dw, dwb, w_down, tm=2048):
    b, t, d = x3.shape
    tm = min(tm, t)
    nf = D_FF // FFN_TF
    assert D_FF % FFN_TF == 0 and t % tm == 0
    xs = pl.BlockSpec((1, tm, d), lambda bi, i, f: (bi, i, 0))
    dwb2 = dwb.reshape(1, -1)
    return pl.pallas_call(
        _ffn_prompt_kernel,
        grid=(b, t // tm, nf),
        in_specs=[xs, pl.BlockSpec((1, d), lambda bi, i, f: (0, 0)),
                  pl.BlockSpec((d, FFN_TF), lambda bi, i, f: (0, f)),
                  pl.BlockSpec((d, FFN_TF), lambda bi, i, f: (0, f + nf)),
                  pl.BlockSpec((FFN_CONV_W, FFN_TF), lambda bi, i, f: (0, f)),
                  pl.BlockSpec((FFN_CONV_W, FFN_TF), lambda bi, i, f: (0, f + nf)),
                  pl.BlockSpec((1, FFN_TF), lambda bi, i, f: (0, f)),
                  pl.BlockSpec((1, FFN_TF), lambda bi, i, f: (0, f + nf)),
                  pl.BlockSpec((FFN_TF, d), lambda bi, i, f: (f, 0))],
        out_specs=xs,
        out_shape=jax.ShapeDtypeStruct((b, t, d), F32),
        scratch_shapes=[pltpu.VMEM((tm, d), BF16), pltpu.VMEM((nf, HALO, 2 * FFN_TF), F32)]
                       + [pltpu.VMEM((HALO + tm // FFN_ROW_CHUNKS, 2 * FFN_TF), F32)] * FFN_ROW_CHUNKS,
        compiler_params=_params("parallel", "arbitrary", "arbitrary"),
        name="ffn_prompt",
    )(x3, g.reshape(1, d), w_up, w_up, dw, dw, dwb2, dwb2, w_down)


CONV_HALO = 32
CONV_RB = 32


def _conv_prompt_kernel(u_ref, x_ref, dw_ref, dwb_ref, lg_ref, lb_ref, w2_ref, b2_ref,
                        o_ref, s_ref, c_ref, dwt_ref):
    t = pl.program_id(1)
    tm = u_ref.shape[1]
    n = CONV_HALO + tm

    @pl.when(t == 0)
    def _():
        s_ref[0, :CONV_HALO, :] = jnp.zeros((CONV_HALO, D_MODEL), F32)

    @pl.when(t > 0)
    def _():
        s_ref[0, :CONV_HALO, :] = s_ref[0, tm:, :]

    s_ref[0, CONV_HALO:, :] = u_ref[0]
    for r in range(1, SUBLANES):
        s_ref[r] = pltpu.roll(s_ref[0], n - r, axis=0)
    base = CONV_HALO - (CONV_W - 1)

    @pl.when(t == 0)
    def _():
        for w in range(CONV_W):
            dwt_ref[w] = jnp.broadcast_to(dw_ref[w:w + 1, :], (SUBLANES, D_MODEL))

    n_sub = CONV_RB // SUBLANES

    def row_block(rb, carry):
        r0 = rb * CONV_RB
        bias = jnp.broadcast_to(dwb_ref[...], (SUBLANES, D_MODEL))
        accs = [bias] * n_sub
        for w in range(CONV_W):
            r = (base + w) % SUBLANES
            dwt = dwt_ref[w]
            for k in range(n_sub):
                start = pl.multiple_of(r0 + (base + w - r) + k * SUBLANES, SUBLANES)
                accs[k] = accs[k] + dwt * s_ref[r, pl.ds(start, SUBLANES), :]
        for k in range(n_sub):
            c_ref[pl.ds(pl.multiple_of(r0 + k * SUBLANES, SUBLANES), SUBLANES), :] = accs[k]
        return carry

    lax.fori_loop(0, tm // CONV_RB, row_block, 0)
    c = c_ref[...]
    mu = jnp.mean(c, axis=-1, keepdims=True)
    cc = c - mu
    var = jnp.mean(cc * cc, axis=-1, keepdims=True)
    y = _silu(cc * lax.rsqrt(var + NORM_EPS) * lg_ref[...] + lb_ref[...])
    o_ref[0] = x_ref[0] + _dot(y.astype(BF16), w2_ref[...]) + b2_ref[...]


def _conv_prompt(u3, x3, dw, dwb, ln_g, ln_b, w2, b2, tm=512):
    b, t, d = x3.shape
    tm = min(tm, t)
    xs = pl.BlockSpec((1, tm, d), lambda bi, i: (bi, i, 0))
    vec = pl.BlockSpec((1, d), lambda bi, i: (0, 0))
    return pl.pallas_call(
        _conv_prompt_kernel,
        grid=(b, t // tm),
        in_specs=[xs, xs, pl.BlockSpec((CONV_W, d), lambda bi, i: (0, 0)), vec, vec, vec,
                  pl.BlockSpec((d, d), lambda bi, i: (0, 0)), vec],
        out_specs=xs,
        out_shape=jax.ShapeDtypeStruct((b, t, d), F32),
        scratch_shapes=[pltpu.VMEM((SUBLANES, CONV_HALO + tm, d), F32), pltpu.VMEM((tm, d), F32),
                        pltpu.VMEM((CONV_W, SUBLANES, d), F32)],
        compiler_params=_params("parallel", "arbitrary"),
        name="conv_prompt",
    )(u3, x3, dw, dwb.reshape(1, d), ln_g.reshape(1, d), ln_b.reshape(1, d), w2, b2.reshape(1, d))


STEP_BB = 8


def _conv_step_kernel(st_ref, u_ref, dw_ref, dwb_ref, o_ref):
    w_taps = dw_ref.shape[0]
    for r in range(STEP_BB):
        acc = dwb_ref[...] + dw_ref[w_taps - 1:w_taps, :] * u_ref[r:r + 1, :]
        for w in range(w_taps - 1):
            acc += dw_ref[w:w + 1, :] * st_ref[r, w:w + 1, :]
        o_ref[r:r + 1, :] = acc


def _conv_step(state, l, u, dw, dwb):
    bs, c = u.shape
    w_taps = dw.shape[0]
    return pl.pallas_call(
        _conv_step_kernel,
        grid=(bs // STEP_BB,),
        in_specs=[pl.BlockSpec((None, STEP_BB, w_taps - 1, c), lambda i: (l, i, 0, 0)),
                  pl.BlockSpec((STEP_BB, c), lambda i: (i, 0)),
                  pl.BlockSpec((w_taps, c), lambda i: (0, 0)),
                  pl.BlockSpec((1, c), lambda i: (0, 0))],
        out_specs=pl.BlockSpec((STEP_BB, c), lambda i: (i, 0)),
        out_shape=jax.ShapeDtypeStruct((bs, c), F32),
        compiler_params=_params("parallel"),
        name="conv_step",
    )(state, u, dw, dwb.reshape(1, c))


def _stack_rows_kernel(*refs, n_layers, col_starts):
    srcs, dst = refs[:n_layers], refs[n_layers]
    rows = srcs[0].shape[1]
    layer = pl.program_id(0)
    for n, src in enumerate(srcs):
        @pl.when(layer == n)
        def _(src=src):
            for a, c0 in enumerate(col_starts):
                dst[pl.ds(a, rows, stride=len(col_starts)), :] = src[0, :, c0:c0 + LANES]


def _stack_rows(srcs, col_block, width, col_starts, tm):
    n_layers = len(srcs)
    b, t, _ = srcs[0].shape
    tm = min(tm, t)
    nt = t // tm
    n_phase = len(col_starts)

    def src_spec(n):
        def idx(l, bi, i):
            on = l == n
            return (jnp.where(on, bi, 0), jnp.where(on, i, 0), col_block)
        return pl.BlockSpec((1, tm, width), idx)

    return pl.pallas_call(
        functools.partial(_stack_rows_kernel, n_layers=n_layers, col_starts=tuple(col_starts)),
        grid=(n_layers, b, nt),
        in_specs=[src_spec(n) for n in range(n_layers)],
        out_specs=pl.BlockSpec((None, None, tm * n_phase, LANES), lambda l, bi, i: (l, bi, i, 0)),
        out_shape=jax.ShapeDtypeStruct((n_layers, b, t * n_phase, LANES), F32),
        compiler_params=_params("arbitrary", "arbitrary", "arbitrary"),
        name="stack_rows",
    )(*srcs)


def kernel(x_prompt, x_sample, cache_sb_k, cache_sb_v, state_ret, state_conv, state_ffn_conv, cache_mem_k, cache_mem_v, page_table, mem_prompt, g_mix, w_in_ab, ret_gn_g, w_out_ab, sb_bias, cv_w1, cv_b1, cv_dw, cv_dwb, cv_ln_g, cv_ln_b, cv_w2, cv_b2, g_cross, xa_wq, xa_wk, xa_wv, xa_wo, g_ffn, ffn_w_up, ffn_dw, ffn_dwb, ffn_w_down, g_final):
    bp, t, d = x_prompt.shape
    bs = x_sample.shape[0]
    n_phys = cache_sb_k.shape[1]

    log_gamma = jnp.log1p(-jnp.exp2(-5.0 - jnp.arange(H_RET, dtype=F32)))
    ret_tabs = _ret_tables(log_gamma)
    cos_p, sin_p = _rope_tables(jnp.arange(t))
    cos_s, sin_s = _rope_tables(PAST_LEN + jnp.arange(1))
    uo = _suffix_matrix()
    cache_k = cache_sb_k.reshape(cache_sb_k.shape[0], n_phys, PAGE_SIZE * H_SB, HD_SB)
    cache_v = cache_sb_v.reshape(cache_sb_v.shape[0], n_phys, PAGE_SIZE * H_SB, HD_SB)
    mem_k = _xattn_cache_view(cache_mem_k)
    mem_v = _xattn_cache_view(cache_mem_v)
    mem2 = mem_prompt.reshape(bp * N_MEM, d)

    xp = x_prompt.reshape(bp * t, d)
    xs = x_sample.reshape(bs, d)
    sbk_p, sbv_p, sbk_s, sbv_s, ret_p, ret_s = [], [], [], [], [], []
    cv_p, cv_s, ff_p, ff_s, mk_p, mv_p = [], [], [], [], [], []

    for l in range(DEPTH):
        if l % 2 == 0:
            e = l // 2
            w_in = w_in_ab[e].astype(BF16)
            w_out = w_out_ab[e].astype(BF16)
            w_out_parts = [w_out[:W_RK], w_out[W_RK:]]
            bias = sb_bias[e].astype(F32)
            bias_row = jnp.repeat(bias, HD_SB).reshape(1, W_SB)
            bias_rows = jnp.zeros((SUBLANES, PAGE_SIZE * H_SB), F32).at[:H_SB].set(
                jnp.broadcast_to(bias[:, None], (H_SB, PAGE_SIZE * H_SB)))
            proj = _mm([xp], [w_in], prologue="rms", pro=(g_mix[l],), tm=512,
                       tn=w_in.shape[1], name="proj_in")
            proj3 = proj.reshape(bp, t, -1)
            ret_o, s_p = _ret_prompt(proj3, cos_p, sin_p, ret_tabs, ret_gn_g[e])
            sb_o = _sb_prompt(proj3, bias_row, uo)
            xp = _mm([ret_o.reshape(bp * t, W_RK), sb_o.reshape(bp * t, W_SB)], w_out_parts,
                     res=xp, tm=1024, tn=d, name="proj_out")
            sbk_p.append(proj3)
            ret_p.append(s_p)
            proj_s = _mm([xs], [w_in], prologue="rms", pro=(g_mix[l],), name="proj_in_s")
            ret_os, s_s = _ret_sample(proj_s, cos_s, sin_s, ret_gn_g[e], state_ret, e)
            sb_os = _sb_sample(proj_s.reshape(bs, 1, -1), bias_rows, uo, cache_k, cache_v,
                               page_table, e)
            xs = _mm([ret_os, sb_os.reshape(bs, W_SB)], w_out_parts, res=xs, tn=d,
                     name="proj_out_s")
            sbk_s.append(proj_s[:, 5 * W_SB:6 * W_SB].reshape(bs, 1, H_SB, HD_SB))
            sbv_s.append(proj_s[:, 6 * W_SB:7 * W_SB].reshape(bs, 1, H_SB, HD_SB))
            ret_s.append(s_s)
        else:
            o = l // 2
            w1 = cv_w1[o].astype(BF16)
            w2 = cv_w2[o].astype(BF16)
            u = _mm([xp], [w1], prologue="rms", pro=(g_mix[l],), bias=cv_b1[o], glu=True,
                    tm=512, tn=d, name="conv_glu")
            u3 = u.reshape(bp, t, d)
            xp = _conv_prompt(u3, xp.reshape(bp, t, d), cv_dw[o], cv_dwb[o], cv_ln_g[o],
                              cv_ln_b[o], w2, cv_b2[o]).reshape(bp * t, d)
            cv_p.append(u3[:, t - (CONV_W - 1):, :])
            u_s = _mm([xs], [w1], prologue="rms", pro=(g_mix[l],), bias=cv_b1[o], glu=True,
                      name="conv_glu_s")
            c_s = _conv_step(state_conv, o, u_s, cv_dw[o], cv_dwb[o])
            xs = _mm([c_s], [w2], prologue="ln_silu", pro=(cv_ln_g[o], cv_ln_b[o]), bias=cv_b2[o],
                     res=xs, tn=d, name="conv_out_s")
            cv_s.append(jnp.concatenate([state_conv[o][:, 1:], u_s[:, None, :]], axis=1))

        wq = xa_wq[l].astype(BF16)
        wo = xa_wo[l].astype(BF16)
        mk = _mm([mem2], [xa_wk[l].astype(BF16)], tn=d, name="mem_k")
        mv = _mm([mem2], [xa_wv[l].astype(BF16)], tn=d, name="mem_v")
        mk_p.append(mk.reshape(bp, N_MEM, d))
        mv_p.append(mv.reshape(bp, N_MEM, d))
        xp = _xattn_prompt(xp.reshape(bp, t, d), g_cross[l], wq, mk.reshape(bp, N_MEM, d),
                           mv.reshape(bp, N_MEM, d), wo)
        q_s = _mm([xs], [wq], prologue="rms", pro=(g_cross[l],), tn=d, name="xattn_q_s")
        att_s = _xattn_sample(q_s.reshape(bs, 1, d), mem_k, mem_v, l)
        xs = _mm([att_s], [wo], res=xs, tn=d, name="xattn_out_s")

        w_up = ffn_w_up[l].astype(BF16)
        w_down = ffn_w_down[l].astype(BF16)
        tail_rows = xp[:, t - (FFN_CONV_W - 1):, :].reshape(bp * (FFN_CONV_W - 1), d)
        ff_p.append(_mm([tail_rows], [w_up], prologue="rms", pro=(g_ffn[l],),
                        name="ffn_tail").reshape(bp, FFN_CONV_W - 1, 2 * D_FF))
        xp = _ffn_prompt(xp, g_ffn[l], w_up, ffn_dw[l], ffn_dwb[l], w_down).reshape(bp * t, d)
        u_s = _mm([xs], [w_up], prologue="rms", pro=(g_ffn[l],), name="ffn_up_s")
        c_s = _conv_step(state_ffn_conv, l, u_s, ffn_dw[l], ffn_dwb[l])
        xs = _mm([c_s], [w_down], prologue="swiglu", res=xs, tn=d, name="ffn_down_s")
        ff_s.append(jnp.concatenate([state_ffn_conv[l][:, 1:], u_s[:, None, :]], axis=1))

    y_prompt = _rmsnorm(xp, g_final).reshape(bp, t, d)
    y_sample = _rmsnorm(xs, g_final).reshape(bs, 1, d)
    sb_cols = [h * HD_SB for h in range(H_SB)]
    sb_shape = (len(sbk_p), bp, t, H_SB, HD_SB)
    sb_k_prompt = _stack_rows(sbk_p, 5, W_SB, sb_cols, 512).reshape(sb_shape)
    sb_v_prompt = _stack_rows(sbk_p, 6, W_SB, sb_cols, 512).reshape(sb_shape)
    mem_cols = [(a % H_X) * HD_X + (a // H_X) * LANES for a in range(XS_ROWS)]

    def mem_out(parts):
        o = _stack_rows(parts, 0, d, mem_cols, N_MEM)
        o = o.reshape(DEPTH, bp, N_MEM, XS_HALVES, H_X, LANES)
        return jnp.swapaxes(o, 3, 4).reshape(DEPTH, bp, N_MEM, H_X, HD_X)

    return (y_prompt, y_sample, sb_k_prompt, sb_v_prompt, jnp.stack(sbk_s),
            jnp.stack(sbv_s), jnp.stack(ret_p), jnp.stack(ret_s), jnp.stack(cv_p),
            jnp.stack(cv_s), jnp.stack(ff_p), jnp.stack(ff_s), mem_out(mk_p), mem_out(mv_p))
```

```python
import functools

import numpy as np
import jax
import jax.numpy as jnp
from jax import lax
from jax.experimental import pallas as pl
from jax.experimental.pallas import tpu as pltpu

F32 = jnp.float32
BF16 = jnp.bfloat16

D_MODEL = 1024
DEPTH = 4
PAST_LEN = 2048
PAGE_SIZE = 128
N_PAGES = PAST_LEN // PAGE_SIZE
H_RET = 4
DK_RET = 128
DV_RET = 128
RET_CHUNK = 128
ROPE_BASE = 10000.0
H_SB = 4
HD_SB = 128
SB_BLOCK = 128
CONV_W = 31
D_FF = 2816
FFN_CONV_W = 3
N_MEM = 256
H_X = 4
HD_X = D_MODEL // H_X
NORM_EPS = 1e-6
W_RK = H_RET * DK_RET
W_SB = H_SB * HD_SB

SUBLANES = 8
LANES = 128
VMEM_LIMIT_BYTES = 56 * 1024 * 1024


def _params(*sem):
    return pltpu.CompilerParams(dimension_semantics=sem, vmem_limit_bytes=VMEM_LIMIT_BYTES)


def _dot(a, b):
    return jnp.dot(a, b, preferred_element_type=F32)


def _dot_nt(a, b):
    return lax.dot_general(a, b, (((1,), (1,)), ((), ())), preferred_element_type=F32)


def _rms(x, g):
    return x * lax.rsqrt(jnp.mean(x * x, axis=-1, keepdims=True) + NORM_EPS) * g


def _silu(x):
    return x * jax.nn.sigmoid(x)


def _softplus(z):
    return jnp.maximum(z, 0.0) + jnp.log(1.0 + jnp.exp(-jnp.abs(z)))


def _split_hi_lo(x):
    hi = x.astype(BF16)
    lo = (x - hi.astype(F32)).astype(BF16)
    return hi, lo


def _mm_kernel(*refs, n_lhs, prologue, has_bias, has_res, glu):
    it = iter(refs)
    x_refs = [next(it) for _ in range(n_lhs)]
    w_refs = [next(it) for _ in range(n_lhs)]
    wg_refs = [next(it) for _ in range(n_lhs)] if glu else []
    n_pro = {None: 0, "rms": 1, "ln_silu": 2, "swiglu": 0}[prologue]
    p_refs = [next(it) for _ in range(n_pro)]
    b_ref = next(it) if has_bias else None
    bg_ref = next(it) if (has_bias and glu) else None
    r_ref = next(it) if has_res else None
    o_ref = next(it)
    xn_ref = next(it) if prologue else None

    if prologue:
        @pl.when(pl.program_id(1) == 0)
        def _():
            x = x_refs[0][...].astype(F32)
            if prologue == "rms":
                y = _rms(x, p_refs[0][...])
            elif prologue == "ln_silu":
                mu = jnp.mean(x, axis=-1, keepdims=True)
                xc = x - mu
                var = jnp.mean(xc * xc, axis=-1, keepdims=True)
                y = _silu(xc * lax.rsqrt(var + NORM_EPS) * p_refs[0][...] + p_refs[1][...])
            else:
                k = x.shape[-1] // 2
                y = _silu(x[:, :k]) * x[:, k:]
            xn_ref[...] = y.astype(BF16)
        lhs = [xn_ref[...]]
    else:
        lhs = [r[...].astype(BF16) for r in x_refs]

    acc = _dot(lhs[0], w_refs[0][...])
    for a, w in zip(lhs[1:], w_refs[1:]):
        acc += _dot(a, w[...])
    if has_bias:
        acc += b_ref[...]
    if glu:
        gate = _dot(lhs[0], wg_refs[0][...])
        for a, w in zip(lhs[1:], wg_refs[1:]):
            gate += _dot(a, w[...])
        if has_bias:
            gate += bg_ref[...]
        acc = acc * jax.nn.sigmoid(gate)
    if has_res:
        acc += r_ref[...]
    o_ref[...] = acc.astype(o_ref.dtype)


def _mm(xs, ws, *, prologue=None, pro=(), bias=None, res=None, glu=False, tm=512, tn=512,
        out_dtype=F32, name="mm"):
    m = xs[0].shape[0]
    n = ws[0].shape[1] // (2 if glu else 1)
    tm = min(tm, m)
    tn = min(tn, n)
    assert m % tm == 0 and n % tn == 0, (m, tm, n, tn)
    nj = n // tn
    args, specs = [], []
    for x in xs:
        args.append(x)
        specs.append(pl.BlockSpec((tm, x.shape[1]), lambda i, j: (i, 0)))
    for w in ws:
        args.append(w)
        specs.append(pl.BlockSpec((w.shape[0], tn), lambda i, j: (0, j)))
    if glu:
        for w in ws:
            args.append(w)
            specs.append(pl.BlockSpec((w.shape[0], tn), lambda i, j: (0, j + nj)))
    for p in pro:
        args.append(p.reshape(1, -1))
        specs.append(pl.BlockSpec((1, p.size), lambda i, j: (0, 0)))
    if bias is not None:
        b2 = bias.reshape(1, -1)
        args.append(b2)
        specs.append(pl.BlockSpec((1, tn), lambda i, j: (0, j)))
        if glu:
            args.append(b2)
            specs.append(pl.BlockSpec((1, tn), lambda i, j: (0, j + nj)))
    if res is not None:
        args.append(res)
        specs.append(pl.BlockSpec((tm, tn), lambda i, j: (i, j)))
    scratch = []
    if prologue:
        k_eff = xs[0].shape[1] // (2 if prologue == "swiglu" else 1)
        scratch.append(pltpu.VMEM((tm, k_eff), BF16))
    kern = functools.partial(_mm_kernel, n_lhs=len(xs), prologue=prologue,
                             has_bias=bias is not None, has_res=res is not None, glu=glu)
    return pl.pallas_call(
        kern,
        grid=(m // tm, nj),
        in_specs=specs,
        out_specs=pl.BlockSpec((tm, tn), lambda i, j: (i, j)),
        out_shape=jax.ShapeDtypeStruct((m, n), out_dtype),
        scratch_shapes=scratch,
        compiler_params=_params("parallel", "arbitrary"),
        name=name,
    )(*args)


def _rmsnorm_kernel(x_ref, g_ref, o_ref):
    o_ref[...] = _rms(x_ref[...], g_ref[...])


def _rmsnorm(x, g, tm=1024):
    m, d = x.shape
    tm = min(tm, m)
    return pl.pallas_call(
        _rmsnorm_kernel,
        grid=(m // tm,),
        in_specs=[pl.BlockSpec((tm, d), lambda i: (i, 0)), pl.BlockSpec((1, d), lambda i: (0, 0))],
        out_specs=pl.BlockSpec((tm, d), lambda i: (i, 0)),
        out_shape=jax.ShapeDtypeStruct((m, d), F32),
        compiler_params=_params("parallel"),
        name="rmsnorm",
    )(x, g.reshape(1, d))


def _rotate(x, cos2, sin2):
    return x * cos2 + pltpu.roll(x, DK_RET // 2, axis=1) * sin2


def _head_ln_gate(o, gate, gn):
    mu = jnp.mean(o, axis=-1, keepdims=True)
    oc = o - mu
    var = jnp.mean(oc * oc, axis=-1, keepdims=True)
    return _silu(gate) * (oc * lax.rsqrt(var + NORM_EPS) * gn)


def _ret_prompt_kernel(rq_ref, rk_ref, rv_ref, rg_ref, cos_ref, sin_ref, dmask_ref, qdec_ref,
                       kdec_ref, cdec_ref, gn_ref, o_ref, s_ref):
    c = pl.program_id(1)

    @pl.when(c == 0)
    def _():
        s_ref[...] = jnp.zeros_like(s_ref)

    cos2 = cos_ref[...]
    sin2 = sin_ref[...]
    pairs = [(i, h, slice(h * DK_RET, (h + 1) * DK_RET))
             for i in range(rq_ref.shape[0]) for h in range(H_RET)]
    ks = [_rotate(rk_ref[i, :, sl], cos2, sin2) * (DK_RET ** -0.5) for i, h, sl in pairs]
    qbs = [_rotate(rq_ref[i, :, sl], cos2, sin2).astype(BF16) for i, h, sl in pairs]
    vbs = [rv_ref[i, :, sl].astype(BF16) for i, h, sl in pairs]
    s0s = [s_ref[i, h] for i, h, sl in pairs]
    scores = [(_dot_nt(qb, k.astype(BF16)) * dmask_ref[h]).astype(BF16)
              for (i, h, sl), qb, k in zip(pairs, qbs, ks)]
    inters = [_dot(qb, s0.astype(BF16)) * qdec_ref[h]
              for (i, h, sl), qb, s0 in zip(pairs, qbs, s0s)]
    for n, (i, h, sl) in enumerate(pairs):
        kd_t = (ks[n] * kdec_ref[h]).T.astype(BF16)
        s_ref[i, h] = cdec_ref[h] * s0s[n] + _dot(kd_t, vbs[n])
    for n, (i, h, sl) in enumerate(pairs):
        o = _dot(scores[n], vbs[n]) + inters[n]
        o_ref[i, :, sl] = _head_ln_gate(o, rg_ref[i, :, sl], gn_ref[:, sl]).astype(o_ref.dtype)


def _ret_tables(log_gamma):
    idx = jnp.arange(RET_CHUNK)
    diff = idx[:, None] - idx[None, :]
    expo = jnp.maximum(diff, 0).astype(F32)[None] * log_gamma[:, None, None]
    dmask = jnp.where(diff[None] >= 0, jnp.exp(expo), 0.0)
    q_decay = jnp.exp((idx + 1).astype(F32)[:, None] * log_gamma[None, :])
    k_decay = jnp.exp((RET_CHUNK - 1 - idx).astype(F32)[:, None] * log_gamma[None, :])
    ones = jnp.ones((1, 1, DK_RET), F32)
    qdec = q_decay.T[:, :, None] * ones
    kdec = k_decay.T[:, :, None] * ones
    cdec = jnp.exp(RET_CHUNK * log_gamma)[:, None, None] * jnp.ones((1, DK_RET, DV_RET), F32)
    return dmask, qdec, kdec, cdec


def _rope_tables(pos):
    half = DK_RET // 2
    inv = ROPE_BASE ** (-jnp.arange(half, dtype=F32) / half)
    ang = pos.astype(F32)[:, None] * inv[None, :]
    cos, sin = jnp.cos(ang), jnp.sin(ang)
    return jnp.concatenate([cos, cos], -1), jnp.concatenate([-sin, sin], -1)


RET_PROMPT_BB = 2


def _ret_prompt(proj3, cos2, sin2, tabs, gn):
    b, t, _ = proj3.shape
    nc = t // RET_CHUNK
    dmask, qdec, kdec, cdec = tabs

    bb = RET_PROMPT_BB

    def col(cb):
        return pl.BlockSpec((bb, RET_CHUNK, W_RK), lambda i, c: (i, c, cb))

    tab = pl.BlockSpec((H_RET, RET_CHUNK, RET_CHUNK), lambda i, c: (0, 0, 0))
    return pl.pallas_call(
        _ret_prompt_kernel,
        grid=(b // bb, nc),
        in_specs=[col(0), col(1), col(2), col(3),
                  pl.BlockSpec((RET_CHUNK, DK_RET), lambda i, c: (c, 0)),
                  pl.BlockSpec((RET_CHUNK, DK_RET), lambda i, c: (c, 0)),
                  tab, tab, tab, tab,
                  pl.BlockSpec((1, W_RK), lambda i, c: (0, 0))],
        out_specs=[pl.BlockSpec((bb, RET_CHUNK, W_RK), lambda i, c: (i, c, 0)),
                   pl.BlockSpec((bb, H_RET, DK_RET, DV_RET), lambda i, c: (i, 0, 0, 0))],
        out_shape=[jax.ShapeDtypeStruct((b, t, W_RK), BF16),
                   jax.ShapeDtypeStruct((b, H_RET, DK_RET, DV_RET), F32)],
        compiler_params=_params("parallel", "arbitrary"),
        name="ret_prompt",
    )(proj3, proj3, proj3, proj3, cos2, sin2, dmask, qdec, kdec, cdec, gn.reshape(1, W_RK))


RET_BB = 8
MXU_ROWS = 16


def _ret_sample_kernel(rq_ref, rk_ref, rv_ref, rg_ref, cos_ref, sin_ref, gn_ref, st_ref,
                       o_ref, so_ref, inter_ref, *, g1):
    cos2 = cos_ref[...]
    sin2 = sin_ref[...]
    eye = (lax.broadcasted_iota(jnp.int32, (DK_RET, DK_RET), 0)
           == lax.broadcasted_iota(jnp.int32, (DK_RET, DK_RET), 1))
    for h in range(H_RET):
        sl = slice(h * DK_RET, (h + 1) * DK_RET)
        q = _rotate(rq_ref[:, sl], cos2, sin2)
        k = _rotate(rk_ref[:, sl], cos2, sin2) * (DK_RET ** -0.5)
        v = rv_ref[:, sl]
        for r in range(RET_BB):
            s0 = st_ref[r, h]
            qr = jnp.broadcast_to(q[r:r + 1], (MXU_ROWS, DK_RET)).astype(BF16)
            inter_ref[r:r + 1, :] = _dot(qr, s0.astype(BF16))[0:1] * g1[h]
            diag_k = jnp.where(eye, jnp.broadcast_to(k[r:r + 1], (DK_RET, DK_RET)), 0.0).astype(BF16)
            v_rows = jnp.broadcast_to(v[r:r + 1], (DK_RET, DV_RET)).astype(BF16)
            so_ref[r, h] = g1[h] * s0 + _dot(diag_k, v_rows)
        o = jnp.sum(q * k, axis=-1, keepdims=True) * v + inter_ref[...]
        o_ref[:, sl] = _head_ln_gate(o, rg_ref[:, sl], gn_ref[:, sl]).astype(o_ref.dtype)


def _ret_sample(proj, cos2, sin2, gn, state, e):
    bs = proj.shape[0]
    log_gamma = np.log1p(-np.exp2(-5.0 - np.arange(H_RET, dtype=np.float32)))
    g1 = tuple(float(x) for x in np.exp(log_gamma).astype(np.float32))

    def col(cb):
        return pl.BlockSpec((RET_BB, W_RK), lambda i: (i, cb))

    row = pl.BlockSpec((1, DK_RET), lambda i: (0, 0))
    return pl.pallas_call(
        functools.partial(_ret_sample_kernel, g1=g1),
        grid=(bs // RET_BB,),
        in_specs=[col(0), col(1), col(2), col(3), row, row,
                  pl.BlockSpec((1, W_RK), lambda i: (0, 0)),
                  pl.BlockSpec((None, RET_BB, H_RET, DK_RET, DV_RET), lambda i: (e, i, 0, 0, 0))],
        out_specs=[pl.BlockSpec((RET_BB, W_RK), lambda i: (i, 0)),
                   pl.BlockSpec((RET_BB, H_RET, DK_RET, DV_RET), lambda i: (i, 0, 0, 0))],
        out_shape=[jax.ShapeDtypeStruct((bs, W_RK), BF16),
                   jax.ShapeDtypeStruct((bs, H_RET, DK_RET, DV_RET), F32)],
        scratch_shapes=[pltpu.VMEM((RET_BB, DV_RET), F32)],
        compiler_params=_params("parallel"),
        name="ret_sample",
    )(proj, proj, proj, proj, cos2, sin2, gn.reshape(1, W_RK), state)


def _suffix_matrix():
    j = jnp.arange(SB_BLOCK)
    u = (j[:, None] > j[None, :]).astype(BF16)
    half = jnp.concatenate([u, jnp.ones((SB_BLOCK, SB_BLOCK), BF16)], axis=1)
    return jnp.concatenate([half, half], axis=0)


def _suffix_sums(log_1m, uo):
    hi, lo = _split_hi_lo(log_1m)
    return _dot(jnp.concatenate([hi, lo], axis=1), uo)


SB_TQ = 256


def _sb_prompt_kernel(q_ref, k_ref, v_ref, bias_ref, uo_ref, o_ref, acc_ref, carry_ref):
    i = pl.program_id(1)
    uo = uo_ref[...]
    heads = [slice(h * HD_SB, (h + 1) * HD_SB) for h in range(H_SB)]

    def blocks(js, masked, first):
        starts = [pl.multiple_of(j * SB_BLOCK, SB_BLOCK) for j in js]
        zs = [jnp.concatenate(
            [_dot_nt(q_ref[0, :, sl].astype(BF16), k_ref[0, pl.ds(st, SB_BLOCK), sl].astype(BF16))
             * (HD_SB ** -0.5) + bias_ref[:, sl] for sl in heads], axis=0) for st in starts]
        sps = [_softplus(z) for z in zs]
        valids = []
        if masked:
            for z, st in zip(zs, starts):
                qpos = i * SB_TQ + (lax.broadcasted_iota(jnp.int32, z.shape, 0) & (SB_TQ - 1))
                kpos = st + lax.broadcasted_iota(jnp.int32, z.shape, 1)
                valids.append(kpos < qpos)
            rs = [_suffix_sums(jnp.where(v, -sp, 0.0), uo) for v, sp in zip(valids, sps)]
        else:
            rs = [_suffix_sums(-sp, uo) for sp in sps]
        carry = None if first else carry_ref[...]
        pvs = []
        for n, st in enumerate(starts):
            after = rs[n][:, :SB_BLOCK]
            if carry is not None:
                after = after + carry
            a = jnp.exp(zs[n] - sps[n] + after)
            if masked:
                a = jnp.where(valids[n], a, 0.0)
            ab = a.astype(BF16)
            pvs.append(jnp.concatenate(
                [_dot(ab[h * SB_TQ:(h + 1) * SB_TQ], v_ref[0, pl.ds(st, SB_BLOCK), sl].astype(BF16))
                 for h, sl in enumerate(heads)], axis=0))
            total = rs[n][:, SB_BLOCK:]
            carry = total if carry is None else carry + total
        pv = pvs[0]
        for p in pvs[1:]:
            pv = pv + p
        acc_ref[...] = pv if first else acc_ref[...] + pv
        carry_ref[...] = carry

    n_diag = SB_TQ // SB_BLOCK
    blocks([i * n_diag + d for d in reversed(range(n_diag))], True, True)

    def body(t, _):
        j = (i - t) * n_diag - 1
        blocks([j - d for d in range(n_diag)], False, False)
        return 0

    lax.fori_loop(0, i, body, 0)
    for h, sl in enumerate(heads):
        o_ref[0, :, sl] = acc_ref[h * SB_TQ:(h + 1) * SB_TQ, :].astype(o_ref.dtype)


def _sb_prompt(proj3, bias_row, uo):
    b, t, _ = proj3.shape
    tq = min(SB_TQ, t)
    assert tq == SB_TQ and t % tq == 0
    return pl.pallas_call(
        _sb_prompt_kernel,
        grid=(b, t // tq),
        in_specs=[pl.BlockSpec((1, tq, W_SB), lambda bi, i: (bi, i, 4)),
                  pl.BlockSpec((1, t, W_SB), lambda bi, i: (bi, 0, 5)),
                  pl.BlockSpec((1, t, W_SB), lambda bi, i: (bi, 0, 6)),
                  pl.BlockSpec((1, W_SB), lambda bi, i: (0, 0)),
                  pl.BlockSpec((2 * SB_BLOCK, 2 * SB_BLOCK), lambda bi, i: (0, 0))],
        out_specs=pl.BlockSpec((1, tq, W_SB), lambda bi, i: (bi, i, 0)),
        out_shape=jax.ShapeDtypeStruct((b, t, W_SB), BF16),
        scratch_shapes=[pltpu.VMEM((H_SB * tq, HD_SB), F32),
                        pltpu.VMEM((H_SB * tq, SB_BLOCK), F32)],
        compiler_params=_params("parallel", "arbitrary"),
        name="sb_prompt",
    )(proj3, proj3, proj3, bias_row, uo)


def _row_select(parts):
    r = lax.broadcasted_iota(jnp.int32, parts[0].shape, 0)
    out = jnp.zeros_like(parts[0])
    for h, p in enumerate(parts):
        out = jnp.where(r == h, p, out)
    return out


def _bcast_rows_b16(row):
    return jnp.broadcast_to(row, (MXU_ROWS, row.shape[-1])).astype(BF16)


def _sb_sample_kernel(pt_ref, q_ref, bias_ref, uo_ref, *refs):
    del pt_ref
    k_refs = refs[:N_PAGES]
    v_refs = refs[N_PAGES:2 * N_PAGES]
    o_ref = refs[2 * N_PAGES]
    uo = uo_ref[...]
    n_col = PAGE_SIZE * H_SB
    n_grp = n_col // SB_BLOCK
    q4 = _row_select([jnp.broadcast_to(q_ref[0, :, h * HD_SB:(h + 1) * HD_SB], (SUBLANES, HD_SB))
                      for h in range(H_SB)])
    qb = jnp.concatenate([q4, q4], axis=0).astype(BF16)
    row = lax.broadcasted_iota(jnp.int32, (SUBLANES, n_col), 0)
    col = lax.broadcasted_iota(jnp.int32, (SUBLANES, n_col), 1)
    valid = (col & (H_SB - 1)) == row
    bias = bias_ref[...]
    zs, sps, his, los = [], [], [], []
    for p in range(N_PAGES):
        z = _dot_nt(qb, k_refs[p][...].astype(BF16))[:SUBLANES] * (HD_SB ** -0.5) + bias
        sp = _softplus(z)
        hi, lo = _split_hi_lo(jnp.where(valid, -sp, 0.0))
        zs.append(z)
        sps.append(sp)
        his += [hi[:, g * SB_BLOCK:(g + 1) * SB_BLOCK] for g in range(n_grp)]
        los += [lo[:, g * SB_BLOCK:(g + 1) * SB_BLOCK] for g in range(n_grp)]
    n_all = N_PAGES * n_grp
    r = _dot(jnp.concatenate([jnp.concatenate(his, axis=0), jnp.concatenate(los, axis=0)],
                             axis=1), uo)
    carry = jnp.zeros((SUBLANES, SB_BLOCK), F32)
    after = [None] * n_all
    for g in reversed(range(n_all)):
        rows = slice(g * SUBLANES, (g + 1) * SUBLANES)
        after[g] = r[rows, :SB_BLOCK] + carry
        carry = carry + r[rows, SB_BLOCK:]
    acc = jnp.zeros((MXU_ROWS, HD_SB), F32)
    for p in range(N_PAGES):
        aft = jnp.concatenate(after[p * n_grp:(p + 1) * n_grp], axis=1)
        a = jnp.where(valid, jnp.exp(zs[p] - sps[p] + aft), 0.0)
        ab = jnp.concatenate([a, a], axis=0).astype(BF16)
        acc += _dot(ab, v_refs[p][...].astype(BF16))
    o_ref[0] = acc[:H_SB].astype(o_ref.dtype)


def _sb_sample(proj3, bias_rows, uo, cache_k, cache_v, page_table, e):
    bs = proj3.shape[0]

    def page(p):
        return pl.BlockSpec((None, None, PAGE_SIZE * H_SB, HD_SB),
                            lambda b, pt: (e, pt[b, p], 0, 0))

    grid_spec = pltpu.PrefetchScalarGridSpec(
        num_scalar_prefetch=1,
        grid=(bs,),
        in_specs=[pl.BlockSpec((1, 1, W_SB), lambda b, pt: (b, 0, 4)),
                  pl.BlockSpec((SUBLANES, PAGE_SIZE * H_SB), lambda b, pt: (0, 0)),
                  pl.BlockSpec((2 * SB_BLOCK, 2 * SB_BLOCK), lambda b, pt: (0, 0))]
                 + [page(p) for p in range(N_PAGES)] + [page(p) for p in range(N_PAGES)],
        out_specs=pl.BlockSpec((1, H_SB, HD_SB), lambda b, pt: (b, 0, 0)),
    )
    return pl.pallas_call(
        _sb_sample_kernel,
        grid_spec=grid_spec,
        out_shape=jax.ShapeDtypeStruct((bs, H_SB, HD_SB), F32),
        compiler_params=_params("arbitrary"),
        name="sb_sample",
    )(page_table, proj3, bias_rows, uo, *([cache_k] * N_PAGES), *([cache_v] * N_PAGES))


def _softmax_rows(s):
    m = jnp.max(s, axis=-1, keepdims=True)
    p = jnp.exp(s - m)
    return p / jnp.sum(p, axis=-1, keepdims=True)


XP_ROW_CHUNKS = 2


def _xattn_prompt_kernel(x_ref, g_ref, wq_ref, mk_ref, mv_ref, wo_ref, o_ref, att_ref):
    tm = x_ref.shape[1]
    n_chunk = XP_ROW_CHUNKS if tm % (XP_ROW_CHUNKS * SUBLANES) == 0 else 1
    rc = tm // n_chunk
    heads = [slice(h * HD_X, (h + 1) * HD_X) for h in range(H_X)]
    mkb = [mk_ref[0, :, sl].astype(BF16) for sl in heads]
    mvb = [mv_ref[0, :, sl].astype(BF16) for sl in heads]

    def rows(c):
        return slice(c * rc, (c + 1) * rc)

    def query(c):
        hn = _rms(x_ref[0, rows(c), :], g_ref[...]).astype(BF16)
        return _dot(hn, wq_ref[...]).astype(BF16)

    def scores(q):
        return [_dot_nt(q[:, sl], mkb[h]) * (HD_X ** -0.5) for h, sl in enumerate(heads)]

    s = scores(query(0))
    for c in range(n_chunk):
        q_next = query(c + 1) if c + 1 < n_chunk else None
        for h, sl in enumerate(heads):
            att_ref[rows(c), sl] = _dot(_softmax_rows(s[h]).astype(BF16), mvb[h]).astype(BF16)
        if q_next is not None:
            s = scores(q_next)
        o_ref[0, rows(c), :] = x_ref[0, rows(c), :] + _dot(att_ref[rows(c), :], wo_ref[...])


def _xattn_prompt(x3, g, wq, mem_kv3, k_block, v_block, wo, tm=512):
    b, t, d = x3.shape
    tm = min(tm, t)
    xs = pl.BlockSpec((1, tm, d), lambda bi, i: (bi, i, 0))
    ws = pl.BlockSpec((d, d), lambda bi, i: (0, 0))
    return pl.pallas_call(
        _xattn_prompt_kernel,
        grid=(b, t // tm),
        in_specs=[xs, pl.BlockSpec((1, d), lambda bi, i: (0, 0)), ws,
                  pl.BlockSpec((1, N_MEM, d), lambda bi, i: (bi, 0, k_block)),
                  pl.BlockSpec((1, N_MEM, d), lambda bi, i: (bi, 0, v_block)), ws],
        out_specs=xs,
        out_shape=jax.ShapeDtypeStruct((b, t, d), F32),
        scratch_shapes=[pltpu.VMEM((tm, d), BF16)],
        compiler_params=_params("parallel", "arbitrary"),
        name="xattn_prompt",
    )(x3, g.reshape(1, d), wq, mem_kv3, mem_kv3, wo)


XS_HALVES = HD_X // LANES
XS_ROWS = XS_HALVES * H_X
XS_BB = 4


def _xattn_sample_kernel(q_ref, mk_ref, mv_ref, o_ref):
    n_col = N_MEM * XS_ROWS
    row = lax.broadcasted_iota(jnp.int32, (SUBLANES, n_col), 0)
    col = lax.broadcasted_iota(jnp.int32, (SUBLANES, n_col), 1)
    valid = (col & (XS_ROWS - 1)) == row
    for i in range(q_ref.shape[0]):
        q8 = _row_select([jnp.broadcast_to(q_ref[i, :, (a % H_X) * HD_X + (a // H_X) * LANES:
                                                 (a % H_X) * HD_X + (a // H_X + 1) * LANES],
                                           (SUBLANES, LANES)) for a in range(XS_ROWS)])
        qb = jnp.concatenate([q8, q8], axis=0).astype(BF16)
        z = jnp.where(valid, _dot_nt(qb, mk_ref[i].astype(BF16))[:SUBLANES], 0.0)
        zr = pltpu.roll(z, H_X, axis=0)
        other = jnp.where(row < H_X, pltpu.roll(zr, n_col - H_X, axis=1),
                          pltpu.roll(zr, H_X, axis=1))
        s = (z + other) * (HD_X ** -0.5)
        m = jnp.max(jnp.where(valid, s, -jnp.inf), axis=-1, keepdims=True)
        p = jnp.where(valid, jnp.exp(s - m), 0.0)
        p = p / jnp.sum(p, axis=-1, keepdims=True)
        pb = jnp.concatenate([p, p], axis=0).astype(BF16)
        o_ref[i] = _dot(pb, mv_ref[i].astype(BF16))[:SUBLANES].astype(o_ref.dtype)


def _xattn_cache_view(cache):
    dp, b = cache.shape[:2]
    c = cache.reshape(dp, b, N_MEM, H_X, XS_HALVES, LANES)
    return jnp.swapaxes(c, 3, 4).reshape(dp, b, N_MEM * XS_ROWS, LANES)


def _xattn_sample(q3, mem_k, mem_v, l):
    bs, _, d = q3.shape
    bb = XS_BB
    ms = pl.BlockSpec((None, bb, N_MEM * XS_ROWS, LANES), lambda b: (l, b, 0, 0))
    o = pl.pallas_call(
        _xattn_sample_kernel,
        grid=(bs // bb,),
        in_specs=[pl.BlockSpec((bb, 1, d), lambda b: (b, 0, 0)), ms, ms],
        out_specs=pl.BlockSpec((bb, XS_ROWS, LANES), lambda b: (b, 0, 0)),
        out_shape=jax.ShapeDtypeStruct((bs, XS_ROWS, LANES), F32),
        compiler_params=_params("parallel"),
        name="xattn_sample",
    )(q3, mem_k, mem_v)
    return jnp.swapaxes(o.reshape(bs, XS_HALVES, H_X, LANES), 1, 2).reshape(bs, d)


FFN_TF = 256
HALO = SUBLANES


FFN_ROW_CHUNKS = 4


def _ffn_prompt_kernel(x_ref, g_ref, wg_ref, wv_ref, dwg_ref, dwv_ref, bg_ref, bv_ref, wd_ref,
                       o_ref, hn_ref, tail_ref, *u_refs):
    t = pl.program_id(1)
    f = pl.program_id(2)
    tm = x_ref.shape[1]
    tf = wg_ref.shape[1]
    rc = tm // len(u_refs)

    @pl.when(f == 0)
    def _():
        hn_ref[...] = _rms(x_ref[0], g_ref[...]).astype(BF16)
        o_ref[0] = x_ref[0]

    def conv(u_ref, dw_ref, b_ref, sl):
        c = b_ref[...] + dw_ref[FFN_CONV_W - 1:FFN_CONV_W, :] * u_ref[HALO:, sl]
        for w in range(FFN_CONV_W - 1):
            off = HALO - (FFN_CONV_W - 1) + w
            c += dw_ref[w:w + 1, :] * u_ref[off:off + rc, sl]
        return c

    def up(k):
        hn = hn_ref[k * rc:(k + 1) * rc, :]
        u_refs[k][HALO:, :tf] = _dot(hn, wg_ref[...])
        u_refs[k][HALO:, tf:] = _dot(hn, wv_ref[...])

    up(0)
    for k, u_ref in enumerate(u_refs):
        if k + 1 < len(u_refs):
            up(k + 1)
        if k == 0:
            u_ref[:HALO, :] = jnp.where(t == 0, 0.0, tail_ref[f])
        else:
            u_ref[:HALO, :] = u_refs[k - 1][rc:, :]
        gated = (_silu(conv(u_ref, dwg_ref, bg_ref, slice(0, tf)))
                 * conv(u_ref, dwv_ref, bv_ref, slice(tf, 2 * tf)))
        o_ref[0, k * rc:(k + 1) * rc, :] += _dot(gated.astype(BF16), wd_ref[...])
    tail_ref[f] = u_refs[-1][rc:, :]


def _ffn_prompt(x3, g, w_up, dw, dwb, w_down, tm=2048):
    b, t, d = x3.shape
    tm = min(tm, t)
    nf = D_FF // FFN_TF
    assert D_FF % FFN_TF == 0 and t % tm == 0
    xs = pl.BlockSpec((1, tm, d), lambda bi, i, f: (bi, i, 0))
    dwb2 = dwb.reshape(1, -1)
    return pl.pallas_call(
        _ffn_prompt_kernel,
        grid=(b, t // tm, nf),
        in_specs=[xs, pl.BlockSpec((1, d), lambda bi, i, f: (0, 0)),
                  pl.BlockSpec((d, FFN_TF), lambda bi, i, f: (0, f)),
                  pl.BlockSpec((d, FFN_TF), lambda bi, i, f: (0, f + nf)),
                  pl.BlockSpec((FFN_CONV_W, FFN_TF), lambda bi, i, f: (0, f)),
                  pl.BlockSpec((FFN_CONV_W, FFN_TF), lambda bi, i, f: (0, f + nf)),
                  pl.BlockSpec((1, FFN_TF), lambda bi, i, f: (0, f)),
                  pl.BlockSpec((1, FFN_TF), lambda bi, i, f: (0, f + nf)),
                  pl.BlockSpec((FFN_TF, d), lambda bi, i, f: (f, 0))],
        out_specs=xs,
        out_shape=jax.ShapeDtypeStruct((b, t, d), F32),
        scratch_shapes=[pltpu.VMEM((tm, d), BF16), pltpu.VMEM((nf, HALO, 2 * FFN_TF), F32)]
                       + [pltpu.VMEM((HALO + tm // FFN_ROW_CHUNKS, 2 * FFN_TF), F32)] * FFN_ROW_CHUNKS,
        compiler_params=_params("parallel", "arbitrary", "arbitrary"),
        name="ffn_prompt",
    )(x3, g.reshape(1, d), w_up, w_up, dw, dw, dwb2, dwb2, w_down)


CONV_HALO = 32
CONV_RB = 32


def _conv_prompt_kernel(u_ref, x_ref, dw_ref, dwb_ref, lg_ref, lb_ref, w2_ref, b2_ref,
                        o_ref, s_ref, c_ref, dwt_ref):
    t = pl.program_id(1)
    tm = u_ref.shape[1]
    n = CONV_HALO + tm

    @pl.when(t == 0)
    def _():
        s_ref[0, :CONV_HALO, :] = jnp.zeros((CONV_HALO, D_MODEL), F32)

    @pl.when(t > 0)
    def _():
        s_ref[0, :CONV_HALO, :] = s_ref[0, tm:, :]

    s_ref[0, CONV_HALO:, :] = u_ref[0]
    for r in range(1, SUBLANES):
        s_ref[r] = pltpu.roll(s_ref[0], n - r, axis=0)
    base = CONV_HALO - (CONV_W - 1)

    @pl.when(t == 0)
    def _():
        for w in range(CONV_W):
            dwt_ref[w] = jnp.broadcast_to(dw_ref[w:w + 1, :], (SUBLANES, D_MODEL))

    n_sub = CONV_RB // SUBLANES

    def row_block(rb, carry):
        r0 = rb * CONV_RB
        bias = jnp.broadcast_to(dwb_ref[...], (SUBLANES, D_MODEL))
        accs = [bias] * n_sub
        for w in range(CONV_W):
            r = (base + w) % SUBLANES
            dwt = dwt_ref[w]
            for k in range(n_sub):
                start = pl.multiple_of(r0 + (base + w - r) + k * SUBLANES, SUBLANES)
                accs[k] = accs[k] + dwt * s_ref[r, pl.ds(start, SUBLANES), :]
        for k in range(n_sub):
            c_ref[pl.ds(pl.multiple_of(r0 + k * SUBLANES, SUBLANES), SUBLANES), :] = accs[k]
        return carry

    lax.fori_loop(0, tm // CONV_RB, row_block, 0)
    c = c_ref[...]
    mu = jnp.mean(c, axis=-1, keepdims=True)
    cc = c - mu
    var = jnp.mean(cc * cc, axis=-1, keepdims=True)
    y = _silu(cc * lax.rsqrt(var + NORM_EPS) * lg_ref[...] + lb_ref[...])
    o_ref[0] = x_ref[0] + _dot(y.astype(BF16), w2_ref[...]) + b2_ref[...]


def _conv_prompt(u3, x3, dw, dwb, ln_g, ln_b, w2, b2, tm=512):
    b, t, d = x3.shape
    tm = min(tm, t)
    xs = pl.BlockSpec((1, tm, d), lambda bi, i: (bi, i, 0))
    vec = pl.BlockSpec((1, d), lambda bi, i: (0, 0))
    return pl.pallas_call(
        _conv_prompt_kernel,
        grid=(b, t // tm),
        in_specs=[xs, xs, pl.BlockSpec((CONV_W, d), lambda bi, i: (0, 0)), vec, vec, vec,
                  pl.BlockSpec((d, d), lambda bi, i: (0, 0)), vec],
        out_specs=xs,
        out_shape=jax.ShapeDtypeStruct((b, t, d), F32),
        scratch_shapes=[pltpu.VMEM((SUBLANES, CONV_HALO + tm, d), F32), pltpu.VMEM((tm, d), F32),
                        pltpu.VMEM((CONV_W, SUBLANES, d), F32)],
        compiler_params=_params("parallel", "arbitrary"),
        name="conv_prompt",
    )(u3, x3, dw, dwb.reshape(1, d), ln_g.reshape(1, d), ln_b.reshape(1, d), w2, b2.reshape(1, d))


STEP_BB = 8


def _conv_step_kernel(st_ref, u_ref, dw_ref, dwb_ref, o_ref):
    w_taps = dw_ref.shape[0]
    for r in range(STEP_BB):
        acc = dwb_ref[...] + dw_ref[w_taps - 1:w_taps, :] * u_ref[r:r + 1, :]
        for w in range(w_taps - 1):
            acc += dw_ref[w:w + 1, :] * st_ref[r, w:w + 1, :]
        o_ref[r:r + 1, :] = acc


def _conv_step(state, l, u, dw, dwb):
    bs, c = u.shape
    w_taps = dw.shape[0]
    return pl.pallas_call(
        _conv_step_kernel,
        grid=(bs // STEP_BB,),
        in_specs=[pl.BlockSpec((None, STEP_BB, w_taps - 1, c), lambda i: (l, i, 0, 0)),
                  pl.BlockSpec((STEP_BB, c), lambda i: (i, 0)),
                  pl.BlockSpec((w_taps, c), lambda i: (0, 0)),
                  pl.BlockSpec((1, c), lambda i: (0, 0))],
        out_specs=pl.BlockSpec((STEP_BB, c), lambda i: (i, 0)),
        out_shape=jax.ShapeDtypeStruct((bs, c), F32),
        compiler_params=_params("parallel"),
        name="conv_step",
    )(state, u, dw, dwb.reshape(1, c))


def _conv_step_slots_kernel(st_ref, u_ref, dw_ref, dwb_ref, o_ref):
    w_taps = dw_ref.shape[0]
    acc = dwb_ref[...] + dw_ref[w_taps - 1:w_taps, :] * u_ref[...]
    for w in range(w_taps - 1):
        acc += dw_ref[w:w + 1, :] * st_ref[w]
    o_ref[...] = acc


def _conv_step_slots(state_t, l, u, dw, dwb):
    bs, c = u.shape
    w_taps = dw.shape[0]
    return pl.pallas_call(
        _conv_step_slots_kernel,
        grid=(bs // STEP_BB,),
        in_specs=[pl.BlockSpec((None, w_taps - 1, STEP_BB, c), lambda i: (l, 0, i, 0)),
                  pl.BlockSpec((STEP_BB, c), lambda i: (i, 0)),
                  pl.BlockSpec((w_taps, c), lambda i: (0, 0)),
                  pl.BlockSpec((1, c), lambda i: (0, 0))],
        out_specs=pl.BlockSpec((STEP_BB, c), lambda i: (i, 0)),
        out_shape=jax.ShapeDtypeStruct((bs, c), F32),
        compiler_params=_params("parallel"),
        name="conv_step_slots",
    )(state_t, u, dw, dwb.reshape(1, c))


def _shift_slots_kernel(*refs, n_layers):
    st_ref, u_refs, o_ref = refs[0], refs[1:1 + n_layers], refs[1 + n_layers]
    n_slots = st_ref.shape[0]
    o_ref[:n_slots - 1] = st_ref[1:]
    layer = pl.program_id(0)
    for n, u_ref in enumerate(u_refs):
        @pl.when(layer == n)
        def _(u_ref=u_ref):
            o_ref[n_slots - 1] = u_ref[...]


def _shift_slots(state_t, us):
    n_layers, n_slots, bs, c = state_t.shape

    def u_spec(n):
        return pl.BlockSpec((STEP_BB, c), lambda l, i: (jnp.where(l == n, i, 0), 0))

    blk = pl.BlockSpec((None, n_slots, STEP_BB, c), lambda l, i: (l, 0, i, 0))
    return pl.pallas_call(
        functools.partial(_shift_slots_kernel, n_layers=n_layers),
        grid=(n_layers, bs // STEP_BB),
        in_specs=[blk] + [u_spec(n) for n in range(n_layers)],
        out_specs=blk,
        out_shape=jax.ShapeDtypeStruct(state_t.shape, F32),
        compiler_params=_params("arbitrary", "arbitrary"),
        name="shift_slots",
    )(state_t, *us)


def _stack_rows_kernel(*refs, n_layers, col_starts):
    srcs, dst = refs[:n_layers], refs[n_layers]
    rows = srcs[0].shape[1]
    layer = pl.program_id(0)
    for n, src in enumerate(srcs):
        @pl.when(layer == n)
        def _(src=src):
            for a, c0 in enumerate(col_starts):
                dst[pl.ds(a, rows, stride=len(col_starts)), :] = src[0, :, c0:c0 + LANES]


def _stack_rows(srcs, col_blocks, width, col_starts, tm):
    n_layers = len(srcs)
    b, t, _ = srcs[0].shape
    tm = min(tm, t)
    nt = t // tm
    n_phase = len(col_starts)

    def src_spec(n):
        def idx(l, bi, i):
            on = l == n
            return (jnp.where(on, bi, 0), jnp.where(on, i, 0), col_blocks[n])
        return pl.BlockSpec((1, tm, width), idx)

    return pl.pallas_call(
        functools.partial(_stack_rows_kernel, n_layers=n_layers, col_starts=tuple(col_starts)),
        grid=(n_layers, b, nt),
        in_specs=[src_spec(n) for n in range(n_layers)],
        out_specs=pl.BlockSpec((None, None, tm * n_phase, LANES), lambda l, bi, i: (l, bi, i, 0)),
        out_shape=jax.ShapeDtypeStruct((n_layers, b, t * n_phase, LANES), F32),
        compiler_params=_params("arbitrary", "arbitrary", "arbitrary"),
        name="stack_rows",
    )(*srcs)


def kernel(x_prompt, x_sample, cache_sb_k, cache_sb_v, state_ret, state_conv, state_ffn_conv, cache_mem_k, cache_mem_v, page_table, mem_prompt, g_mix, w_in_ab, ret_gn_g, w_out_ab, sb_bias, cv_w1, cv_b1, cv_dw, cv_dwb, cv_ln_g, cv_ln_b, cv_w2, cv_b2, g_cross, xa_wq, xa_wk, xa_wv, xa_wo, g_ffn, ffn_w_up, ffn_dw, ffn_dwb, ffn_w_down, g_final):
    bp, t, d = x_prompt.shape
    bs = x_sample.shape[0]
    n_phys = cache_sb_k.shape[1]

    log_gamma = jnp.log1p(-jnp.exp2(-5.0 - jnp.arange(H_RET, dtype=F32)))
    ret_tabs = _ret_tables(log_gamma)
    cos_p, sin_p = _rope_tables(jnp.arange(t))
    cos_s, sin_s = _rope_tables(PAST_LEN + jnp.arange(1))
    uo = _suffix_matrix()
    conv_slots = jnp.swapaxes(state_conv, 1, 2)
    cache_k = cache_sb_k.reshape(cache_sb_k.shape[0], n_phys, PAGE_SIZE * H_SB, HD_SB)
    cache_v = cache_sb_v.reshape(cache_sb_v.shape[0], n_phys, PAGE_SIZE * H_SB, HD_SB)
    mem_k = _xattn_cache_view(cache_mem_k)
    mem_v = _xattn_cache_view(cache_mem_v)
    w_mem = jnp.concatenate([w[l] for l in range(DEPTH) for w in (xa_wk, xa_wv)], axis=1)
    mem_kv3 = _mm([mem_prompt.reshape(bp * N_MEM, d)], [w_mem.astype(BF16)], tn=d,
                  name="mem_kv").reshape(bp, N_MEM, 2 * DEPTH * d)

    xp = x_prompt.reshape(bp * t, d)
    xs = x_sample.reshape(bs, d)
    sbk_p, sbv_p, sbk_s, sbv_s, ret_p, ret_s = [], [], [], [], [], []
    cv_p, cv_s, ff_p, ff_s = [], [], [], []

    for l in range(DEPTH):
        if l % 2 == 0:
            e = l // 2
            w_in = w_in_ab[e].astype(BF16)
            w_out = w_out_ab[e].astype(BF16)
            w_out_parts = [w_out[:W_RK], w_out[W_RK:]]
            bias = sb_bias[e].astype(F32)
            bias_row = jnp.repeat(bias, HD_SB).reshape(1, W_SB)
            bias_rows = jnp.zeros((SUBLANES, PAGE_SIZE * H_SB), F32).at[:H_SB].set(
                jnp.broadcast_to(bias[:, None], (H_SB, PAGE_SIZE * H_SB)))
            proj = _mm([xp], [w_in], prologue="rms", pro=(g_mix[l],), tm=512,
                       tn=w_in.shape[1], name="proj_in")
            proj3 = proj.reshape(bp, t, -1)
            ret_o, s_p = _ret_prompt(proj3, cos_p, sin_p, ret_tabs, ret_gn_g[e])
            sb_o = _sb_prompt(proj3, bias_row, uo)
            xp = _mm([ret_o.reshape(bp * t, W_RK), sb_o.reshape(bp * t, W_SB)], w_out_parts,
                     res=xp, tm=1024, tn=d, name="proj_out")
            sbk_p.append(proj3)
            ret_p.append(s_p)
            proj_s = _mm([xs], [w_in], prologue="rms", pro=(g_mix[l],), name="proj_in_s")
            ret_os, s_s = _ret_sample(proj_s, cos_s, sin_s, ret_gn_g[e], state_ret, e)
            sb_os = _sb_sample(proj_s.reshape(bs, 1, -1), bias_rows, uo, cache_k, cache_v,
                               page_table, e)
            xs = _mm([ret_os, sb_os.reshape(bs, W_SB)], w_out_parts, res=xs, tn=d,
                     name="proj_out_s")
            sbk_s.append(proj_s[:, 5 * W_SB:6 * W_SB].reshape(bs, 1, H_SB, HD_SB))
            sbv_s.append(proj_s[:, 6 * W_SB:7 * W_SB].reshape(bs, 1, H_SB, HD_SB))
            ret_s.append(s_s)
        else:
            o = l // 2
            w1 = cv_w1[o].astype(BF16)
            w2 = cv_w2[o].astype(BF16)
            u = _mm([xp], [w1], prologue="rms", pro=(g_mix[l],), bias=cv_b1[o], glu=True,
                    tm=512, tn=d, name="conv_glu")
            u3 = u.reshape(bp, t, d)
            xp = _conv_prompt(u3, xp.reshape(bp, t, d), cv_dw[o], cv_dwb[o], cv_ln_g[o],
                              cv_ln_b[o], w2, cv_b2[o]).reshape(bp * t, d)
            cv_p.append(u3[:, t - (CONV_W - 1):, :])
            u_s = _mm([xs], [w1], prologue="rms", pro=(g_mix[l],), bias=cv_b1[o], glu=True,
                      name="conv_glu_s")
            c_s = _conv_step_slots(conv_slots, o, u_s, cv_dw[o], cv_dwb[o])
            xs = _mm([c_s], [w2], prologue="ln_silu", pro=(cv_ln_g[o], cv_ln_b[o]), bias=cv_b2[o],
                     res=xs, tn=d, name="conv_out_s")
            cv_s.append(u_s)

        wq = xa_wq[l].astype(BF16)
        wo = xa_wo[l].astype(BF16)
        xp = _xattn_prompt(xp.reshape(bp, t, d), g_cross[l], wq, mem_kv3, 2 * l, 2 * l + 1, wo)
        q_s = _mm([xs], [wq], prologue="rms", pro=(g_cross[l],), tn=d, name="xattn_q_s")
        att_s = _xattn_sample(q_s.reshape(bs, 1, d), mem_k, mem_v, l)
        xs = _mm([att_s], [wo], res=xs, tn=d, name="xattn_out_s")

        w_up = ffn_w_up[l].astype(BF16)
        w_down = ffn_w_down[l].astype(BF16)
        tail_rows = xp[:, t - (FFN_CONV_W - 1):, :].reshape(bp * (FFN_CONV_W - 1), d)
        ff_p.append(_mm([tail_rows], [w_up], prologue="rms", pro=(g_ffn[l],),
                        name="ffn_tail").reshape(bp, FFN_CONV_W - 1, 2 * D_FF))
        xp = _ffn_prompt(xp, g_ffn[l], w_up, ffn_dw[l], ffn_dwb[l], w_down).reshape(bp * t, d)
        u_s = _mm([xs], [w_up], prologue="rms", pro=(g_ffn[l],), name="ffn_up_s")
        c_s = _conv_step(state_ffn_conv, l, u_s, ffn_dw[l], ffn_dwb[l])
        xs = _mm([c_s], [w_down], prologue="swiglu", res=xs, tn=d, name="ffn_down_s")
        ff_s.append(jnp.concatenate([state_ffn_conv[l][:, 1:], u_s[:, None, :]], axis=1))

    y_prompt = _rmsnorm(xp, g_final).reshape(bp, t, d)
    y_sample = _rmsnorm(xs, g_final).reshape(bs, 1, d)
    sb_cols = [h * HD_SB for h in range(H_SB)]
    sb_shape = (len(sbk_p), bp, t, H_SB, HD_SB)
    sb_k_prompt = _stack_rows(sbk_p, [5] * len(sbk_p), W_SB, sb_cols, 512).reshape(sb_shape)
    sb_v_prompt = _stack_rows(sbk_p, [6] * len(sbk_p), W_SB, sb_cols, 512).reshape(sb_shape)
    mem_cols = [(a % H_X) * HD_X + (a // H_X) * LANES for a in range(XS_ROWS)]

    def mem_out(first_block):
        o = _stack_rows([mem_kv3] * DEPTH, [2 * l + first_block for l in range(DEPTH)], d,
                        mem_cols, N_MEM)
        o = o.reshape(DEPTH, bp, N_MEM, XS_HALVES, H_X, LANES)
        return jnp.swapaxes(o, 3, 4).reshape(DEPTH, bp, N_MEM, H_X, HD_X)

    return (y_prompt, y_sample, sb_k_prompt, sb_v_prompt, jnp.stack(sbk_s),
            jnp.stack(sbv_s), jnp.stack(ret_p), jnp.stack(ret_s), jnp.stack(cv_p),
            jnp.swapaxes(_shift_slots(conv_slots, cv_s), 1, 2), jnp.stack(ff_p),
            jnp.stack(ff_s), mem_out(0), mem_out(1))
```

```python
import functools

import numpy as np
import jax
import jax.numpy as jnp
from jax import lax
from jax.experimental import pallas as pl
from jax.experimental.pallas import tpu as pltpu

F32 = jnp.float32
BF16 = jnp.bfloat16

D_MODEL = 1024
DEPTH = 4
PAST_LEN = 2048
PAGE_SIZE = 128
N_PAGES = PAST_LEN // PAGE_SIZE
H_RET = 4
DK_RET = 128
DV_RET = 128
RET_CHUNK = 128
ROPE_BASE = 10000.0
H_SB = 4
HD_SB = 128
SB_BLOCK = 128
CONV_W = 31
D_FF = 2816
FFN_CONV_W = 3
N_MEM = 256
H_X = 4
HD_X = D_MODEL // H_X
NORM_EPS = 1e-6
W_RK = H_RET * DK_RET
W_SB = H_SB * HD_SB

SUBLANES = 8
LANES = 128
VMEM_LIMIT_BYTES = 56 * 1024 * 1024


def _params(*sem):
    return pltpu.CompilerParams(dimension_semantics=sem, vmem_limit_bytes=VMEM_LIMIT_BYTES)


def _dot(a, b):
    return jnp.dot(a, b, preferred_element_type=F32)


def _dot_nt(a, b):
    return lax.dot_general(a, b, (((1,), (1,)), ((), ())), preferred_element_type=F32)


def _rms(x, g):
    return x * lax.rsqrt(jnp.mean(x * x, axis=-1, keepdims=True) + NORM_EPS) * g


def _silu(x):
    return x * jax.nn.sigmoid(x)


def _softplus(z):
    return jnp.maximum(z, 0.0) + jnp.log(1.0 + jnp.exp(-jnp.abs(z)))


def _split_hi_lo(x):
    hi = x.astype(BF16)
    lo = (x - hi.astype(F32)).astype(BF16)
    return hi, lo


def _mm_kernel(*refs, n_lhs, prologue, has_bias, has_res, glu):
    it = iter(refs)
    x_refs = [next(it) for _ in range(n_lhs)]
    w_refs = [next(it) for _ in range(n_lhs)]
    wg_refs = [next(it) for _ in range(n_lhs)] if glu else []
    n_pro = {None: 0, "rms": 1, "ln_silu": 2, "swiglu": 0}[prologue]
    p_refs = [next(it) for _ in range(n_pro)]
    b_ref = next(it) if has_bias else None
    bg_ref = next(it) if (has_bias and glu) else None
    r_ref = next(it) if has_res else None
    o_ref = next(it)
    xn_ref = next(it) if prologue else None

    if prologue:
        @pl.when(pl.program_id(1) == 0)
        def _():
            x = x_refs[0][...].astype(F32)
            if prologue == "rms":
                y = _rms(x, p_refs[0][...])
            elif prologue == "ln_silu":
                mu = jnp.mean(x, axis=-1, keepdims=True)
                xc = x - mu
                var = jnp.mean(xc * xc, axis=-1, keepdims=True)
                y = _silu(xc * lax.rsqrt(var + NORM_EPS) * p_refs[0][...] + p_refs[1][...])
            else:
                k = x.shape[-1] // 2
                y = _silu(x[:, :k]) * x[:, k:]
            xn_ref[...] = y.astype(BF16)
        lhs = [xn_ref[...]]
    else:
        lhs = [r[...].astype(BF16) for r in x_refs]

    acc = _dot(lhs[0], w_refs[0][...])
    for a, w in zip(lhs[1:], w_refs[1:]):
        acc += _dot(a, w[...])
    if has_bias:
        acc += b_ref[...]
    if glu:
        gate = _dot(lhs[0], wg_refs[0][...])
        for a, w in zip(lhs[1:], wg_refs[1:]):
            gate += _dot(a, w[...])
        if has_bias:
            gate += bg_ref[...]
        acc = acc * jax.nn.sigmoid(gate)
    if has_res:
        acc += r_ref[...]
    o_ref[...] = acc.astype(o_ref.dtype)


def _mm(xs, ws, *, prologue=None, pro=(), bias=None, res=None, glu=False, tm=512, tn=512,
        out_dtype=F32, name="mm"):
    m = xs[0].shape[0]
    n = ws[0].shape[1] // (2 if glu else 1)
    tm = min(tm, m)
    tn = min(tn, n)
    assert m % tm == 0 and n % tn == 0, (m, tm, n, tn)
    nj = n // tn
    args, specs = [], []
    for x in xs:
        args.append(x)
        specs.append(pl.BlockSpec((tm, x.shape[1]), lambda i, j: (i, 0)))
    for w in ws:
        args.append(w)
        specs.append(pl.BlockSpec((w.shape[0], tn), lambda i, j: (0, j)))
    if glu:
        for w in ws:
            args.append(w)
            specs.append(pl.BlockSpec((w.shape[0], tn), lambda i, j: (0, j + nj)))
    for p in pro:
        args.append(p.reshape(1, -1))
        specs.append(pl.BlockSpec((1, p.size), lambda i, j: (0, 0)))
    if bias is not None:
        b2 = bias.reshape(1, -1)
        args.append(b2)
        specs.append(pl.BlockSpec((1, tn), lambda i, j: (0, j)))
        if glu:
            args.append(b2)
            specs.append(pl.BlockSpec((1, tn), lambda i, j: (0, j + nj)))
    if res is not None:
        args.append(res)
        specs.append(pl.BlockSpec((tm, tn), lambda i, j: (i, j)))
    scratch = []
    if prologue:
        k_eff = xs[0].shape[1] // (2 if prologue == "swiglu" else 1)
        scratch.append(pltpu.VMEM((tm, k_eff), BF16))
    kern = functools.partial(_mm_kernel, n_lhs=len(xs), prologue=prologue,
                             has_bias=bias is not None, has_res=res is not None, glu=glu)
    return pl.pallas_call(
        kern,
        grid=(m // tm, nj),
        in_specs=specs,
        out_specs=pl.BlockSpec((tm, tn), lambda i, j: (i, j)),
        out_shape=jax.ShapeDtypeStruct((m, n), out_dtype),
        scratch_shapes=scratch,
        compiler_params=_params("parallel", "arbitrary"),
        name=name,
    )(*args)


def _rmsnorm_kernel(x_ref, g_ref, o_ref):
    o_ref[...] = _rms(x_ref[...], g_ref[...])


def _rmsnorm(x, g, tm=1024):
    m, d = x.shape
    tm = min(tm, m)
    return pl.pallas_call(
        _rmsnorm_kernel,
        grid=(m // tm,),
        in_specs=[pl.BlockSpec((tm, d), lambda i: (i, 0)), pl.BlockSpec((1, d), lambda i: (0, 0))],
        out_specs=pl.BlockSpec((tm, d), lambda i: (i, 0)),
        out_shape=jax.ShapeDtypeStruct((m, d), F32),
        compiler_params=_params("parallel"),
        name="rmsnorm",
    )(x, g.reshape(1, d))


def _rotate(x, cos2, sin2):
    return x * cos2 + pltpu.roll(x, DK_RET // 2, axis=1) * sin2


def _head_ln_gate(o, gate, gn):
    mu = jnp.mean(o, axis=-1, keepdims=True)
    oc = o - mu
    var = jnp.mean(oc * oc, axis=-1, keepdims=True)
    return _silu(gate) * (oc * lax.rsqrt(var + NORM_EPS) * gn)


def _ret_prompt_kernel(rq_ref, rk_ref, rv_ref, rg_ref, cos_ref, sin_ref, dmask_ref, qdec_ref,
                       kdec_ref, cdec_ref, gn_ref, o_ref, s_ref):
    c = pl.program_id(1)

    @pl.when(c == 0)
    def _():
        s_ref[...] = jnp.zeros_like(s_ref)

    cos2 = cos_ref[...]
    sin2 = sin_ref[...]
    pairs = [(i, h, slice(h * DK_RET, (h + 1) * DK_RET))
             for i in range(rq_ref.shape[0]) for h in range(H_RET)]
    ks = [_rotate(rk_ref[i, :, sl], cos2, sin2) * (DK_RET ** -0.5) for i, h, sl in pairs]
    qbs = [_rotate(rq_ref[i, :, sl], cos2, sin2).astype(BF16) for i, h, sl in pairs]
    vbs = [rv_ref[i, :, sl].astype(BF16) for i, h, sl in pairs]
    s0s = [s_ref[i, h] for i, h, sl in pairs]
    scores = [(_dot_nt(qb, k.astype(BF16)) * dmask_ref[h]).astype(BF16)
              for (i, h, sl), qb, k in zip(pairs, qbs, ks)]
    inters = [_dot(qb, s0.astype(BF16)) * qdec_ref[h]
              for (i, h, sl), qb, s0 in zip(pairs, qbs, s0s)]
    for n, (i, h, sl) in enumerate(pairs):
        kd_t = (ks[n] * kdec_ref[h]).T.astype(BF16)
        s_ref[i, h] = cdec_ref[h] * s0s[n] + _dot(kd_t, vbs[n])
    for n, (i, h, sl) in enumerate(pairs):
        o = _dot(scores[n], vbs[n]) + inters[n]
        o_ref[i, :, sl] = _head_ln_gate(o, rg_ref[i, :, sl], gn_ref[:, sl]).astype(o_ref.dtype)


def _ret_tables(log_gamma):
    idx = jnp.arange(RET_CHUNK)
    diff = idx[:, None] - idx[None, :]
    expo = jnp.maximum(diff, 0).astype(F32)[None] * log_gamma[:, None, None]
    dmask = jnp.where(diff[None] >= 0, jnp.exp(expo), 0.0)
    q_decay = jnp.exp((idx + 1).astype(F32)[:, None] * log_gamma[None, :])
    k_decay = jnp.exp((RET_CHUNK - 1 - idx).astype(F32)[:, None] * log_gamma[None, :])
    ones = jnp.ones((1, 1, DK_RET), F32)
    qdec = q_decay.T[:, :, None] * ones
    kdec = k_decay.T[:, :, None] * ones
    cdec = jnp.exp(RET_CHUNK * log_gamma)[:, None, None] * jnp.ones((1, DK_RET, DV_RET), F32)
    return dmask, qdec, kdec, cdec


def _rope_tables(pos):
    half = DK_RET // 2
    inv = ROPE_BASE ** (-jnp.arange(half, dtype=F32) / half)
    ang = pos.astype(F32)[:, None] * inv[None, :]
    cos, sin = jnp.cos(ang), jnp.sin(ang)
    return jnp.concatenate([cos, cos], -1), jnp.concatenate([-sin, sin], -1)


RET_PROMPT_BB = 2


def _ret_prompt(proj3, cos2, sin2, tabs, gn):
    b, t, _ = proj3.shape
    nc = t // RET_CHUNK
    dmask, qdec, kdec, cdec = tabs

    bb = RET_PROMPT_BB

    def col(cb):
        return pl.BlockSpec((bb, RET_CHUNK, W_RK), lambda i, c: (i, c, cb))

    tab = pl.BlockSpec((H_RET, RET_CHUNK, RET_CHUNK), lambda i, c: (0, 0, 0))
    return pl.pallas_call(
        _ret_prompt_kernel,
        grid=(b // bb, nc),
        in_specs=[col(0), col(1), col(2), col(3),
                  pl.BlockSpec((RET_CHUNK, DK_RET), lambda i, c: (c, 0)),
                  pl.BlockSpec((RET_CHUNK, DK_RET), lambda i, c: (c, 0)),
                  tab, tab, tab, tab,
                  pl.BlockSpec((1, W_RK), lambda i, c: (0, 0))],
        out_specs=[pl.BlockSpec((bb, RET_CHUNK, W_RK), lambda i, c: (i, c, 0)),
                   pl.BlockSpec((bb, H_RET, DK_RET, DV_RET), lambda i, c: (i, 0, 0, 0))],
        out_shape=[jax.ShapeDtypeStruct((b, t, W_RK), BF16),
                   jax.ShapeDtypeStruct((b, H_RET, DK_RET, DV_RET), F32)],
        compiler_params=_params("parallel", "arbitrary"),
        name="ret_prompt",
    )(proj3, proj3, proj3, proj3, cos2, sin2, dmask, qdec, kdec, cdec, gn.reshape(1, W_RK))


RET_BB = 8
MXU_ROWS = 16


def _ret_sample_kernel(rq_ref, rk_ref, rv_ref, rg_ref, cos_ref, sin_ref, gn_ref, st_ref,
                       o_ref, so_ref, inter_ref, *, g1):
    cos2 = cos_ref[...]
    sin2 = sin_ref[...]
    eye = (lax.broadcasted_iota(jnp.int32, (DK_RET, DK_RET), 0)
           == lax.broadcasted_iota(jnp.int32, (DK_RET, DK_RET), 1))
    for h in range(H_RET):
        sl = slice(h * DK_RET, (h + 1) * DK_RET)
        q = _rotate(rq_ref[:, sl], cos2, sin2)
        k = _rotate(rk_ref[:, sl], cos2, sin2) * (DK_RET ** -0.5)
        v = rv_ref[:, sl]
        for r in range(RET_BB):
            s0 = st_ref[r, h]
            qr = jnp.broadcast_to(q[r:r + 1], (MXU_ROWS, DK_RET)).astype(BF16)
            inter_ref[r:r + 1, :] = _dot(qr, s0.astype(BF16))[0:1] * g1[h]
            diag_k = jnp.where(eye, jnp.broadcast_to(k[r:r + 1], (DK_RET, DK_RET)), 0.0).astype(BF16)
            v_rows = jnp.broadcast_to(v[r:r + 1], (DK_RET, DV_RET)).astype(BF16)
            so_ref[r, h] = g1[h] * s0 + _dot(diag_k, v_rows)
        o = jnp.sum(q * k, axis=-1, keepdims=True) * v + inter_ref[...]
        o_ref[:, sl] = _head_ln_gate(o, rg_ref[:, sl], gn_ref[:, sl]).astype(o_ref.dtype)


def _ret_sample(proj, cos2, sin2, gn, state, e):
    bs = proj.shape[0]
    log_gamma = np.log1p(-np.exp2(-5.0 - np.arange(H_RET, dtype=np.float32)))
    g1 = tuple(float(x) for x in np.exp(log_gamma).astype(np.float32))

    def col(cb):
        return pl.BlockSpec((RET_BB, W_RK), lambda i: (i, cb))

    row = pl.BlockSpec((1, DK_RET), lambda i: (0, 0))
    return pl.pallas_call(
        functools.partial(_ret_sample_kernel, g1=g1),
        grid=(bs // RET_BB,),
        in_specs=[col(0), col(1), col(2), col(3), row, row,
                  pl.BlockSpec((1, W_RK), lambda i: (0, 0)),
                  pl.BlockSpec((None, RET_BB, H_RET, DK_RET, DV_RET), lambda i: (e, i, 0, 0, 0))],
        out_specs=[pl.BlockSpec((RET_BB, W_RK), lambda i: (i, 0)),
                   pl.BlockSpec((RET_BB, H_RET, DK_RET, DV_RET), lambda i: (i, 0, 0, 0))],
        out_shape=[jax.ShapeDtypeStruct((bs, W_RK), BF16),
                   jax.ShapeDtypeStruct((bs, H_RET, DK_RET, DV_RET), F32)],
        scratch_shapes=[pltpu.VMEM((RET_BB, DV_RET), F32)],
        compiler_params=_params("parallel"),
        name="ret_sample",
    )(proj, proj, proj, proj, cos2, sin2, gn.reshape(1, W_RK), state)


def _suffix_matrix():
    j = jnp.arange(SB_BLOCK)
    u = (j[:, None] > j[None, :]).astype(BF16)
    half = jnp.concatenate([u, jnp.ones((SB_BLOCK, SB_BLOCK), BF16)], axis=1)
    return jnp.concatenate([half, half], axis=0)


def _suffix_sums(log_1m, uo):
    hi, lo = _split_hi_lo(log_1m)
    return _dot(jnp.concatenate([hi, lo], axis=1), uo)


SB_TQ = 256


def _sb_prompt_kernel(q_ref, k_ref, v_ref, bias_ref, uo_ref, o_ref, acc_ref, carry_ref):
    i = pl.program_id(1)
    uo = uo_ref[...]
    heads = [slice(h * HD_SB, (h + 1) * HD_SB) for h in range(H_SB)]

    n_sub = SB_TQ // SB_BLOCK

    def block(kb, masked, first):
        start = pl.multiple_of(kb * SB_TQ, SB_TQ)
        z = jnp.concatenate(
            [_dot_nt(q_ref[0, :, sl].astype(BF16), k_ref[0, pl.ds(start, SB_TQ), sl].astype(BF16))
             * (HD_SB ** -0.5) + bias_ref[:, h * SB_TQ:(h + 1) * SB_TQ]
             for h, sl in enumerate(heads)], axis=0)
        sp = _softplus(z)
        if masked:
            qpos = i * SB_TQ + (lax.broadcasted_iota(jnp.int32, z.shape, 0) & (SB_TQ - 1))
            kpos = start + lax.broadcasted_iota(jnp.int32, z.shape, 1)
            valid = kpos < qpos
            log_1m = jnp.where(valid, -sp, 0.0)
        else:
            log_1m = -sp
        subs = [slice(n * SB_BLOCK, (n + 1) * SB_BLOCK) for n in range(n_sub)]
        rs = [_suffix_sums(log_1m[:, sb], uo) for sb in subs]
        carry = None if first else carry_ref[...]
        after = [None] * n_sub
        for n in reversed(range(n_sub)):
            after[n] = rs[n][:, :SB_BLOCK] if carry is None else rs[n][:, :SB_BLOCK] + carry
            total = rs[n][:, SB_BLOCK:]
            carry = total if carry is None else carry + total
        a = jnp.exp(z - sp + jnp.concatenate(after, axis=1))
        if masked:
            a = jnp.where(valid, a, 0.0)
        ab = a.astype(BF16)
        pv = jnp.concatenate(
            [_dot(ab[h * SB_TQ:(h + 1) * SB_TQ], v_ref[0, pl.ds(start, SB_TQ), sl].astype(BF16))
             for h, sl in enumerate(heads)], axis=0)
        acc_ref[...] = pv if first else acc_ref[...] + pv
        carry_ref[...] = carry

    block(i, True, True)

    def body(t, _):
        block(i - 1 - t, False, False)
        return 0

    lax.fori_loop(0, i, body, 0)
    for h, sl in enumerate(heads):
        o_ref[0, :, sl] = acc_ref[h * SB_TQ:(h + 1) * SB_TQ, :].astype(o_ref.dtype)


def _sb_prompt(proj3, bias_row, uo):
    b, t, _ = proj3.shape
    tq = min(SB_TQ, t)
    assert tq == SB_TQ and t % tq == 0
    return pl.pallas_call(
        _sb_prompt_kernel,
        grid=(b, t // tq),
        in_specs=[pl.BlockSpec((1, tq, W_SB), lambda bi, i: (bi, i, 4)),
                  pl.BlockSpec((1, t, W_SB), lambda bi, i: (bi, 0, 5)),
                  pl.BlockSpec((1, t, W_SB), lambda bi, i: (bi, 0, 6)),
                  pl.BlockSpec((1, H_SB * SB_TQ), lambda bi, i: (0, 0)),
                  pl.BlockSpec((2 * SB_BLOCK, 2 * SB_BLOCK), lambda bi, i: (0, 0))],
        out_specs=pl.BlockSpec((1, tq, W_SB), lambda bi, i: (bi, i, 0)),
        out_shape=jax.ShapeDtypeStruct((b, t, W_SB), BF16),
        scratch_shapes=[pltpu.VMEM((H_SB * tq, HD_SB), F32),
                        pltpu.VMEM((H_SB * tq, SB_BLOCK), F32)],
        compiler_params=_params("parallel", "arbitrary"),
        name="sb_prompt",
    )(proj3, proj3, proj3, bias_row, uo)


def _row_select(parts):
    r = lax.broadcasted_iota(jnp.int32, parts[0].shape, 0)
    out = jnp.zeros_like(parts[0])
    for h, p in enumerate(parts):
        out = jnp.where(r == h, p, out)
    return out


def _bcast_rows_b16(row):
    return jnp.broadcast_to(row, (MXU_ROWS, row.shape[-1])).astype(BF16)


def _sb_sample_kernel(pt_ref, q_ref, bias_ref, uo_ref, *refs):
    del pt_ref
    k_refs = refs[:N_PAGES]
    v_refs = refs[N_PAGES:2 * N_PAGES]
    o_ref = refs[2 * N_PAGES]
    uo = uo_ref[...]
    n_col = PAGE_SIZE * H_SB
    n_grp = n_col // SB_BLOCK
    q4 = _row_select([jnp.broadcast_to(q_ref[0, :, h * HD_SB:(h + 1) * HD_SB], (SUBLANES, HD_SB))
                      for h in range(H_SB)])
    qb = jnp.concatenate([q4, q4], axis=0).astype(BF16)
    row = lax.broadcasted_iota(jnp.int32, (SUBLANES, n_col), 0)
    col = lax.broadcasted_iota(jnp.int32, (SUBLANES, n_col), 1)
    valid = (col & (H_SB - 1)) == row
    bias = bias_ref[...]
    zs, sps, his, los = [], [], [], []
    for p in range(N_PAGES):
        z = _dot_nt(qb, k_refs[p][...].astype(BF16))[:SUBLANES] * (HD_SB ** -0.5) + bias
        sp = _softplus(z)
        hi, lo = _split_hi_lo(jnp.where(valid, -sp, 0.0))
        zs.append(z)
        sps.append(sp)
        his += [hi[:, g * SB_BLOCK:(g + 1) * SB_BLOCK] for g in range(n_grp)]
        los += [lo[:, g * SB_BLOCK:(g + 1) * SB_BLOCK] for g in range(n_grp)]
    n_all = N_PAGES * n_grp
    r = _dot(jnp.concatenate([jnp.concatenate(his, axis=0), jnp.concatenate(los, axis=0)],
                             axis=1), uo)
    carry = jnp.zeros((SUBLANES, SB_BLOCK), F32)
    after = [None] * n_all
    for g in reversed(range(n_all)):
        rows = slice(g * SUBLANES, (g + 1) * SUBLANES)
        after[g] = r[rows, :SB_BLOCK] + carry
        carry = carry + r[rows, SB_BLOCK:]
    acc = jnp.zeros((MXU_ROWS, HD_SB), F32)
    for p in range(N_PAGES):
        aft = jnp.concatenate(after[p * n_grp:(p + 1) * n_grp], axis=1)
        a = jnp.where(valid, jnp.exp(zs[p] - sps[p] + aft), 0.0)
        ab = jnp.concatenate([a, a], axis=0).astype(BF16)
        acc += _dot(ab, v_refs[p][...].astype(BF16))
    o_ref[0] = acc[:H_SB].astype(o_ref.dtype)


def _sb_sample(proj3, bias_rows, uo, cache_k, cache_v, page_table, e):
    bs = proj3.shape[0]

    def page(p):
        return pl.BlockSpec((None, None, PAGE_SIZE * H_SB, HD_SB),
                            lambda b, pt: (e, pt[b, p], 0, 0))

    grid_spec = pltpu.PrefetchScalarGridSpec(
        num_scalar_prefetch=1,
        grid=(bs,),
        in_specs=[pl.BlockSpec((1, 1, W_SB), lambda b, pt: (b, 0, 4)),
                  pl.BlockSpec((SUBLANES, PAGE_SIZE * H_SB), lambda b, pt: (0, 0)),
                  pl.BlockSpec((2 * SB_BLOCK, 2 * SB_BLOCK), lambda b, pt: (0, 0))]
                 + [page(p) for p in range(N_PAGES)] + [page(p) for p in range(N_PAGES)],
        out_specs=pl.BlockSpec((1, H_SB, HD_SB), lambda b, pt: (b, 0, 0)),
    )
    return pl.pallas_call(
        _sb_sample_kernel,
        grid_spec=grid_spec,
        out_shape=jax.ShapeDtypeStruct((bs, H_SB, HD_SB), F32),
        compiler_params=_params("arbitrary"),
        name="sb_sample",
    )(page_table, proj3, bias_rows, uo, *([cache_k] * N_PAGES), *([cache_v] * N_PAGES))


def _softmax_rows(s):
    m = jnp.max(s, axis=-1, keepdims=True)
    p = jnp.exp(s - m)
    return p / jnp.sum(p, axis=-1, keepdims=True)


XP_ROW_CHUNKS = 2


def _xattn_prompt_kernel(x_ref, g_ref, wq_ref, mk_ref, mv_ref, wo_ref, o_ref, att_ref, *,
                         n_chunk):
    tm = x_ref.shape[1]
    rc = tm // n_chunk
    heads = [slice(h * HD_X, (h + 1) * HD_X) for h in range(H_X)]
    mkb = [mk_ref[0, :, sl].astype(BF16) for sl in heads]
    mvb = [mv_ref[0, :, sl].astype(BF16) for sl in heads]

    def rows(c):
        return slice(c * rc, (c + 1) * rc)

    def query(c):
        hn = _rms(x_ref[0, rows(c), :], g_ref[...]).astype(BF16)
        return _dot(hn, wq_ref[...]).astype(BF16)

    def scores(q):
        return [_dot_nt(q[:, sl], mkb[h]) * (HD_X ** -0.5) for h, sl in enumerate(heads)]

    s = scores(query(0))
    for c in range(n_chunk):
        q_next = query(c + 1) if c + 1 < n_chunk else None
        for h, sl in enumerate(heads):
            att_ref[rows(c), sl] = _dot(_softmax_rows(s[h]).astype(BF16), mvb[h]).astype(BF16)
        if q_next is not None:
            s = scores(q_next)
        o_ref[0, rows(c), :] = x_ref[0, rows(c), :] + _dot(att_ref[rows(c), :], wo_ref[...])


def _xattn_prompt(x3, g, wq, mem_kv3, k_block, v_block, wo, tm=512, row_chunks=XP_ROW_CHUNKS):
    b, t, d = x3.shape
    tm = min(tm, t)
    xs = pl.BlockSpec((1, tm, d), lambda bi, i: (bi, i, 0))
    ws = pl.BlockSpec((d, d), lambda bi, i: (0, 0))
    assert tm % (row_chunks * SUBLANES) == 0
    return pl.pallas_call(
        functools.partial(_xattn_prompt_kernel, n_chunk=row_chunks),
        grid=(b, t // tm),
        in_specs=[xs, pl.BlockSpec((1, d), lambda bi, i: (0, 0)), ws,
                  pl.BlockSpec((1, N_MEM, d), lambda bi, i: (bi, 0, k_block)),
                  pl.BlockSpec((1, N_MEM, d), lambda bi, i: (bi, 0, v_block)), ws],
        out_specs=xs,
        out_shape=jax.ShapeDtypeStruct((b, t, d), F32),
        scratch_shapes=[pltpu.VMEM((tm, d), BF16)],
        compiler_params=_params("parallel", "arbitrary"),
        name="xattn_prompt",
    )(x3, g.reshape(1, d), wq, mem_kv3, mem_kv3, wo)


XS_HALVES = HD_X // LANES
XS_ROWS = XS_HALVES * H_X
XS_BB = 4


def _xattn_sample_kernel(q_ref, mk_ref, mv_ref, o_ref):
    n_col = N_MEM * XS_ROWS
    row = lax.broadcasted_iota(jnp.int32, (SUBLANES, n_col), 0)
    col = lax.broadcasted_iota(jnp.int32, (SUBLANES, n_col), 1)
    valid = (col & (XS_ROWS - 1)) == row
    for i in range(q_ref.shape[0]):
        q8 = _row_select([jnp.broadcast_to(q_ref[i, :, (a % H_X) * HD_X + (a // H_X) * LANES:
                                                 (a % H_X) * HD_X + (a // H_X + 1) * LANES],
                                           (SUBLANES, LANES)) for a in range(XS_ROWS)])
        qb = jnp.concatenate([q8, q8], axis=0).astype(BF16)
        z = jnp.where(valid, _dot_nt(qb, mk_ref[i].astype(BF16))[:SUBLANES], 0.0)
        zr = pltpu.roll(z, H_X, axis=0)
        other = jnp.where(row < H_X, pltpu.roll(zr, n_col - H_X, axis=1),
                          pltpu.roll(zr, H_X, axis=1))
        s = (z + other) * (HD_X ** -0.5)
        m = jnp.max(jnp.where(valid, s, -jnp.inf), axis=-1, keepdims=True)
        p = jnp.where(valid, jnp.exp(s - m), 0.0)
        p = p / jnp.sum(p, axis=-1, keepdims=True)
        pb = jnp.concatenate([p, p], axis=0).astype(BF16)
        o_ref[i] = _dot(pb, mv_ref[i].astype(BF16))[:SUBLANES].astype(o_ref.dtype)


def _xattn_cache_view(cache):
    dp, b = cache.shape[:2]
    c = cache.reshape(dp, b, N_MEM, H_X, XS_HALVES, LANES)
    return jnp.swapaxes(c, 3, 4).reshape(dp, b, N_MEM * XS_ROWS, LANES)


def _xattn_sample(q3, mem_k, mem_v, l):
    bs, _, d = q3.shape
    bb = XS_BB
    ms = pl.BlockSpec((None, bb, N_MEM * XS_ROWS, LANES), lambda b: (l, b, 0, 0))
    o = pl.pallas_call(
        _xattn_sample_kernel,
        grid=(bs // bb,),
        in_specs=[pl.BlockSpec((bb, 1, d), lambda b: (b, 0, 0)), ms, ms],
        out_specs=pl.BlockSpec((bb, XS_ROWS, LANES), lambda b: (b, 0, 0)),
        out_shape=jax.ShapeDtypeStruct((bs, XS_ROWS, LANES), F32),
        compiler_params=_params("parallel"),
        name="xattn_sample",
    )(q3, mem_k, mem_v)
    return jnp.swapaxes(o.reshape(bs, XS_HALVES, H_X, LANES), 1, 2).reshape(bs, d)


FFN_TF = 256
HALO = SUBLANES


FFN_ROW_CHUNKS = 4


def _ffn_prompt_kernel(x_ref, g_ref, wg_ref, wv_ref, dwg_ref, dwv_ref, bg_ref, bv_ref, wd_ref,
                       gout_ref, o_ref, hn_ref, tail_ref, *u_refs, out_norm):
    t = pl.program_id(1)
    f = pl.program_id(2)
    tm = x_ref.shape[1]
    tf = wg_ref.shape[1]
    rc = tm // len(u_refs)

    @pl.when(f == 0)
    def _():
        hn_ref[...] = _rms(x_ref[0], g_ref[...]).astype(BF16)
        o_ref[0] = x_ref[0]

    def conv(u_ref, dw_ref, b_ref, sl):
        c = b_ref[...] + dw_ref[FFN_CONV_W - 1:FFN_CONV_W, :] * u_ref[HALO:, sl]
        for w in range(FFN_CONV_W - 1):
            off = HALO - (FFN_CONV_W - 1) + w
            c += dw_ref[w:w + 1, :] * u_ref[off:off + rc, sl]
        return c

    def up(k):
        hn = hn_ref[k * rc:(k + 1) * rc, :]
        u_refs[k][HALO:, :tf] = _dot(hn, wg_ref[...])
        u_refs[k][HALO:, tf:] = _dot(hn, wv_ref[...])

    up(0)
    for k, u_ref in enumerate(u_refs):
        if k + 1 < len(u_refs):
            up(k + 1)
        if k == 0:
            u_ref[:HALO, :] = jnp.where(t == 0, 0.0, tail_ref[f])
        else:
            u_ref[:HALO, :] = u_refs[k - 1][rc:, :]
        gated = (_silu(conv(u_ref, dwg_ref, bg_ref, slice(0, tf)))
                 * conv(u_ref, dwv_ref, bv_ref, slice(tf, 2 * tf)))
        o_ref[0, k * rc:(k + 1) * rc, :] += _dot(gated.astype(BF16), wd_ref[...])
    tail_ref[f] = u_refs[-1][rc:, :]

    if out_norm:
        @pl.when(f == pl.num_programs(2) - 1)
        def _():
            o_ref[0] = _rms(o_ref[0], gout_ref[...])


def _ffn_prompt(x3, g, w_up, dw, dwb, w_down, g_out=None, tm=2048, row_chunks=FFN_ROW_CHUNKS):
    b, t, d = x3.shape
    tm = min(tm, t)
    nf = D_FF // FFN_TF
    assert D_FF % FFN_TF == 0 and t % tm == 0
    xs = pl.BlockSpec((1, tm, d), lambda bi, i, f: (bi, i, 0))
    dwb2 = dwb.reshape(1, -1)
    vec = pl.BlockSpec((1, d), lambda bi, i, f: (0, 0))
    return pl.pallas_call(
        functools.partial(_ffn_prompt_kernel, out_norm=g_out is not None),
        grid=(b, t // tm, nf),
        in_specs=[xs, pl.BlockSpec((1, d), lambda bi, i, f: (0, 0)),
                  pl.BlockSpec((d, FFN_TF), lambda bi, i, f: (0, f)),
                  pl.BlockSpec((d, FFN_TF), lambda bi, i, f: (0, f + nf)),
                  pl.BlockSpec((FFN_CONV_W, FFN_TF), lambda bi, i, f: (0, f)),
                  pl.BlockSpec((FFN_CONV_W, FFN_TF), lambda bi, i, f: (0, f + nf)),
                  pl.BlockSpec((1, FFN_TF), lambda bi, i, f: (0, f)),
                  pl.BlockSpec((1, FFN_TF), lambda bi, i, f: (0, f + nf)),
                  pl.BlockSpec((FFN_TF, d), lambda bi, i, f: (f, 0)), vec],
        out_specs=xs,
        out_shape=jax.ShapeDtypeStruct((b, t, d), F32),
        scratch_shapes=[pltpu.VMEM((tm, d), BF16), pltpu.VMEM((nf, HALO, 2 * FFN_TF), F32)]
                       + [pltpu.VMEM((HALO + tm // row_chunks, 2 * FFN_TF), F32)] * row_chunks,
        compiler_params=_params("parallel", "arbitrary", "arbitrary"),
        name="ffn_prompt",
    )(x3, g.reshape(1, d), w_up, w_up, dw, dw, dwb2, dwb2, w_down,
      (g if g_out is None else g_out).reshape(1, d))


CONV_HALO = 32
CONV_RB = 32


def _conv_prompt_kernel(u_ref, x_ref, dw_ref, dwb_ref, lg_ref, lb_ref, w2_ref, b2_ref,
                        o_ref, s_ref, c_ref, dwt_ref):
    t = pl.program_id(1)
    tm = u_ref.shape[1]
    n = CONV_HALO + tm

    @pl.when(t == 0)
    def _():
        s_ref[0, :CONV_HALO, :] = jnp.zeros((CONV_HALO, D_MODEL), F32)

    @pl.when(t > 0)
    def _():
        s_ref[0, :CONV_HALO, :] = s_ref[0, tm:, :]

    s_ref[0, CONV_HALO:, :] = u_ref[0]
    for r in range(1, SUBLANES):
        s_ref[r] = pltpu.roll(s_ref[0], n - r, axis=0)
    base = CONV_HALO - (CONV_W - 1)

    @pl.when(t == 0)
    def _():
        for w in range(CONV_W):
            dwt_ref[w] = jnp.broadcast_to(dw_ref[w:w + 1, :], (SUBLANES, D_MODEL))

    n_sub = CONV_RB // SUBLANES

    def row_block(rb, carry):
        r0 = rb * CONV_RB
        bias = jnp.broadcast_to(dwb_ref[...], (SUBLANES, D_MODEL))
        accs = [bias] * n_sub
        for w in range(CONV_W):
            r = (base + w) % SUBLANES
            dwt = dwt_ref[w]
            for k in range(n_sub):
                start = pl.multiple_of(r0 + (base + w - r) + k * SUBLANES, SUBLANES)
                accs[k] = accs[k] + dwt * s_ref[r, pl.ds(start, SUBLANES), :]
        for k in range(n_sub):
            c_ref[pl.ds(pl.multiple_of(r0 + k * SUBLANES, SUBLANES), SUBLANES), :] = accs[k]
        return carry

    lax.fori_loop(0, tm // CONV_RB, row_block, 0)
    c = c_ref[...]
    mu = jnp.mean(c, axis=-1, keepdims=True)
    cc = c - mu
    var = jnp.mean(cc * cc, axis=-1, keepdims=True)
    y = _silu(cc * lax.rsqrt(var + NORM_EPS) * lg_ref[...] + lb_ref[...])
    o_ref[0] = x_ref[0] + _dot(y.astype(BF16), w2_ref[...]) + b2_ref[...]


def _conv_prompt(u3, x3, dw, dwb, ln_g, ln_b, w2, b2, tm=512):
    b, t, d = x3.shape
    tm = min(tm, t)
    xs = pl.BlockSpec((1, tm, d), lambda bi, i: (bi, i, 0))
    vec = pl.BlockSpec((1, d), lambda bi, i: (0, 0))
    return pl.pallas_call(
        _conv_prompt_kernel,
        grid=(b, t // tm),
        in_specs=[xs, xs, pl.BlockSpec((CONV_W, d), lambda bi, i: (0, 0)), vec, vec, vec,
                  pl.BlockSpec((d, d), lambda bi, i: (0, 0)), vec],
        out_specs=xs,
        out_shape=jax.ShapeDtypeStruct((b, t, d), F32),
        scratch_shapes=[pltpu.VMEM((SUBLANES, CONV_HALO + tm, d), F32), pltpu.VMEM((tm, d), F32),
                        pltpu.VMEM((CONV_W, SUBLANES, d), F32)],
        compiler_params=_params("parallel", "arbitrary"),
        name="conv_prompt",
    )(u3, x3, dw, dwb.reshape(1, d), ln_g.reshape(1, d), ln_b.reshape(1, d), w2, b2.reshape(1, d))


STEP_BB = 8


def _conv_step_kernel(st_ref, u_ref, dw_ref, dwb_ref, o_ref):
    w_taps = dw_ref.shape[0]
    for r in range(STEP_BB):
        acc = dwb_ref[...] + dw_ref[w_taps - 1:w_taps, :] * u_ref[r:r + 1, :]
        for w in range(w_taps - 1):
            acc += dw_ref[w:w + 1, :] * st_ref[r, w:w + 1, :]
        o_ref[r:r + 1, :] = acc


def _conv_step(state, l, u, dw, dwb):
    bs, c = u.shape
    w_taps = dw.shape[0]
    return pl.pallas_call(
        _conv_step_kernel,
        grid=(bs // STEP_BB,),
        in_specs=[pl.BlockSpec((None, STEP_BB, w_taps - 1, c), lambda i: (l, i, 0, 0)),
                  pl.BlockSpec((STEP_BB, c), lambda i: (i, 0)),
                  pl.BlockSpec((w_taps, c), lambda i: (0, 0)),
                  pl.BlockSpec((1, c), lambda i: (0, 0))],
        out_specs=pl.BlockSpec((STEP_BB, c), lambda i: (i, 0)),
        out_shape=jax.ShapeDtypeStruct((bs, c), F32),
        compiler_params=_params("parallel"),
        name="conv_step",
    )(state, u, dw, dwb.reshape(1, c))


def _conv_step_slots_kernel(st_ref, u_ref, dw_ref, dwb_ref, o_ref):
    w_taps = dw_ref.shape[0]
    acc = dwb_ref[...] + dw_ref[w_taps - 1:w_taps, :] * u_ref[...]
    for w in range(w_taps - 1):
        acc += dw_ref[w:w + 1, :] * st_ref[w]
    o_ref[...] = acc


def _conv_step_slots(state_t, l, u, dw, dwb):
    bs, c = u.shape
    w_taps = dw.shape[0]
    return pl.pallas_call(
        _conv_step_slots_kernel,
        grid=(bs // STEP_BB,),
        in_specs=[pl.BlockSpec((None, w_taps - 1, STEP_BB, c), lambda i: (l, 0, i, 0)),
                  pl.BlockSpec((STEP_BB, c), lambda i: (i, 0)),
                  pl.BlockSpec((w_taps, c), lambda i: (0, 0)),
                  pl.BlockSpec((1, c), lambda i: (0, 0))],
        out_specs=pl.BlockSpec((STEP_BB, c), lambda i: (i, 0)),
        out_shape=jax.ShapeDtypeStruct((bs, c), F32),
        compiler_params=_params("parallel"),
        name="conv_step_slots",
    )(state_t, u, dw, dwb.reshape(1, c))


def _shift_slots_kernel(*refs, n_layers):
    st_ref, u_refs, o_ref = refs[0], refs[1:1 + n_layers], refs[1 + n_layers]
    n_slots = st_ref.shape[0]
    o_ref[:n_slots - 1] = st_ref[1:]
    layer = pl.program_id(0)
    for n, u_ref in enumerate(u_refs):
        @pl.when(layer == n)
        def _(u_ref=u_ref):
            o_ref[n_slots - 1] = u_ref[...]


def _shift_slots(state_t, us):
    n_layers, n_slots, bs, c = state_t.shape

    def u_spec(n):
        return pl.BlockSpec((STEP_BB, c), lambda l, i: (jnp.where(l == n, i, 0), 0))

    blk = pl.BlockSpec((None, n_slots, STEP_BB, c), lambda l, i: (l, 0, i, 0))
    return pl.pallas_call(
        functools.partial(_shift_slots_kernel, n_layers=n_layers),
        grid=(n_layers, bs // STEP_BB),
        in_specs=[blk] + [u_spec(n) for n in range(n_layers)],
        out_specs=blk,
        out_shape=jax.ShapeDtypeStruct(state_t.shape, F32),
        compiler_params=_params("arbitrary", "arbitrary"),
        name="shift_slots",
    )(state_t, *us)


def _stack_rows_kernel(*refs, n_layers, col_starts):
    srcs, dst = refs[:n_layers], refs[n_layers]
    rows = srcs[0].shape[1]
    layer = pl.program_id(0)
    for n, src in enumerate(srcs):
        @pl.when(layer == n)
        def _(src=src):
            for a, c0 in enumerate(col_starts):
                dst[pl.ds(a, rows, stride=len(col_starts)), :] = src[0, :, c0:c0 + LANES]


def _stack_rows(srcs, col_blocks, width, col_starts, tm):
    n_layers = len(srcs)
    b, t, _ = srcs[0].shape
    tm = min(tm, t)
    nt = t // tm
    n_phase = len(col_starts)

    def src_spec(n):
        def idx(l, bi, i):
            on = l == n
            return (jnp.where(on, bi, 0), jnp.where(on, i, 0), col_blocks[n])
        return pl.BlockSpec((1, tm, width), idx)

    return pl.pallas_call(
        functools.partial(_stack_rows_kernel, n_layers=n_layers, col_starts=tuple(col_starts)),
        grid=(n_layers, b, nt),
        in_specs=[src_spec(n) for n in range(n_layers)],
        out_specs=pl.BlockSpec((None, None, tm * n_phase, LANES), lambda l, bi, i: (l, bi, i, 0)),
        out_shape=jax.ShapeDtypeStruct((n_layers, b, t * n_phase, LANES), F32),
        compiler_params=_params("arbitrary", "arbitrary", "arbitrary"),
        name="stack_rows",
    )(*srcs)


FFN_LAYER_TILING = [(2048, 4), (2048, 2), (1024, 2), (2048, 4)]
XP_LAYER_TILING = [(512, 2), (512, 4), (1024, 4), (1024, 2)]

def kernel(x_prompt, x_sample, cache_sb_k, cache_sb_v, state_ret, state_conv, state_ffn_conv, cache_mem_k, cache_mem_v, page_table, mem_prompt, g_mix, w_in_ab, ret_gn_g, w_out_ab, sb_bias, cv_w1, cv_b1, cv_dw, cv_dwb, cv_ln_g, cv_ln_b, cv_w2, cv_b2, g_cross, xa_wq, xa_wk, xa_wv, xa_wo, g_ffn, ffn_w_up, ffn_dw, ffn_dwb, ffn_w_down, g_final):
    bp, t, d = x_prompt.shape
    bs = x_sample.shape[0]
    n_phys = cache_sb_k.shape[1]

    log_gamma = jnp.log1p(-jnp.exp2(-5.0 - jnp.arange(H_RET, dtype=F32)))
    ret_tabs = _ret_tables(log_gamma)
    cos_p, sin_p = _rope_tables(jnp.arange(t))
    cos_s, sin_s = _rope_tables(PAST_LEN + jnp.arange(1))
    uo = _suffix_matrix()
    conv_slots = jnp.swapaxes(state_conv, 1, 2)
    cache_k = cache_sb_k.reshape(cache_sb_k.shape[0], n_phys, PAGE_SIZE * H_SB, HD_SB)
    cache_v = cache_sb_v.reshape(cache_sb_v.shape[0], n_phys, PAGE_SIZE * H_SB, HD_SB)
    mem_k = _xattn_cache_view(cache_mem_k)
    mem_v = _xattn_cache_view(cache_mem_v)
    w_mem = jnp.concatenate([w[l] for l in range(DEPTH) for w in (xa_wk, xa_wv)], axis=1)
    mem_kv3 = _mm([mem_prompt.reshape(bp * N_MEM, d)], [w_mem.astype(BF16)], tn=d,
                  name="mem_kv").reshape(bp, N_MEM, 2 * DEPTH * d)

    xp = x_prompt.reshape(bp * t, d)
    xs = x_sample.reshape(bs, d)
    sbk_p, sbv_p, sbk_s, sbv_s, ret_p, ret_s = [], [], [], [], [], []
    cv_p, cv_s, ff_p, ff_s = [], [], [], []

    for l in range(DEPTH):
        if l % 2 == 0:
            e = l // 2
            w_in = w_in_ab[e].astype(BF16)
            w_out = w_out_ab[e].astype(BF16)
            w_out_parts = [w_out[:W_RK], w_out[W_RK:]]
            bias = sb_bias[e].astype(F32)
            bias_row = jnp.repeat(bias, SB_TQ).reshape(1, H_SB * SB_TQ)
            bias_rows = jnp.zeros((SUBLANES, PAGE_SIZE * H_SB), F32).at[:H_SB].set(
                jnp.broadcast_to(bias[:, None], (H_SB, PAGE_SIZE * H_SB)))
            proj = _mm([xp], [w_in], prologue="rms", pro=(g_mix[l],), tm=512,
                       tn=w_in.shape[1], name="proj_in")
            proj3 = proj.reshape(bp, t, -1)
            ret_o, s_p = _ret_prompt(proj3, cos_p, sin_p, ret_tabs, ret_gn_g[e])
            sb_o = _sb_prompt(proj3, bias_row, uo)
            xp = _mm([ret_o.reshape(bp * t, W_RK), sb_o.reshape(bp * t, W_SB)], w_out_parts,
                     res=xp, tm=1024, tn=d, name="proj_out")
            sbk_p.append(proj3)
            ret_p.append(s_p)
            proj_s = _mm([xs], [w_in], prologue="rms", pro=(g_mix[l],), name="proj_in_s")
            ret_os, s_s = _ret_sample(proj_s, cos_s, sin_s, ret_gn_g[e], state_ret, e)
            sb_os = _sb_sample(proj_s.reshape(bs, 1, -1), bias_rows, uo, cache_k, cache_v,
                               page_table, e)
            xs = _mm([ret_os, sb_os.reshape(bs, W_SB)], w_out_parts, res=xs, tn=d,
                     name="proj_out_s")
            sbk_s.append(proj_s[:, 5 * W_SB:6 * W_SB].reshape(bs, 1, H_SB, HD_SB))
            sbv_s.append(proj_s[:, 6 * W_SB:7 * W_SB].reshape(bs, 1, H_SB, HD_SB))
            ret_s.append(s_s)
        else:
            o = l // 2
            w1 = cv_w1[o].astype(BF16)
            w2 = cv_w2[o].astype(BF16)
            u = _mm([xp], [w1], prologue="rms", pro=(g_mix[l],), bias=cv_b1[o], glu=True,
                    tm=512, tn=d, name="conv_glu")
            u3 = u.reshape(bp, t, d)
            xp = _conv_prompt(u3, xp.reshape(bp, t, d), cv_dw[o], cv_dwb[o], cv_ln_g[o],
                              cv_ln_b[o], w2, cv_b2[o]).reshape(bp * t, d)
            cv_p.append(u3[:, t - (CONV_W - 1):, :])
            u_s = _mm([xs], [w1], prologue="rms", pro=(g_mix[l],), bias=cv_b1[o], glu=True,
                      name="conv_glu_s")
            c_s = _conv_step_slots(conv_slots, o, u_s, cv_dw[o], cv_dwb[o])
            xs = _mm([c_s], [w2], prologue="ln_silu", pro=(cv_ln_g[o], cv_ln_b[o]), bias=cv_b2[o],
                     res=xs, tn=d, name="conv_out_s")
            cv_s.append(u_s)

        wq = xa_wq[l].astype(BF16)
        wo = xa_wo[l].astype(BF16)
        xp = _xattn_prompt(xp.reshape(bp, t, d), g_cross[l], wq, mem_kv3, 2 * l, 2 * l + 1, wo,
                           tm=XP_LAYER_TILING[l][0], row_chunks=XP_LAYER_TILING[l][1])
        q_s = _mm([xs], [wq], prologue="rms", pro=(g_cross[l],), tn=d, name="xattn_q_s")
        att_s = _xattn_sample(q_s.reshape(bs, 1, d), mem_k, mem_v, l)
        xs = _mm([att_s], [wo], res=xs, tn=d, name="xattn_out_s")

        w_up = ffn_w_up[l].astype(BF16)
        w_down = ffn_w_down[l].astype(BF16)
        tail_rows = xp[:, t - (FFN_CONV_W - 1):, :].reshape(bp * (FFN_CONV_W - 1), d)
        u_st = _mm([jnp.concatenate([xs, tail_rows], axis=0)], [w_up], prologue="rms",
                   pro=(g_ffn[l],), name="ffn_up_s")
        u_s = u_st[:bs]
        ff_p.append(u_st[bs:].reshape(bp, FFN_CONV_W - 1, 2 * D_FF))
        xp = _ffn_prompt(xp, g_ffn[l], w_up, ffn_dw[l], ffn_dwb[l], w_down,
                         g_out=g_final if l == DEPTH - 1 else None, tm=FFN_LAYER_TILING[l][0],
                         row_chunks=FFN_LAYER_TILING[l][1]).reshape(bp * t, d)
        c_s = _conv_step(state_ffn_conv, l, u_s, ffn_dw[l], ffn_dwb[l])
        xs = _mm([c_s], [w_down], prologue="swiglu", res=xs, tn=d, name="ffn_down_s")
        ff_s.append(jnp.concatenate([state_ffn_conv[l][:, 1:], u_s[:, None, :]], axis=1))

    y_prompt = xp.reshape(bp, t, d)
    y_sample = _rmsnorm(xs, g_final).reshape(bs, 1, d)
    sb_cols = [h * HD_SB for h in range(H_SB)]
    sb_shape = (len(sbk_p), bp, t, H_SB, HD_SB)
    sb_k_prompt = _stack_rows(sbk_p, [5] * len(sbk_p), W_SB, sb_cols, 512).reshape(sb_shape)
    sb_v_prompt = _stack_rows(sbk_p, [6] * len(sbk_p), W_SB, sb_cols, 512).reshape(sb_shape)
    mem_cols = [(a % H_X) * HD_X + (a // H_X) * LANES for a in range(XS_ROWS)]

    def mem_out(first_block):
        o = _stack_rows([mem_kv3] * DEPTH, [2 * l + first_block for l in range(DEPTH)], d,
                        mem_cols, N_MEM)
        o = o.reshape(DEPTH, bp, N_MEM, XS_HALVES, H_X, LANES)
        return jnp.swapaxes(o, 3, 4).reshape(DEPTH, bp, N_MEM, H_X, HD_X)

    return (y_prompt, y_sample, sb_k_prompt, sb_v_prompt, jnp.stack(sbk_s),
            jnp.stack(sbv_s), jnp.stack(ret_p), jnp.stack(ret_s), jnp.stack(cv_p),
            jnp.swapaxes(_shift_slots(conv_slots, cv_s), 1, 2), jnp.stack(ff_p),
            jnp.stack(ff_s), mem_out(0), mem_out(1))
```

```python
import functools

import numpy as np
import jax
import jax.numpy as jnp
from jax import lax
from jax.experimental import pallas as pl
from jax.experimental.pallas import tpu as pltpu

F32 = jnp.float32
BF16 = jnp.bfloat16

D_MODEL = 1024
DEPTH = 4
PAST_LEN = 2048
PAGE_SIZE = 128
N_PAGES = PAST_LEN // PAGE_SIZE
H_RET = 4
DK_RET = 128
DV_RET = 128
RET_CHUNK = 128
ROPE_BASE = 10000.0
H_SB = 4
HD_SB = 128
SB_BLOCK = 128
CONV_W = 31
D_FF = 2816
FFN_CONV_W = 3
N_MEM = 256
H_X = 4
HD_X = D_MODEL // H_X
NORM_EPS = 1e-6
W_RK = H_RET * DK_RET
W_SB = H_SB * HD_SB

SUBLANES = 8
LANES = 128
VMEM_LIMIT_BYTES = 56 * 1024 * 1024


def _params(*sem):
    return pltpu.CompilerParams(dimension_semantics=sem, vmem_limit_bytes=VMEM_LIMIT_BYTES)


def _dot(a, b):
    return jnp.dot(a, b, preferred_element_type=F32)


def _dot_nt(a, b):
    return lax.dot_general(a, b, (((1,), (1,)), ((), ())), preferred_element_type=F32)


def _rms(x, g):
    return x * lax.rsqrt(jnp.mean(x * x, axis=-1, keepdims=True) + NORM_EPS) * g


def _silu(x):
    return x * jax.nn.sigmoid(x)


def _softplus(z):
    return jnp.maximum(z, 0.0) + jnp.log(1.0 + jnp.exp(-jnp.abs(z)))


def _split_hi_lo(x):
    hi = x.astype(BF16)
    lo = (x - hi.astype(F32)).astype(BF16)
    return hi, lo


def _mm_kernel(*refs, n_lhs, prologue, has_bias, has_res, glu):
    it = iter(refs)
    x_refs = [next(it) for _ in range(n_lhs)]
    w_refs = [next(it) for _ in range(n_lhs)]
    wg_refs = [next(it) for _ in range(n_lhs)] if glu else []
    n_pro = {None: 0, "rms": 1, "ln_silu": 2, "swiglu": 0}[prologue]
    p_refs = [next(it) for _ in range(n_pro)]
    b_ref = next(it) if has_bias else None
    bg_ref = next(it) if (has_bias and glu) else None
    r_ref = next(it) if has_res else None
    o_ref = next(it)
    xn_ref = next(it) if prologue else None

    if prologue:
        @pl.when(pl.program_id(1) == 0)
        def _():
            x = x_refs[0][...].astype(F32)
            if prologue == "rms":
                y = _rms(x, p_refs[0][...])
            elif prologue == "ln_silu":
                mu = jnp.mean(x, axis=-1, keepdims=True)
                xc = x - mu
                var = jnp.mean(xc * xc, axis=-1, keepdims=True)
                y = _silu(xc * lax.rsqrt(var + NORM_EPS) * p_refs[0][...] + p_refs[1][...])
            else:
                k = x.shape[-1] // 2
                y = _silu(x[:, :k]) * x[:, k:]
            xn_ref[...] = y.astype(BF16)
        lhs = [xn_ref[...]]
    else:
        lhs = [r[...].astype(BF16) for r in x_refs]

    acc = _dot(lhs[0], w_refs[0][...])
    for a, w in zip(lhs[1:], w_refs[1:]):
        acc += _dot(a, w[...])
    if has_bias:
        acc += b_ref[...]
    if glu:
        gate = _dot(lhs[0], wg_refs[0][...])
        for a, w in zip(lhs[1:], wg_refs[1:]):
            gate += _dot(a, w[...])
        if has_bias:
            gate += bg_ref[...]
        acc = acc * jax.nn.sigmoid(gate)
    if has_res:
        acc += r_ref[...]
    o_ref[...] = acc.astype(o_ref.dtype)


def _mm(xs, ws, *, prologue=None, pro=(), bias=None, res=None, glu=False, tm=512, tn=512,
        out_dtype=F32, name="mm"):
    m = xs[0].shape[0]
    n = (ws[0][0] if isinstance(ws[0], tuple) else ws[0]).shape[-1] // (2 if glu else 1)
    tm = min(tm, m)
    tn = min(tn, n)
    assert m % tm == 0 and n % tn == 0, (m, tm, n, tn)
    nj = n // tn

    def w_spec(w, j_off):
        if isinstance(w, tuple):
            arr, layer = w[0], w[1]
            kb, ksz = (w[2], w[3]) if len(w) == 4 else (0, arr.shape[1])
            return arr, pl.BlockSpec((None, ksz, tn), lambda i, j: (layer, kb, j + j_off))
        return w, pl.BlockSpec((w.shape[0], tn), lambda i, j: (0, j + j_off))

    args, specs = [], []
    for x in xs:
        args.append(x)
        specs.append(pl.BlockSpec((tm, x.shape[1]), lambda i, j: (i, 0)))
    for j_off in ([0, nj] if glu else [0]):
        for w in ws:
            arr, spec = w_spec(w, j_off)
            args.append(arr)
            specs.append(spec)
    for p in pro:
        args.append(p.reshape(1, -1))
        specs.append(pl.BlockSpec((1, p.size), lambda i, j: (0, 0)))
    if bias is not None:
        b2 = bias.reshape(1, -1)
        args.append(b2)
        specs.append(pl.BlockSpec((1, tn), lambda i, j: (0, j)))
        if glu:
            args.append(b2)
            specs.append(pl.BlockSpec((1, tn), lambda i, j: (0, j + nj)))
    if res is not None:
        args.append(res)
        specs.append(pl.BlockSpec((tm, tn), lambda i, j: (i, j)))
    scratch = []
    if prologue:
        k_eff = xs[0].shape[1] // (2 if prologue == "swiglu" else 1)
        scratch.append(pltpu.VMEM((tm, k_eff), BF16))
    kern = functools.partial(_mm_kernel, n_lhs=len(xs), prologue=prologue,
                             has_bias=bias is not None, has_res=res is not None, glu=glu)
    return pl.pallas_call(
        kern,
        grid=(m // tm, nj),
        in_specs=specs,
        out_specs=pl.BlockSpec((tm, tn), lambda i, j: (i, j)),
        out_shape=jax.ShapeDtypeStruct((m, n), out_dtype),
        scratch_shapes=scratch,
        compiler_params=_params("parallel", "arbitrary"),
        name=name,
    )(*args)


def _rmsnorm_kernel(x_ref, g_ref, o_ref):
    o_ref[...] = _rms(x_ref[...], g_ref[...])


def _rmsnorm(x, g, tm=1024):
    m, d = x.shape
    tm = min(tm, m)
    return pl.pallas_call(
        _rmsnorm_kernel,
        grid=(m // tm,),
        in_specs=[pl.BlockSpec((tm, d), lambda i: (i, 0)), pl.BlockSpec((1, d), lambda i: (0, 0))],
        out_specs=pl.BlockSpec((tm, d), lambda i: (i, 0)),
        out_shape=jax.ShapeDtypeStruct((m, d), F32),
        compiler_params=_params("parallel"),
        name="rmsnorm",
    )(x, g.reshape(1, d))


def _rotate(x, cos2, sin2):
    return x * cos2 + pltpu.roll(x, DK_RET // 2, axis=1) * sin2


def _head_ln_gate(o, gate, gn):
    mu = jnp.mean(o, axis=-1, keepdims=True)
    oc = o - mu
    var = jnp.mean(oc * oc, axis=-1, keepdims=True)
    return _silu(gate) * (oc * lax.rsqrt(var + NORM_EPS) * gn)


def _ret_prompt_kernel(rq_ref, rk_ref, rv_ref, rg_ref, cos_ref, sin_ref, dmask_ref, qdec_ref,
                       kdec_ref, cdec_ref, gn_ref, o_ref, s_ref):
    c = pl.program_id(1)

    @pl.when(c == 0)
    def _():
        s_ref[...] = jnp.zeros_like(s_ref)

    cos2 = cos_ref[...]
    sin2 = sin_ref[...]
    pairs = [(i, h, slice(h * DK_RET, (h + 1) * DK_RET))
             for i in range(rq_ref.shape[0]) for h in range(H_RET)]
    ks = [_rotate(rk_ref[i, :, sl], cos2, sin2) * (DK_RET ** -0.5) for i, h, sl in pairs]
    qbs = [_rotate(rq_ref[i, :, sl], cos2, sin2).astype(BF16) for i, h, sl in pairs]
    vbs = [rv_ref[i, :, sl].astype(BF16) for i, h, sl in pairs]
    s0s = [s_ref[i, h] for i, h, sl in pairs]
    scores = [(_dot_nt(qb, k.astype(BF16)) * dmask_ref[h]).astype(BF16)
              for (i, h, sl), qb, k in zip(pairs, qbs, ks)]
    inters = [_dot(qb, s0.astype(BF16)) * qdec_ref[h]
              for (i, h, sl), qb, s0 in zip(pairs, qbs, s0s)]
    for n, (i, h, sl) in enumerate(pairs):
        kd_t = (ks[n] * kdec_ref[h]).T.astype(BF16)
        s_ref[i, h] = cdec_ref[h] * s0s[n] + _dot(kd_t, vbs[n])
    for n, (i, h, sl) in enumerate(pairs):
        o = _dot(scores[n], vbs[n]) + inters[n]
        o_ref[i, :, sl] = _head_ln_gate(o, rg_ref[i, :, sl], gn_ref[:, sl]).astype(o_ref.dtype)


def _ret_tables(log_gamma):
    idx = jnp.arange(RET_CHUNK)
    diff = idx[:, None] - idx[None, :]
    expo = jnp.maximum(diff, 0).astype(F32)[None] * log_gamma[:, None, None]
    dmask = jnp.where(diff[None] >= 0, jnp.exp(expo), 0.0)
    q_decay = jnp.exp((idx + 1).astype(F32)[:, None] * log_gamma[None, :])
    k_decay = jnp.exp((RET_CHUNK - 1 - idx).astype(F32)[:, None] * log_gamma[None, :])
    ones = jnp.ones((1, 1, DK_RET), F32)
    qdec = q_decay.T[:, :, None] * ones
    kdec = k_decay.T[:, :, None] * ones
    cdec = jnp.exp(RET_CHUNK * log_gamma)[:, None, None] * jnp.ones((1, DK_RET, DV_RET), F32)
    return dmask, qdec, kdec, cdec


def _rope_tables(pos):
    half = DK_RET // 2
    inv = ROPE_BASE ** (-jnp.arange(half, dtype=F32) / half)
    ang = pos.astype(F32)[:, None] * inv[None, :]
    cos, sin = jnp.cos(ang), jnp.sin(ang)
    return jnp.concatenate([cos, cos], -1), jnp.concatenate([-sin, sin], -1)


def _ret_prompt(proj3, cos2, sin2, tabs, gn, bb):
    b, t, _ = proj3.shape
    nc = t // RET_CHUNK
    dmask, qdec, kdec, cdec = tabs
    bb = min(bb, b)

    def col(cb):
        return pl.BlockSpec((bb, RET_CHUNK, W_RK), lambda i, c: (i, c, cb))

    tab = pl.BlockSpec((H_RET, RET_CHUNK, RET_CHUNK), lambda i, c: (0, 0, 0))
    return pl.pallas_call(
        _ret_prompt_kernel,
        grid=(b // bb, nc),
        in_specs=[col(0), col(1), col(2), col(3),
                  pl.BlockSpec((RET_CHUNK, DK_RET), lambda i, c: (c, 0)),
                  pl.BlockSpec((RET_CHUNK, DK_RET), lambda i, c: (c, 0)),
                  tab, tab, tab, tab,
                  pl.BlockSpec((1, W_RK), lambda i, c: (0, 0))],
        out_specs=[pl.BlockSpec((bb, RET_CHUNK, W_RK), lambda i, c: (i, c, 0)),
                   pl.BlockSpec((bb, H_RET, DK_RET, DV_RET), lambda i, c: (i, 0, 0, 0))],
        out_shape=[jax.ShapeDtypeStruct((b, t, W_RK), BF16),
                   jax.ShapeDtypeStruct((b, H_RET, DK_RET, DV_RET), F32)],
        compiler_params=_params("parallel", "arbitrary"),
        name="ret_prompt",
    )(proj3, proj3, proj3, proj3, cos2, sin2, dmask, qdec, kdec, cdec, gn.reshape(1, W_RK))


RET_BB = 8
MXU_ROWS = 16


def _ret_sample_kernel(rq_ref, rk_ref, rv_ref, rg_ref, cos_ref, sin_ref, gn_ref, st_ref,
                       o_ref, so_ref, inter_ref, *, g1):
    cos2 = cos_ref[...]
    sin2 = sin_ref[...]
    eye = (lax.broadcasted_iota(jnp.int32, (DK_RET, DK_RET), 0)
           == lax.broadcasted_iota(jnp.int32, (DK_RET, DK_RET), 1))
    for h in range(H_RET):
        sl = slice(h * DK_RET, (h + 1) * DK_RET)
        q = _rotate(rq_ref[:, sl], cos2, sin2)
        k = _rotate(rk_ref[:, sl], cos2, sin2) * (DK_RET ** -0.5)
        v = rv_ref[:, sl]
        for r in range(RET_BB):
            s0 = st_ref[r, h]
            qr = jnp.broadcast_to(q[r:r + 1], (MXU_ROWS, DK_RET)).astype(BF16)
            inter_ref[r:r + 1, :] = _dot(qr, s0.astype(BF16))[0:1] * g1[h]
            diag_k = jnp.where(eye, jnp.broadcast_to(k[r:r + 1], (DK_RET, DK_RET)), 0.0).astype(BF16)
            v_rows = jnp.broadcast_to(v[r:r + 1], (DK_RET, DV_RET)).astype(BF16)
            so_ref[r, h] = g1[h] * s0 + _dot(diag_k, v_rows)
        o = jnp.sum(q * k, axis=-1, keepdims=True) * v + inter_ref[...]
        o_ref[:, sl] = _head_ln_gate(o, rg_ref[:, sl], gn_ref[:, sl]).astype(o_ref.dtype)


def _ret_sample(proj, cos2, sin2, gn, state, e):
    bs = proj.shape[0]
    log_gamma = np.log1p(-np.exp2(-5.0 - np.arange(H_RET, dtype=np.float32)))
    g1 = tuple(float(x) for x in np.exp(log_gamma).astype(np.float32))

    def col(cb):
        return pl.BlockSpec((RET_BB, W_RK), lambda i: (i, cb))

    row = pl.BlockSpec((1, DK_RET), lambda i: (0, 0))
    return pl.pallas_call(
        functools.partial(_ret_sample_kernel, g1=g1),
        grid=(bs // RET_BB,),
        in_specs=[col(0), col(1), col(2), col(3), row, row,
                  pl.BlockSpec((1, W_RK), lambda i: (0, 0)),
                  pl.BlockSpec((None, RET_BB, H_RET, DK_RET, DV_RET), lambda i: (e, i, 0, 0, 0))],
        out_specs=[pl.BlockSpec((RET_BB, W_RK), lambda i: (i, 0)),
                   pl.BlockSpec((RET_BB, H_RET, DK_RET, DV_RET), lambda i: (i, 0, 0, 0))],
        out_shape=[jax.ShapeDtypeStruct((bs, W_RK), BF16),
                   jax.ShapeDtypeStruct((bs, H_RET, DK_RET, DV_RET), F32)],
        scratch_shapes=[pltpu.VMEM((RET_BB, DV_RET), F32)],
        compiler_params=_params("parallel"),
        name="ret_sample",
    )(proj, proj, proj, proj, cos2, sin2, gn.reshape(1, W_RK), state)


def _suffix_matrix():
    j = jnp.arange(SB_BLOCK)
    u = (j[:, None] > j[None, :]).astype(BF16)
    half = jnp.concatenate([u, jnp.ones((SB_BLOCK, SB_BLOCK), BF16)], axis=1)
    return jnp.concatenate([half, half], axis=0)


def _suffix_sums(log_1m, uo):
    hi, lo = _split_hi_lo(log_1m)
    return _dot(jnp.concatenate([hi, lo], axis=1), uo)


SB_TQ = 256


def _sb_prompt_kernel(q_ref, k_ref, v_ref, bias_ref, uo_ref, o_ref, acc_ref, carry_ref):
    i = pl.program_id(1)
    uo = uo_ref[...]
    heads = [slice(h * HD_SB, (h + 1) * HD_SB) for h in range(H_SB)]

    n_sub = SB_TQ // SB_BLOCK

    def block(kb, masked, first):
        start = pl.multiple_of(kb * SB_TQ, SB_TQ)
        z = jnp.concatenate(
            [_dot_nt(q_ref[0, :, sl].astype(BF16), k_ref[0, pl.ds(start, SB_TQ), sl].astype(BF16))
             * (HD_SB ** -0.5) + bias_ref[:, h * SB_TQ:(h + 1) * SB_TQ]
             for h, sl in enumerate(heads)], axis=0)
        sp = _softplus(z)
        if masked:
            qpos = i * SB_TQ + (lax.broadcasted_iota(jnp.int32, z.shape, 0) & (SB_TQ - 1))
            kpos = start + lax.broadcasted_iota(jnp.int32, z.shape, 1)
            valid = kpos < qpos
            log_1m = jnp.where(valid, -sp, 0.0)
        else:
            log_1m = -sp
        subs = [slice(n * SB_BLOCK, (n + 1) * SB_BLOCK) for n in range(n_sub)]
        rs = [_suffix_sums(log_1m[:, sb], uo) for sb in subs]
        carry = None if first else carry_ref[...]
        after = [None] * n_sub
        for n in reversed(range(n_sub)):
            after[n] = rs[n][:, :SB_BLOCK] if carry is None else rs[n][:, :SB_BLOCK] + carry
            total = rs[n][:, SB_BLOCK:]
            carry = total if carry is None else carry + total
        a = jnp.exp(z - sp + jnp.concatenate(after, axis=1))
        if masked:
            a = jnp.where(valid, a, 0.0)
        ab = a.astype(BF16)
        pv = jnp.concatenate(
            [_dot(ab[h * SB_TQ:(h + 1) * SB_TQ], v_ref[0, pl.ds(start, SB_TQ), sl].astype(BF16))
             for h, sl in enumerate(heads)], axis=0)
        acc_ref[...] = pv if first else acc_ref[...] + pv
        carry_ref[...] = carry

    block(i, True, True)

    def body(t, _):
        block(i - 1 - t, False, False)
        return 0

    lax.fori_loop(0, i, body, 0)
    for h, sl in enumerate(heads):
        o_ref[0, :, sl] = acc_ref[h * SB_TQ:(h + 1) * SB_TQ, :].astype(o_ref.dtype)


def _sb_prompt(proj3, bias_row, uo):
    b, t, _ = proj3.shape
    tq = min(SB_TQ, t)
    assert tq == SB_TQ and t % tq == 0
    return pl.pallas_call(
        _sb_prompt_kernel,
        grid=(b, t // tq),
        in_specs=[pl.BlockSpec((1, tq, W_SB), lambda bi, i: (bi, i, 4)),
                  pl.BlockSpec((1, t, W_SB), lambda bi, i: (bi, 0, 5)),
                  pl.BlockSpec((1, t, W_SB), lambda bi, i: (bi, 0, 6)),
                  pl.BlockSpec((1, H_SB * SB_TQ), lambda bi, i: (0, 0)),
                  pl.BlockSpec((2 * SB_BLOCK, 2 * SB_BLOCK), lambda bi, i: (0, 0))],
        out_specs=pl.BlockSpec((1, tq, W_SB), lambda bi, i: (bi, i, 0)),
        out_shape=jax.ShapeDtypeStruct((b, t, W_SB), BF16),
        scratch_shapes=[pltpu.VMEM((H_SB * tq, HD_SB), F32),
                        pltpu.VMEM((H_SB * tq, SB_BLOCK), F32)],
        compiler_params=_params("parallel", "arbitrary"),
        name="sb_prompt",
    )(proj3, proj3, proj3, bias_row, uo)


def _row_select(parts):
    r = lax.broadcasted_iota(jnp.int32, parts[0].shape, 0)
    out = jnp.zeros_like(parts[0])
    for h, p in enumerate(parts):
        out = jnp.where(r == h, p, out)
    return out


def _sb_sample_kernel(pt_ref, q_ref, bias_ref, uo_ref, *refs):
    del pt_ref
    k_refs = refs[:N_PAGES]
    v_refs = refs[N_PAGES:2 * N_PAGES]
    o_ref = refs[2 * N_PAGES]
    uo = uo_ref[...]
    n_col = PAGE_SIZE * H_SB
    n_grp = n_col // SB_BLOCK
    q4 = _row_select([jnp.broadcast_to(q_ref[0, :, h * HD_SB:(h + 1) * HD_SB], (SUBLANES, HD_SB))
                      for h in range(H_SB)])
    qb = jnp.concatenate([q4, q4], axis=0).astype(BF16)
    row = lax.broadcasted_iota(jnp.int32, (SUBLANES, n_col), 0)
    col = lax.broadcasted_iota(jnp.int32, (SUBLANES, n_col), 1)
    valid = (col & (H_SB - 1)) == row
    bias = bias_ref[...]
    zs, sps, his, los = [], [], [], []
    for p in range(N_PAGES):
        z = _dot_nt(qb, k_refs[p][...].astype(BF16))[:SUBLANES] * (HD_SB ** -0.5) + bias
        sp = _softplus(z)
        hi, lo = _split_hi_lo(jnp.where(valid, -sp, 0.0))
        zs.append(z)
        sps.append(sp)
        his += [hi[:, g * SB_BLOCK:(g + 1) * SB_BLOCK] for g in range(n_grp)]
        los += [lo[:, g * SB_BLOCK:(g + 1) * SB_BLOCK] for g in range(n_grp)]
    n_all = N_PAGES * n_grp
    r = _dot(jnp.concatenate([jnp.concatenate(his, axis=0), jnp.concatenate(los, axis=0)],
                             axis=1), uo)
    carry = jnp.zeros((SUBLANES, SB_BLOCK), F32)
    after = [None] * n_all
    for g in reversed(range(n_all)):
        rows = slice(g * SUBLANES, (g + 1) * SUBLANES)
        after[g] = r[rows, :SB_BLOCK] + carry
        carry = carry + r[rows, SB_BLOCK:]
    acc = jnp.zeros((MXU_ROWS, HD_SB), F32)
    for p in range(N_PAGES):
        aft = jnp.concatenate(after[p * n_grp:(p + 1) * n_grp], axis=1)
        a = jnp.where(valid, jnp.exp(zs[p] - sps[p] + aft), 0.0)
        ab = jnp.concatenate([a, a], axis=0).astype(BF16)
        acc += _dot(ab, v_refs[p][...].astype(BF16))
    o_ref[0] = acc[:H_SB].astype(o_ref.dtype)


def _sb_sample(proj3, bias_rows, uo, cache_k, cache_v, page_table, e):
    bs = proj3.shape[0]

    def page(p):
        return pl.BlockSpec((None, None, PAGE_SIZE * H_SB, HD_SB),
                            lambda b, pt: (e, pt[b, p], 0, 0))

    grid_spec = pltpu.PrefetchScalarGridSpec(
        num_scalar_prefetch=1,
        grid=(bs,),
        in_specs=[pl.BlockSpec((1, 1, W_SB), lambda b, pt: (b, 0, 4)),
                  pl.BlockSpec((SUBLANES, PAGE_SIZE * H_SB), lambda b, pt: (0, 0)),
                  pl.BlockSpec((2 * SB_BLOCK, 2 * SB_BLOCK), lambda b, pt: (0, 0))]
                 + [page(p) for p in range(N_PAGES)] + [page(p) for p in range(N_PAGES)],
        out_specs=pl.BlockSpec((1, H_SB, HD_SB), lambda b, pt: (b, 0, 0)),
    )
    return pl.pallas_call(
        _sb_sample_kernel,
        grid_spec=grid_spec,
        out_shape=jax.ShapeDtypeStruct((bs, H_SB, HD_SB), F32),
        compiler_params=_params("arbitrary"),
        name="sb_sample",
    )(page_table, proj3, bias_rows, uo, *([cache_k] * N_PAGES), *([cache_v] * N_PAGES))


def _softmax_rows(s):
    m = jnp.max(s, axis=-1, keepdims=True)
    p = jnp.exp(s - m)
    return p / jnp.sum(p, axis=-1, keepdims=True)


XP_ROW_CHUNKS = 4


def _xattn_prompt_kernel(x_ref, g_ref, wq_ref, mk_ref, mv_ref, wo_ref, o_ref, att_ref, *,
                         n_chunk):
    tm = x_ref.shape[1]
    rc = tm // n_chunk
    heads = [slice(h * HD_X, (h + 1) * HD_X) for h in range(H_X)]
    mkb = [mk_ref[0, :, sl].astype(BF16) for sl in heads]
    mvb = [mv_ref[0, :, sl].astype(BF16) for sl in heads]

    def rows(c):
        return slice(c * rc, (c + 1) * rc)

    def query(c):
        hn = _rms(x_ref[0, rows(c), :], g_ref[...]).astype(BF16)
        return _dot(hn, wq_ref[...]).astype(BF16)

    def scores(q):
        return [_dot_nt(q[:, sl], mkb[h]) * (HD_X ** -0.5) for h, sl in enumerate(heads)]

    s = scores(query(0))
    for c in range(n_chunk):
        q_next = query(c + 1) if c + 1 < n_chunk else None
        for h, sl in enumerate(heads):
            att_ref[rows(c), sl] = _dot(_softmax_rows(s[h]).astype(BF16), mvb[h]).astype(BF16)
        if q_next is not None:
            s = scores(q_next)
        o_ref[0, rows(c), :] = x_ref[0, rows(c), :] + _dot(att_ref[rows(c), :], wo_ref[...])


def _xattn_prompt(x3, g, wq, mem_kv3, k_block, v_block, wo, tm=1024, row_chunks=XP_ROW_CHUNKS):
    b, t, d = x3.shape
    tm = min(tm, t)
    xs = pl.BlockSpec((1, tm, d), lambda bi, i: (bi, i, 0))
    ws = pl.BlockSpec((d, d), lambda bi, i: (0, 0))
    assert tm % (row_chunks * SUBLANES) == 0
    return pl.pallas_call(
        functools.partial(_xattn_prompt_kernel, n_chunk=row_chunks),
        grid=(b, t // tm),
        in_specs=[xs, pl.BlockSpec((1, d), lambda bi, i: (0, 0)), ws,
                  pl.BlockSpec((1, N_MEM, d), lambda bi, i: (bi, 0, k_block)),
                  pl.BlockSpec((1, N_MEM, d), lambda bi, i: (bi, 0, v_block)), ws],
        out_specs=xs,
        out_shape=jax.ShapeDtypeStruct((b, t, d), F32),
        scratch_shapes=[pltpu.VMEM((tm, d), BF16)],
        compiler_params=_params("parallel", "arbitrary"),
        name="xattn_prompt",
    )(x3, g.reshape(1, d), wq, mem_kv3, mem_kv3, wo)


XS_HALVES = HD_X // LANES
XS_ROWS = XS_HALVES * H_X
XS_BB = 4


def _xattn_sample_kernel(q_ref, mk_ref, mv_ref, o_ref):
    n_col = N_MEM * XS_ROWS
    row = lax.broadcasted_iota(jnp.int32, (SUBLANES, n_col), 0)
    col = lax.broadcasted_iota(jnp.int32, (SUBLANES, n_col), 1)
    valid = (col & (XS_ROWS - 1)) == row
    for i in range(q_ref.shape[0]):
        q8 = _row_select([jnp.broadcast_to(q_ref[i, :, (a % H_X) * HD_X + (a // H_X) * LANES:
                                                 (a % H_X) * HD_X + (a // H_X + 1) * LANES],
                                           (SUBLANES, LANES)) for a in range(XS_ROWS)])
        qb = jnp.concatenate([q8, q8], axis=0).astype(BF16)
        z = jnp.where(valid, _dot_nt(qb, mk_ref[i].astype(BF16))[:SUBLANES], 0.0)
        zr = pltpu.roll(z, H_X, axis=0)
        other = jnp.where(row < H_X, pltpu.roll(zr, n_col - H_X, axis=1),
                          pltpu.roll(zr, H_X, axis=1))
        s = (z + other) * (HD_X ** -0.5)
        m = jnp.max(jnp.where(valid, s, -jnp.inf), axis=-1, keepdims=True)
        p = jnp.where(valid, jnp.exp(s - m), 0.0)
        p = p / jnp.sum(p, axis=-1, keepdims=True)
        pb = jnp.concatenate([p, p], axis=0).astype(BF16)
        o_ref[i] = _dot(pb, mv_ref[i].astype(BF16))[:SUBLANES].astype(o_ref.dtype)


def _xattn_cache_view(cache):
    dp, b = cache.shape[:2]
    c = cache.reshape(dp, b, N_MEM, H_X, XS_HALVES, LANES)
    return jnp.swapaxes(c, 3, 4).reshape(dp, b, N_MEM * XS_ROWS, LANES)


def _xattn_sample(q3, mem_k, mem_v, l):
    bs, _, d = q3.shape
    bb = XS_BB
    ms = pl.BlockSpec((None, bb, N_MEM * XS_ROWS, LANES), lambda b: (l, b, 0, 0))
    o = pl.pallas_call(
        _xattn_sample_kernel,
        grid=(bs // bb,),
        in_specs=[pl.BlockSpec((bb, 1, d), lambda b: (b, 0, 0)), ms, ms],
        out_specs=pl.BlockSpec((bb, XS_ROWS, LANES), lambda b: (b, 0, 0)),
        out_shape=jax.ShapeDtypeStruct((bs, XS_ROWS, LANES), F32),
        compiler_params=_params("parallel"),
        name="xattn_sample",
    )(q3, mem_k, mem_v)
    return jnp.swapaxes(o.reshape(bs, XS_HALVES, H_X, LANES), 1, 2).reshape(bs, d)


FFN_TF = 256
HALO = SUBLANES


FFN_ROW_CHUNKS = 4


def _ffn_prompt_kernel(x_ref, g_ref, wg_ref, wv_ref, dwg_ref, dwv_ref, bg_ref, bv_ref, wd_ref,
                       gout_ref, o_ref, hn_ref, tail_ref, *u_refs, out_norm, lookahead):
    t = pl.program_id(1)
    f = pl.program_id(2)
    tm = x_ref.shape[1]
    tf = wg_ref.shape[1]
    rc = tm // len(u_refs)

    @pl.when(f == 0)
    def _():
        hn_ref[...] = _rms(x_ref[0], g_ref[...]).astype(BF16)
        o_ref[0] = x_ref[0]

    def conv(u_ref, dw_ref, b_ref, sl):
        c = b_ref[...] + dw_ref[FFN_CONV_W - 1:FFN_CONV_W, :] * u_ref[HALO:, sl]
        for w in range(FFN_CONV_W - 1):
            off = HALO - (FFN_CONV_W - 1) + w
            c += dw_ref[w:w + 1, :] * u_ref[off:off + rc, sl]
        return c

    def up(k):
        hn = hn_ref[k * rc:(k + 1) * rc, :]
        u_refs[k][HALO:, :tf] = _dot(hn, wg_ref[...])
        u_refs[k][HALO:, tf:] = _dot(hn, wv_ref[...])

    for k in range(min(lookahead, len(u_refs))):
        up(k)
    for k, u_ref in enumerate(u_refs):
        if k + lookahead < len(u_refs):
            up(k + lookahead)
        if k == 0:
            u_ref[:HALO, :] = jnp.where(t == 0, 0.0, tail_ref[f])
        else:
            u_ref[:HALO, :] = u_refs[k - 1][rc:, :]
        gated = (_silu(conv(u_ref, dwg_ref, bg_ref, slice(0, tf)))
                 * conv(u_ref, dwv_ref, bv_ref, slice(tf, 2 * tf)))
        o_ref[0, k * rc:(k + 1) * rc, :] += _dot(gated.astype(BF16), wd_ref[...])
    tail_ref[f] = u_refs[-1][rc:, :]

    if out_norm:
        @pl.when(f == pl.num_programs(2) - 1)
        def _():
            o_ref[0] = _rms(o_ref[0], gout_ref[...])


def _ffn_prompt(x3, g, w_up, dw, dwb, w_down, layer, g_out=None, tm=2048,
                row_chunks=FFN_ROW_CHUNKS, lookahead=1):
    b, t, d = x3.shape
    tm = min(tm, t)
    nf = D_FF // FFN_TF
    assert D_FF % FFN_TF == 0 and t % tm == 0
    xs = pl.BlockSpec((1, tm, d), lambda bi, i, f: (bi, i, 0))
    dwb2 = dwb.reshape(1, -1)
    vec = pl.BlockSpec((1, d), lambda bi, i, f: (0, 0))
    return pl.pallas_call(
        functools.partial(_ffn_prompt_kernel, out_norm=g_out is not None, lookahead=lookahead),
        grid=(b, t // tm, nf),
        in_specs=[xs, pl.BlockSpec((1, d), lambda bi, i, f: (0, 0)),
                  pl.BlockSpec((None, d, FFN_TF), lambda bi, i, f: (layer, 0, f)),
                  pl.BlockSpec((None, d, FFN_TF), lambda bi, i, f: (layer, 0, f + nf)),
                  pl.BlockSpec((FFN_CONV_W, FFN_TF), lambda bi, i, f: (0, f)),
                  pl.BlockSpec((FFN_CONV_W, FFN_TF), lambda bi, i, f: (0, f + nf)),
                  pl.BlockSpec((1, FFN_TF), lambda bi, i, f: (0, f)),
                  pl.BlockSpec((1, FFN_TF), lambda bi, i, f: (0, f + nf)),
                  pl.BlockSpec((None, FFN_TF, d), lambda bi, i, f: (layer, f, 0)), vec],
        out_specs=xs,
        out_shape=jax.ShapeDtypeStruct((b, t, d), F32),
        scratch_shapes=[pltpu.VMEM((tm, d), BF16), pltpu.VMEM((nf, HALO, 2 * FFN_TF), F32)]
                       + [pltpu.VMEM((HALO + tm // row_chunks, 2 * FFN_TF), F32)] * row_chunks,
        compiler_params=_params("parallel", "arbitrary", "arbitrary"),
        name="ffn_prompt",
    )(x3, g.reshape(1, d), w_up, w_up, dw, dw, dwb2, dwb2, w_down,
      (g if g_out is None else g_out).reshape(1, d))


CONV_HALO = 32
CONV_RB = 32


def _conv_prompt_kernel(u_ref, x_ref, dw_ref, dwb_ref, lg_ref, lb_ref, w2_ref, b2_ref,
                        o_ref, s_ref, c_ref, dwt_ref):
    t = pl.program_id(1)
    tm = u_ref.shape[1]
    n = CONV_HALO + tm

    @pl.when(t == 0)
    def _():
        s_ref[0, :CONV_HALO, :] = jnp.zeros((CONV_HALO, D_MODEL), F32)

    @pl.when(t > 0)
    def _():
        s_ref[0, :CONV_HALO, :] = s_ref[0, tm:, :]

    s_ref[0, CONV_HALO:, :] = u_ref[0]
    for r in range(1, SUBLANES):
        s_ref[r] = pltpu.roll(s_ref[0], n - r, axis=0)
    base = CONV_HALO - (CONV_W - 1)

    @pl.when(t == 0)
    def _():
        for w in range(CONV_W):
            dwt_ref[w] = jnp.broadcast_to(dw_ref[w:w + 1, :], (SUBLANES, D_MODEL))

    n_sub = CONV_RB // SUBLANES

    def row_block(rb, carry):
        r0 = rb * CONV_RB
        bias = jnp.broadcast_to(dwb_ref[...], (SUBLANES, D_MODEL))
        accs = [bias] * n_sub
        for w in range(CONV_W):
            r = (base + w) % SUBLANES
            dwt = dwt_ref[w]
            for k in range(n_sub):
                start = pl.multiple_of(r0 + (base + w - r) + k * SUBLANES, SUBLANES)
                accs[k] = accs[k] + dwt * s_ref[r, pl.ds(start, SUBLANES), :]
        for k in range(n_sub):
            c_ref[pl.ds(pl.multiple_of(r0 + k * SUBLANES, SUBLANES), SUBLANES), :] = accs[k]
        return carry

    lax.fori_loop(0, tm // CONV_RB, row_block, 0)
    c = c_ref[...]
    mu = jnp.mean(c, axis=-1, keepdims=True)
    cc = c - mu
    var = jnp.mean(cc * cc, axis=-1, keepdims=True)
    y = _silu(cc * lax.rsqrt(var + NORM_EPS) * lg_ref[...] + lb_ref[...])
    o_ref[0] = x_ref[0] + _dot(y.astype(BF16), w2_ref[...]) + b2_ref[...]


def _conv_prompt(u3, x3, dw, dwb, ln_g, ln_b, w2, b2, tm=512):
    b, t, d = x3.shape
    tm = min(tm, t)
    xs = pl.BlockSpec((1, tm, d), lambda bi, i: (bi, i, 0))
    vec = pl.BlockSpec((1, d), lambda bi, i: (0, 0))
    return pl.pallas_call(
        _conv_prompt_kernel,
        grid=(b, t // tm),
        in_specs=[xs, xs, pl.BlockSpec((CONV_W, d), lambda bi, i: (0, 0)), vec, vec, vec,
                  pl.BlockSpec((d, d), lambda bi, i: (0, 0)), vec],
        out_specs=xs,
        out_shape=jax.ShapeDtypeStruct((b, t, d), F32),
        scratch_shapes=[pltpu.VMEM((SUBLANES, CONV_HALO + tm, d), F32), pltpu.VMEM((tm, d), F32),
                        pltpu.VMEM((CONV_W, SUBLANES, d), F32)],
        compiler_params=_params("parallel", "arbitrary"),
        name="conv_prompt",
    )(u3, x3, dw, dwb.reshape(1, d), ln_g.reshape(1, d), ln_b.reshape(1, d), w2, b2.reshape(1, d))


STEP_BB = 8


def _conv_step_kernel(st_ref, u_ref, dw_ref, dwb_ref, o_ref):
    w_taps = dw_ref.shape[0]
    for r in range(STEP_BB):
        acc = dwb_ref[...] + dw_ref[w_taps - 1:w_taps, :] * u_ref[r:r + 1, :]
        for w in range(w_taps - 1):
            acc += dw_ref[w:w + 1, :] * st_ref[r, w:w + 1, :]
        o_ref[r:r + 1, :] = acc


def _conv_step(state, l, u, dw, dwb):
    bs, c = u.shape
    w_taps = dw.shape[0]
    return pl.pallas_call(
        _conv_step_kernel,
        grid=(bs // STEP_BB,),
        in_specs=[pl.BlockSpec((None, STEP_BB, w_taps - 1, c), lambda i: (l, i, 0, 0)),
                  pl.BlockSpec((STEP_BB, c), lambda i: (i, 0)),
                  pl.BlockSpec((w_taps, c), lambda i: (0, 0)),
                  pl.BlockSpec((1, c), lambda i: (0, 0))],
        out_specs=pl.BlockSpec((STEP_BB, c), lambda i: (i, 0)),
        out_shape=jax.ShapeDtypeStruct((bs, c), F32),
        compiler_params=_params("parallel"),
        name="conv_step",
    )(state, u, dw, dwb.reshape(1, c))


def _conv_step_slots_kernel(st_ref, u_ref, dw_ref, dwb_ref, o_ref):
    w_taps = dw_ref.shape[0]
    acc = dwb_ref[...] + dw_ref[w_taps - 1:w_taps, :] * u_ref[...]
    for w in range(w_taps - 1):
        acc += dw_ref[w:w + 1, :] * st_ref[w]
    o_ref[...] = acc


def _conv_step_slots(state_t, l, u, dw, dwb):
    bs, c = u.shape
    w_taps = dw.shape[0]
    return pl.pallas_call(
        _conv_step_slots_kernel,
        grid=(bs // STEP_BB,),
        in_specs=[pl.BlockSpec((None, w_taps - 1, STEP_BB, c), lambda i: (l, 0, i, 0)),
                  pl.BlockSpec((STEP_BB, c), lambda i: (i, 0)),
                  pl.BlockSpec((w_taps, c), lambda i: (0, 0)),
                  pl.BlockSpec((1, c), lambda i: (0, 0))],
        out_specs=pl.BlockSpec((STEP_BB, c), lambda i: (i, 0)),
        out_shape=jax.ShapeDtypeStruct((bs, c), F32),
        compiler_params=_params("parallel"),
        name="conv_step_slots",
    )(state_t, u, dw, dwb.reshape(1, c))


def _shift_slots_kernel(*refs, n_layers):
    st_ref, u_refs, o_ref = refs[0], refs[1:1 + n_layers], refs[1 + n_layers]
    n_slots = st_ref.shape[0]
    o_ref[:n_slots - 1] = st_ref[1:]
    layer = pl.program_id(0)
    for n, u_ref in enumerate(u_refs):
        @pl.when(layer == n)
        def _(u_ref=u_ref):
            o_ref[n_slots - 1] = u_ref[...]


def _shift_slots(state_t, us):
    n_layers, n_slots, bs, c = state_t.shape

    def u_spec(n):
        return pl.BlockSpec((STEP_BB, c), lambda l, i: (jnp.where(l == n, i, 0), 0))

    blk = pl.BlockSpec((None, n_slots, STEP_BB, c), lambda l, i: (l, 0, i, 0))
    return pl.pallas_call(
        functools.partial(_shift_slots_kernel, n_layers=n_layers),
        grid=(n_layers, bs // STEP_BB),
        in_specs=[blk] + [u_spec(n) for n in range(n_layers)],
        out_specs=blk,
        out_shape=jax.ShapeDtypeStruct(state_t.shape, F32),
        compiler_params=_params("arbitrary", "arbitrary"),
        name="shift_slots",
    )(state_t, *us)


def _stack_rows_kernel(*refs, n_layers, col_starts):
    srcs, dst = refs[:n_layers], refs[n_layers]
    rows = srcs[0].shape[1]
    layer = pl.program_id(0)
    for n, src in enumerate(srcs):
        @pl.when(layer == n)
        def _(src=src):
            for a, c0 in enumerate(col_starts):
                dst[pl.ds(a, rows, stride=len(col_starts)), :] = src[0, :, c0:c0 + LANES]


def _stack_rows(srcs, col_blocks, width, col_starts, tm):
    n_layers = len(srcs)
    b, t, _ = srcs[0].shape
    tm = min(tm, t)
    nt = t // tm
    n_phase = len(col_starts)

    def src_spec(n):
        def idx(l, bi, i):
            on = l == n
            return (jnp.where(on, bi, 0), jnp.where(on, i, 0), col_blocks[n])
        return pl.BlockSpec((1, tm, width), idx)

    return pl.pallas_call(
        functools.partial(_stack_rows_kernel, n_layers=n_layers, col_starts=tuple(col_starts)),
        grid=(n_layers, b, nt),
        in_specs=[src_spec(n) for n in range(n_layers)],
        out_specs=pl.BlockSpec((None, None, tm * n_phase, LANES), lambda l, bi, i: (l, bi, i, 0)),
        out_shape=jax.ShapeDtypeStruct((n_layers, b, t * n_phase, LANES), F32),
        compiler_params=_params("arbitrary", "arbitrary", "arbitrary"),
        name="stack_rows",
    )(*srcs)


FFN_LOOKAHEAD = [1, 2, 3, 1]
RET_PROMPT_BB = [2, 4]

def kernel(x_prompt, x_sample, cache_sb_k, cache_sb_v, state_ret, state_conv, state_ffn_conv, cache_mem_k, cache_mem_v, page_table, mem_prompt, g_mix, w_in_ab, ret_gn_g, w_out_ab, sb_bias, cv_w1, cv_b1, cv_dw, cv_dwb, cv_ln_g, cv_ln_b, cv_w2, cv_b2, g_cross, xa_wq, xa_wk, xa_wv, xa_wo, g_ffn, ffn_w_up, ffn_dw, ffn_dwb, ffn_w_down, g_final):
    bp, t, d = x_prompt.shape
    bs = x_sample.shape[0]
    n_phys = cache_sb_k.shape[1]

    log_gamma = jnp.log1p(-jnp.exp2(-5.0 - jnp.arange(H_RET, dtype=F32)))
    ret_tabs = _ret_tables(log_gamma)
    cos_p, sin_p = _rope_tables(jnp.arange(t))
    cos_s, sin_s = _rope_tables(PAST_LEN + jnp.arange(1))
    uo = _suffix_matrix()
    w_up_all = ffn_w_up.astype(BF16)
    w_down_all = ffn_w_down.astype(BF16)
    conv_slots = jnp.swapaxes(state_conv, 1, 2)
    cache_k = cache_sb_k.reshape(cache_sb_k.shape[0], n_phys, PAGE_SIZE * H_SB, HD_SB)
    cache_v = cache_sb_v.reshape(cache_sb_v.shape[0], n_phys, PAGE_SIZE * H_SB, HD_SB)
    mem_k = _xattn_cache_view(cache_mem_k)
    mem_v = _xattn_cache_view(cache_mem_v)
    w_mem = jnp.concatenate([w[l] for l in range(DEPTH) for w in (xa_wk, xa_wv)], axis=1)
    mem_kv3 = _mm([mem_prompt.reshape(bp * N_MEM, d)], [w_mem.astype(BF16)], tn=d,
                  name="mem_kv").reshape(bp, N_MEM, 2 * DEPTH * d)

    xp = x_prompt.reshape(bp * t, d)
    xs = x_sample.reshape(bs, d)
    sbk_p, sbk_s, sbv_s, ret_p, ret_s = [], [], [], [], []
    cv_p, cv_s, ff_p, ff_s = [], [], [], []

    for l in range(DEPTH):
        if l % 2 == 0:
            e = l // 2
            w_in = w_in_ab[e].astype(BF16)
            w_out = w_out_ab[e].astype(BF16)
            w_out_parts = [w_out[:W_RK], w_out[W_RK:]]
            bias = sb_bias[e].astype(F32)
            bias_row = jnp.repeat(bias, SB_TQ).reshape(1, H_SB * SB_TQ)
            bias_rows = jnp.zeros((SUBLANES, PAGE_SIZE * H_SB), F32).at[:H_SB].set(
                jnp.broadcast_to(bias[:, None], (H_SB, PAGE_SIZE * H_SB)))
            proj = _mm([xp], [w_in], prologue="rms", pro=(g_mix[l],), tm=512,
                       tn=w_in.shape[1], name="proj_in")
            proj3 = proj.reshape(bp, t, -1)
            ret_o, s_p = _ret_prompt(proj3, cos_p, sin_p, ret_tabs, ret_gn_g[e],
                                     bb=RET_PROMPT_BB[e])
            sb_o = _sb_prompt(proj3, bias_row, uo)
            xp = _mm([ret_o.reshape(bp * t, W_RK), sb_o.reshape(bp * t, W_SB)], w_out_parts,
                     res=xp, tm=1024, tn=d, name="proj_out")
            sbk_p.append(proj3)
            ret_p.append(s_p)
            proj_s = _mm([xs], [w_in], prologue="rms", pro=(g_mix[l],), name="proj_in_s")
            ret_os, s_s = _ret_sample(proj_s, cos_s, sin_s, ret_gn_g[e], state_ret, e)
            sb_os = _sb_sample(proj_s.reshape(bs, 1, -1), bias_rows, uo, cache_k, cache_v,
                               page_table, e)
            xs = _mm([ret_os, sb_os.reshape(bs, W_SB)], w_out_parts, res=xs, tn=d,
                     name="proj_out_s")
            sbk_s.append(proj_s[:, 5 * W_SB:6 * W_SB].reshape(bs, 1, H_SB, HD_SB))
            sbv_s.append(proj_s[:, 6 * W_SB:7 * W_SB].reshape(bs, 1, H_SB, HD_SB))
            ret_s.append(s_s)
        else:
            o = l // 2
            w1 = cv_w1[o].astype(BF16)
            w2 = cv_w2[o].astype(BF16)
            u = _mm([xp], [w1], prologue="rms", pro=(g_mix[l],), bias=cv_b1[o], glu=True,
                    tm=512, tn=d, name="conv_glu")
            u3 = u.reshape(bp, t, d)
            xp = _conv_prompt(u3, xp.reshape(bp, t, d), cv_dw[o], cv_dwb[o], cv_ln_g[o],
                              cv_ln_b[o], w2, cv_b2[o]).reshape(bp * t, d)
            cv_p.append(u3[:, t - (CONV_W - 1):, :])
            u_s = _mm([xs], [w1], prologue="rms", pro=(g_mix[l],), bias=cv_b1[o], glu=True,
                      name="conv_glu_s")
            c_s = _conv_step_slots(conv_slots, o, u_s, cv_dw[o], cv_dwb[o])
            xs = _mm([c_s], [w2], prologue="ln_silu", pro=(cv_ln_g[o], cv_ln_b[o]), bias=cv_b2[o],
                     res=xs, tn=d, name="conv_out_s")
            cv_s.append(u_s)

        wq = xa_wq[l].astype(BF16)
        wo = xa_wo[l].astype(BF16)
        xp = _xattn_prompt(xp.reshape(bp, t, d), g_cross[l], wq, mem_kv3, 2 * l, 2 * l + 1, wo)
        q_s = _mm([xs], [wq], prologue="rms", pro=(g_cross[l],), tn=d, name="xattn_q_s")
        att_s = _xattn_sample(q_s.reshape(bs, 1, d), mem_k, mem_v, l)
        xs = _mm([att_s], [wo], res=xs, tn=d, name="xattn_out_s")

        tail_rows = xp[:, t - (FFN_CONV_W - 1):, :].reshape(bp * (FFN_CONV_W - 1), d)
        u_st = _mm([jnp.concatenate([xs, tail_rows], axis=0)], [(w_up_all, l)], prologue="rms",
                   pro=(g_ffn[l],), name="ffn_up_s")
        u_s = u_st[:bs]
        ff_p.append(u_st[bs:].reshape(bp, FFN_CONV_W - 1, 2 * D_FF))
        xp = _ffn_prompt(xp, g_ffn[l], w_up_all, ffn_dw[l], ffn_dwb[l], w_down_all, l,
                         g_out=g_final if l == DEPTH - 1 else None,
                         lookahead=FFN_LOOKAHEAD[l]).reshape(bp * t, d)
        c_s = _conv_step(state_ffn_conv, l, u_s, ffn_dw[l], ffn_dwb[l])
        xs = _mm([c_s], [(w_down_all, l)], prologue="swiglu", res=xs, tn=d, name="ffn_down_s")
        ff_s.append(jnp.concatenate([state_ffn_conv[l][:, 1:], u_s[:, None, :]], axis=1))

    y_prompt = xp.reshape(bp, t, d)
    y_sample = _rmsnorm(xs, g_final).reshape(bs, 1, d)
    sb_cols = [h * HD_SB for h in range(H_SB)]
    sb_shape = (len(sbk_p), bp, t, H_SB, HD_SB)
    sb_k_prompt = _stack_rows(sbk_p, [5] * len(sbk_p), W_SB, sb_cols, 512).reshape(sb_shape)
    sb_v_prompt = _stack_rows(sbk_p, [6] * len(sbk_p), W_SB, sb_cols, 512).reshape(sb_shape)
    mem_cols = [(a % H_X) * HD_X + (a // H_X) * LANES for a in range(XS_ROWS)]

    def mem_out(first_block):
        o = _stack_rows([mem_kv3] * DEPTH, [2 * l + first_block for l in range(DEPTH)], d,
                        mem_cols, N_MEM)
        o = o.reshape(DEPTH, bp, N_MEM, XS_HALVES, H_X, LANES)
        return jnp.swapaxes(o, 3, 4).reshape(DEPTH, bp, N_MEM, H_X, HD_X)

    return (y_prompt, y_sample, sb_k_prompt, sb_v_prompt, jnp.stack(sbk_s),
            jnp.stack(sbv_s), jnp.stack(ret_p), jnp.stack(ret_s), jnp.stack(cv_p),
            jnp.swapaxes(_shift_slots(conv_slots, cv_s), 1, 2), jnp.stack(ff_p),
            jnp.stack(ff_s), mem_out(0), mem_out(1))
```

```python
import functools

import numpy as np
import jax
import jax.numpy as jnp
from jax import lax
from jax.experimental import pallas as pl
from jax.experimental.pallas import tpu as pltpu

F32 = jnp.float32
BF16 = jnp.bfloat16

D_MODEL = 1024
DEPTH = 4
PAST_LEN = 2048
PAGE_SIZE = 128
N_PAGES = PAST_LEN // PAGE_SIZE
H_RET = 4
DK_RET = 128
DV_RET = 128
RET_CHUNK = 128
ROPE_BASE = 10000.0
H_SB = 4
HD_SB = 128
SB_BLOCK = 128
CONV_W = 31
D_FF = 2816
FFN_CONV_W = 3
N_MEM = 256
H_X = 4
HD_X = D_MODEL // H_X
NORM_EPS = 1e-6
W_RK = H_RET * DK_RET
W_SB = H_SB * HD_SB

SUBLANES = 8
LANES = 128
VMEM_LIMIT_BYTES = 56 * 1024 * 1024


def _params(*sem):
    return pltpu.CompilerParams(dimension_semantics=sem, vmem_limit_bytes=VMEM_LIMIT_BYTES)


def _dot(a, b):
    return jnp.dot(a, b, preferred_element_type=F32)


def _dot_nt(a, b):
    return lax.dot_general(a, b, (((1,), (1,)), ((), ())), preferred_element_type=F32)


def _rms(x, g):
    return x * lax.rsqrt(jnp.mean(x * x, axis=-1, keepdims=True) + NORM_EPS) * g


def _silu(x):
    return x * jax.nn.sigmoid(x)


def _softplus(z):
    return jnp.maximum(z, 0.0) + jnp.log(1.0 + jnp.exp(-jnp.abs(z)))


def _split_hi_lo(x):
    hi = x.astype(BF16)
    lo = (x - hi.astype(F32)).astype(BF16)
    return hi, lo


def _mm_kernel(*refs, n_lhs, prologue, has_bias, has_res, glu):
    it = iter(refs)
    x_refs = [next(it) for _ in range(n_lhs)]
    w_refs = [next(it) for _ in range(n_lhs)]
    wg_refs = [next(it) for _ in range(n_lhs)] if glu else []
    n_pro = {None: 0, "rms": 1, "ln_silu": 2, "swiglu": 0}[prologue]
    p_refs = [next(it) for _ in range(n_pro)]
    b_ref = next(it) if has_bias else None
    bg_ref = next(it) if (has_bias and glu) else None
    r_ref = next(it) if has_res else None
    o_ref = next(it)
    xn_ref = next(it) if prologue else None

    if prologue:
        @pl.when(pl.program_id(1) == 0)
        def _():
            x = x_refs[0][...].astype(F32)
            if prologue == "rms":
                y = _rms(x, p_refs[0][...])
            elif prologue == "ln_silu":
                mu = jnp.mean(x, axis=-1, keepdims=True)
                xc = x - mu
                var = jnp.mean(xc * xc, axis=-1, keepdims=True)
                y = _silu(xc * lax.rsqrt(var + NORM_EPS) * p_refs[0][...] + p_refs[1][...])
            else:
                k = x.shape[-1] // 2
                y = _silu(x[:, :k]) * x[:, k:]
            xn_ref[...] = y.astype(BF16)
        lhs = [xn_ref[...]]
    else:
        lhs = [r[...].astype(BF16) for r in x_refs]

    acc = _dot(lhs[0], w_refs[0][...])
    for a, w in zip(lhs[1:], w_refs[1:]):
        acc += _dot(a, w[...])
    if has_bias:
        acc += b_ref[...]
    if glu:
        gate = _dot(lhs[0], wg_refs[0][...])
        for a, w in zip(lhs[1:], wg_refs[1:]):
            gate += _dot(a, w[...])
        if has_bias:
            gate += bg_ref[...]
        acc = acc * jax.nn.sigmoid(gate)
    if has_res:
        acc += r_ref[...]
    o_ref[...] = acc.astype(o_ref.dtype)


def _mm(xs, ws, *, prologue=None, pro=(), bias=None, res=None, glu=False, tm=512, tn=512,
        out_dtype=F32, name="mm"):
    m = xs[0].shape[0]
    n = (ws[0][0] if isinstance(ws[0], tuple) else ws[0]).shape[-1] // (2 if glu else 1)
    tm = min(tm, m)
    tn = min(tn, n)
    assert m % tm == 0 and n % tn == 0, (m, tm, n, tn)
    nj = n // tn

    def w_spec(w, j_off):
        if isinstance(w, tuple):
            arr, layer = w[0], w[1]
            kb, ksz = (w[2], w[3]) if len(w) == 4 else (0, arr.shape[1])
            return arr, pl.BlockSpec((None, ksz, tn), lambda i, j: (layer, kb, j + j_off))
        return w, pl.BlockSpec((w.shape[0], tn), lambda i, j: (0, j + j_off))

    args, specs = [], []
    for x in xs:
        args.append(x)
        specs.append(pl.BlockSpec((tm, x.shape[1]), lambda i, j: (i, 0)))
    for j_off in ([0, nj] if glu else [0]):
        for w in ws:
            arr, spec = w_spec(w, j_off)
            args.append(arr)
            specs.append(spec)
    for p in pro:
        args.append(p.reshape(1, -1))
        specs.append(pl.BlockSpec((1, p.size), lambda i, j: (0, 0)))
    if bias is not None:
        b2 = bias.reshape(1, -1)
        args.append(b2)
        specs.append(pl.BlockSpec((1, tn), lambda i, j: (0, j)))
        if glu:
            args.append(b2)
            specs.append(pl.BlockSpec((1, tn), lambda i, j: (0, j + nj)))
    if res is not None:
        args.append(res)
        specs.append(pl.BlockSpec((tm, tn), lambda i, j: (i, j)))
    scratch = []
    if prologue:
        k_eff = xs[0].shape[1] // (2 if prologue == "swiglu" else 1)
        scratch.append(pltpu.VMEM((tm, k_eff), BF16))
    kern = functools.partial(_mm_kernel, n_lhs=len(xs), prologue=prologue,
                             has_bias=bias is not None, has_res=res is not None, glu=glu)
    return pl.pallas_call(
        kern,
        grid=(m // tm, nj),
        in_specs=specs,
        out_specs=pl.BlockSpec((tm, tn), lambda i, j: (i, j)),
        out_shape=jax.ShapeDtypeStruct((m, n), out_dtype),
        scratch_shapes=scratch,
        compiler_params=_params("parallel", "arbitrary"),
        name=name,
    )(*args)


def _rmsnorm_kernel(x_ref, g_ref, o_ref):
    o_ref[...] = _rms(x_ref[...], g_ref[...])


def _rmsnorm(x, g, tm=1024):
    m, d = x.shape
    tm = min(tm, m)
    return pl.pallas_call(
        _rmsnorm_kernel,
        grid=(m // tm,),
        in_specs=[pl.BlockSpec((tm, d), lambda i: (i, 0)), pl.BlockSpec((1, d), lambda i: (0, 0))],
        out_specs=pl.BlockSpec((tm, d), lambda i: (i, 0)),
        out_shape=jax.ShapeDtypeStruct((m, d), F32),
        compiler_params=_params("parallel"),
        name="rmsnorm",
    )(x, g.reshape(1, d))


def _rotate(x, cos2, sin2):
    return x * cos2 + pltpu.roll(x, DK_RET // 2, axis=1) * sin2


def _head_ln_gate(o, gate, gn):
    mu = jnp.mean(o, axis=-1, keepdims=True)
    oc = o - mu
    var = jnp.mean(oc * oc, axis=-1, keepdims=True)
    return _silu(gate) * (oc * lax.rsqrt(var + NORM_EPS) * gn)


def _ret_prompt_kernel(rq_ref, rk_ref, rv_ref, rg_ref, cos_ref, sin_ref, dmask_ref, qdec_ref,
                       kdec_ref, cdec_ref, gn_ref, o_ref, s_ref):
    c = pl.program_id(1)

    @pl.when(c == 0)
    def _():
        s_ref[...] = jnp.zeros_like(s_ref)

    cos2 = cos_ref[...]
    sin2 = sin_ref[...]
    pairs = [(i, h, slice(h * DK_RET, (h + 1) * DK_RET))
             for i in range(rq_ref.shape[0]) for h in range(H_RET)]
    ks = [_rotate(rk_ref[i, :, sl], cos2, sin2) * (DK_RET ** -0.5) for i, h, sl in pairs]
    qbs = [_rotate(rq_ref[i, :, sl], cos2, sin2).astype(BF16) for i, h, sl in pairs]
    vbs = [rv_ref[i, :, sl].astype(BF16) for i, h, sl in pairs]
    s0s = [s_ref[i, h] for i, h, sl in pairs]
    scores = [(_dot_nt(qb, k.astype(BF16)) * dmask_ref[h]).astype(BF16)
              for (i, h, sl), qb, k in zip(pairs, qbs, ks)]
    inters = [_dot(qb, s0.astype(BF16)) * qdec_ref[h]
              for (i, h, sl), qb, s0 in zip(pairs, qbs, s0s)]
    for n, (i, h, sl) in enumerate(pairs):
        kd_t = (ks[n] * kdec_ref[h]).T.astype(BF16)
        s_ref[i, h] = cdec_ref[h] * s0s[n] + _dot(kd_t, vbs[n])
    for n, (i, h, sl) in enumerate(pairs):
        o = _dot(scores[n], vbs[n]) + inters[n]
        o_ref[i, :, sl] = _head_ln_gate(o, rg_ref[i, :, sl], gn_ref[:, sl]).astype(o_ref.dtype)


def _ret_tables(log_gamma):
    idx = jnp.arange(RET_CHUNK)
    diff = idx[:, None] - idx[None, :]
    expo = jnp.maximum(diff, 0).astype(F32)[None] * log_gamma[:, None, None]
    dmask = jnp.where(diff[None] >= 0, jnp.exp(expo), 0.0)
    q_decay = jnp.exp((idx + 1).astype(F32)[:, None] * log_gamma[None, :])
    k_decay = jnp.exp((RET_CHUNK - 1 - idx).astype(F32)[:, None] * log_gamma[None, :])
    ones = jnp.ones((1, 1, DK_RET), F32)
    qdec = q_decay.T[:, :, None] * ones
    kdec = k_decay.T[:, :, None] * ones
    cdec = jnp.exp(RET_CHUNK * log_gamma)[:, None, None] * jnp.ones((1, DK_RET, DV_RET), F32)
    return dmask, qdec, kdec, cdec


def _rope_tables(pos):
    half = DK_RET // 2
    inv = ROPE_BASE ** (-jnp.arange(half, dtype=F32) / half)
    ang = pos.astype(F32)[:, None] * inv[None, :]
    cos, sin = jnp.cos(ang), jnp.sin(ang)
    return jnp.concatenate([cos, cos], -1), jnp.concatenate([-sin, sin], -1)


def _ret_prompt(proj3, cos2, sin2, tabs, gn, bb):
    b, t, _ = proj3.shape
    nc = t // RET_CHUNK
    dmask, qdec, kdec, cdec = tabs
    bb = min(bb, b)

    def col(cb):
        return pl.BlockSpec((bb, RET_CHUNK, W_RK), lambda i, c: (i, c, cb))

    tab = pl.BlockSpec((H_RET, RET_CHUNK, RET_CHUNK), lambda i, c: (0, 0, 0))
    return pl.pallas_call(
        _ret_prompt_kernel,
        grid=(b // bb, nc),
        in_specs=[col(0), col(1), col(2), col(3),
                  pl.BlockSpec((RET_CHUNK, DK_RET), lambda i, c: (c, 0)),
                  pl.BlockSpec((RET_CHUNK, DK_RET), lambda i, c: (c, 0)),
                  tab, tab, tab, tab,
                  pl.BlockSpec((1, W_RK), lambda i, c: (0, 0))],
        out_specs=[pl.BlockSpec((bb, RET_CHUNK, W_RK), lambda i, c: (i, c, 0)),
                   pl.BlockSpec((bb, H_RET, DK_RET, DV_RET), lambda i, c: (i, 0, 0, 0))],
        out_shape=[jax.ShapeDtypeStruct((b, t, W_RK), BF16),
                   jax.ShapeDtypeStruct((b, H_RET, DK_RET, DV_RET), F32)],
        compiler_params=_params("parallel", "arbitrary"),
        name="ret_prompt",
    )(proj3, proj3, proj3, proj3, cos2, sin2, dmask, qdec, kdec, cdec, gn.reshape(1, W_RK))


RET_BB = 8
MXU_ROWS = 16


def _ret_sample_kernel(rq_ref, rk_ref, rv_ref, rg_ref, cos_ref, sin_ref, gn_ref, st_ref,
                       o_ref, so_ref, inter_ref, *, g1):
    cos2 = cos_ref[...]
    sin2 = sin_ref[...]
    eye = (lax.broadcasted_iota(jnp.int32, (DK_RET, DK_RET), 0)
           == lax.broadcasted_iota(jnp.int32, (DK_RET, DK_RET), 1))
    for h in range(H_RET):
        sl = slice(h * DK_RET, (h + 1) * DK_RET)
        q = _rotate(rq_ref[:, sl], cos2, sin2)
        k = _rotate(rk_ref[:, sl], cos2, sin2) * (DK_RET ** -0.5)
        v = rv_ref[:, sl]
        for r in range(RET_BB):
            s0 = st_ref[r, h]
            qr = jnp.broadcast_to(q[r:r + 1], (MXU_ROWS, DK_RET)).astype(BF16)
            inter_ref[r:r + 1, :] = _dot(qr, s0.astype(BF16))[0:1] * g1[h]
            diag_k = jnp.where(eye, jnp.broadcast_to(k[r:r + 1], (DK_RET, DK_RET)), 0.0).astype(BF16)
            v_rows = jnp.broadcast_to(v[r:r + 1], (DK_RET, DV_RET)).astype(BF16)
            so_ref[r, h] = g1[h] * s0 + _dot(diag_k, v_rows)
        o = jnp.sum(q * k, axis=-1, keepdims=True) * v + inter_ref[...]
        o_ref[:, sl] = _head_ln_gate(o, rg_ref[:, sl], gn_ref[:, sl]).astype(o_ref.dtype)


def _ret_sample(proj, cos2, sin2, gn, state, e):
    bs = proj.shape[0]
    log_gamma = np.log1p(-np.exp2(-5.0 - np.arange(H_RET, dtype=np.float32)))
    g1 = tuple(float(x) for x in np.exp(log_gamma).astype(np.float32))

    def col(cb):
        return pl.BlockSpec((RET_BB, W_RK), lambda i: (i, cb))

    row = pl.BlockSpec((1, DK_RET), lambda i: (0, 0))
    return pl.pallas_call(
        functools.partial(_ret_sample_kernel, g1=g1),
        grid=(bs // RET_BB,),
        in_specs=[col(0), col(1), col(2), col(3), row, row,
                  pl.BlockSpec((1, W_RK), lambda i: (0, 0)),
                  pl.BlockSpec((None, RET_BB, H_RET, DK_RET, DV_RET), lambda i: (e, i, 0, 0, 0))],
        out_specs=[pl.BlockSpec((RET_BB, W_RK), lambda i: (i, 0)),
                   pl.BlockSpec((RET_BB, H_RET, DK_RET, DV_RET), lambda i: (i, 0, 0, 0))],
        out_shape=[jax.ShapeDtypeStruct((bs, W_RK), BF16),
                   jax.ShapeDtypeStruct((bs, H_RET, DK_RET, DV_RET), F32)],
        scratch_shapes=[pltpu.VMEM((RET_BB, DV_RET), F32)],
        compiler_params=_params("parallel"),
        name="ret_sample",
    )(proj, proj, proj, proj, cos2, sin2, gn.reshape(1, W_RK), state)


def _suffix_matrix():
    j = jnp.arange(SB_BLOCK)
    u = (j[:, None] > j[None, :]).astype(BF16)
    half = jnp.concatenate([u, jnp.ones((SB_BLOCK, SB_BLOCK), BF16)], axis=1)
    return jnp.concatenate([half, half], axis=0)


def _suffix_sums(log_1m, uo):
    hi, lo = _split_hi_lo(log_1m)
    return _dot(jnp.concatenate([hi, lo], axis=1), uo)


SB_TQ = 256


def _sb_prompt_kernel(q_ref, k_ref, v_ref, bias_ref, uo_ref, o_ref, acc_ref, carry_ref):
    i = pl.program_id(1)
    uo = uo_ref[...]
    heads = [slice(h * HD_SB, (h + 1) * HD_SB) for h in range(H_SB)]

    n_sub = SB_TQ // SB_BLOCK

    def block(kb, masked, first):
        start = pl.multiple_of(kb * SB_TQ, SB_TQ)
        z = jnp.concatenate(
            [_dot_nt(q_ref[0, :, sl].astype(BF16), k_ref[0, pl.ds(start, SB_TQ), sl].astype(BF16))
             * (HD_SB ** -0.5) + bias_ref[:, h * SB_TQ:(h + 1) * SB_TQ]
             for h, sl in enumerate(heads)], axis=0)
        sp = _softplus(z)
        if masked:
            qpos = i * SB_TQ + (lax.broadcasted_iota(jnp.int32, z.shape, 0) & (SB_TQ - 1))
            kpos = start + lax.broadcasted_iota(jnp.int32, z.shape, 1)
            valid = kpos < qpos
            log_1m = jnp.where(valid, -sp, 0.0)
        else:
            log_1m = -sp
        subs = [slice(n * SB_BLOCK, (n + 1) * SB_BLOCK) for n in range(n_sub)]
        rs = [_suffix_sums(log_1m[:, sb], uo) for sb in subs]
        carry = None if first else carry_ref[...]
        after = [None] * n_sub
        for n in reversed(range(n_sub)):
            after[n] = rs[n][:, :SB_BLOCK] if carry is None else rs[n][:, :SB_BLOCK] + carry
            total = rs[n][:, SB_BLOCK:]
            carry = total if carry is None else carry + total
        a = jnp.exp(z - sp + jnp.concatenate(after, axis=1))
        if masked:
            a = jnp.where(valid, a, 0.0)
        ab = a.astype(BF16)
        pv = jnp.concatenate(
            [_dot(ab[h * SB_TQ:(h + 1) * SB_TQ], v_ref[0, pl.ds(start, SB_TQ), sl].astype(BF16))
             for h, sl in enumerate(heads)], axis=0)
        acc_ref[...] = pv if first else acc_ref[...] + pv
        carry_ref[...] = carry

    block(i, True, True)

    def body(t, _):
        block(i - 1 - t, False, False)
        return 0

    lax.fori_loop(0, i, body, 0)
    for h, sl in enumerate(heads):
        o_ref[0, :, sl] = acc_ref[h * SB_TQ:(h + 1) * SB_TQ, :].astype(o_ref.dtype)


def _sb_prompt(proj3, bias_row, uo):
    b, t, _ = proj3.shape
    tq = min(SB_TQ, t)
    assert tq == SB_TQ and t % tq == 0
    return pl.pallas_call(
        _sb_prompt_kernel,
        grid=(b, t // tq),
        in_specs=[pl.BlockSpec((1, tq, W_SB), lambda bi, i: (bi, i, 4)),
                  pl.BlockSpec((1, t, W_SB), lambda bi, i: (bi, 0, 5)),
                  pl.BlockSpec((1, t, W_SB), lambda bi, i: (bi, 0, 6)),
                  pl.BlockSpec((1, H_SB * SB_TQ), lambda bi, i: (0, 0)),
                  pl.BlockSpec((2 * SB_BLOCK, 2 * SB_BLOCK), lambda bi, i: (0, 0))],
        out_specs=pl.BlockSpec((1, tq, W_SB), lambda bi, i: (bi, i, 0)),
        out_shape=jax.ShapeDtypeStruct((b, t, W_SB), BF16),
        scratch_shapes=[pltpu.VMEM((H_SB * tq, HD_SB), F32),
                        pltpu.VMEM((H_SB * tq, SB_BLOCK), F32)],
        compiler_params=_params("parallel", "arbitrary"),
        name="sb_prompt",
    )(proj3, proj3, proj3, bias_row, uo)


def _row_select(parts):
    r = lax.broadcasted_iota(jnp.int32, parts[0].shape, 0)
    out = jnp.zeros_like(parts[0])
    for h, p in enumerate(parts):
        out = jnp.where(r == h, p, out)
    return out


def _sb_sample_kernel(pt_ref, q_ref, bias_ref, uo_ref, *refs):
    del pt_ref
    k_refs = refs[:N_PAGES]
    v_refs = refs[N_PAGES:2 * N_PAGES]
    o_ref = refs[2 * N_PAGES]
    uo = uo_ref[...]
    n_col = PAGE_SIZE * H_SB
    n_grp = n_col // SB_BLOCK
    q4 = _row_select([jnp.broadcast_to(q_ref[0, :, h * HD_SB:(h + 1) * HD_SB], (SUBLANES, HD_SB))
                      for h in range(H_SB)])
    qb = jnp.concatenate([q4, q4], axis=0).astype(BF16)
    row = lax.broadcasted_iota(jnp.int32, (SUBLANES, n_col), 0)
    col = lax.broadcasted_iota(jnp.int32, (SUBLANES, n_col), 1)
    valid = (col & (H_SB - 1)) == row
    bias = bias_ref[...]
    zs, sps, his, los = [], [], [], []
    for p in range(N_PAGES):
        z = _dot_nt(qb, k_refs[p][...].astype(BF16))[:SUBLANES] * (HD_SB ** -0.5) + bias
        sp = _softplus(z)
        hi, lo = _split_hi_lo(jnp.where(valid, -sp, 0.0))
        zs.append(z)
        sps.append(sp)
        his += [hi[:, g * SB_BLOCK:(g + 1) * SB_BLOCK] for g in range(n_grp)]
        los += [lo[:, g * SB_BLOCK:(g + 1) * SB_BLOCK] for g in range(n_grp)]
    n_all = N_PAGES * n_grp
    r = _dot(jnp.concatenate([jnp.concatenate(his, axis=0), jnp.concatenate(los, axis=0)],
                             axis=1), uo)
    carry = jnp.zeros((SUBLANES, SB_BLOCK), F32)
    after = [None] * n_all
    for g in reversed(range(n_all)):
        rows = slice(g * SUBLANES, (g + 1) * SUBLANES)
        after[g] = r[rows, :SB_BLOCK] + carry
        carry = carry + r[rows, SB_BLOCK:]
    acc = jnp.zeros((MXU_ROWS, HD_SB), F32)
    for p in range(N_PAGES):
        aft = jnp.concatenate(after[p * n_grp:(p + 1) * n_grp], axis=1)
        a = jnp.where(valid, jnp.exp(zs[p] - sps[p] + aft), 0.0)
        ab = jnp.concatenate([a, a], axis=0).astype(BF16)
        acc += _dot(ab, v_refs[p][...].astype(BF16))
    o_ref[0] = acc[:H_SB].astype(o_ref.dtype)


def _sb_sample(proj3, bias_rows, uo, cache_k, cache_v, page_table, e):
    bs = proj3.shape[0]

    def page(p):
        return pl.BlockSpec((None, None, PAGE_SIZE * H_SB, HD_SB),
                            lambda b, pt: (e, pt[b, p], 0, 0))

    grid_spec = pltpu.PrefetchScalarGridSpec(
        num_scalar_prefetch=1,
        grid=(bs,),
        in_specs=[pl.BlockSpec((1, 1, W_SB), lambda b, pt: (b, 0, 4)),
                  pl.BlockSpec((SUBLANES, PAGE_SIZE * H_SB), lambda b, pt: (0, 0)),
                  pl.BlockSpec((2 * SB_BLOCK, 2 * SB_BLOCK), lambda b, pt: (0, 0))]
                 + [page(p) for p in range(N_PAGES)] + [page(p) for p in range(N_PAGES)],
        out_specs=pl.BlockSpec((1, H_SB, HD_SB), lambda b, pt: (b, 0, 0)),
    )
    return pl.pallas_call(
        _sb_sample_kernel,
        grid_spec=grid_spec,
        out_shape=jax.ShapeDtypeStruct((bs, H_SB, HD_SB), F32),
        compiler_params=_params("arbitrary"),
        name="sb_sample",
    )(page_table, proj3, bias_rows, uo, *([cache_k] * N_PAGES), *([cache_v] * N_PAGES))


def _softmax_rows(s):
    m = jnp.max(s, axis=-1, keepdims=True)
    p = jnp.exp(s - m)
    return p / jnp.sum(p, axis=-1, keepdims=True)


XP_ROW_CHUNKS = 4


def _xattn_prompt_kernel(x_ref, g_ref, wq_ref, mk_ref, mv_ref, wo_ref, o_ref, att_ref, *,
                         n_chunk):
    tm = x_ref.shape[1]
    rc = tm // n_chunk
    heads = [slice(h * HD_X, (h + 1) * HD_X) for h in range(H_X)]
    mkb = [mk_ref[0, :, sl].astype(BF16) for sl in heads]
    mvb = [mv_ref[0, :, sl].astype(BF16) for sl in heads]

    def rows(c):
        return slice(c * rc, (c + 1) * rc)

    def query(c):
        hn = _rms(x_ref[0, rows(c), :], g_ref[...]).astype(BF16)
        return _dot(hn, wq_ref[...]).astype(BF16)

    def scores(q):
        return [_dot_nt(q[:, sl], mkb[h]) * (HD_X ** -0.5) for h, sl in enumerate(heads)]

    s = scores(query(0))
    for c in range(n_chunk):
        q_next = query(c + 1) if c + 1 < n_chunk else None
        for h, sl in enumerate(heads):
            att_ref[rows(c), sl] = _dot(_softmax_rows(s[h]).astype(BF16), mvb[h]).astype(BF16)
        if q_next is not None:
            s = scores(q_next)
        o_ref[0, rows(c), :] = x_ref[0, rows(c), :] + _dot(att_ref[rows(c), :], wo_ref[...])


def _xattn_prompt(x3, g, wq, mem_kv3, k_block, v_block, wo, tm=1024, row_chunks=XP_ROW_CHUNKS):
    b, t, d = x3.shape
    tm = min(tm, t)
    xs = pl.BlockSpec((1, tm, d), lambda bi, i: (bi, i, 0))
    ws = pl.BlockSpec((d, d), lambda bi, i: (0, 0))
    assert tm % (row_chunks * SUBLANES) == 0
    return pl.pallas_call(
        functools.partial(_xattn_prompt_kernel, n_chunk=row_chunks),
        grid=(b, t // tm),
        in_specs=[xs, pl.BlockSpec((1, d), lambda bi, i: (0, 0)), ws,
                  pl.BlockSpec((1, N_MEM, d), lambda bi, i: (bi, 0, k_block)),
                  pl.BlockSpec((1, N_MEM, d), lambda bi, i: (bi, 0, v_block)), ws],
        out_specs=xs,
        out_shape=jax.ShapeDtypeStruct((b, t, d), F32),
        scratch_shapes=[pltpu.VMEM((tm, d), BF16)],
        compiler_params=_params("parallel", "arbitrary"),
        name="xattn_prompt",
    )(x3, g.reshape(1, d), wq, mem_kv3, mem_kv3, wo)


XS_HALVES = HD_X // LANES
XS_ROWS = XS_HALVES * H_X
XS_BB = 4


def _xattn_sample_kernel(q_ref, mk_ref, mv_ref, o_ref):
    n_col = N_MEM * XS_ROWS
    row = lax.broadcasted_iota(jnp.int32, (SUBLANES, n_col), 0)
    col = lax.broadcasted_iota(jnp.int32, (SUBLANES, n_col), 1)
    valid = (col & (XS_ROWS - 1)) == row
    for i in range(q_ref.shape[0]):
        q8 = _row_select([jnp.broadcast_to(q_ref[i, :, (a % H_X) * HD_X + (a // H_X) * LANES:
                                                 (a % H_X) * HD_X + (a // H_X + 1) * LANES],
                                           (SUBLANES, LANES)) for a in range(XS_ROWS)])
        qb = jnp.concatenate([q8, q8], axis=0).astype(BF16)
        z = jnp.where(valid, _dot_nt(qb, mk_ref[i].astype(BF16))[:SUBLANES], 0.0)
        zr = pltpu.roll(z, H_X, axis=0)
        other = jnp.where(row < H_X, pltpu.roll(zr, n_col - H_X, axis=1),
                          pltpu.roll(zr, H_X, axis=1))
        s = (z + other) * (HD_X ** -0.5)
        m = jnp.max(jnp.where(valid, s, -jnp.inf), axis=-1, keepdims=True)
        p = jnp.where(valid, jnp.exp(s - m), 0.0)
        p = p / jnp.sum(p, axis=-1, keepdims=True)
        pb = jnp.concatenate([p, p], axis=0).astype(BF16)
        o_ref[i] = _dot(pb, mv_ref[i].astype(BF16))[:SUBLANES].astype(o_ref.dtype)


def _xattn_cache_view(cache):
    dp, b = cache.shape[:2]
    c = cache.reshape(dp, b, N_MEM, H_X, XS_HALVES, LANES)
    return jnp.swapaxes(c, 3, 4).reshape(dp, b, N_MEM * XS_ROWS, LANES)


def _xattn_sample(q3, mem_k, mem_v, l):
    bs, _, d = q3.shape
    bb = XS_BB
    ms = pl.BlockSpec((None, bb, N_MEM * XS_ROWS, LANES), lambda b: (l, b, 0, 0))
    o = pl.pallas_call(
        _xattn_sample_kernel,
        grid=(bs // bb,),
        in_specs=[pl.BlockSpec((bb, 1, d), lambda b: (b, 0, 0)), ms, ms],
        out_specs=pl.BlockSpec((bb, XS_ROWS, LANES), lambda b: (b, 0, 0)),
        out_shape=jax.ShapeDtypeStruct((bs, XS_ROWS, LANES), F32),
        compiler_params=_params("parallel"),
        name="xattn_sample",
    )(q3, mem_k, mem_v)
    return jnp.swapaxes(o.reshape(bs, XS_HALVES, H_X, LANES), 1, 2).reshape(bs, d)


FFN_TF = 256
HALO = SUBLANES


FFN_ROW_CHUNKS = 4


def _ffn_prompt_kernel(x_ref, g_ref, wg_ref, wv_ref, dwg_ref, dwv_ref, bg_ref, bv_ref, wd_ref,
                       gout_ref, o_ref, hn_ref, tail_ref, *u_refs, out_norm, lookahead):
    t = pl.program_id(1)
    f = pl.program_id(2)
    tm = x_ref.shape[1]
    tf = wg_ref.shape[1]
    rc = tm // len(u_refs)

    @pl.when(f == 0)
    def _():
        hn_ref[...] = _rms(x_ref[0], g_ref[...]).astype(BF16)
        o_ref[0] = x_ref[0]

    def conv(u_ref, dw_ref, b_ref, sl):
        c = b_ref[...] + dw_ref[FFN_CONV_W - 1:FFN_CONV_W, :] * u_ref[HALO:, sl]
        for w in range(FFN_CONV_W - 1):
            off = HALO - (FFN_CONV_W - 1) + w
            c += dw_ref[w:w + 1, :] * u_ref[off:off + rc, sl]
        return c

    def up(k):
        hn = hn_ref[k * rc:(k + 1) * rc, :]
        u_refs[k][HALO:, :tf] = _dot(hn, wg_ref[...])
        u_refs[k][HALO:, tf:] = _dot(hn, wv_ref[...])

    for k in range(min(lookahead, len(u_refs))):
        up(k)
    for k, u_ref in enumerate(u_refs):
        if k + lookahead < len(u_refs):
            up(k + lookahead)
        if k == 0:
            u_ref[:HALO, :] = jnp.where(t == 0, 0.0, tail_ref[f])
        else:
            u_ref[:HALO, :] = u_refs[k - 1][rc:, :]
        gated = (_silu(conv(u_ref, dwg_ref, bg_ref, slice(0, tf)))
                 * conv(u_ref, dwv_ref, bv_ref, slice(tf, 2 * tf)))
        o_ref[0, k * rc:(k + 1) * rc, :] += _dot(gated.astype(BF16), wd_ref[...])
    tail_ref[f] = u_refs[-1][rc:, :]

    if out_norm:
        @pl.when(f == pl.num_programs(2) - 1)
        def _():
            o_ref[0] = _rms(o_ref[0], gout_ref[...])


def _ffn_prompt(x3, g, w_up, dw, dwb, w_down, layer, g_out=None, tm=2048,
                row_chunks=FFN_ROW_CHUNKS, lookahead=1):
    b, t, d = x3.shape
    tm = min(tm, t)
    nf = D_FF // FFN_TF
    assert D_FF % FFN_TF == 0 and t % tm == 0
    xs = pl.BlockSpec((1, tm, d), lambda bi, i, f: (bi, i, 0))
    dwb2 = dwb.reshape(1, -1)
    vec = pl.BlockSpec((1, d), lambda bi, i, f: (0, 0))
    return pl.pallas_call(
        functools.partial(_ffn_prompt_kernel, out_norm=g_out is not None, lookahead=lookahead),
        grid=(b, t // tm, nf),
        in_specs=[xs, pl.BlockSpec((1, d), lambda bi, i, f: (0, 0)),
                  pl.BlockSpec((None, d, FFN_TF), lambda bi, i, f: (layer, 0, f)),
                  pl.BlockSpec((None, d, FFN_TF), lambda bi, i, f: (layer, 0, f + nf)),
                  pl.BlockSpec((FFN_CONV_W, FFN_TF), lambda bi, i, f: (0, f)),
                  pl.BlockSpec((FFN_CONV_W, FFN_TF), lambda bi, i, f: (0, f + nf)),
                  pl.BlockSpec((1, FFN_TF), lambda bi, i, f: (0, f)),
                  pl.BlockSpec((1, FFN_TF), lambda bi, i, f: (0, f + nf)),
                  pl.BlockSpec((None, FFN_TF, d), lambda bi, i, f: (layer, f, 0)), vec],
        out_specs=xs,
        out_shape=jax.ShapeDtypeStruct((b, t, d), F32),
        scratch_shapes=[pltpu.VMEM((tm, d), BF16), pltpu.VMEM((nf, HALO, 2 * FFN_TF), F32)]
                       + [pltpu.VMEM((HALO + tm // row_chunks, 2 * FFN_TF), F32)] * row_chunks,
        compiler_params=_params("parallel", "arbitrary", "arbitrary"),
        name="ffn_prompt",
    )(x3, g.reshape(1, d), w_up, w_up, dw, dw, dwb2, dwb2, w_down,
      (g if g_out is None else g_out).reshape(1, d))


CONV_HALO = 32
CONV_RB = 32


def _conv_prompt_kernel(u_ref, x_ref, dw_ref, dwb_ref, lg_ref, lb_ref, w2_ref, b2_ref,
                        o_ref, s_ref, c_ref, dwt_ref):
    t = pl.program_id(1)
    tm = u_ref.shape[1]
    n = CONV_HALO + tm

    @pl.when(t == 0)
    def _():
        s_ref[0, :CONV_HALO, :] = jnp.zeros((CONV_HALO, D_MODEL), F32)

    @pl.when(t > 0)
    def _():
        s_ref[0, :CONV_HALO, :] = s_ref[0, tm:, :]

    s_ref[0, CONV_HALO:, :] = u_ref[0]
    for r in range(1, SUBLANES):
        s_ref[r] = pltpu.roll(s_ref[0], n - r, axis=0)
    base = CONV_HALO - (CONV_W - 1)

    @pl.when(t == 0)
    def _():
        for w in range(CONV_W):
            dwt_ref[w] = jnp.broadcast_to(dw_ref[w:w + 1, :], (SUBLANES, D_MODEL))

    n_sub = CONV_RB // SUBLANES

    def row_block(rb, carry):
        r0 = rb * CONV_RB
        bias = jnp.broadcast_to(dwb_ref[...], (SUBLANES, D_MODEL))
        accs = [bias] * n_sub
        for w in range(CONV_W):
            r = (base + w) % SUBLANES
            dwt = dwt_ref[w]
            for k in range(n_sub):
                start = pl.multiple_of(r0 + (base + w - r) + k * SUBLANES, SUBLANES)
                accs[k] = accs[k] + dwt * s_ref[r, pl.ds(start, SUBLANES), :]
        for k in range(n_sub):
            c_ref[pl.ds(pl.multiple_of(r0 + k * SUBLANES, SUBLANES), SUBLANES), :] = accs[k]
        return carry

    lax.fori_loop(0, tm // CONV_RB, row_block, 0)
    c = c_ref[...]
    mu = jnp.mean(c, axis=-1, keepdims=True)
    cc = c - mu
    var = jnp.mean(cc * cc, axis=-1, keepdims=True)
    y = _silu(cc * lax.rsqrt(var + NORM_EPS) * lg_ref[...] + lb_ref[...])
    o_ref[0] = x_ref[0] + _dot(y.astype(BF16), w2_ref[...]) + b2_ref[...]


def _conv_prompt(u3, x3, dw, dwb, ln_g, ln_b, w2, b2, tm=512):
    b, t, d = x3.shape
    tm = min(tm, t)
    xs = pl.BlockSpec((1, tm, d), lambda bi, i: (bi, i, 0))
    vec = pl.BlockSpec((1, d), lambda bi, i: (0, 0))
    return pl.pallas_call(
        _conv_prompt_kernel,
        grid=(b, t // tm),
        in_specs=[xs, xs, pl.BlockSpec((CONV_W, d), lambda bi, i: (0, 0)), vec, vec, vec,
                  pl.BlockSpec((d, d), lambda bi, i: (0, 0)), vec],
        out_specs=xs,
        out_shape=jax.ShapeDtypeStruct((b, t, d), F32),
        scratch_shapes=[pltpu.VMEM((SUBLANES, CONV_HALO + tm, d), F32), pltpu.VMEM((tm, d), F32),
                        pltpu.VMEM((CONV_W, SUBLANES, d), F32)],
        compiler_params=_params("parallel", "arbitrary"),
        name="conv_prompt",
    )(u3, x3, dw, dwb.reshape(1, d), ln_g.reshape(1, d), ln_b.reshape(1, d), w2, b2.reshape(1, d))


STEP_BB = 8


def _conv_step_kernel(st_ref, u_ref, dw_ref, dwb_ref, o_ref):
    w_taps = dw_ref.shape[0]
    for r in range(STEP_BB):
        acc = dwb_ref[...] + dw_ref[w_taps - 1:w_taps, :] * u_ref[r:r + 1, :]
        for w in range(w_taps - 1):
            acc += dw_ref[w:w + 1, :] * st_ref[r, w:w + 1, :]
        o_ref[r:r + 1, :] = acc


def _conv_step(state, l, u, dw, dwb):
    bs, c = u.shape
    w_taps = dw.shape[0]
    return pl.pallas_call(
        _conv_step_kernel,
        grid=(bs // STEP_BB,),
        in_specs=[pl.BlockSpec((None, STEP_BB, w_taps - 1, c), lambda i: (l, i, 0, 0)),
                  pl.BlockSpec((STEP_BB, c), lambda i: (i, 0)),
                  pl.BlockSpec((w_taps, c), lambda i: (0, 0)),
                  pl.BlockSpec((1, c), lambda i: (0, 0))],
        out_specs=pl.BlockSpec((STEP_BB, c), lambda i: (i, 0)),
        out_shape=jax.ShapeDtypeStruct((bs, c), F32),
        compiler_params=_params("parallel"),
        name="conv_step",
    )(state, u, dw, dwb.reshape(1, c))


def _conv_step_slots_kernel(st_ref, u_ref, dw_ref, dwb_ref, o_ref):
    w_taps = dw_ref.shape[0]
    acc = dwb_ref[...] + dw_ref[w_taps - 1:w_taps, :] * u_ref[...]
    for w in range(w_taps - 1):
        acc += dw_ref[w:w + 1, :] * st_ref[w]
    o_ref[...] = acc


def _conv_step_slots(state_t, l, u, dw, dwb):
    bs, c = u.shape
    w_taps = dw.shape[0]
    return pl.pallas_call(
        _conv_step_slots_kernel,
        grid=(bs // STEP_BB,),
        in_specs=[pl.BlockSpec((None, w_taps - 1, STEP_BB, c), lambda i: (l, 0, i, 0)),
                  pl.BlockSpec((STEP_BB, c), lambda i: (i, 0)),
                  pl.BlockSpec((w_taps, c), lambda i: (0, 0)),
                  pl.BlockSpec((1, c), lambda i: (0, 0))],
        out_specs=pl.BlockSpec((STEP_BB, c), lambda i: (i, 0)),
        out_shape=jax.ShapeDtypeStruct((bs, c), F32),
        compiler_params=_params("parallel"),
        name="conv_step_slots",
    )(state_t, u, dw, dwb.reshape(1, c))


def _shift_slots_kernel(*refs, n_layers):
    st_ref, u_refs, o_ref = refs[0], refs[1:1 + n_layers], refs[1 + n_layers]
    n_slots = st_ref.shape[0]
    o_ref[:n_slots - 1] = st_ref[1:]
    layer = pl.program_id(0)
    for n, u_ref in enumerate(u_refs):
        @pl.when(layer == n)
        def _(u_ref=u_ref):
            o_ref[n_slots - 1] = u_ref[...]


def _shift_slots(state_t, us):
    n_layers, n_slots, bs, c = state_t.shape

    def u_spec(n):
        return pl.BlockSpec((STEP_BB, c), lambda l, i: (jnp.where(l == n, i, 0), 0))

    blk = pl.BlockSpec((None, n_slots, STEP_BB, c), lambda l, i: (l, 0, i, 0))
    return pl.pallas_call(
        functools.partial(_shift_slots_kernel, n_layers=n_layers),
        grid=(n_layers, bs // STEP_BB),
        in_specs=[blk] + [u_spec(n) for n in range(n_layers)],
        out_specs=blk,
        out_shape=jax.ShapeDtypeStruct(state_t.shape, F32),
        compiler_params=_params("arbitrary", "arbitrary"),
        name="shift_slots",
    )(state_t, *us)


def _stack_rows_kernel(*refs, n_layers, col_starts):
    srcs, dst = refs[:n_layers], refs[n_layers]
    rows = srcs[0].shape[1]
    layer = pl.program_id(0)
    for n, src in enumerate(srcs):
        @pl.when(layer == n)
        def _(src=src):
            for a, c0 in enumerate(col_starts):
                dst[pl.ds(a, rows, stride=len(col_starts)), :] = src[0, :, c0:c0 + LANES]


def _stack_rows(srcs, col_blocks, width, col_starts, tm):
    n_layers = len(srcs)
    b, t, _ = srcs[0].shape
    tm = min(tm, t)
    nt = t // tm
    n_phase = len(col_starts)

    def src_spec(n):
        def idx(l, bi, i):
            on = l == n
            return (jnp.where(on, bi, 0), jnp.where(on, i, 0), col_blocks[n])
        return pl.BlockSpec((1, tm, width), idx)

    return pl.pallas_call(
        functools.partial(_stack_rows_kernel, n_layers=n_layers, col_starts=tuple(col_starts)),
        grid=(n_layers, b, nt),
        in_specs=[src_spec(n) for n in range(n_layers)],
        out_specs=pl.BlockSpec((None, None, tm * n_phase, LANES), lambda l, bi, i: (l, bi, i, 0)),
        out_shape=jax.ShapeDtypeStruct((n_layers, b, t * n_phase, LANES), F32),
        compiler_params=_params("arbitrary", "arbitrary", "arbitrary"),
        name="stack_rows",
    )(*srcs)


FFN_PIPELINE = [(4, 2), (8, 3), (8, 4), (4, 2)]
XP_TILING = [(1024, 4), (2048, 8), (1024, 8), (2048, 4)]
RET_PROMPT_BB = [4, 8]

def kernel(x_prompt, x_sample, cache_sb_k, cache_sb_v, state_ret, state_conv, state_ffn_conv, cache_mem_k, cache_mem_v, page_table, mem_prompt, g_mix, w_in_ab, ret_gn_g, w_out_ab, sb_bias, cv_w1, cv_b1, cv_dw, cv_dwb, cv_ln_g, cv_ln_b, cv_w2, cv_b2, g_cross, xa_wq, xa_wk, xa_wv, xa_wo, g_ffn, ffn_w_up, ffn_dw, ffn_dwb, ffn_w_down, g_final):
    bp, t, d = x_prompt.shape
    bs = x_sample.shape[0]
    n_phys = cache_sb_k.shape[1]

    log_gamma = jnp.log1p(-jnp.exp2(-5.0 - jnp.arange(H_RET, dtype=F32)))
    ret_tabs = _ret_tables(log_gamma)
    cos_p, sin_p = _rope_tables(jnp.arange(t))
    cos_s, sin_s = _rope_tables(PAST_LEN + jnp.arange(1))
    uo = _suffix_matrix()
    w_up_all = ffn_w_up.astype(BF16)
    w_down_all = ffn_w_down.astype(BF16)
    conv_slots = jnp.swapaxes(state_conv, 1, 2)
    cache_k = cache_sb_k.reshape(cache_sb_k.shape[0], n_phys, PAGE_SIZE * H_SB, HD_SB)
    cache_v = cache_sb_v.reshape(cache_sb_v.shape[0], n_phys, PAGE_SIZE * H_SB, HD_SB)
    mem_k = _xattn_cache_view(cache_mem_k)
    mem_v = _xattn_cache_view(cache_mem_v)
    w_mem = jnp.concatenate([w[l] for l in range(DEPTH) for w in (xa_wk, xa_wv)], axis=1)
    mem_kv3 = _mm([mem_prompt.reshape(bp * N_MEM, d)], [w_mem.astype(BF16)], tn=d,
                  name="mem_kv").reshape(bp, N_MEM, 2 * DEPTH * d)

    xp = x_prompt.reshape(bp * t, d)
    xs = x_sample.reshape(bs, d)
    sbk_p, sbk_s, sbv_s, ret_p, ret_s = [], [], [], [], []
    cv_p, cv_s, ff_p, ff_s = [], [], [], []

    for l in range(DEPTH):
        if l % 2 == 0:
            e = l // 2
            w_in = w_in_ab[e].astype(BF16)
            w_out = w_out_ab[e].astype(BF16)
            w_out_parts = [w_out[:W_RK], w_out[W_RK:]]
            bias = sb_bias[e].astype(F32)
            bias_row = jnp.repeat(bias, SB_TQ).reshape(1, H_SB * SB_TQ)
            bias_rows = jnp.zeros((SUBLANES, PAGE_SIZE * H_SB), F32).at[:H_SB].set(
                jnp.broadcast_to(bias[:, None], (H_SB, PAGE_SIZE * H_SB)))
            proj = _mm([xp], [w_in], prologue="rms", pro=(g_mix[l],), tm=512,
                       tn=w_in.shape[1], name="proj_in")
            proj3 = proj.reshape(bp, t, -1)
            ret_o, s_p = _ret_prompt(proj3, cos_p, sin_p, ret_tabs, ret_gn_g[e],
                                     bb=RET_PROMPT_BB[e])
            sb_o = _sb_prompt(proj3, bias_row, uo)
            xp = _mm([ret_o.reshape(bp * t, W_RK), sb_o.reshape(bp * t, W_SB)], w_out_parts,
                     res=xp, tm=1024, tn=d, name="proj_out")
            sbk_p.append(proj3)
            ret_p.append(s_p)
            proj_s = _mm([xs], [w_in], prologue="rms", pro=(g_mix[l],), name="proj_in_s")
            ret_os, s_s = _ret_sample(proj_s, cos_s, sin_s, ret_gn_g[e], state_ret, e)
            sb_os = _sb_sample(proj_s.reshape(bs, 1, -1), bias_rows, uo, cache_k, cache_v,
                               page_table, e)
            xs = _mm([ret_os, sb_os.reshape(bs, W_SB)], w_out_parts, res=xs, tn=d,
                     name="proj_out_s")
            sbk_s.append(proj_s[:, 5 * W_SB:6 * W_SB].reshape(bs, 1, H_SB, HD_SB))
            sbv_s.append(proj_s[:, 6 * W_SB:7 * W_SB].reshape(bs, 1, H_SB, HD_SB))
            ret_s.append(s_s)
        else:
            o = l // 2
            w1 = cv_w1[o].astype(BF16)
            w2 = cv_w2[o].astype(BF16)
            u = _mm([xp], [w1], prologue="rms", pro=(g_mix[l],), bias=cv_b1[o], glu=True,
                    tm=512, tn=d, name="conv_glu")
            u3 = u.reshape(bp, t, d)
            xp = _conv_prompt(u3, xp.reshape(bp, t, d), cv_dw[o], cv_dwb[o], cv_ln_g[o],
                              cv_ln_b[o], w2, cv_b2[o]).reshape(bp * t, d)
            cv_p.append(u3[:, t - (CONV_W - 1):, :])
            u_s = _mm([xs], [w1], prologue="rms", pro=(g_mix[l],), bias=cv_b1[o], glu=True,
                      name="conv_glu_s")
            c_s = _conv_step_slots(conv_slots, o, u_s, cv_dw[o], cv_dwb[o])
            xs = _mm([c_s], [w2], prologue="ln_silu", pro=(cv_ln_g[o], cv_ln_b[o]), bias=cv_b2[o],
                     res=xs, tn=d, name="conv_out_s")
            cv_s.append(u_s)

        wq = xa_wq[l].astype(BF16)
        wo = xa_wo[l].astype(BF16)
        xp = _xattn_prompt(xp.reshape(bp, t, d), g_cross[l], wq, mem_kv3, 2 * l, 2 * l + 1, wo,
                           tm=XP_TILING[l][0], row_chunks=XP_TILING[l][1])
        q_s = _mm([xs], [wq], prologue="rms", pro=(g_cross[l],), tn=d, name="xattn_q_s")
        att_s = _xattn_sample(q_s.reshape(bs, 1, d), mem_k, mem_v, l)
        xs = _mm([att_s], [wo], res=xs, tn=d, name="xattn_out_s")

        tail_rows = xp[:, t - (FFN_CONV_W - 1):, :].reshape(bp * (FFN_CONV_W - 1), d)
        u_st = _mm([jnp.concatenate([xs, tail_rows], axis=0)], [(w_up_all, l)], prologue="rms",
                   pro=(g_ffn[l],), name="ffn_up_s")
        u_s = u_st[:bs]
        ff_p.append(u_st[bs:].reshape(bp, FFN_CONV_W - 1, 2 * D_FF))
        xp = _ffn_prompt(xp, g_ffn[l], w_up_all, ffn_dw[l], ffn_dwb[l], w_down_all, l,
                         g_out=g_final if l == DEPTH - 1 else None,
                         row_chunks=FFN_PIPELINE[l][0],
                         lookahead=FFN_PIPELINE[l][1]).reshape(bp * t, d)
        c_s = _conv_step(state_ffn_conv, l, u_s, ffn_dw[l], ffn_dwb[l])
        xs = _mm([c_s], [(w_down_all, l)], prologue="swiglu", res=xs, tn=d, name="ffn_down_s")
        ff_s.append(jnp.concatenate([state_ffn_conv[l][:, 1:], u_s[:, None, :]], axis=1))

    y_prompt = xp.reshape(bp, t, d)
    y_sample = _rmsnorm(xs, g_final).reshape(bs, 1, d)
    sb_cols = [h * HD_SB for h in range(H_SB)]
    sb_shape = (len(sbk_p), bp, t, H_SB, HD_SB)
    sb_k_prompt = _stack_rows(sbk_p, [5] * len(sbk_p), W_SB, sb_cols, 512).reshape(sb_shape)
    sb_v_prompt = _stack_rows(sbk_p, [6] * len(sbk_p), W_SB, sb_cols, 512).reshape(sb_shape)
    mem_cols = [(a % H_X) * HD_X + (a // H_X) * LANES for a in range(XS_ROWS)]

    def mem_out(first_block):
        o = _stack_rows([mem_kv3] * DEPTH, [2 * l + first_block for l in range(DEPTH)], d,
                        mem_cols, N_MEM)
        o = o.reshape(DEPTH, bp, N_MEM, XS_HALVES, H_X, LANES)
        return jnp.swapaxes(o, 3, 4).reshape(DEPTH, bp, N_MEM, H_X, HD_X)

    return (y_prompt, y_sample, sb_k_prompt, sb_v_prompt, jnp.stack(sbk_s),
            jnp.stack(sbv_s), jnp.stack(ret_p), jnp.stack(ret_s), jnp.stack(cv_p),
            jnp.swapaxes(_shift_slots(conv_slots, cv_s), 1, 2), jnp.stack(ff_p),
            jnp.stack(ff_s), mem_out(0), mem_out(1))
```

```python
import functools

import numpy as np
import jax
import jax.numpy as jnp
from jax import lax
from jax.experimental import pallas as pl
from jax.experimental.pallas import tpu as pltpu

F32 = jnp.float32
BF16 = jnp.bfloat16

D_MODEL = 1024
DEPTH = 4
PAST_LEN = 2048
PAGE_SIZE = 128
N_PAGES = PAST_LEN // PAGE_SIZE
H_RET = 4
DK_RET = 128
DV_RET = 128
RET_CHUNK = 128
ROPE_BASE = 10000.0
H_SB = 4
HD_SB = 128
SB_BLOCK = 128
CONV_W = 31
D_FF = 2816
FFN_CONV_W = 3
N_MEM = 256
H_X = 4
HD_X = D_MODEL // H_X
NORM_EPS = 1e-6
W_RK = H_RET * DK_RET
W_SB = H_SB * HD_SB

SUBLANES = 8
LANES = 128
VMEM_LIMIT_BYTES = 56 * 1024 * 1024


def _params(*sem):
    return pltpu.CompilerParams(dimension_semantics=sem, vmem_limit_bytes=VMEM_LIMIT_BYTES)


def _dot(a, b):
    return jnp.dot(a, b, preferred_element_type=F32)


def _dot_nt(a, b):
    return lax.dot_general(a, b, (((1,), (1,)), ((), ())), preferred_element_type=F32)


def _rms(x, g):
    return x * lax.rsqrt(jnp.mean(x * x, axis=-1, keepdims=True) + NORM_EPS) * g


def _silu(x):
    return x * jax.nn.sigmoid(x)


def _softplus(z):
    return jnp.maximum(z, 0.0) + jnp.log(1.0 + jnp.exp(-jnp.abs(z)))


def _split_hi_lo(x):
    hi = x.astype(BF16)
    lo = (x - hi.astype(F32)).astype(BF16)
    return hi, lo


def _mm_kernel(*refs, n_lhs, prologue, has_bias, has_res, glu):
    it = iter(refs)
    x_refs = [next(it) for _ in range(n_lhs)]
    w_refs = [next(it) for _ in range(n_lhs)]
    wg_refs = [next(it) for _ in range(n_lhs)] if glu else []
    n_pro = {None: 0, "rms": 1, "ln_silu": 2, "swiglu": 0}[prologue]
    p_refs = [next(it) for _ in range(n_pro)]
    b_ref = next(it) if has_bias else None
    bg_ref = next(it) if (has_bias and glu) else None
    r_ref = next(it) if has_res else None
    o_ref = next(it)
    xn_ref = next(it) if prologue else None

    if prologue:
        @pl.when(pl.program_id(1) == 0)
        def _():
            x = x_refs[0][...].astype(F32)
            if prologue == "rms":
                y = _rms(x, p_refs[0][...])
            elif prologue == "ln_silu":
                mu = jnp.mean(x, axis=-1, keepdims=True)
                xc = x - mu
                var = jnp.mean(xc * xc, axis=-1, keepdims=True)
                y = _silu(xc * lax.rsqrt(var + NORM_EPS) * p_refs[0][...] + p_refs[1][...])
            else:
                k = x.shape[-1] // 2
                y = _silu(x[:, :k]) * x[:, k:]
            xn_ref[...] = y.astype(BF16)
        lhs = [xn_ref[...]]
    else:
        lhs = [r[...].astype(BF16) for r in x_refs]

    acc = _dot(lhs[0], w_refs[0][...])
    for a, w in zip(lhs[1:], w_refs[1:]):
        acc += _dot(a, w[...])
    if has_bias:
        acc += b_ref[...]
    if glu:
        gate = _dot(lhs[0], wg_refs[0][...])
        for a, w in zip(lhs[1:], wg_refs[1:]):
            gate += _dot(a, w[...])
        if has_bias:
            gate += bg_ref[...]
        acc = acc * jax.nn.sigmoid(gate)
    if has_res:
        acc += r_ref[...]
    o_ref[...] = acc.astype(o_ref.dtype)


def _mm(xs, ws, *, prologue=None, pro=(), bias=None, res=None, glu=False, tm=512, tn=512,
        out_dtype=F32, name="mm"):
    m = xs[0].shape[0]
    n = (ws[0][0] if isinstance(ws[0], tuple) else ws[0]).shape[-1] // (2 if glu else 1)
    tm = min(tm, m)
    tn = min(tn, n)
    assert m % tm == 0 and n % tn == 0, (m, tm, n, tn)
    nj = n // tn

    def w_spec(w, j_off):
        if isinstance(w, tuple):
            arr, layer = w[0], w[1]
            kb, ksz = (w[2], w[3]) if len(w) == 4 else (0, arr.shape[1])
            return arr, pl.BlockSpec((None, ksz, tn), lambda i, j: (layer, kb, j + j_off))
        return w, pl.BlockSpec((w.shape[0], tn), lambda i, j: (0, j + j_off))

    args, specs = [], []
    for x in xs:
        args.append(x)
        specs.append(pl.BlockSpec((tm, x.shape[1]), lambda i, j: (i, 0)))
    for j_off in ([0, nj] if glu else [0]):
        for w in ws:
            arr, spec = w_spec(w, j_off)
            args.append(arr)
            specs.append(spec)
    for p in pro:
        args.append(p.reshape(1, -1))
        specs.append(pl.BlockSpec((1, p.size), lambda i, j: (0, 0)))
    if bias is not None:
        b2 = bias.reshape(1, -1)
        args.append(b2)
        specs.append(pl.BlockSpec((1, tn), lambda i, j: (0, j)))
        if glu:
            args.append(b2)
            specs.append(pl.BlockSpec((1, tn), lambda i, j: (0, j + nj)))
    if res is not None:
        args.append(res)
        specs.append(pl.BlockSpec((tm, tn), lambda i, j: (i, j)))
    scratch = []
    if prologue:
        k_eff = xs[0].shape[1] // (2 if prologue == "swiglu" else 1)
        scratch.append(pltpu.VMEM((tm, k_eff), BF16))
    kern = functools.partial(_mm_kernel, n_lhs=len(xs), prologue=prologue,
                             has_bias=bias is not None, has_res=res is not None, glu=glu)
    return pl.pallas_call(
        kern,
        grid=(m // tm, nj),
        in_specs=specs,
        out_specs=pl.BlockSpec((tm, tn), lambda i, j: (i, j)),
        out_shape=jax.ShapeDtypeStruct((m, n), out_dtype),
        scratch_shapes=scratch,
        compiler_params=_params("parallel", "arbitrary"),
        name=name,
    )(*args)


def _rmsnorm_kernel(x_ref, g_ref, o_ref):
    o_ref[...] = _rms(x_ref[...], g_ref[...])


def _rmsnorm(x, g, tm=1024):
    m, d = x.shape
    tm = min(tm, m)
    return pl.pallas_call(
        _rmsnorm_kernel,
        grid=(m // tm,),
        in_specs=[pl.BlockSpec((tm, d), lambda i: (i, 0)), pl.BlockSpec((1, d), lambda i: (0, 0))],
        out_specs=pl.BlockSpec((tm, d), lambda i: (i, 0)),
        out_shape=jax.ShapeDtypeStruct((m, d), F32),
        compiler_params=_params("parallel"),
        name="rmsnorm",
    )(x, g.reshape(1, d))


def _rotate(x, cos2, sin2):
    return x * cos2 + pltpu.roll(x, DK_RET // 2, axis=1) * sin2


def _head_ln_gate(o, gate, gn):
    mu = jnp.mean(o, axis=-1, keepdims=True)
    oc = o - mu
    var = jnp.mean(oc * oc, axis=-1, keepdims=True)
    return _silu(gate) * (oc * lax.rsqrt(var + NORM_EPS) * gn)


def _ret_prompt_kernel(rq_ref, rk_ref, rv_ref, rg_ref, cos_ref, sin_ref, dmask_ref, qdec_ref,
                       kdec_ref, cdec_ref, gn_ref, o_ref, s_ref):
    c = pl.program_id(1)

    @pl.when(c == 0)
    def _():
        s_ref[...] = jnp.zeros_like(s_ref)

    cos2 = cos_ref[...]
    sin2 = sin_ref[...]
    pairs = [(i, h, slice(h * DK_RET, (h + 1) * DK_RET))
             for i in range(rq_ref.shape[0]) for h in range(H_RET)]
    ks = [_rotate(rk_ref[i, :, sl], cos2, sin2) * (DK_RET ** -0.5) for i, h, sl in pairs]
    qbs = [_rotate(rq_ref[i, :, sl], cos2, sin2).astype(BF16) for i, h, sl in pairs]
    vbs = [rv_ref[i, :, sl].astype(BF16) for i, h, sl in pairs]
    s0s = [s_ref[i, h] for i, h, sl in pairs]
    scores = [(_dot_nt(qb, k.astype(BF16)) * dmask_ref[h]).astype(BF16)
              for (i, h, sl), qb, k in zip(pairs, qbs, ks)]
    inters = [_dot(qb, s0.astype(BF16)) * qdec_ref[h]
              for (i, h, sl), qb, s0 in zip(pairs, qbs, s0s)]
    for n, (i, h, sl) in enumerate(pairs):
        kd_t = (ks[n] * kdec_ref[h]).T.astype(BF16)
        s_ref[i, h] = cdec_ref[h] * s0s[n] + _dot(kd_t, vbs[n])
    for n, (i, h, sl) in enumerate(pairs):
        o = _dot(scores[n], vbs[n]) + inters[n]
        o_ref[i, :, sl] = _head_ln_gate(o, rg_ref[i, :, sl], gn_ref[:, sl]).astype(o_ref.dtype)


def _ret_tables(log_gamma):
    idx = jnp.arange(RET_CHUNK)
    diff = idx[:, None] - idx[None, :]
    expo = jnp.maximum(diff, 0).astype(F32)[None] * log_gamma[:, None, None]
    dmask = jnp.where(diff[None] >= 0, jnp.exp(expo), 0.0)
    q_decay = jnp.exp((idx + 1).astype(F32)[:, None] * log_gamma[None, :])
    k_decay = jnp.exp((RET_CHUNK - 1 - idx).astype(F32)[:, None] * log_gamma[None, :])
    ones = jnp.ones((1, 1, DK_RET), F32)
    qdec = q_decay.T[:, :, None] * ones
    kdec = k_decay.T[:, :, None] * ones
    cdec = jnp.exp(RET_CHUNK * log_gamma)[:, None, None] * jnp.ones((1, DK_RET, DV_RET), F32)
    return dmask, qdec, kdec, cdec


def _rope_tables(pos):
    half = DK_RET // 2
    inv = ROPE_BASE ** (-jnp.arange(half, dtype=F32) / half)
    ang = pos.astype(F32)[:, None] * inv[None, :]
    cos, sin = jnp.cos(ang), jnp.sin(ang)
    return jnp.concatenate([cos, cos], -1), jnp.concatenate([-sin, sin], -1)


RET_PROMPT_BB = 4


def _ret_prompt(proj3, cos2, sin2, tabs, gn, bb):
    b, t, _ = proj3.shape
    nc = t // RET_CHUNK
    dmask, qdec, kdec, cdec = tabs
    bb = min(bb, b)

    def col(cb):
        return pl.BlockSpec((bb, RET_CHUNK, W_RK), lambda i, c: (i, c, cb))

    tab = pl.BlockSpec((H_RET, RET_CHUNK, RET_CHUNK), lambda i, c: (0, 0, 0))
    return pl.pallas_call(
        _ret_prompt_kernel,
        grid=(b // bb, nc),
        in_specs=[col(0), col(1), col(2), col(3),
                  pl.BlockSpec((RET_CHUNK, DK_RET), lambda i, c: (c, 0)),
                  pl.BlockSpec((RET_CHUNK, DK_RET), lambda i, c: (c, 0)),
                  tab, tab, tab, tab,
                  pl.BlockSpec((1, W_RK), lambda i, c: (0, 0))],
        out_specs=[pl.BlockSpec((bb, RET_CHUNK, W_RK), lambda i, c: (i, c, 0)),
                   pl.BlockSpec((bb, H_RET, DK_RET, DV_RET), lambda i, c: (i, 0, 0, 0))],
        out_shape=[jax.ShapeDtypeStruct((b, t, W_RK), BF16),
                   jax.ShapeDtypeStruct((b, H_RET, DK_RET, DV_RET), F32)],
        compiler_params=_params("parallel", "arbitrary"),
        name="ret_prompt",
    )(proj3, proj3, proj3, proj3, cos2, sin2, dmask, qdec, kdec, cdec, gn.reshape(1, W_RK))


RET_BB = 8
MXU_ROWS = 16


def _ret_sample_kernel(rq_ref, rk_ref, rv_ref, rg_ref, cos_ref, sin_ref, gn_ref, st_ref,
                       o_ref, so_ref, inter_ref, *, g1):
    cos2 = cos_ref[...]
    sin2 = sin_ref[...]
    eye = (lax.broadcasted_iota(jnp.int32, (DK_RET, DK_RET), 0)
           == lax.broadcasted_iota(jnp.int32, (DK_RET, DK_RET), 1))
    for h in range(H_RET):
        sl = slice(h * DK_RET, (h + 1) * DK_RET)
        q = _rotate(rq_ref[:, sl], cos2, sin2)
        k = _rotate(rk_ref[:, sl], cos2, sin2) * (DK_RET ** -0.5)
        v = rv_ref[:, sl]
        for r in range(RET_BB):
            s0 = st_ref[r, h]
            qr = jnp.broadcast_to(q[r:r + 1], (MXU_ROWS, DK_RET)).astype(BF16)
            inter_ref[r:r + 1, :] = _dot(qr, s0.astype(BF16))[0:1] * g1[h]
            diag_k = jnp.where(eye, jnp.broadcast_to(k[r:r + 1], (DK_RET, DK_RET)), 0.0).astype(BF16)
            v_rows = jnp.broadcast_to(v[r:r + 1], (DK_RET, DV_RET)).astype(BF16)
            so_ref[r, h] = g1[h] * s0 + _dot(diag_k, v_rows)
        o = jnp.sum(q * k, axis=-1, keepdims=True) * v + inter_ref[...]
        o_ref[:, sl] = _head_ln_gate(o, rg_ref[:, sl], gn_ref[:, sl]).astype(o_ref.dtype)


def _ret_sample(proj, cos2, sin2, gn, state, e):
    bs = proj.shape[0]
    log_gamma = np.log1p(-np.exp2(-5.0 - np.arange(H_RET, dtype=np.float32)))
    g1 = tuple(float(x) for x in np.exp(log_gamma).astype(np.float32))

    def col(cb):
        return pl.BlockSpec((RET_BB, W_RK), lambda i: (i, cb))

    row = pl.BlockSpec((1, DK_RET), lambda i: (0, 0))
    return pl.pallas_call(
        functools.partial(_ret_sample_kernel, g1=g1),
        grid=(bs // RET_BB,),
        in_specs=[col(0), col(1), col(2), col(3), row, row,
                  pl.BlockSpec((1, W_RK), lambda i: (0, 0)),
                  pl.BlockSpec((None, RET_BB, H_RET, DK_RET, DV_RET), lambda i: (e, i, 0, 0, 0))],
        out_specs=[pl.BlockSpec((RET_BB, W_RK), lambda i: (i, 0)),
                   pl.BlockSpec((RET_BB, H_RET, DK_RET, DV_RET), lambda i: (i, 0, 0, 0))],
        out_shape=[jax.ShapeDtypeStruct((bs, W_RK), BF16),
                   jax.ShapeDtypeStruct((bs, H_RET, DK_RET, DV_RET), F32)],
        scratch_shapes=[pltpu.VMEM((RET_BB, DV_RET), F32)],
        compiler_params=_params("parallel"),
        name="ret_sample",
    )(proj, proj, proj, proj, cos2, sin2, gn.reshape(1, W_RK), state)


def _suffix_matrix():
    j = jnp.arange(SB_BLOCK)
    u = (j[:, None] > j[None, :]).astype(BF16)
    half = jnp.concatenate([u, jnp.ones((SB_BLOCK, SB_BLOCK), BF16)], axis=1)
    return jnp.concatenate([half, half], axis=0)


def _suffix_sums(log_1m, uo):
    hi, lo = _split_hi_lo(log_1m)
    return _dot(jnp.concatenate([hi, lo], axis=1), uo)


SB_TQ = 256


def _sb_prompt_kernel(q_ref, k_ref, v_ref, bias_ref, uo_ref, o_ref, acc_ref, carry_ref):
    i = pl.program_id(1)
    uo = uo_ref[...]
    heads = [slice(h * HD_SB, (h + 1) * HD_SB) for h in range(H_SB)]

    n_sub = SB_TQ // SB_BLOCK

    def block(kb, masked, first):
        start = pl.multiple_of(kb * SB_TQ, SB_TQ)
        z = jnp.concatenate(
            [_dot_nt(q_ref[0, :, sl].astype(BF16), k_ref[0, pl.ds(start, SB_TQ), sl].astype(BF16))
             * (HD_SB ** -0.5) + bias_ref[:, h * SB_TQ:(h + 1) * SB_TQ]
             for h, sl in enumerate(heads)], axis=0)
        sp = _softplus(z)
        if masked:
            qpos = i * SB_TQ + (lax.broadcasted_iota(jnp.int32, z.shape, 0) & (SB_TQ - 1))
            kpos = start + lax.broadcasted_iota(jnp.int32, z.shape, 1)
            valid = kpos < qpos
            log_1m = jnp.where(valid, -sp, 0.0)
        else:
            log_1m = -sp
        subs = [slice(n * SB_BLOCK, (n + 1) * SB_BLOCK) for n in range(n_sub)]
        rs = [_suffix_sums(log_1m[:, sb], uo) for sb in subs]
        carry = None if first else carry_ref[...]
        after = [None] * n_sub
        for n in reversed(range(n_sub)):
            after[n] = rs[n][:, :SB_BLOCK] if carry is None else rs[n][:, :SB_BLOCK] + carry
            total = rs[n][:, SB_BLOCK:]
            carry = total if carry is None else carry + total
        a = jnp.exp(z - sp + jnp.concatenate(after, axis=1))
        if masked:
            a = jnp.where(valid, a, 0.0)
        ab = a.astype(BF16)
        pv = jnp.concatenate(
            [_dot(ab[h * SB_TQ:(h + 1) * SB_TQ], v_ref[0, pl.ds(start, SB_TQ), sl].astype(BF16))
             for h, sl in enumerate(heads)], axis=0)
        acc_ref[...] = pv if first else acc_ref[...] + pv
        carry_ref[...] = carry

    block(i, True, True)

    def body(t, _):
        block(i - 1 - t, False, False)
        return 0

    lax.fori_loop(0, i, body, 0)
    for h, sl in enumerate(heads):
        o_ref[0, :, sl] = acc_ref[h * SB_TQ:(h + 1) * SB_TQ, :].astype(o_ref.dtype)


def _sb_prompt(proj3, bias_row, uo):
    b, t, _ = proj3.shape
    tq = min(SB_TQ, t)
    assert tq == SB_TQ and t % tq == 0
    return pl.pallas_call(
        _sb_prompt_kernel,
        grid=(b, t // tq),
        in_specs=[pl.BlockSpec((1, tq, W_SB), lambda bi, i: (bi, i, 4)),
                  pl.BlockSpec((1, t, W_SB), lambda bi, i: (bi, 0, 5)),
                  pl.BlockSpec((1, t, W_SB), lambda bi, i: (bi, 0, 6)),
                  pl.BlockSpec((1, H_SB * SB_TQ), lambda bi, i: (0, 0)),
                  pl.BlockSpec((2 * SB_BLOCK, 2 * SB_BLOCK), lambda bi, i: (0, 0))],
        out_specs=pl.BlockSpec((1, tq, W_SB), lambda bi, i: (bi, i, 0)),
        out_shape=jax.ShapeDtypeStruct((b, t, W_SB), BF16),
        scratch_shapes=[pltpu.VMEM((H_SB * tq, HD_SB), F32),
                        pltpu.VMEM((H_SB * tq, SB_BLOCK), F32)],
        compiler_params=_params("parallel", "arbitrary"),
        name="sb_prompt",
    )(proj3, proj3, proj3, bias_row, uo)


def _row_select(parts):
    r = lax.broadcasted_iota(jnp.int32, parts[0].shape, 0)
    out = jnp.zeros_like(parts[0])
    for h, p in enumerate(parts):
        out = jnp.where(r == h, p, out)
    return out


def _sb_sample_kernel(pt_ref, q_ref, bias_ref, uo_ref, *refs):
    del pt_ref
    k_refs = refs[:N_PAGES]
    v_refs = refs[N_PAGES:2 * N_PAGES]
    o_ref = refs[2 * N_PAGES]
    uo = uo_ref[...]
    n_col = PAGE_SIZE * H_SB
    n_grp = n_col // SB_BLOCK
    q4 = _row_select([jnp.broadcast_to(q_ref[0, :, h * HD_SB:(h + 1) * HD_SB], (SUBLANES, HD_SB))
                      for h in range(H_SB)])
    qb = jnp.concatenate([q4, q4], axis=0).astype(BF16)
    row = lax.broadcasted_iota(jnp.int32, (SUBLANES, n_col), 0)
    col = lax.broadcasted_iota(jnp.int32, (SUBLANES, n_col), 1)
    valid = (col & (H_SB - 1)) == row
    bias = bias_ref[...]
    zs, sps, his, los = [], [], [], []
    for p in range(N_PAGES):
        z = _dot_nt(qb, k_refs[p][...].astype(BF16))[:SUBLANES] * (HD_SB ** -0.5) + bias
        sp = _softplus(z)
        hi, lo = _split_hi_lo(jnp.where(valid, -sp, 0.0))
        zs.append(z)
        sps.append(sp)
        his += [hi[:, g * SB_BLOCK:(g + 1) * SB_BLOCK] for g in range(n_grp)]
        los += [lo[:, g * SB_BLOCK:(g + 1) * SB_BLOCK] for g in range(n_grp)]
    n_all = N_PAGES * n_grp
    r = _dot(jnp.concatenate([jnp.concatenate(his, axis=0), jnp.concatenate(los, axis=0)],
                             axis=1), uo)
    carry = jnp.zeros((SUBLANES, SB_BLOCK), F32)
    after = [None] * n_all
    for g in reversed(range(n_all)):
        rows = slice(g * SUBLANES, (g + 1) * SUBLANES)
        after[g] = r[rows, :SB_BLOCK] + carry
        carry = carry + r[rows, SB_BLOCK:]
    acc = jnp.zeros((MXU_ROWS, HD_SB), F32)
    for p in range(N_PAGES):
        aft = jnp.concatenate(after[p * n_grp:(p + 1) * n_grp], axis=1)
        a = jnp.where(valid, jnp.exp(zs[p] - sps[p] + aft), 0.0)
        ab = jnp.concatenate([a, a], axis=0).astype(BF16)
        acc += _dot(ab, v_refs[p][...].astype(BF16))
    o_ref[0] = acc[:H_SB].astype(o_ref.dtype)


def _sb_sample(proj3, bias_rows, uo, cache_k, cache_v, page_table, e):
    bs = proj3.shape[0]

    def page(p):
        return pl.BlockSpec((None, None, PAGE_SIZE * H_SB, HD_SB),
                            lambda b, pt: (e, pt[b, p], 0, 0))

    grid_spec = pltpu.PrefetchScalarGridSpec(
        num_scalar_prefetch=1,
        grid=(bs,),
        in_specs=[pl.BlockSpec((1, 1, W_SB), lambda b, pt: (b, 0, 4)),
                  pl.BlockSpec((SUBLANES, PAGE_SIZE * H_SB), lambda b, pt: (0, 0)),
                  pl.BlockSpec((2 * SB_BLOCK, 2 * SB_BLOCK), lambda b, pt: (0, 0))]
                 + [page(p) for p in range(N_PAGES)] + [page(p) for p in range(N_PAGES)],
        out_specs=pl.BlockSpec((1, H_SB, HD_SB), lambda b, pt: (b, 0, 0)),
    )
    return pl.pallas_call(
        _sb_sample_kernel,
        grid_spec=grid_spec,
        out_shape=jax.ShapeDtypeStruct((bs, H_SB, HD_SB), F32),
        compiler_params=_params("arbitrary"),
        name="sb_sample",
    )(page_table, proj3, bias_rows, uo, *([cache_k] * N_PAGES), *([cache_v] * N_PAGES))


def _softmax_rows(s):
    m = jnp.max(s, axis=-1, keepdims=True)
    p = jnp.exp(s - m)
    return p / jnp.sum(p, axis=-1, keepdims=True)


XP_ROW_CHUNKS = 4


def _xattn_prompt_kernel(x_ref, g_ref, wq_ref, mk_ref, mv_ref, wo_ref, o_ref, att_ref, *,
                         n_chunk):
    tm = x_ref.shape[1]
    rc = tm // n_chunk
    heads = [slice(h * HD_X, (h + 1) * HD_X) for h in range(H_X)]
    mkb = [mk_ref[0, :, sl].astype(BF16) for sl in heads]
    mvb = [mv_ref[0, :, sl].astype(BF16) for sl in heads]

    def rows(c):
        return slice(c * rc, (c + 1) * rc)

    def query(c):
        hn = _rms(x_ref[0, rows(c), :], g_ref[...]).astype(BF16)
        return _dot(hn, wq_ref[...]).astype(BF16)

    def scores(q):
        return [_dot_nt(q[:, sl], mkb[h]) * (HD_X ** -0.5) for h, sl in enumerate(heads)]

    s = scores(query(0))
    for c in range(n_chunk):
        q_next = query(c + 1) if c + 1 < n_chunk else None
        for h, sl in enumerate(heads):
            att_ref[rows(c), sl] = _dot(_softmax_rows(s[h]).astype(BF16), mvb[h]).astype(BF16)
        if q_next is not None:
            s = scores(q_next)
        o_ref[0, rows(c), :] = x_ref[0, rows(c), :] + _dot(att_ref[rows(c), :], wo_ref[...])


def _xattn_prompt(x3, g, wq, mem_kv3, k_block, v_block, wo, tm=1024, row_chunks=XP_ROW_CHUNKS):
    b, t, d = x3.shape
    tm = min(tm, t)
    xs = pl.BlockSpec((1, tm, d), lambda bi, i: (bi, i, 0))
    ws = pl.BlockSpec((d, d), lambda bi, i: (0, 0))
    assert tm % (row_chunks * SUBLANES) == 0
    return pl.pallas_call(
        functools.partial(_xattn_prompt_kernel, n_chunk=row_chunks),
        grid=(b, t // tm),
        in_specs=[xs, pl.BlockSpec((1, d), lambda bi, i: (0, 0)), ws,
                  pl.BlockSpec((1, N_MEM, d), lambda bi, i: (bi, 0, k_block)),
                  pl.BlockSpec((1, N_MEM, d), lambda bi, i: (bi, 0, v_block)), ws],
        out_specs=xs,
        out_shape=jax.ShapeDtypeStruct((b, t, d), F32),
        scratch_shapes=[pltpu.VMEM((tm, d), BF16)],
        compiler_params=_params("parallel", "arbitrary"),
        name="xattn_prompt",
    )(x3, g.reshape(1, d), wq, mem_kv3, mem_kv3, wo)


XS_HALVES = HD_X // LANES
XS_ROWS = XS_HALVES * H_X


def _xattn_sample_kernel(q_ref, mk_ref, mv_ref, o_ref):
    n_col = N_MEM * XS_ROWS
    row = lax.broadcasted_iota(jnp.int32, (SUBLANES, n_col), 0)
    col = lax.broadcasted_iota(jnp.int32, (SUBLANES, n_col), 1)
    valid = (col & (XS_ROWS - 1)) == row
    for i in range(q_ref.shape[0]):
        q8 = _row_select([jnp.broadcast_to(q_ref[i, :, (a % H_X) * HD_X + (a // H_X) * LANES:
                                                 (a % H_X) * HD_X + (a // H_X + 1) * LANES],
                                           (SUBLANES, LANES)) for a in range(XS_ROWS)])
        qb = jnp.concatenate([q8, q8], axis=0).astype(BF16)
        z = jnp.where(valid, _dot_nt(qb, mk_ref[i].astype(BF16))[:SUBLANES], 0.0)
        zr = pltpu.roll(z, H_X, axis=0)
        other = jnp.where(row < H_X, pltpu.roll(zr, n_col - H_X, axis=1),
                          pltpu.roll(zr, H_X, axis=1))
        s = (z + other) * (HD_X ** -0.5)
        m = jnp.max(jnp.where(valid, s, -jnp.inf), axis=-1, keepdims=True)
        p = jnp.where(valid, jnp.exp(s - m), 0.0)
        p = p / jnp.sum(p, axis=-1, keepdims=True)
        pb = jnp.concatenate([p, p], axis=0).astype(BF16)
        o_ref[i] = _dot(pb, mv_ref[i].astype(BF16))[:SUBLANES].astype(o_ref.dtype)


def _xattn_cache_view(cache):
    dp, b = cache.shape[:2]
    c = cache.reshape(dp, b, N_MEM, H_X, XS_HALVES, LANES)
    return jnp.swapaxes(c, 3, 4).reshape(dp, b, N_MEM * XS_ROWS, LANES)


def _xattn_sample(q3, mem_k, mem_v, l, bb):
    bs, _, d = q3.shape
    bb = min(bb, bs)
    ms = pl.BlockSpec((None, bb, N_MEM * XS_ROWS, LANES), lambda b: (l, b, 0, 0))
    o = pl.pallas_call(
        _xattn_sample_kernel,
        grid=(bs // bb,),
        in_specs=[pl.BlockSpec((bb, 1, d), lambda b: (b, 0, 0)), ms, ms],
        out_specs=pl.BlockSpec((bb, XS_ROWS, LANES), lambda b: (b, 0, 0)),
        out_shape=jax.ShapeDtypeStruct((bs, XS_ROWS, LANES), F32),
        compiler_params=_params("parallel"),
        name="xattn_sample",
    )(q3, mem_k, mem_v)
    return jnp.swapaxes(o.reshape(bs, XS_HALVES, H_X, LANES), 1, 2).reshape(bs, d)


FFN_TF = 256
HALO = SUBLANES


FFN_LOOKAHEAD = 2
FFN_ROW_CHUNKS = 4


def _ffn_prompt_kernel(x_ref, g_ref, wg_ref, wv_ref, dwg_ref, dwv_ref, bg_ref, bv_ref, wd_ref,
                       gout_ref, o_ref, hn_ref, tail_ref, *u_refs, out_norm, lookahead):
    t = pl.program_id(1)
    f = pl.program_id(2)
    tm = x_ref.shape[1]
    tf = wg_ref.shape[1]
    rc = tm // len(u_refs)

    @pl.when(f == 0)
    def _():
        hn_ref[...] = _rms(x_ref[0], g_ref[...]).astype(BF16)
        o_ref[0] = x_ref[0]

    def conv(u_ref, dw_ref, b_ref, sl):
        c = b_ref[...] + dw_ref[FFN_CONV_W - 1:FFN_CONV_W, :] * u_ref[HALO:, sl]
        for w in range(FFN_CONV_W - 1):
            off = HALO - (FFN_CONV_W - 1) + w
            c += dw_ref[w:w + 1, :] * u_ref[off:off + rc, sl]
        return c

    def up(k):
        hn = hn_ref[k * rc:(k + 1) * rc, :]
        u_refs[k][HALO:, :tf] = _dot(hn, wg_ref[...])
        u_refs[k][HALO:, tf:] = _dot(hn, wv_ref[...])

    for k in range(min(lookahead, len(u_refs))):
        up(k)
    for k, u_ref in enumerate(u_refs):
        if k + lookahead < len(u_refs):
            up(k + lookahead)
        if k == 0:
            u_ref[:HALO, :] = jnp.where(t == 0, 0.0, tail_ref[f])
        else:
            u_ref[:HALO, :] = u_refs[k - 1][rc:, :]
        gated = (_silu(conv(u_ref, dwg_ref, bg_ref, slice(0, tf)))
                 * conv(u_ref, dwv_ref, bv_ref, slice(tf, 2 * tf)))
        o_ref[0, k * rc:(k + 1) * rc, :] += _dot(gated.astype(BF16), wd_ref[...])
    tail_ref[f] = u_refs[-1][rc:, :]

    if out_norm:
        @pl.when(f == pl.num_programs(2) - 1)
        def _():
            o_ref[0] = _rms(o_ref[0], gout_ref[...])


def _ffn_prompt(x3, g, w_up, dw, dwb, w_down, layer, g_out=None, tm=2048,
                row_chunks=FFN_ROW_CHUNKS, lookahead=FFN_LOOKAHEAD):
    b, t, d = x3.shape
    tm = min(tm, t)
    nf = D_FF // FFN_TF
    assert D_FF % FFN_TF == 0 and t % tm == 0
    xs = pl.BlockSpec((1, tm, d), lambda bi, i, f: (bi, i, 0))
    dwb2 = dwb.reshape(1, -1)
    vec = pl.BlockSpec((1, d), lambda bi, i, f: (0, 0))
    return pl.pallas_call(
        functools.partial(_ffn_prompt_kernel, out_norm=g_out is not None, lookahead=lookahead),
        grid=(b, t // tm, nf),
        in_specs=[xs, pl.BlockSpec((1, d), lambda bi, i, f: (0, 0)),
                  pl.BlockSpec((None, d, FFN_TF), lambda bi, i, f: (layer, 0, f)),
                  pl.BlockSpec((None, d, FFN_TF), lambda bi, i, f: (layer, 0, f + nf)),
                  pl.BlockSpec((FFN_CONV_W, FFN_TF), lambda bi, i, f: (0, f)),
                  pl.BlockSpec((FFN_CONV_W, FFN_TF), lambda bi, i, f: (0, f + nf)),
                  pl.BlockSpec((1, FFN_TF), lambda bi, i, f: (0, f)),
                  pl.BlockSpec((1, FFN_TF), lambda bi, i, f: (0, f + nf)),
                  pl.BlockSpec((None, FFN_TF, d), lambda bi, i, f: (layer, f, 0)), vec],
        out_specs=xs,
        out_shape=jax.ShapeDtypeStruct((b, t, d), F32),
        scratch_shapes=[pltpu.VMEM((tm, d), BF16), pltpu.VMEM((nf, HALO, 2 * FFN_TF), F32)]
                       + [pltpu.VMEM((HALO + tm // row_chunks, 2 * FFN_TF), F32)] * row_chunks,
        compiler_params=_params("parallel", "arbitrary", "arbitrary"),
        name="ffn_prompt",
    )(x3, g.reshape(1, d), w_up, w_up, dw, dw, dwb2, dwb2, w_down,
      (g if g_out is None else g_out).reshape(1, d))


CONV_HALO = 32
CONV_RB = 32


def _conv_prompt_kernel(u_ref, x_ref, dw_ref, dwb_ref, lg_ref, lb_ref, w2_ref, b2_ref,
                        o_ref, s_ref, c_ref, dwt_ref):
    t = pl.program_id(1)
    tm = u_ref.shape[1]
    n = CONV_HALO + tm

    @pl.when(t == 0)
    def _():
        s_ref[0, :CONV_HALO, :] = jnp.zeros((CONV_HALO, D_MODEL), F32)

    @pl.when(t > 0)
    def _():
        s_ref[0, :CONV_HALO, :] = s_ref[0, tm:, :]

    s_ref[0, CONV_HALO:, :] = u_ref[0]
    for r in range(1, SUBLANES):
        s_ref[r] = pltpu.roll(s_ref[0], n - r, axis=0)
    base = CONV_HALO - (CONV_W - 1)

    @pl.when(t == 0)
    def _():
        for w in range(CONV_W):
            dwt_ref[w] = jnp.broadcast_to(dw_ref[w:w + 1, :], (SUBLANES, D_MODEL))

    n_sub = CONV_RB // SUBLANES

    def row_block(rb, carry):
        r0 = rb * CONV_RB
        bias = jnp.broadcast_to(dwb_ref[...], (SUBLANES, D_MODEL))
        accs = [bias] * n_sub
        for w in range(CONV_W):
            r = (base + w) % SUBLANES
            dwt = dwt_ref[w]
            for k in range(n_sub):
                start = pl.multiple_of(r0 + (base + w - r) + k * SUBLANES, SUBLANES)
                accs[k] = accs[k] + dwt * s_ref[r, pl.ds(start, SUBLANES), :]
        for k in range(n_sub):
            c_ref[pl.ds(pl.multiple_of(r0 + k * SUBLANES, SUBLANES), SUBLANES), :] = accs[k]
        return carry

    lax.fori_loop(0, tm // CONV_RB, row_block, 0)
    c = c_ref[...]
    mu = jnp.mean(c, axis=-1, keepdims=True)
    cc = c - mu
    var = jnp.mean(cc * cc, axis=-1, keepdims=True)
    y = _silu(cc * lax.rsqrt(var + NORM_EPS) * lg_ref[...] + lb_ref[...])
    o_ref[0] = x_ref[0] + _dot(y.astype(BF16), w2_ref[...]) + b2_ref[...]


def _conv_prompt(u3, x3, dw, dwb, ln_g, ln_b, w2, b2, tm=512):
    b, t, d = x3.shape
    tm = min(tm, t)
    xs = pl.BlockSpec((1, tm, d), lambda bi, i: (bi, i, 0))
    vec = pl.BlockSpec((1, d), lambda bi, i: (0, 0))
    return pl.pallas_call(
        _conv_prompt_kernel,
        grid=(b, t // tm),
        in_specs=[xs, xs, pl.BlockSpec((CONV_W, d), lambda bi, i: (0, 0)), vec, vec, vec,
                  pl.BlockSpec((d, d), lambda bi, i: (0, 0)), vec],
        out_specs=xs,
        out_shape=jax.ShapeDtypeStruct((b, t, d), F32),
        scratch_shapes=[pltpu.VMEM((SUBLANES, CONV_HALO + tm, d), F32), pltpu.VMEM((tm, d), F32),
                        pltpu.VMEM((CONV_W, SUBLANES, d), F32)],
        compiler_params=_params("parallel", "arbitrary"),
        name="conv_prompt",
    )(u3, x3, dw, dwb.reshape(1, d), ln_g.reshape(1, d), ln_b.reshape(1, d), w2, b2.reshape(1, d))


STEP_BB = 8


def _conv_step_kernel(st_ref, u_ref, dw_ref, dwb_ref, o_ref):
    w_taps = dw_ref.shape[0]
    for r in range(STEP_BB):
        acc = dwb_ref[...] + dw_ref[w_taps - 1:w_taps, :] * u_ref[r:r + 1, :]
        for w in range(w_taps - 1):
            acc += dw_ref[w:w + 1, :] * st_ref[r, w:w + 1, :]
        o_ref[r:r + 1, :] = acc


def _conv_step(state, l, u, dw, dwb):
    bs, c = u.shape
    w_taps = dw.shape[0]
    return pl.pallas_call(
        _conv_step_kernel,
        grid=(bs // STEP_BB,),
        in_specs=[pl.BlockSpec((None, STEP_BB, w_taps - 1, c), lambda i: (l, i, 0, 0)),
                  pl.BlockSpec((STEP_BB, c), lambda i: (i, 0)),
                  pl.BlockSpec((w_taps, c), lambda i: (0, 0)),
                  pl.BlockSpec((1, c), lambda i: (0, 0))],
        out_specs=pl.BlockSpec((STEP_BB, c), lambda i: (i, 0)),
        out_shape=jax.ShapeDtypeStruct((bs, c), F32),
        compiler_params=_params("parallel"),
        name="conv_step",
    )(state, u, dw, dwb.reshape(1, c))


def _conv_step_slots_kernel(st_ref, u_ref, dw_ref, dwb_ref, o_ref):
    w_taps = dw_ref.shape[0]
    acc = dwb_ref[...] + dw_ref[w_taps - 1:w_taps, :] * u_ref[...]
    for w in range(w_taps - 1):
        acc += dw_ref[w:w + 1, :] * st_ref[w]
    o_ref[...] = acc


def _conv_step_slots(state_t, l, u, dw, dwb):
    bs, c = u.shape
    w_taps = dw.shape[0]
    return pl.pallas_call(
        _conv_step_slots_kernel,
        grid=(bs // STEP_BB,),
        in_specs=[pl.BlockSpec((None, w_taps - 1, STEP_BB, c), lambda i: (l, 0, i, 0)),
                  pl.BlockSpec((STEP_BB, c), lambda i: (i, 0)),
                  pl.BlockSpec((w_taps, c), lambda i: (0, 0)),
                  pl.BlockSpec((1, c), lambda i: (0, 0))],
        out_specs=pl.BlockSpec((STEP_BB, c), lambda i: (i, 0)),
        out_shape=jax.ShapeDtypeStruct((bs, c), F32),
        compiler_params=_params("parallel"),
        name="conv_step_slots",
    )(state_t, u, dw, dwb.reshape(1, c))


def _shift_slots_kernel(*refs, n_layers):
    st_ref, u_refs, o_ref = refs[0], refs[1:1 + n_layers], refs[1 + n_layers]
    n_slots = st_ref.shape[0]
    o_ref[:n_slots - 1] = st_ref[1:]
    layer = pl.program_id(0)
    for n, u_ref in enumerate(u_refs):
        @pl.when(layer == n)
        def _(u_ref=u_ref):
            o_ref[n_slots - 1] = u_ref[...]


def _shift_slots(state_t, us):
    n_layers, n_slots, bs, c = state_t.shape

    def u_spec(n):
        return pl.BlockSpec((STEP_BB, c), lambda l, i: (jnp.where(l == n, i, 0), 0))

    blk = pl.BlockSpec((None, n_slots, STEP_BB, c), lambda l, i: (l, 0, i, 0))
    return pl.pallas_call(
        functools.partial(_shift_slots_kernel, n_layers=n_layers),
        grid=(n_layers, bs // STEP_BB),
        in_specs=[blk] + [u_spec(n) for n in range(n_layers)],
        out_specs=blk,
        out_shape=jax.ShapeDtypeStruct(state_t.shape, F32),
        compiler_params=_params("arbitrary", "arbitrary"),
        name="shift_slots",
    )(state_t, *us)


def _stack_rows_kernel(*refs, n_layers, col_starts):
    srcs, dst = refs[:n_layers], refs[n_layers]
    rows = srcs[0].shape[1]
    layer = pl.program_id(0)
    for n, src in enumerate(srcs):
        @pl.when(layer == n)
        def _(src=src):
            for a, c0 in enumerate(col_starts):
                dst[pl.ds(a, rows, stride=len(col_starts)), :] = src[0, :, c0:c0 + LANES]


def _stack_rows(srcs, col_blocks, width, col_starts, tm):
    n_layers = len(srcs)
    b, t, _ = srcs[0].shape
    tm = min(tm, t)
    nt = t // tm
    n_phase = len(col_starts)

    def src_spec(n):
        def idx(l, bi, i):
            on = l == n
            return (jnp.where(on, bi, 0), jnp.where(on, i, 0), col_blocks[n])
        return pl.BlockSpec((1, tm, width), idx)

    return pl.pallas_call(
        functools.partial(_stack_rows_kernel, n_layers=n_layers, col_starts=tuple(col_starts)),
        grid=(n_layers, b, nt),
        in_specs=[src_spec(n) for n in range(n_layers)],
        out_specs=pl.BlockSpec((None, None, tm * n_phase, LANES), lambda l, bi, i: (l, bi, i, 0)),
        out_shape=jax.ShapeDtypeStruct((n_layers, b, t * n_phase, LANES), F32),
        compiler_params=_params("arbitrary", "arbitrary", "arbitrary"),
        name="stack_rows",
    )(*srcs)


XS_LAYER_BB = [4, 4, 8, 8]
STACK_TM = [512, 1024]

def kernel(x_prompt, x_sample, cache_sb_k, cache_sb_v, state_ret, state_conv, state_ffn_conv, cache_mem_k, cache_mem_v, page_table, mem_prompt, g_mix, w_in_ab, ret_gn_g, w_out_ab, sb_bias, cv_w1, cv_b1, cv_dw, cv_dwb, cv_ln_g, cv_ln_b, cv_w2, cv_b2, g_cross, xa_wq, xa_wk, xa_wv, xa_wo, g_ffn, ffn_w_up, ffn_dw, ffn_dwb, ffn_w_down, g_final):
    bp, t, d = x_prompt.shape
    bs = x_sample.shape[0]
    n_phys = cache_sb_k.shape[1]

    log_gamma = jnp.log1p(-jnp.exp2(-5.0 - jnp.arange(H_RET, dtype=F32)))
    ret_tabs = _ret_tables(log_gamma)
    cos_p, sin_p = _rope_tables(jnp.arange(t))
    cos_s, sin_s = _rope_tables(PAST_LEN + jnp.arange(1))
    uo = _suffix_matrix()
    w_up_all = ffn_w_up.astype(BF16)
    w_down_all = ffn_w_down.astype(BF16)
    conv_slots = jnp.swapaxes(state_conv, 1, 2)
    cache_k = cache_sb_k.reshape(cache_sb_k.shape[0], n_phys, PAGE_SIZE * H_SB, HD_SB)
    cache_v = cache_sb_v.reshape(cache_sb_v.shape[0], n_phys, PAGE_SIZE * H_SB, HD_SB)
    mem_k = _xattn_cache_view(cache_mem_k)
    mem_v = _xattn_cache_view(cache_mem_v)
    w_mem = jnp.concatenate([w[l] for l in range(DEPTH) for w in (xa_wk, xa_wv)], axis=1)
    mem_kv3 = _mm([mem_prompt.reshape(bp * N_MEM, d)], [w_mem.astype(BF16)], tn=d,
                  name="mem_kv").reshape(bp, N_MEM, 2 * DEPTH * d)

    xp = x_prompt.reshape(bp * t, d)
    xs = x_sample.reshape(bs, d)
    sbk_p, sbk_s, sbv_s, ret_p, ret_s = [], [], [], [], []
    cv_p, cv_s, ff_p, ff_s = [], [], [], []

    for l in range(DEPTH):
        if l % 2 == 0:
            e = l // 2
            w_in = w_in_ab[e].astype(BF16)
            w_out = w_out_ab[e].astype(BF16)
            w_out_parts = [w_out[:W_RK], w_out[W_RK:]]
            bias = sb_bias[e].astype(F32)
            bias_row = jnp.repeat(bias, SB_TQ).reshape(1, H_SB * SB_TQ)
            bias_rows = jnp.zeros((SUBLANES, PAGE_SIZE * H_SB), F32).at[:H_SB].set(
                jnp.broadcast_to(bias[:, None], (H_SB, PAGE_SIZE * H_SB)))
            proj = _mm([xp], [w_in], prologue="rms", pro=(g_mix[l],), tm=512,
                       tn=w_in.shape[1], name="proj_in")
            proj3 = proj.reshape(bp, t, -1)
            ret_o, s_p = _ret_prompt(proj3, cos_p, sin_p, ret_tabs, ret_gn_g[e],
                                     bb=RET_PROMPT_BB)
            sb_o = _sb_prompt(proj3, bias_row, uo)
            xp = _mm([ret_o.reshape(bp * t, W_RK), sb_o.reshape(bp * t, W_SB)], w_out_parts,
                     res=xp, tm=1024, tn=d, name="proj_out")
            sbk_p.append(proj3)
            ret_p.append(s_p)
            proj_s = _mm([xs], [w_in], prologue="rms", pro=(g_mix[l],), name="proj_in_s")
            ret_os, s_s = _ret_sample(proj_s, cos_s, sin_s, ret_gn_g[e], state_ret, e)
            sb_os = _sb_sample(proj_s.reshape(bs, 1, -1), bias_rows, uo, cache_k, cache_v,
                               page_table, e)
            xs = _mm([ret_os, sb_os.reshape(bs, W_SB)], w_out_parts, res=xs, tn=d,
                     name="proj_out_s")
            sbk_s.append(proj_s[:, 5 * W_SB:6 * W_SB].reshape(bs, 1, H_SB, HD_SB))
            sbv_s.append(proj_s[:, 6 * W_SB:7 * W_SB].reshape(bs, 1, H_SB, HD_SB))
            ret_s.append(s_s)
        else:
            o = l // 2
            w1 = cv_w1[o].astype(BF16)
            w2 = cv_w2[o].astype(BF16)
            u = _mm([xp], [w1], prologue="rms", pro=(g_mix[l],), bias=cv_b1[o], glu=True,
                    tm=512, tn=d, name="conv_glu")
            u3 = u.reshape(bp, t, d)
            xp = _conv_prompt(u3, xp.reshape(bp, t, d), cv_dw[o], cv_dwb[o], cv_ln_g[o],
                              cv_ln_b[o], w2, cv_b2[o]).reshape(bp * t, d)
            cv_p.append(u3[:, t - (CONV_W - 1):, :])
            u_s = _mm([xs], [w1], prologue="rms", pro=(g_mix[l],), bias=cv_b1[o], glu=True,
                      name="conv_glu_s")
            c_s = _conv_step_slots(conv_slots, o, u_s, cv_dw[o], cv_dwb[o])
            xs = _mm([c_s], [w2], prologue="ln_silu", pro=(cv_ln_g[o], cv_ln_b[o]), bias=cv_b2[o],
                     res=xs, tn=d, name="conv_out_s")
            cv_s.append(u_s)

        wq = xa_wq[l].astype(BF16)
        wo = xa_wo[l].astype(BF16)
        xp = _xattn_prompt(xp.reshape(bp, t, d), g_cross[l], wq, mem_kv3, 2 * l, 2 * l + 1, wo)
        q_s = _mm([xs], [wq], prologue="rms", pro=(g_cross[l],), tn=d, name="xattn_q_s")
        att_s = _xattn_sample(q_s.reshape(bs, 1, d), mem_k, mem_v, l, XS_LAYER_BB[l])
        xs = _mm([att_s], [wo], res=xs, tn=d, name="xattn_out_s")

        tail_rows = xp[:, t - (FFN_CONV_W - 1):, :].reshape(bp * (FFN_CONV_W - 1), d)
        u_st = _mm([jnp.concatenate([xs, tail_rows], axis=0)], [(w_up_all, l)], prologue="rms",
                   pro=(g_ffn[l],), name="ffn_up_s")
        u_s = u_st[:bs]
        ff_p.append(u_st[bs:].reshape(bp, FFN_CONV_W - 1, 2 * D_FF))
        xp = _ffn_prompt(xp, g_ffn[l], w_up_all, ffn_dw[l], ffn_dwb[l], w_down_all, l,
                         g_out=g_final if l == DEPTH - 1 else None).reshape(bp * t, d)
        c_s = _conv_step(state_ffn_conv, l, u_s, ffn_dw[l], ffn_dwb[l])
        xs = _mm([c_s], [(w_down_all, l)], prologue="swiglu", res=xs, tn=d, name="ffn_down_s")
        ff_s.append(jnp.concatenate([state_ffn_conv[l][:, 1:], u_s[:, None, :]], axis=1))

    y_prompt = xp.reshape(bp, t, d)
    y_sample = _rmsnorm(xs, g_final).reshape(bs, 1, d)
    sb_cols = [h * HD_SB for h in range(H_SB)]
    sb_shape = (len(sbk_p), bp, t, H_SB, HD_SB)
    sb_k_prompt = _stack_rows(sbk_p, [5] * len(sbk_p), W_SB, sb_cols, STACK_TM[0]).reshape(sb_shape)
    sb_v_prompt = _stack_rows(sbk_p, [6] * len(sbk_p), W_SB, sb_cols, STACK_TM[1]).reshape(sb_shape)
    mem_cols = [(a % H_X) * HD_X + (a // H_X) * LANES for a in range(XS_ROWS)]

    def mem_out(first_block):
        o = _stack_rows([mem_kv3] * DEPTH, [2 * l + first_block for l in range(DEPTH)], d,
                        mem_cols, N_MEM)
        o = o.reshape(DEPTH, bp, N_MEM, XS_HALVES, H_X, LANES)
        return jnp.swapaxes(o, 3, 4).reshape(DEPTH, bp, N_MEM, H_X, HD_X)

    return (y_prompt, y_sample, sb_k_prompt, sb_v_prompt, jnp.stack(sbk_s),
            jnp.stack(sbv_s), jnp.stack(ret_p), jnp.stack(ret_s), jnp.stack(cv_p),
            jnp.swapaxes(_shift_slots(conv_slots, cv_s), 1, 2), jnp.stack(ff_p),
            jnp.stack(ff_s), mem_out(0), mem_out(1))
```

```python
import functools

import numpy as np
import jax
import jax.numpy as jnp
from jax import lax
from jax.experimental import pallas as pl
from jax.experimental.pallas import tpu as pltpu

F32 = jnp.float32
BF16 = jnp.bfloat16

D_MODEL = 1024
DEPTH = 4
PAST_LEN = 2048
PAGE_SIZE = 128
N_PAGES = PAST_LEN // PAGE_SIZE
H_RET = 4
DK_RET = 128
DV_RET = 128
RET_CHUNK = 128
ROPE_BASE = 10000.0
H_SB = 4
HD_SB = 128
SB_BLOCK = 128
CONV_W = 31
D_FF = 2816
FFN_CONV_W = 3
N_MEM = 256
H_X = 4
HD_X = D_MODEL // H_X
NORM_EPS = 1e-6
W_RK = H_RET * DK_RET
W_SB = H_SB * HD_SB

SUBLANES = 8
LANES = 128
VMEM_LIMIT_BYTES = 56 * 1024 * 1024


def _params(*sem):
    return pltpu.CompilerParams(dimension_semantics=sem, vmem_limit_bytes=VMEM_LIMIT_BYTES)


def _dot(a, b):
    return jnp.dot(a, b, preferred_element_type=F32)


def _dot_nt(a, b):
    return lax.dot_general(a, b, (((1,), (1,)), ((), ())), preferred_element_type=F32)


def _rms(x, g):
    return x * lax.rsqrt(jnp.mean(x * x, axis=-1, keepdims=True) + NORM_EPS) * g


def _silu(x):
    return x * jax.nn.sigmoid(x)


def _softplus(z):
    return jnp.maximum(z, 0.0) + jnp.log(1.0 + jnp.exp(-jnp.abs(z)))


def _split_hi_lo(x):
    hi = x.astype(BF16)
    lo = (x - hi.astype(F32)).astype(BF16)
    return hi, lo


def _mm_kernel(*refs, n_lhs, prologue, has_bias, has_res, glu):
    it = iter(refs)
    x_refs = [next(it) for _ in range(n_lhs)]
    w_refs = [next(it) for _ in range(n_lhs)]
    wg_refs = [next(it) for _ in range(n_lhs)] if glu else []
    n_pro = {None: 0, "rms": 1, "ln_silu": 2, "swiglu": 0}[prologue]
    p_refs = [next(it) for _ in range(n_pro)]
    b_ref = next(it) if has_bias else None
    bg_ref = next(it) if (has_bias and glu) else None
    r_ref = next(it) if has_res else None
    o_ref = next(it)
    xn_ref = next(it) if prologue else None

    if prologue:
        @pl.when(pl.program_id(1) == 0)
        def _():
            x = x_refs[0][...].astype(F32)
            if prologue == "rms":
                y = _rms(x, p_refs[0][...])
            elif prologue == "ln_silu":
                mu = jnp.mean(x, axis=-1, keepdims=True)
                xc = x - mu
                var = jnp.mean(xc * xc, axis=-1, keepdims=True)
                y = _silu(xc * lax.rsqrt(var + NORM_EPS) * p_refs[0][...] + p_refs[1][...])
            else:
                k = x.shape[-1] // 2
                y = _silu(x[:, :k]) * x[:, k:]
            xn_ref[...] = y.astype(BF16)
        lhs = [xn_ref[...]]
    else:
        lhs = [r[...].astype(BF16) for r in x_refs]

    acc = _dot(lhs[0], w_refs[0][...])
    for a, w in zip(lhs[1:], w_refs[1:]):
        acc += _dot(a, w[...])
    if has_bias:
        acc += b_ref[...]
    if glu:
        gate = _dot(lhs[0], wg_refs[0][...])
        for a, w in zip(lhs[1:], wg_refs[1:]):
            gate += _dot(a, w[...])
        if has_bias:
            gate += bg_ref[...]
        acc = acc * jax.nn.sigmoid(gate)
    if has_res:
        acc += r_ref[...]
    o_ref[...] = acc.astype(o_ref.dtype)


def _mm(xs, ws, *, prologue=None, pro=(), bias=None, res=None, glu=False, tm=512, tn=512,
        out_dtype=F32, name="mm"):
    m = xs[0].shape[0]
    n = (ws[0][0] if isinstance(ws[0], tuple) else ws[0]).shape[-1] // (2 if glu else 1)
    tm = min(tm, m)
    tn = min(tn, n)
    assert m % tm == 0 and n % tn == 0, (m, tm, n, tn)
    nj = n // tn

    def w_spec(w, j_off):
        if isinstance(w, tuple):
            arr, layer = w[0], w[1]
            kb, ksz = (w[2], w[3]) if len(w) == 4 else (0, arr.shape[1])
            return arr, pl.BlockSpec((None, ksz, tn), lambda i, j: (layer, kb, j + j_off))
        return w, pl.BlockSpec((w.shape[0], tn), lambda i, j: (0, j + j_off))

    args, specs = [], []
    for x in xs:
        args.append(x)
        specs.append(pl.BlockSpec((tm, x.shape[1]), lambda i, j: (i, 0)))
    for j_off in ([0, nj] if glu else [0]):
        for w in ws:
            arr, spec = w_spec(w, j_off)
            args.append(arr)
            specs.append(spec)
    for p in pro:
        args.append(p.reshape(1, -1))
        specs.append(pl.BlockSpec((1, p.size), lambda i, j: (0, 0)))
    if bias is not None:
        b2 = bias.reshape(1, -1)
        args.append(b2)
        specs.append(pl.BlockSpec((1, tn), lambda i, j: (0, j)))
        if glu:
            args.append(b2)
            specs.append(pl.BlockSpec((1, tn), lambda i, j: (0, j + nj)))
    if res is not None:
        args.append(res)
        specs.append(pl.BlockSpec((tm, tn), lambda i, j: (i, j)))
    scratch = []
    if prologue:
        k_eff = xs[0].shape[1] // (2 if prologue == "swiglu" else 1)
        scratch.append(pltpu.VMEM((tm, k_eff), BF16))
    kern = functools.partial(_mm_kernel, n_lhs=len(xs), prologue=prologue,
                             has_bias=bias is not None, has_res=res is not None, glu=glu)
    return pl.pallas_call(
        kern,
        grid=(m // tm, nj),
        in_specs=specs,
        out_specs=pl.BlockSpec((tm, tn), lambda i, j: (i, j)),
        out_shape=jax.ShapeDtypeStruct((m, n), out_dtype),
        scratch_shapes=scratch,
        compiler_params=_params("parallel", "arbitrary"),
        name=name,
    )(*args)


def _rmsnorm_kernel(x_ref, g_ref, o_ref):
    o_ref[...] = _rms(x_ref[...], g_ref[...])


def _rmsnorm(x, g, tm=1024):
    m, d = x.shape
    tm = min(tm, m)
    return pl.pallas_call(
        _rmsnorm_kernel,
        grid=(m // tm,),
        in_specs=[pl.BlockSpec((tm, d), lambda i: (i, 0)), pl.BlockSpec((1, d), lambda i: (0, 0))],
        out_specs=pl.BlockSpec((tm, d), lambda i: (i, 0)),
        out_shape=jax.ShapeDtypeStruct((m, d), F32),
        compiler_params=_params("parallel"),
        name="rmsnorm",
    )(x, g.reshape(1, d))


def _rotate(x, cos2, sin2):
    return x * cos2 + pltpu.roll(x, DK_RET // 2, axis=1) * sin2


def _head_ln_gate(o, gate, gn):
    mu = jnp.mean(o, axis=-1, keepdims=True)
    oc = o - mu
    var = jnp.mean(oc * oc, axis=-1, keepdims=True)
    return _silu(gate) * (oc * lax.rsqrt(var + NORM_EPS) * gn)


def _ret_prompt_kernel(rq_ref, rk_ref, rv_ref, rg_ref, cos_ref, sin_ref, dmask_ref, qdec_ref,
                       kdec_ref, cdec_ref, gn_ref, o_ref, s_ref):
    c = pl.program_id(1)

    @pl.when(c == 0)
    def _():
        s_ref[...] = jnp.zeros_like(s_ref)

    cos2 = cos_ref[...]
    sin2 = sin_ref[...]
    pairs = [(i, h, slice(h * DK_RET, (h + 1) * DK_RET))
             for i in range(rq_ref.shape[0]) for h in range(H_RET)]
    ks = [_rotate(rk_ref[i, :, sl], cos2, sin2) * (DK_RET ** -0.5) for i, h, sl in pairs]
    qbs = [_rotate(rq_ref[i, :, sl], cos2, sin2).astype(BF16) for i, h, sl in pairs]
    vbs = [rv_ref[i, :, sl].astype(BF16) for i, h, sl in pairs]
    s0s = [s_ref[i, h] for i, h, sl in pairs]
    scores = [(_dot_nt(qb, k.astype(BF16)) * dmask_ref[h]).astype(BF16)
              for (i, h, sl), qb, k in zip(pairs, qbs, ks)]
    inters = [_dot(qb, s0.astype(BF16)) * qdec_ref[h]
              for (i, h, sl), qb, s0 in zip(pairs, qbs, s0s)]
    for n, (i, h, sl) in enumerate(pairs):
        kd_t = (ks[n] * kdec_ref[h]).T.astype(BF16)
        s_ref[i, h] = cdec_ref[h] * s0s[n] + _dot(kd_t, vbs[n])
    for n, (i, h, sl) in enumerate(pairs):
        o = _dot(scores[n], vbs[n]) + inters[n]
        o_ref[i, :, sl] = _head_ln_gate(o, rg_ref[i, :, sl], gn_ref[:, sl]).astype(o_ref.dtype)


def _ret_tables(log_gamma):
    idx = jnp.arange(RET_CHUNK)
    diff = idx[:, None] - idx[None, :]
    expo = jnp.maximum(diff, 0).astype(F32)[None] * log_gamma[:, None, None]
    dmask = jnp.where(diff[None] >= 0, jnp.exp(expo), 0.0)
    q_decay = jnp.exp((idx + 1).astype(F32)[:, None] * log_gamma[None, :])
    k_decay = jnp.exp((RET_CHUNK - 1 - idx).astype(F32)[:, None] * log_gamma[None, :])
    ones = jnp.ones((1, 1, DK_RET), F32)
    qdec = q_decay.T[:, :, None] * ones
    kdec = k_decay.T[:, :, None] * ones
    cdec = jnp.exp(RET_CHUNK * log_gamma)[:, None, None] * jnp.ones((1, DK_RET, DV_RET), F32)
    return dmask, qdec, kdec, cdec


def _rope_tables(pos):
    half = DK_RET // 2
    inv = ROPE_BASE ** (-jnp.arange(half, dtype=F32) / half)
    ang = pos.astype(F32)[:, None] * inv[None, :]
    cos, sin = jnp.cos(ang), jnp.sin(ang)
    return jnp.concatenate([cos, cos], -1), jnp.concatenate([-sin, sin], -1)


RET_PROMPT_BB = 4


def _ret_prompt(proj3, cos2, sin2, tabs, gn, bb):
    b, t, _ = proj3.shape
    nc = t // RET_CHUNK
    dmask, qdec, kdec, cdec = tabs
    bb = min(bb, b)

    def col(cb):
        return pl.BlockSpec((bb, RET_CHUNK, W_RK), lambda i, c: (i, c, cb))

    tab = pl.BlockSpec((H_RET, RET_CHUNK, RET_CHUNK), lambda i, c: (0, 0, 0))
    return pl.pallas_call(
        _ret_prompt_kernel,
        grid=(b // bb, nc),
        in_specs=[col(0), col(1), col(2), col(3),
                  pl.BlockSpec((RET_CHUNK, DK_RET), lambda i, c: (c, 0)),
                  pl.BlockSpec((RET_CHUNK, DK_RET), lambda i, c: (c, 0)),
                  tab, tab, tab, tab,
                  pl.BlockSpec((1, W_RK), lambda i, c: (0, 0))],
        out_specs=[pl.BlockSpec((bb, RET_CHUNK, W_RK), lambda i, c: (i, c, 0)),
                   pl.BlockSpec((bb, H_RET, DK_RET, DV_RET), lambda i, c: (i, 0, 0, 0))],
        out_shape=[jax.ShapeDtypeStruct((b, t, W_RK), BF16),
                   jax.ShapeDtypeStruct((b, H_RET, DK_RET, DV_RET), F32)],
        compiler_params=_params("parallel", "arbitrary"),
        name="ret_prompt",
    )(proj3, proj3, proj3, proj3, cos2, sin2, dmask, qdec, kdec, cdec, gn.reshape(1, W_RK))


RET_BB = 8
MXU_ROWS = 16


def _ret_sample_kernel(rq_ref, rk_ref, rv_ref, rg_ref, cos_ref, sin_ref, gn_ref, st_ref,
                       o_ref, so_ref, inter_ref, *, g1):
    cos2 = cos_ref[...]
    sin2 = sin_ref[...]
    eye = (lax.broadcasted_iota(jnp.int32, (DK_RET, DK_RET), 0)
           == lax.broadcasted_iota(jnp.int32, (DK_RET, DK_RET), 1))
    for h in range(H_RET):
        sl = slice(h * DK_RET, (h + 1) * DK_RET)
        q = _rotate(rq_ref[:, sl], cos2, sin2)
        k = _rotate(rk_ref[:, sl], cos2, sin2) * (DK_RET ** -0.5)
        v = rv_ref[:, sl]
        for r in range(RET_BB):
            s0 = st_ref[r, h]
            qr = jnp.broadcast_to(q[r:r + 1], (MXU_ROWS, DK_RET)).astype(BF16)
            inter_ref[r:r + 1, :] = _dot(qr, s0.astype(BF16))[0:1] * g1[h]
            diag_k = jnp.where(eye, jnp.broadcast_to(k[r:r + 1], (DK_RET, DK_RET)), 0.0).astype(BF16)
            v_rows = jnp.broadcast_to(v[r:r + 1], (DK_RET, DV_RET)).astype(BF16)
            so_ref[r, h] = g1[h] * s0 + _dot(diag_k, v_rows)
        o = jnp.sum(q * k, axis=-1, keepdims=True) * v + inter_ref[...]
        o_ref[:, sl] = _head_ln_gate(o, rg_ref[:, sl], gn_ref[:, sl]).astype(o_ref.dtype)


def _ret_sample(proj, cos2, sin2, gn, state, e):
    bs = proj.shape[0]
    log_gamma = np.log1p(-np.exp2(-5.0 - np.arange(H_RET, dtype=np.float32)))
    g1 = tuple(float(x) for x in np.exp(log_gamma).astype(np.float32))

    def col(cb):
        return pl.BlockSpec((RET_BB, W_RK), lambda i: (i, cb))

    row = pl.BlockSpec((1, DK_RET), lambda i: (0, 0))
    return pl.pallas_call(
        functools.partial(_ret_sample_kernel, g1=g1),
        grid=(bs // RET_BB,),
        in_specs=[col(0), col(1), col(2), col(3), row, row,
                  pl.BlockSpec((1, W_RK), lambda i: (0, 0)),
                  pl.BlockSpec((None, RET_BB, H_RET, DK_RET, DV_RET), lambda i: (e, i, 0, 0, 0))],
        out_specs=[pl.BlockSpec((RET_BB, W_RK), lambda i: (i, 0)),
                   pl.BlockSpec((RET_BB, H_RET, DK_RET, DV_RET), lambda i: (i, 0, 0, 0))],
        out_shape=[jax.ShapeDtypeStruct((bs, W_RK), BF16),
                   jax.ShapeDtypeStruct((bs, H_RET, DK_RET, DV_RET), F32)],
        scratch_shapes=[pltpu.VMEM((RET_BB, DV_RET), F32)],
        compiler_params=_params("parallel"),
        name="ret_sample",
    )(proj, proj, proj, proj, cos2, sin2, gn.reshape(1, W_RK), state)


def _suffix_matrix():
    j = jnp.arange(SB_BLOCK)
    u = (j[:, None] > j[None, :]).astype(BF16)
    half = jnp.concatenate([u, jnp.ones((SB_BLOCK, SB_BLOCK), BF16)], axis=1)
    return jnp.concatenate([half, half], axis=0)


def _suffix_sums(log_1m, uo):
    hi, lo = _split_hi_lo(log_1m)
    return _dot(jnp.concatenate([hi, lo], axis=1), uo)


SB_TQ = 256


def _sb_prompt_kernel(q_ref, k_ref, v_ref, bias_ref, uo_ref, o_ref, acc_ref, carry_ref):
    i = pl.program_id(1)
    uo = uo_ref[...]
    heads = [slice(h * HD_SB, (h + 1) * HD_SB) for h in range(H_SB)]

    n_sub = SB_TQ // SB_BLOCK

    def block(kb, masked, first):
        start = pl.multiple_of(kb * SB_TQ, SB_TQ)
        z = jnp.concatenate(
            [_dot_nt(q_ref[0, :, sl].astype(BF16), k_ref[0, pl.ds(start, SB_TQ), sl].astype(BF16))
             * (HD_SB ** -0.5) + bias_ref[:, h * SB_TQ:(h + 1) * SB_TQ]
             for h, sl in enumerate(heads)], axis=0)
        sp = _softplus(z)
        if masked:
            qpos = i * SB_TQ + (lax.broadcasted_iota(jnp.int32, z.shape, 0) & (SB_TQ - 1))
            kpos = start + lax.broadcasted_iota(jnp.int32, z.shape, 1)
            valid = kpos < qpos
            log_1m = jnp.where(valid, -sp, 0.0)
        else:
            log_1m = -sp
        subs = [slice(n * SB_BLOCK, (n + 1) * SB_BLOCK) for n in range(n_sub)]
        rs = [_suffix_sums(log_1m[:, sb], uo) for sb in subs]
        carry = None if first else carry_ref[...]
        after = [None] * n_sub
        for n in reversed(range(n_sub)):
            after[n] = rs[n][:, :SB_BLOCK] if carry is None else rs[n][:, :SB_BLOCK] + carry
            total = rs[n][:, SB_BLOCK:]
            carry = total if carry is None else carry + total
        a = jnp.exp(z - sp + jnp.concatenate(after, axis=1))
        if masked:
            a = jnp.where(valid, a, 0.0)
        ab = a.astype(BF16)
        pv = jnp.concatenate(
            [_dot(ab[h * SB_TQ:(h + 1) * SB_TQ], v_ref[0, pl.ds(start, SB_TQ), sl].astype(BF16))
             for h, sl in enumerate(heads)], axis=0)
        acc_ref[...] = pv if first else acc_ref[...] + pv
        carry_ref[...] = carry

    block(i, True, True)

    def body(t, _):
        block(i - 1 - t, False, False)
        return 0

    lax.fori_loop(0, i, body, 0)
    for h, sl in enumerate(heads):
        o_ref[0, :, sl] = acc_ref[h * SB_TQ:(h + 1) * SB_TQ, :].astype(o_ref.dtype)


def _sb_prompt(proj3, bias_row, uo):
    b, t, _ = proj3.shape
    tq = min(SB_TQ, t)
    assert tq == SB_TQ and t % tq == 0
    return pl.pallas_call(
        _sb_prompt_kernel,
        grid=(b, t // tq),
        in_specs=[pl.BlockSpec((1, tq, W_SB), lambda bi, i: (bi, i, 4)),
                  pl.BlockSpec((1, t, W_SB), lambda bi, i: (bi, 0, 5)),
                  pl.BlockSpec((1, t, W_SB), lambda bi, i: (bi, 0, 6)),
                  pl.BlockSpec((1, H_SB * SB_TQ), lambda bi, i: (0, 0)),
                  pl.BlockSpec((2 * SB_BLOCK, 2 * SB_BLOCK), lambda bi, i: (0, 0))],
        out_specs=pl.BlockSpec((1, tq, W_SB), lambda bi, i: (bi, i, 0)),
        out_shape=jax.ShapeDtypeStruct((b, t, W_SB), BF16),
        scratch_shapes=[pltpu.VMEM((H_SB * tq, HD_SB), F32),
                        pltpu.VMEM((H_SB * tq, SB_BLOCK), F32)],
        compiler_params=_params("parallel", "arbitrary"),
        name="sb_prompt",
    )(proj3, proj3, proj3, bias_row, uo)


def _row_select(parts):
    r = lax.broadcasted_iota(jnp.int32, parts[0].shape, 0)
    out = jnp.zeros_like(parts[0])
    for h, p in enumerate(parts):
        out = jnp.where(r == h, p, out)
    return out


def _sb_sample_kernel(pt_ref, q_ref, bias_ref, uo_ref, *refs):
    del pt_ref
    k_refs = refs[:N_PAGES]
    v_refs = refs[N_PAGES:2 * N_PAGES]
    o_ref = refs[2 * N_PAGES]
    uo = uo_ref[...]
    n_col = PAGE_SIZE * H_SB
    n_grp = n_col // SB_BLOCK
    q4 = _row_select([jnp.broadcast_to(q_ref[0, :, h * HD_SB:(h + 1) * HD_SB], (SUBLANES, HD_SB))
                      for h in range(H_SB)])
    qb = jnp.concatenate([q4, q4], axis=0).astype(BF16)
    row = lax.broadcasted_iota(jnp.int32, (SUBLANES, n_col), 0)
    col = lax.broadcasted_iota(jnp.int32, (SUBLANES, n_col), 1)
    valid = (col & (H_SB - 1)) == row
    bias = bias_ref[...]
    zs, sps, his, los = [], [], [], []
    for p in range(N_PAGES):
        z = _dot_nt(qb, k_refs[p][...].astype(BF16))[:SUBLANES] * (HD_SB ** -0.5) + bias
        sp = _softplus(z)
        hi, lo = _split_hi_lo(jnp.where(valid, -sp, 0.0))
        zs.append(z)
        sps.append(sp)
        his += [hi[:, g * SB_BLOCK:(g + 1) * SB_BLOCK] for g in range(n_grp)]
        los += [lo[:, g * SB_BLOCK:(g + 1) * SB_BLOCK] for g in range(n_grp)]
    n_all = N_PAGES * n_grp
    r = _dot(jnp.concatenate([jnp.concatenate(his, axis=0), jnp.concatenate(los, axis=0)],
                             axis=1), uo)
    carry = jnp.zeros((SUBLANES, SB_BLOCK), F32)
    after = [None] * n_all
    for g in reversed(range(n_all)):
        rows = slice(g * SUBLANES, (g + 1) * SUBLANES)
        after[g] = r[rows, :SB_BLOCK] + carry
        carry = carry + r[rows, SB_BLOCK:]
    acc = jnp.zeros((MXU_ROWS, HD_SB), F32)
    for p in range(N_PAGES):
        aft = jnp.concatenate(after[p * n_grp:(p + 1) * n_grp], axis=1)
        a = jnp.where(valid, jnp.exp(zs[p] - sps[p] + aft), 0.0)
        ab = jnp.concatenate([a, a], axis=0).astype(BF16)
        acc += _dot(ab, v_refs[p][...].astype(BF16))
    o_ref[0] = acc[:H_SB].astype(o_ref.dtype)


def _sb_sample(proj3, bias_rows, uo, cache_k, cache_v, page_table, e):
    bs = proj3.shape[0]

    def page(p):
        return pl.BlockSpec((None, None, PAGE_SIZE * H_SB, HD_SB),
                            lambda b, pt: (e, pt[b, p], 0, 0))

    grid_spec = pltpu.PrefetchScalarGridSpec(
        num_scalar_prefetch=1,
        grid=(bs,),
        in_specs=[pl.BlockSpec((1, 1, W_SB), lambda b, pt: (b, 0, 4)),
                  pl.BlockSpec((SUBLANES, PAGE_SIZE * H_SB), lambda b, pt: (0, 0)),
                  pl.BlockSpec((2 * SB_BLOCK, 2 * SB_BLOCK), lambda b, pt: (0, 0))]
                 + [page(p) for p in range(N_PAGES)] + [page(p) for p in range(N_PAGES)],
        out_specs=pl.BlockSpec((1, H_SB, HD_SB), lambda b, pt: (b, 0, 0)),
    )
    return pl.pallas_call(
        _sb_sample_kernel,
        grid_spec=grid_spec,
        out_shape=jax.ShapeDtypeStruct((bs, H_SB, HD_SB), F32),
        compiler_params=_params("arbitrary"),
        name="sb_sample",
    )(page_table, proj3, bias_rows, uo, *([cache_k] * N_PAGES), *([cache_v] * N_PAGES))


def _softmax_rows(s):
    m = jnp.max(s, axis=-1, keepdims=True)
    p = jnp.exp(s - m)
    return p / jnp.sum(p, axis=-1, keepdims=True)


XP_ROW_CHUNKS = 4


def _xattn_prompt_kernel(x_ref, g_ref, wq_ref, mk_ref, mv_ref, wo_ref, o_ref, att_ref, *,
                         n_chunk):
    tm = x_ref.shape[1]
    rc = tm // n_chunk
    heads = [slice(h * HD_X, (h + 1) * HD_X) for h in range(H_X)]
    mkb = [mk_ref[0, :, sl].astype(BF16) for sl in heads]
    mvb = [mv_ref[0, :, sl].astype(BF16) for sl in heads]

    def rows(c):
        return slice(c * rc, (c + 1) * rc)

    def query(c):
        hn = _rms(x_ref[0, rows(c), :], g_ref[...]).astype(BF16)
        return _dot(hn, wq_ref[...]).astype(BF16)

    def scores(q):
        return [_dot_nt(q[:, sl], mkb[h]) * (HD_X ** -0.5) for h, sl in enumerate(heads)]

    s = scores(query(0))
    for c in range(n_chunk):
        q_next = query(c + 1) if c + 1 < n_chunk else None
        for h, sl in enumerate(heads):
            att_ref[rows(c), sl] = _dot(_softmax_rows(s[h]).astype(BF16), mvb[h]).astype(BF16)
        if q_next is not None:
            s = scores(q_next)
        o_ref[0, rows(c), :] = x_ref[0, rows(c), :] + _dot(att_ref[rows(c), :], wo_ref[...])


def _xattn_prompt(x3, g, wq, mem_kv3, k_block, v_block, wo, tm=1024, row_chunks=XP_ROW_CHUNKS):
    b, t, d = x3.shape
    tm = min(tm, t)
    xs = pl.BlockSpec((1, tm, d), lambda bi, i: (bi, i, 0))
    ws = pl.BlockSpec((d, d), lambda bi, i: (0, 0))
    assert tm % (row_chunks * SUBLANES) == 0
    return pl.pallas_call(
        functools.partial(_xattn_prompt_kernel, n_chunk=row_chunks),
        grid=(b, t // tm),
        in_specs=[xs, pl.BlockSpec((1, d), lambda bi, i: (0, 0)), ws,
                  pl.BlockSpec((1, N_MEM, d), lambda bi, i: (bi, 0, k_block)),
                  pl.BlockSpec((1, N_MEM, d), lambda bi, i: (bi, 0, v_block)), ws],
        out_specs=xs,
        out_shape=jax.ShapeDtypeStruct((b, t, d), F32),
        scratch_shapes=[pltpu.VMEM((tm, d), BF16)],
        compiler_params=_params("parallel", "arbitrary"),
        name="xattn_prompt",
    )(x3, g.reshape(1, d), wq, mem_kv3, mem_kv3, wo)


XS_HALVES = HD_X // LANES
XS_ROWS = XS_HALVES * H_X


def _xattn_sample_kernel(q_ref, mk_ref, mv_ref, o_ref):
    n_col = N_MEM * XS_ROWS
    row = lax.broadcasted_iota(jnp.int32, (SUBLANES, n_col), 0)
    col = lax.broadcasted_iota(jnp.int32, (SUBLANES, n_col), 1)
    valid = (col & (XS_ROWS - 1)) == row
    for i in range(q_ref.shape[0]):
        q8 = _row_select([jnp.broadcast_to(q_ref[i, :, (a % H_X) * HD_X + (a // H_X) * LANES:
                                                 (a % H_X) * HD_X + (a // H_X + 1) * LANES],
                                           (SUBLANES, LANES)) for a in range(XS_ROWS)])
        qb = jnp.concatenate([q8, q8], axis=0).astype(BF16)
        z = jnp.where(valid, _dot_nt(qb, mk_ref[i].astype(BF16))[:SUBLANES], 0.0)
        zr = pltpu.roll(z, H_X, axis=0)
        other = jnp.where(row < H_X, pltpu.roll(zr, n_col - H_X, axis=1),
                          pltpu.roll(zr, H_X, axis=1))
        s = (z + other) * (HD_X ** -0.5)
        m = jnp.max(jnp.where(valid, s, -jnp.inf), axis=-1, keepdims=True)
        p = jnp.where(valid, jnp.exp(s - m), 0.0)
        p = p / jnp.sum(p, axis=-1, keepdims=True)
        pb = jnp.concatenate([p, p], axis=0).astype(BF16)
        o_ref[i] = _dot(pb, mv_ref[i].astype(BF16))[:SUBLANES].astype(o_ref.dtype)


def _xattn_cache_view(cache):
    dp, b = cache.shape[:2]
    c = cache.reshape(dp, b, N_MEM, H_X, XS_HALVES, LANES)
    return jnp.swapaxes(c, 3, 4).reshape(dp, b, N_MEM * XS_ROWS, LANES)


def _xattn_sample(q3, mem_k, mem_v, l, bb):
    bs, _, d = q3.shape
    bb = min(bb, bs)
    ms = pl.BlockSpec((None, bb, N_MEM * XS_ROWS, LANES), lambda b: (l, b, 0, 0))
    o = pl.pallas_call(
        _xattn_sample_kernel,
        grid=(bs // bb,),
        in_specs=[pl.BlockSpec((bb, 1, d), lambda b: (b, 0, 0)), ms, ms],
        out_specs=pl.BlockSpec((bb, XS_ROWS, LANES), lambda b: (b, 0, 0)),
        out_shape=jax.ShapeDtypeStruct((bs, XS_ROWS, LANES), F32),
        compiler_params=_params("parallel"),
        name="xattn_sample",
    )(q3, mem_k, mem_v)
    return jnp.swapaxes(o.reshape(bs, XS_HALVES, H_X, LANES), 1, 2).reshape(bs, d)


FFN_TF = 256
HALO = SUBLANES


FFN_LOOKAHEAD = 2
FFN_ROW_CHUNKS = 4


def _ffn_prompt_kernel(x_ref, g_ref, wg_ref, wv_ref, dwg_ref, dwv_ref, bg_ref, bv_ref, wd_ref,
                       gout_ref, o_ref, hn_ref, tail_ref, *u_refs, out_norm, lookahead):
    t = pl.program_id(1)
    f = pl.program_id(2)
    tm = x_ref.shape[1]
    tf = wg_ref.shape[1]
    rc = tm // len(u_refs)

    @pl.when(f == 0)
    def _():
        hn_ref[...] = _rms(x_ref[0], g_ref[...]).astype(BF16)
        o_ref[0] = x_ref[0]

    def conv(u_ref, dw_ref, b_ref, sl):
        c = b_ref[...] + dw_ref[FFN_CONV_W - 1:FFN_CONV_W, :] * u_ref[HALO:, sl]
        for w in range(FFN_CONV_W - 1):
            off = HALO - (FFN_CONV_W - 1) + w
            c += dw_ref[w:w + 1, :] * u_ref[off:off + rc, sl]
        return c

    def up(k):
        hn = hn_ref[k * rc:(k + 1) * rc, :]
        u_refs[k][HALO:, :tf] = _dot(hn, wg_ref[...])
        u_refs[k][HALO:, tf:] = _dot(hn, wv_ref[...])

    for k in range(min(lookahead, len(u_refs))):
        up(k)
    for k, u_ref in enumerate(u_refs):
        if k + lookahead < len(u_refs):
            up(k + lookahead)
        if k == 0:
            u_ref[:HALO, :] = jnp.where(t == 0, 0.0, tail_ref[f])
        else:
            u_ref[:HALO, :] = u_refs[k - 1][rc:, :]
        gated = (_silu(conv(u_ref, dwg_ref, bg_ref, slice(0, tf)))
                 * conv(u_ref, dwv_ref, bv_ref, slice(tf, 2 * tf)))
        o_ref[0, k * rc:(k + 1) * rc, :] += _dot(gated.astype(BF16), wd_ref[...])
    tail_ref[f] = u_refs[-1][rc:, :]

    if out_norm:
        @pl.when(f == pl.num_programs(2) - 1)
        def _():
            o_ref[0] = _rms(o_ref[0], gout_ref[...])


def _ffn_prompt(x3, g, w_up, dw, dwb, w_down, layer, g_out=None, tm=2048,
                row_chunks=FFN_ROW_CHUNKS, lookahead=FFN_LOOKAHEAD):
    b, t, d = x3.shape
    tm = min(tm, t)
    nf = D_FF // FFN_TF
    assert D_FF % FFN_TF == 0 and t % tm == 0
    xs = pl.BlockSpec((1, tm, d), lambda bi, i, f: (bi, i, 0))
    dwb2 = dwb.reshape(1, -1)
    vec = pl.BlockSpec((1, d), lambda bi, i, f: (0, 0))
    return pl.pallas_call(
        functools.partial(_ffn_prompt_kernel, out_norm=g_out is not None, lookahead=lookahead),
        grid=(b, t // tm, nf),
        in_specs=[xs, pl.BlockSpec((1, d), lambda bi, i, f: (0, 0)),
                  pl.BlockSpec((None, d, FFN_TF), lambda bi, i, f: (layer, 0, f)),
                  pl.BlockSpec((None, d, FFN_TF), lambda bi, i, f: (layer, 0, f + nf)),
                  pl.BlockSpec((FFN_CONV_W, FFN_TF), lambda bi, i, f: (0, f)),
                  pl.BlockSpec((FFN_CONV_W, FFN_TF), lambda bi, i, f: (0, f + nf)),
                  pl.BlockSpec((1, FFN_TF), lambda bi, i, f: (0, f)),
                  pl.BlockSpec((1, FFN_TF), lambda bi, i, f: (0, f + nf)),
                  pl.BlockSpec((None, FFN_TF, d), lambda bi, i, f: (layer, f, 0)), vec],
        out_specs=xs,
        out_shape=jax.ShapeDtypeStruct((b, t, d), F32),
        scratch_shapes=[pltpu.VMEM((tm, d), BF16), pltpu.VMEM((nf, HALO, 2 * FFN_TF), F32)]
                       + [pltpu.VMEM((HALO + tm // row_chunks, 2 * FFN_TF), F32)] * row_chunks,
        compiler_params=_params("parallel", "arbitrary", "arbitrary"),
        name="ffn_prompt",
    )(x3, g.reshape(1, d), w_up, w_up, dw, dw, dwb2, dwb2, w_down,
      (g if g_out is None else g_out).reshape(1, d))


CONV_HALO = 32
CONV_RB = 32


def _conv_prompt_kernel(u_ref, x_ref, dw_ref, dwb_ref, lg_ref, lb_ref, w2_ref, b2_ref,
                        o_ref, s_ref, c_ref, dwt_ref):
    t = pl.program_id(1)
    tm = u_ref.shape[1]
    n = CONV_HALO + tm

    @pl.when(t == 0)
    def _():
        s_ref[0, :CONV_HALO, :] = jnp.zeros((CONV_HALO, D_MODEL), F32)

    @pl.when(t > 0)
    def _():
        s_ref[0, :CONV_HALO, :] = s_ref[0, tm:, :]

    s_ref[0, CONV_HALO:, :] = u_ref[0]
    for r in range(1, SUBLANES):
        s_ref[r] = pltpu.roll(s_ref[0], n - r, axis=0)
    base = CONV_HALO - (CONV_W - 1)

    @pl.when(t == 0)
    def _():
        for w in range(CONV_W):
            dwt_ref[w] = jnp.broadcast_to(dw_ref[w:w + 1, :], (SUBLANES, D_MODEL))

    n_sub = CONV_RB // SUBLANES

    def row_block(rb, carry):
        r0 = rb * CONV_RB
        bias = jnp.broadcast_to(dwb_ref[...], (SUBLANES, D_MODEL))
        accs = [bias] * n_sub
        for w in range(CONV_W):
            r = (base + w) % SUBLANES
            dwt = dwt_ref[w]
            for k in range(n_sub):
                start = pl.multiple_of(r0 + (base + w - r) + k * SUBLANES, SUBLANES)
                accs[k] = accs[k] + dwt * s_ref[r, pl.ds(start, SUBLANES), :]
        for k in range(n_sub):
            c_ref[pl.ds(pl.multiple_of(r0 + k * SUBLANES, SUBLANES), SUBLANES), :] = accs[k]
        return carry

    lax.fori_loop(0, tm // CONV_RB, row_block, 0)
    c = c_ref[...]
    mu = jnp.mean(c, axis=-1, keepdims=True)
    cc = c - mu
    var = jnp.mean(cc * cc, axis=-1, keepdims=True)
    y = _silu(cc * lax.rsqrt(var + NORM_EPS) * lg_ref[...] + lb_ref[...])
    o_ref[0] = x_ref[0] + _dot(y.astype(BF16), w2_ref[...]) + b2_ref[...]


def _conv_prompt(u3, x3, dw, dwb, ln_g, ln_b, w2, b2, tm=512):
    b, t, d = x3.shape
    tm = min(tm, t)
    xs = pl.BlockSpec((1, tm, d), lambda bi, i: (bi, i, 0))
    vec = pl.BlockSpec((1, d), lambda bi, i: (0, 0))
    return pl.pallas_call(
        _conv_prompt_kernel,
        grid=(b, t // tm),
        in_specs=[xs, xs, pl.BlockSpec((CONV_W, d), lambda bi, i: (0, 0)), vec, vec, vec,
                  pl.BlockSpec((d, d), lambda bi, i: (0, 0)), vec],
        out_specs=xs,
        out_shape=jax.ShapeDtypeStruct((b, t, d), F32),
        scratch_shapes=[pltpu.VMEM((SUBLANES, CONV_HALO + tm, d), F32), pltpu.VMEM((tm, d), F32),
                        pltpu.VMEM((CONV_W, SUBLANES, d), F32)],
        compiler_params=_params("parallel", "arbitrary"),
        name="conv_prompt",
    )(u3, x3, dw, dwb.reshape(1, d), ln_g.reshape(1, d), ln_b.reshape(1, d), w2, b2.reshape(1, d))


STEP_BB = 8


def _conv_step_kernel(st_ref, u_ref, dw_ref, dwb_ref, o_ref):
    w_taps = dw_ref.shape[0]
    for r in range(STEP_BB):
        acc = dwb_ref[...] + dw_ref[w_taps - 1:w_taps, :] * u_ref[r:r + 1, :]
        for w in range(w_taps - 1):
            acc += dw_ref[w:w + 1, :] * st_ref[r, w:w + 1, :]
        o_ref[r:r + 1, :] = acc


def _conv_step(state, l, u, dw, dwb):
    bs, c = u.shape
    w_taps = dw.shape[0]
    return pl.pallas_call(
        _conv_step_kernel,
        grid=(bs // STEP_BB,),
        in_specs=[pl.BlockSpec((None, STEP_BB, w_taps - 1, c), lambda i: (l, i, 0, 0)),
                  pl.BlockSpec((STEP_BB, c), lambda i: (i, 0)),
                  pl.BlockSpec((w_taps, c), lambda i: (0, 0)),
                  pl.BlockSpec((1, c), lambda i: (0, 0))],
        out_specs=pl.BlockSpec((STEP_BB, c), lambda i: (i, 0)),
        out_shape=jax.ShapeDtypeStruct((bs, c), F32),
        compiler_params=_params("parallel"),
        name="conv_step",
    )(state, u, dw, dwb.reshape(1, c))


def _conv_step_slots_kernel(st_ref, u_ref, dw_ref, dwb_ref, o_ref):
    w_taps = dw_ref.shape[0]
    acc = dwb_ref[...] + dw_ref[w_taps - 1:w_taps, :] * u_ref[...]
    for w in range(w_taps - 1):
        acc += dw_ref[w:w + 1, :] * st_ref[w]
    o_ref[...] = acc


def _conv_step_slots(state_t, l, u, dw, dwb):
    bs, c = u.shape
    w_taps = dw.shape[0]
    return pl.pallas_call(
        _conv_step_slots_kernel,
        grid=(bs // STEP_BB,),
        in_specs=[pl.BlockSpec((None, w_taps - 1, STEP_BB, c), lambda i: (l, 0, i, 0)),
                  pl.BlockSpec((STEP_BB, c), lambda i: (i, 0)),
                  pl.BlockSpec((w_taps, c), lambda i: (0, 0)),
                  pl.BlockSpec((1, c), lambda i: (0, 0))],
        out_specs=pl.BlockSpec((STEP_BB, c), lambda i: (i, 0)),
        out_shape=jax.ShapeDtypeStruct((bs, c), F32),
        compiler_params=_params("parallel"),
        name="conv_step_slots",
    )(state_t, u, dw, dwb.reshape(1, c))


def _shift_slots_kernel(*refs, n_layers):
    st_ref, u_refs, o_ref = refs[0], refs[1:1 + n_layers], refs[1 + n_layers]
    n_slots = st_ref.shape[0]
    o_ref[:n_slots - 1] = st_ref[1:]
    layer = pl.program_id(0)
    for n, u_ref in enumerate(u_refs):
        @pl.when(layer == n)
        def _(u_ref=u_ref):
            o_ref[n_slots - 1] = u_ref[...]


def _shift_slots(state_t, us):
    n_layers, n_slots, bs, c = state_t.shape

    def u_spec(n):
        return pl.BlockSpec((STEP_BB, c), lambda l, i: (jnp.where(l == n, i, 0), 0))

    blk = pl.BlockSpec((None, n_slots, STEP_BB, c), lambda l, i: (l, 0, i, 0))
    return pl.pallas_call(
        functools.partial(_shift_slots_kernel, n_layers=n_layers),
        grid=(n_layers, bs // STEP_BB),
        in_specs=[blk] + [u_spec(n) for n in range(n_layers)],
        out_specs=blk,
        out_shape=jax.ShapeDtypeStruct(state_t.shape, F32),
        compiler_params=_params("arbitrary", "arbitrary"),
        name="shift_slots",
    )(state_t, *us)


def _stack_rows_kernel(*refs, n_layers, col_starts):
    srcs, dst = refs[:n_layers], refs[n_layers]
    rows = srcs[0].shape[1]
    layer = pl.program_id(0)
    for n, src in enumerate(srcs):
        @pl.when(layer == n)
        def _(src=src):
            for a, c0 in enumerate(col_starts):
                dst[pl.ds(a, rows, stride=len(col_starts)), :] = src[0, :, c0:c0 + LANES]


def _stack_rows(srcs, col_blocks, width, col_starts, tm):
    n_layers = len(srcs)
    b, t, _ = srcs[0].shape
    tm = min(tm, t)
    nt = t // tm
    n_phase = len(col_starts)

    def src_spec(n):
        def idx(l, bi, i):
            on = l == n
            return (jnp.where(on, bi, 0), jnp.where(on, i, 0), col_blocks[n])
        return pl.BlockSpec((1, tm, width), idx)

    return pl.pallas_call(
        functools.partial(_stack_rows_kernel, n_layers=n_layers, col_starts=tuple(col_starts)),
        grid=(n_layers, b, nt),
        in_specs=[src_spec(n) for n in range(n_layers)],
        out_specs=pl.BlockSpec((None, None, tm * n_phase, LANES), lambda l, bi, i: (l, bi, i, 0)),
        out_shape=jax.ShapeDtypeStruct((n_layers, b, t * n_phase, LANES), F32),
        compiler_params=_params("arbitrary", "arbitrary", "arbitrary"),
        name="stack_rows",
    )(*srcs)


XS_BB = 8
STACK_TM = 1024


def kernel(x_prompt, x_sample, cache_sb_k, cache_sb_v, state_ret, state_conv, state_ffn_conv, cache_mem_k, cache_mem_v, page_table, mem_prompt, g_mix, w_in_ab, ret_gn_g, w_out_ab, sb_bias, cv_w1, cv_b1, cv_dw, cv_dwb, cv_ln_g, cv_ln_b, cv_w2, cv_b2, g_cross, xa_wq, xa_wk, xa_wv, xa_wo, g_ffn, ffn_w_up, ffn_dw, ffn_dwb, ffn_w_down, g_final):
    bp, t, d = x_prompt.shape
    bs = x_sample.shape[0]
    n_phys = cache_sb_k.shape[1]

    log_gamma = jnp.log1p(-jnp.exp2(-5.0 - jnp.arange(H_RET, dtype=F32)))
    ret_tabs = _ret_tables(log_gamma)
    cos_p, sin_p = _rope_tables(jnp.arange(t))
    cos_s, sin_s = _rope_tables(PAST_LEN + jnp.arange(1))
    uo = _suffix_matrix()
    w_up_all = ffn_w_up.astype(BF16)
    w_down_all = ffn_w_down.astype(BF16)
    conv_slots = jnp.swapaxes(state_conv, 1, 2)
    cache_k = cache_sb_k.reshape(cache_sb_k.shape[0], n_phys, PAGE_SIZE * H_SB, HD_SB)
    cache_v = cache_sb_v.reshape(cache_sb_v.shape[0], n_phys, PAGE_SIZE * H_SB, HD_SB)
    mem_k = _xattn_cache_view(cache_mem_k)
    mem_v = _xattn_cache_view(cache_mem_v)
    w_mem = jnp.concatenate([w[l] for l in range(DEPTH) for w in (xa_wk, xa_wv)], axis=1)
    mem_kv3 = _mm([mem_prompt.reshape(bp * N_MEM, d)], [w_mem.astype(BF16)], tn=d,
                  name="mem_kv").reshape(bp, N_MEM, 2 * DEPTH * d)

    xp = x_prompt.reshape(bp * t, d)
    xs = x_sample.reshape(bs, d)
    sbk_p, sbk_s, sbv_s, ret_p, ret_s = [], [], [], [], []
    cv_p, cv_s, ff_p, ff_s = [], [], [], []

    for l in range(DEPTH):
        if l % 2 == 0:
            e = l // 2
            w_in = w_in_ab[e].astype(BF16)
            w_out = w_out_ab[e].astype(BF16)
            w_out_parts = [w_out[:W_RK], w_out[W_RK:]]
            bias = sb_bias[e].astype(F32)
            bias_row = jnp.repeat(bias, SB_TQ).reshape(1, H_SB * SB_TQ)
            bias_rows = jnp.zeros((SUBLANES, PAGE_SIZE * H_SB), F32).at[:H_SB].set(
                jnp.broadcast_to(bias[:, None], (H_SB, PAGE_SIZE * H_SB)))
            proj = _mm([xp], [w_in], prologue="rms", pro=(g_mix[l],), tm=512,
                       tn=w_in.shape[1], name="proj_in")
            proj3 = proj.reshape(bp, t, -1)
            ret_o, s_p = _ret_prompt(proj3, cos_p, sin_p, ret_tabs, ret_gn_g[e],
                                     bb=RET_PROMPT_BB)
            sb_o = _sb_prompt(proj3, bias_row, uo)
            xp = _mm([ret_o.reshape(bp * t, W_RK), sb_o.reshape(bp * t, W_SB)], w_out_parts,
                     res=xp, tm=1024, tn=d, name="proj_out")
            sbk_p.append(proj3)
            ret_p.append(s_p)
            proj_s = _mm([xs], [w_in], prologue="rms", pro=(g_mix[l],), name="proj_in_s")
            ret_os, s_s = _ret_sample(proj_s, cos_s, sin_s, ret_gn_g[e], state_ret, e)
            sb_os = _sb_sample(proj_s.reshape(bs, 1, -1), bias_rows, uo, cache_k, cache_v,
                               page_table, e)
            xs = _mm([ret_os, sb_os.reshape(bs, W_SB)], w_out_parts, res=xs, tn=d,
                     name="proj_out_s")
            sbk_s.append(proj_s[:, 5 * W_SB:6 * W_SB].reshape(bs, 1, H_SB, HD_SB))
            sbv_s.append(proj_s[:, 6 * W_SB:7 * W_SB].reshape(bs, 1, H_SB, HD_SB))
            ret_s.append(s_s)
        else:
            o = l // 2
            w1 = cv_w1[o].astype(BF16)
            w2 = cv_w2[o].astype(BF16)
            u = _mm([xp], [w1], prologue="rms", pro=(g_mix[l],), bias=cv_b1[o], glu=True,
                    tm=512, tn=d, name="conv_glu")
            u3 = u.reshape(bp, t, d)
            xp = _conv_prompt(u3, xp.reshape(bp, t, d), cv_dw[o], cv_dwb[o], cv_ln_g[o],
                              cv_ln_b[o], w2, cv_b2[o]).reshape(bp * t, d)
            cv_p.append(u3[:, t - (CONV_W - 1):, :])
            u_s = _mm([xs], [w1], prologue="rms", pro=(g_mix[l],), bias=cv_b1[o], glu=True,
                      name="conv_glu_s")
            c_s = _conv_step_slots(conv_slots, o, u_s, cv_dw[o], cv_dwb[o])
            xs = _mm([c_s], [w2], prologue="ln_silu", pro=(cv_ln_g[o], cv_ln_b[o]), bias=cv_b2[o],
                     res=xs, tn=d, name="conv_out_s")
            cv_s.append(u_s)

        wq = xa_wq[l].astype(BF16)
        wo = xa_wo[l].astype(BF16)
        xp = _xattn_prompt(xp.reshape(bp, t, d), g_cross[l], wq, mem_kv3, 2 * l, 2 * l + 1, wo)
        q_s = _mm([xs], [wq], prologue="rms", pro=(g_cross[l],), tn=d, name="xattn_q_s")
        att_s = _xattn_sample(q_s.reshape(bs, 1, d), mem_k, mem_v, l, XS_BB)
        xs = _mm([att_s], [wo], res=xs, tn=d, name="xattn_out_s")

        tail_rows = xp[:, t - (FFN_CONV_W - 1):, :].reshape(bp * (FFN_CONV_W - 1), d)
        u_st = _mm([jnp.concatenate([xs, tail_rows], axis=0)], [(w_up_all, l)], prologue="rms",
                   pro=(g_ffn[l],), name="ffn_up_s")
        u_s = u_st[:bs]
        ff_p.append(u_st[bs:].reshape(bp, FFN_CONV_W - 1, 2 * D_FF))
        xp = _ffn_prompt(xp, g_ffn[l], w_up_all, ffn_dw[l], ffn_dwb[l], w_down_all, l,
                         g_out=g_final if l == DEPTH - 1 else None).reshape(bp * t, d)
        c_s = _conv_step(state_ffn_conv, l, u_s, ffn_dw[l], ffn_dwb[l])
        xs = _mm([c_s], [(w_down_all, l)], prologue="swiglu", res=xs, tn=d, name="ffn_down_s")
        ff_s.append(jnp.concatenate([state_ffn_conv[l][:, 1:], u_s[:, None, :]], axis=1))

    y_prompt = xp.reshape(bp, t, d)
    y_sample = _rmsnorm(xs, g_final).reshape(bs, 1, d)
    sb_cols = [h * HD_SB for h in range(H_SB)]
    sb_shape = (len(sbk_p), bp, t, H_SB, HD_SB)
    sb_k_prompt = _stack_rows(sbk_p, [5] * len(sbk_p), W_SB, sb_cols, STACK_TM).reshape(sb_shape)
    sb_v_prompt = _stack_rows(sbk_p, [6] * len(sbk_p), W_SB, sb_cols, STACK_TM).reshape(sb_shape)
    mem_cols = [(a % H_X) * HD_X + (a // H_X) * LANES for a in range(XS_ROWS)]

    def mem_out(first_block):
        o = _stack_rows([mem_kv3] * DEPTH, [2 * l + first_block for l in range(DEPTH)], d,
                        mem_cols, N_MEM)
        o = o.reshape(DEPTH, bp, N_MEM, XS_HALVES, H_X, LANES)
        return jnp.swapaxes(o, 3, 4).reshape(DEPTH, bp, N_MEM, H_X, HD_X)

    return (y_prompt, y_sample, sb_k_prompt, sb_v_prompt, jnp.stack(sbk_s),
            jnp.stack(sbv_s), jnp.stack(ret_p), jnp.stack(ret_s), jnp.stack(cv_p),
            jnp.swapaxes(_shift_slots(conv_slots, cv_s), 1, 2), jnp.stack(ff_p),
            jnp.stack(ff_s), mem_out(0), mem_out(1))
```

```python
import functools

import numpy as np
import jax
import jax.numpy as jnp
from jax import lax
from jax.experimental import pallas as pl
from jax.experimental.pallas import tpu as pltpu

F32 = jnp.float32
BF16 = jnp.bfloat16

D_MODEL = 1024
DEPTH = 4
PAST_LEN = 2048
PAGE_SIZE = 128
N_PAGES = PAST_LEN // PAGE_SIZE
H_RET = 4
DK_RET = 128
DV_RET = 128
RET_CHUNK = 128
ROPE_BASE = 10000.0
H_SB = 4
HD_SB = 128
SB_BLOCK = 128
CONV_W = 31
D_FF = 2816
FFN_CONV_W = 3
N_MEM = 256
H_X = 4
HD_X = D_MODEL // H_X
NORM_EPS = 1e-6
W_RK = H_RET * DK_RET
W_SB = H_SB * HD_SB

SUBLANES = 8
LANES = 128
VMEM_LIMIT_BYTES = 56 * 1024 * 1024


def _params(*sem):
    return pltpu.CompilerParams(dimension_semantics=sem, vmem_limit_bytes=VMEM_LIMIT_BYTES)


def _dot(a, b):
    return jnp.dot(a, b, preferred_element_type=F32)


def _dot_nt(a, b):
    return lax.dot_general(a, b, (((1,), (1,)), ((), ())), preferred_element_type=F32)


def _rms(x, g):
    return x * lax.rsqrt(jnp.mean(x * x, axis=-1, keepdims=True) + NORM_EPS) * g


def _silu(x):
    return x * jax.nn.sigmoid(x)


def _softplus(z):
    return jnp.maximum(z, 0.0) + jnp.log(1.0 + jnp.exp(-jnp.abs(z)))


def _split_hi_lo(x):
    hi = x.astype(BF16)
    lo = (x - hi.astype(F32)).astype(BF16)
    return hi, lo


def _mm_kernel(*refs, n_lhs, prologue, has_bias, has_res, glu):
    it = iter(refs)
    x_refs = [next(it) for _ in range(n_lhs)]
    w_refs = [next(it) for _ in range(n_lhs)]
    wg_refs = [next(it) for _ in range(n_lhs)] if glu else []
    n_pro = {None: 0, "rms": 1, "ln_silu": 2, "swiglu": 0}[prologue]
    p_refs = [next(it) for _ in range(n_pro)]
    b_ref = next(it) if has_bias else None
    bg_ref = next(it) if (has_bias and glu) else None
    r_ref = next(it) if has_res else None
    o_ref = next(it)
    xn_ref = next(it) if prologue else None

    if prologue:
        @pl.when(pl.program_id(1) == 0)
        def _():
            x = x_refs[0][...].astype(F32)
            if prologue == "rms":
                y = _rms(x, p_refs[0][...])
            elif prologue == "ln_silu":
                mu = jnp.mean(x, axis=-1, keepdims=True)
                xc = x - mu
                var = jnp.mean(xc * xc, axis=-1, keepdims=True)
                y = _silu(xc * lax.rsqrt(var + NORM_EPS) * p_refs[0][...] + p_refs[1][...])
            else:
                k = x.shape[-1] // 2
                y = _silu(x[:, :k]) * x[:, k:]
            xn_ref[...] = y.astype(BF16)
        lhs = [xn_ref[...]]
    else:
        lhs = [r[...].astype(BF16) for r in x_refs]

    acc = _dot(lhs[0], w_refs[0][...])
    for a, w in zip(lhs[1:], w_refs[1:]):
        acc += _dot(a, w[...])
    if has_bias:
        acc += b_ref[...]
    if glu:
        gate = _dot(lhs[0], wg_refs[0][...])
        for a, w in zip(lhs[1:], wg_refs[1:]):
            gate += _dot(a, w[...])
        if has_bias:
            gate += bg_ref[...]
        acc = acc * jax.nn.sigmoid(gate)
    if has_res:
        acc += r_ref[...]
    o_ref[...] = acc.astype(o_ref.dtype)


def _mm(xs, ws, *, prologue=None, pro=(), bias=None, res=None, glu=False, tm=512, tn=512,
        out_dtype=F32, name="mm"):
    m = xs[0].shape[0]
    n = (ws[0][0] if isinstance(ws[0], tuple) else ws[0]).shape[-1] // (2 if glu else 1)
    tm = min(tm, m)
    tn = min(tn, n)
    assert m % tm == 0 and n % tn == 0, (m, tm, n, tn)
    nj = n // tn

    def w_spec(w, j_off):
        if isinstance(w, tuple):
            arr, layer = w[0], w[1]
            kb, ksz = (w[2], w[3]) if len(w) == 4 else (0, arr.shape[1])
            return arr, pl.BlockSpec((None, ksz, tn), lambda i, j: (layer, kb, j + j_off))
        return w, pl.BlockSpec((w.shape[0], tn), lambda i, j: (0, j + j_off))

    args, specs = [], []
    for x in xs:
        args.append(x)
        specs.append(pl.BlockSpec((tm, x.shape[1]), lambda i, j: (i, 0)))
    for j_off in ([0, nj] if glu else [0]):
        for w in ws:
            arr, spec = w_spec(w, j_off)
            args.append(arr)
            specs.append(spec)
    for p in pro:
        args.append(p.reshape(1, -1))
        specs.append(pl.BlockSpec((1, p.size), lambda i, j: (0, 0)))
    if bias is not None:
        b2 = bias.reshape(1, -1)
        args.append(b2)
        specs.append(pl.BlockSpec((1, tn), lambda i, j: (0, j)))
        if glu:
            args.append(b2)
            specs.append(pl.BlockSpec((1, tn), lambda i, j: (0, j + nj)))
    if res is not None:
        args.append(res)
        specs.append(pl.BlockSpec((tm, tn), lambda i, j: (i, j)))
    scratch = []
    if prologue:
        k_eff = xs[0].shape[1] // (2 if prologue == "swiglu" else 1)
        scratch.append(pltpu.VMEM((tm, k_eff), BF16))
    kern = functools.partial(_mm_kernel, n_lhs=len(xs), prologue=prologue,
                             has_bias=bias is not None, has_res=res is not None, glu=glu)
    return pl.pallas_call(
        kern,
        grid=(m // tm, nj),
        in_specs=specs,
        out_specs=pl.BlockSpec((tm, tn), lambda i, j: (i, j)),
        out_shape=jax.ShapeDtypeStruct((m, n), out_dtype),
        scratch_shapes=scratch,
        compiler_params=_params("parallel", "arbitrary"),
        name=name,
    )(*args)


def _rmsnorm_kernel(x_ref, g_ref, o_ref):
    o_ref[...] = _rms(x_ref[...], g_ref[...])


def _rmsnorm(x, g, tm=1024):
    m, d = x.shape
    tm = min(tm, m)
    return pl.pallas_call(
        _rmsnorm_kernel,
        grid=(m // tm,),
        in_specs=[pl.BlockSpec((tm, d), lambda i: (i, 0)), pl.BlockSpec((1, d), lambda i: (0, 0))],
        out_specs=pl.BlockSpec((tm, d), lambda i: (i, 0)),
        out_shape=jax.ShapeDtypeStruct((m, d), F32),
        compiler_params=_params("parallel"),
        name="rmsnorm",
    )(x, g.reshape(1, d))


def _rotate(x, cos2, sin2):
    return x * cos2 + pltpu.roll(x, DK_RET // 2, axis=1) * sin2


def _head_ln_gate(o, gate, gn):
    mu = jnp.mean(o, axis=-1, keepdims=True)
    oc = o - mu
    var = jnp.mean(oc * oc, axis=-1, keepdims=True)
    return _silu(gate) * (oc * lax.rsqrt(var + NORM_EPS) * gn)


def _ret_prompt_kernel(rq_ref, rk_ref, rv_ref, rg_ref, cos_ref, sin_ref, dmask_ref, qdec_ref,
                       kdec_ref, cdec_ref, gn_ref, o_ref, s_ref):
    c = pl.program_id(1)

    @pl.when(c == 0)
    def _():
        s_ref[...] = jnp.zeros_like(s_ref)

    cos2 = cos_ref[...]
    sin2 = sin_ref[...]
    pairs = [(i, h, slice(h * DK_RET, (h + 1) * DK_RET))
             for i in range(rq_ref.shape[0]) for h in range(H_RET)]
    ks = [_rotate(rk_ref[i, :, sl], cos2, sin2) * (DK_RET ** -0.5) for i, h, sl in pairs]
    qbs = [_rotate(rq_ref[i, :, sl], cos2, sin2).astype(BF16) for i, h, sl in pairs]
    vbs = [rv_ref[i, :, sl].astype(BF16) for i, h, sl in pairs]
    s0s = [s_ref[i, h] for i, h, sl in pairs]
    scores = [(_dot_nt(qb, k.astype(BF16)) * dmask_ref[h]).astype(BF16)
              for (i, h, sl), qb, k in zip(pairs, qbs, ks)]
    inters = [_dot(qb, s0.astype(BF16)) * qdec_ref[h]
              for (i, h, sl), qb, s0 in zip(pairs, qbs, s0s)]
    for n, (i, h, sl) in enumerate(pairs):
        kd_t = (ks[n] * kdec_ref[h]).T.astype(BF16)
        s_ref[i, h] = cdec_ref[h] * s0s[n] + _dot(kd_t, vbs[n])
    for n, (i, h, sl) in enumerate(pairs):
        o = _dot(scores[n], vbs[n]) + inters[n]
        o_ref[i, :, sl] = _head_ln_gate(o, rg_ref[i, :, sl], gn_ref[:, sl]).astype(o_ref.dtype)


def _ret_tables(log_gamma):
    idx = jnp.arange(RET_CHUNK)
    diff = idx[:, None] - idx[None, :]
    expo = jnp.maximum(diff, 0).astype(F32)[None] * log_gamma[:, None, None]
    dmask = jnp.where(diff[None] >= 0, jnp.exp(expo), 0.0)
    q_decay = jnp.exp((idx + 1).astype(F32)[:, None] * log_gamma[None, :])
    k_decay = jnp.exp((RET_CHUNK - 1 - idx).astype(F32)[:, None] * log_gamma[None, :])
    ones = jnp.ones((1, 1, DK_RET), F32)
    qdec = q_decay.T[:, :, None] * ones
    kdec = k_decay.T[:, :, None] * ones
    cdec = jnp.exp(RET_CHUNK * log_gamma)[:, None, None] * jnp.ones((1, DK_RET, DV_RET), F32)
    return dmask, qdec, kdec, cdec


def _rope_tables(pos):
    half = DK_RET // 2
    inv = ROPE_BASE ** (-jnp.arange(half, dtype=F32) / half)
    ang = pos.astype(F32)[:, None] * inv[None, :]
    cos, sin = jnp.cos(ang), jnp.sin(ang)
    return jnp.concatenate([cos, cos], -1), jnp.concatenate([-sin, sin], -1)


RET_PROMPT_BB = 4


def _ret_prompt(proj3, cos2, sin2, tabs, gn, bb):
    b, t, _ = proj3.shape
    nc = t // RET_CHUNK
    dmask, qdec, kdec, cdec = tabs
    bb = min(bb, b)

    def col(cb):
        return pl.BlockSpec((bb, RET_CHUNK, W_RK), lambda i, c: (i, c, cb))

    tab = pl.BlockSpec((H_RET, RET_CHUNK, RET_CHUNK), lambda i, c: (0, 0, 0))
    return pl.pallas_call(
        _ret_prompt_kernel,
        grid=(b // bb, nc),
        in_specs=[col(0), col(1), col(2), col(3),
                  pl.BlockSpec((RET_CHUNK, DK_RET), lambda i, c: (c, 0)),
                  pl.BlockSpec((RET_CHUNK, DK_RET), lambda i, c: (c, 0)),
                  tab, tab, tab, tab,
                  pl.BlockSpec((1, W_RK), lambda i, c: (0, 0))],
        out_specs=[pl.BlockSpec((bb, RET_CHUNK, W_RK), lambda i, c: (i, c, 0)),
                   pl.BlockSpec((bb, H_RET, DK_RET, DV_RET), lambda i, c: (i, 0, 0, 0))],
        out_shape=[jax.ShapeDtypeStruct((b, t, W_RK), BF16),
                   jax.ShapeDtypeStruct((b, H_RET, DK_RET, DV_RET), F32)],
        compiler_params=_params("parallel", "arbitrary"),
        name="ret_prompt",
    )(proj3, proj3, proj3, proj3, cos2, sin2, dmask, qdec, kdec, cdec, gn.reshape(1, W_RK))


RET_BB = 8
MXU_ROWS = 16


def _ret_sample_kernel(rq_ref, rk_ref, rv_ref, rg_ref, cos_ref, sin_ref, gn_ref, st_ref,
                       o_ref, so_ref, inter_ref, *, g1):
    cos2 = cos_ref[...]
    sin2 = sin_ref[...]
    eye = (lax.broadcasted_iota(jnp.int32, (DK_RET, DK_RET), 0)
           == lax.broadcasted_iota(jnp.int32, (DK_RET, DK_RET), 1))
    for h in range(H_RET):
        sl = slice(h * DK_RET, (h + 1) * DK_RET)
        q = _rotate(rq_ref[:, sl], cos2, sin2)
        k = _rotate(rk_ref[:, sl], cos2, sin2) * (DK_RET ** -0.5)
        v = rv_ref[:, sl]
        for r in range(RET_BB):
            s0 = st_ref[r, h]
            qr = jnp.broadcast_to(q[r:r + 1], (MXU_ROWS, DK_RET)).astype(BF16)
            inter_ref[r:r + 1, :] = _dot(qr, s0.astype(BF16))[0:1] * g1[h]
            diag_k = jnp.where(eye, jnp.broadcast_to(k[r:r + 1], (DK_RET, DK_RET)), 0.0).astype(BF16)
            v_rows = jnp.broadcast_to(v[r:r + 1], (DK_RET, DV_RET)).astype(BF16)
            so_ref[r, h] = g1[h] * s0 + _dot(diag_k, v_rows)
        o = jnp.sum(q * k, axis=-1, keepdims=True) * v + inter_ref[...]
        o_ref[:, sl] = _head_ln_gate(o, rg_ref[:, sl], gn_ref[:, sl]).astype(o_ref.dtype)


def _ret_sample(proj, cos2, sin2, gn, state, e):
    bs = proj.shape[0]
    log_gamma = np.log1p(-np.exp2(-5.0 - np.arange(H_RET, dtype=np.float32)))
    g1 = tuple(float(x) for x in np.exp(log_gamma).astype(np.float32))

    def col(cb):
        return pl.BlockSpec((RET_BB, W_RK), lambda i: (i, cb))

    row = pl.BlockSpec((1, DK_RET), lambda i: (0, 0))
    return pl.pallas_call(
        functools.partial(_ret_sample_kernel, g1=g1),
        grid=(bs // RET_BB,),
        in_specs=[col(0), col(1), col(2), col(3), row, row,
                  pl.BlockSpec((1, W_RK), lambda i: (0, 0)),
                  pl.BlockSpec((None, RET_BB, H_RET, DK_RET, DV_RET), lambda i: (e, i, 0, 0, 0))],
        out_specs=[pl.BlockSpec((RET_BB, W_RK), lambda i: (i, 0)),
                   pl.BlockSpec((RET_BB, H_RET, DK_RET, DV_RET), lambda i: (i, 0, 0, 0))],
        out_shape=[jax.ShapeDtypeStruct((bs, W_RK), BF16),
                   jax.ShapeDtypeStruct((bs, H_RET, DK_RET, DV_RET), F32)],
        scratch_shapes=[pltpu.VMEM((RET_BB, DV_RET), F32)],
        compiler_params=_params("parallel"),
        name="ret_sample",
    )(proj, proj, proj, proj, cos2, sin2, gn.reshape(1, W_RK), state)


def _suffix_matrix():
    j = jnp.arange(SB_BLOCK)
    u = (j[:, None] > j[None, :]).astype(BF16)
    half = jnp.concatenate([u, jnp.ones((SB_BLOCK, SB_BLOCK), BF16)], axis=1)
    return jnp.concatenate([half, half], axis=0)


def _suffix_sums(log_1m, uo):
    hi, lo = _split_hi_lo(log_1m)
    return _dot(jnp.concatenate([hi, lo], axis=1), uo)


SB_TQ = 256


def _sb_prompt_kernel(q_ref, k_ref, v_ref, bias_ref, uo_ref, o_ref, acc_ref, carry_ref):
    i = pl.program_id(1)
    uo = uo_ref[...]
    heads = [slice(h * HD_SB, (h + 1) * HD_SB) for h in range(H_SB)]

    n_sub = SB_TQ // SB_BLOCK

    def block(kb, masked, first):
        start = pl.multiple_of(kb * SB_TQ, SB_TQ)
        z = jnp.concatenate(
            [_dot_nt(q_ref[0, :, sl].astype(BF16), k_ref[0, pl.ds(start, SB_TQ), sl].astype(BF16))
             * (HD_SB ** -0.5) + bias_ref[:, h * SB_TQ:(h + 1) * SB_TQ]
             for h, sl in enumerate(heads)], axis=0)
        sp = _softplus(z)
        if masked:
            qpos = i * SB_TQ + (lax.broadcasted_iota(jnp.int32, z.shape, 0) & (SB_TQ - 1))
            kpos = start + lax.broadcasted_iota(jnp.int32, z.shape, 1)
            valid = kpos < qpos
            log_1m = jnp.where(valid, -sp, 0.0)
        else:
            log_1m = -sp
        subs = [slice(n * SB_BLOCK, (n + 1) * SB_BLOCK) for n in range(n_sub)]
        rs = [_suffix_sums(log_1m[:, sb], uo) for sb in subs]
        carry = None if first else carry_ref[...]
        after = [None] * n_sub
        for n in reversed(range(n_sub)):
            after[n] = rs[n][:, :SB_BLOCK] if carry is None else rs[n][:, :SB_BLOCK] + carry
            total = rs[n][:, SB_BLOCK:]
            carry = total if carry is None else carry + total
        a = jnp.exp(z - sp + jnp.concatenate(after, axis=1))
        if masked:
            a = jnp.where(valid, a, 0.0)
        ab = a.astype(BF16)
        pv = jnp.concatenate(
            [_dot(ab[h * SB_TQ:(h + 1) * SB_TQ], v_ref[0, pl.ds(start, SB_TQ), sl].astype(BF16))
             for h, sl in enumerate(heads)], axis=0)
        acc_ref[...] = pv if first else acc_ref[...] + pv
        carry_ref[...] = carry

    block(i, True, True)

    def body(t, _):
        block(i - 1 - t, False, False)
        return 0

    lax.fori_loop(0, i, body, 0)
    for h, sl in enumerate(heads):
        o_ref[0, :, sl] = acc_ref[h * SB_TQ:(h + 1) * SB_TQ, :].astype(o_ref.dtype)


def _sb_prompt(proj3, bias_row, uo):
    b, t, _ = proj3.shape
    tq = min(SB_TQ, t)
    assert tq == SB_TQ and t % tq == 0
    return pl.pallas_call(
        _sb_prompt_kernel,
        grid=(b, t // tq),
        in_specs=[pl.BlockSpec((1, tq, W_SB), lambda bi, i: (bi, i, 4)),
                  pl.BlockSpec((1, t, W_SB), lambda bi, i: (bi, 0, 5)),
                  pl.BlockSpec((1, t, W_SB), lambda bi, i: (bi, 0, 6)),
                  pl.BlockSpec((1, H_SB * SB_TQ), lambda bi, i: (0, 0)),
                  pl.BlockSpec((2 * SB_BLOCK, 2 * SB_BLOCK), lambda bi, i: (0, 0))],
        out_specs=pl.BlockSpec((1, tq, W_SB), lambda bi, i: (bi, i, 0)),
        out_shape=jax.ShapeDtypeStruct((b, t, W_SB), BF16),
        scratch_shapes=[pltpu.VMEM((H_SB * tq, HD_SB), F32),
                        pltpu.VMEM((H_SB * tq, SB_BLOCK), F32)],
        compiler_params=_params("parallel", "arbitrary"),
        name="sb_prompt",
    )(proj3, proj3, proj3, bias_row, uo)


def _row_select(parts):
    r = lax.broadcasted_iota(jnp.int32, parts[0].shape, 0)
    out = jnp.zeros_like(parts[0])
    for h, p in enumerate(parts):
        out = jnp.where(r == h, p, out)
    return out


def _sb_sample_kernel(pt_ref, q_ref, bias_ref, uo_ref, *refs):
    del pt_ref
    k_refs = refs[:N_PAGES]
    v_refs = refs[N_PAGES:2 * N_PAGES]
    o_ref = refs[2 * N_PAGES]
    uo = uo_ref[...]
    n_col = PAGE_SIZE * H_SB
    n_grp = n_col // SB_BLOCK
    q4 = _row_select([jnp.broadcast_to(q_ref[0, :, h * HD_SB:(h + 1) * HD_SB], (SUBLANES, HD_SB))
                      for h in range(H_SB)])
    qb = jnp.concatenate([q4, q4], axis=0).astype(BF16)
    row = lax.broadcasted_iota(jnp.int32, (SUBLANES, n_col), 0)
    col = lax.broadcasted_iota(jnp.int32, (SUBLANES, n_col), 1)
    valid = (col & (H_SB - 1)) == row
    bias = bias_ref[...]
    zs, sps, his, los = [], [], [], []
    for p in range(N_PAGES):
        z = _dot_nt(qb, k_refs[p][...].astype(BF16))[:SUBLANES] * (HD_SB ** -0.5) + bias
        sp = _softplus(z)
        hi, lo = _split_hi_lo(jnp.where(valid, -sp, 0.0))
        zs.append(z)
        sps.append(sp)
        his += [hi[:, g * SB_BLOCK:(g + 1) * SB_BLOCK] for g in range(n_grp)]
        los += [lo[:, g * SB_BLOCK:(g + 1) * SB_BLOCK] for g in range(n_grp)]
    n_all = N_PAGES * n_grp
    r = _dot(jnp.concatenate([jnp.concatenate(his, axis=0), jnp.concatenate(los, axis=0)],
                             axis=1), uo)
    carry = jnp.zeros((SUBLANES, SB_BLOCK), F32)
    after = [None] * n_all
    for g in reversed(range(n_all)):
        rows = slice(g * SUBLANES, (g + 1) * SUBLANES)
        after[g] = r[rows, :SB_BLOCK] + carry
        carry = carry + r[rows, SB_BLOCK:]
    acc = jnp.zeros((MXU_ROWS, HD_SB), F32)
    for p in range(N_PAGES):
        aft = jnp.concatenate(after[p * n_grp:(p + 1) * n_grp], axis=1)
        a = jnp.where(valid, jnp.exp(zs[p] - sps[p] + aft), 0.0)
        ab = jnp.concatenate([a, a], axis=0).astype(BF16)
        acc += _dot(ab, v_refs[p][...].astype(BF16))
    o_ref[0] = acc[:H_SB].astype(o_ref.dtype)


def _sb_sample(proj3, bias_rows, uo, cache_k, cache_v, page_table, e):
    bs = proj3.shape[0]

    def page(p):
        return pl.BlockSpec((None, None, PAGE_SIZE * H_SB, HD_SB),
                            lambda b, pt: (e, pt[b, p], 0, 0))

    grid_spec = pltpu.PrefetchScalarGridSpec(
        num_scalar_prefetch=1,
        grid=(bs,),
        in_specs=[pl.BlockSpec((1, 1, W_SB), lambda b, pt: (b, 0, 4)),
                  pl.BlockSpec((SUBLANES, PAGE_SIZE * H_SB), lambda b, pt: (0, 0)),
                  pl.BlockSpec((2 * SB_BLOCK, 2 * SB_BLOCK), lambda b, pt: (0, 0))]
                 + [page(p) for p in range(N_PAGES)] + [page(p) for p in range(N_PAGES)],
        out_specs=pl.BlockSpec((1, H_SB, HD_SB), lambda b, pt: (b, 0, 0)),
    )
    return pl.pallas_call(
        _sb_sample_kernel,
        grid_spec=grid_spec,
        out_shape=jax.ShapeDtypeStruct((bs, H_SB, HD_SB), F32),
        compiler_params=_params("arbitrary"),
        name="sb_sample",
    )(page_table, proj3, bias_rows, uo, *([cache_k] * N_PAGES), *([cache_v] * N_PAGES))


def _softmax_rows(s):
    m = jnp.max(s, axis=-1, keepdims=True)
    p = jnp.exp(s - m)
    return p / jnp.sum(p, axis=-1, keepdims=True)


XP_ROW_CHUNKS = 4


def _xattn_prompt_kernel(x_ref, g_ref, wq_ref, mk_ref, mv_ref, wo_ref, o_ref, att_ref, *,
                         n_chunk):
    tm = x_ref.shape[1]
    rc = tm // n_chunk
    heads = [slice(h * HD_X, (h + 1) * HD_X) for h in range(H_X)]
    mkb = [mk_ref[0, :, sl].astype(BF16) for sl in heads]
    mvb = [mv_ref[0, :, sl].astype(BF16) for sl in heads]

    def rows(c):
        return slice(c * rc, (c + 1) * rc)

    def query(c):
        hn = _rms(x_ref[0, rows(c), :], g_ref[...]).astype(BF16)
        return _dot(hn, wq_ref[...]).astype(BF16)

    def scores(q):
        return [_dot_nt(q[:, sl], mkb[h]) * (HD_X ** -0.5) for h, sl in enumerate(heads)]

    s = scores(query(0))
    for c in range(n_chunk):
        q_next = query(c + 1) if c + 1 < n_chunk else None
        for h, sl in enumerate(heads):
            att_ref[rows(c), sl] = _dot(_softmax_rows(s[h]).astype(BF16), mvb[h]).astype(BF16)
        if q_next is not None:
            s = scores(q_next)
        o_ref[0, rows(c), :] = x_ref[0, rows(c), :] + _dot(att_ref[rows(c), :], wo_ref[...])


def _xattn_prompt(x3, g, wq, mem_kv3, k_block, v_block, wo, tm=1024, row_chunks=XP_ROW_CHUNKS):
    b, t, d = x3.shape
    tm = min(tm, t)
    xs = pl.BlockSpec((1, tm, d), lambda bi, i: (bi, i, 0))
    ws = pl.BlockSpec((d, d), lambda bi, i: (0, 0))
    assert tm % (row_chunks * SUBLANES) == 0
    return pl.pallas_call(
        functools.partial(_xattn_prompt_kernel, n_chunk=row_chunks),
        grid=(b, t // tm),
        in_specs=[xs, pl.BlockSpec((1, d), lambda bi, i: (0, 0)), ws,
                  pl.BlockSpec((1, N_MEM, d), lambda bi, i: (bi, 0, k_block)),
                  pl.BlockSpec((1, N_MEM, d), lambda bi, i: (bi, 0, v_block)), ws],
        out_specs=xs,
        out_shape=jax.ShapeDtypeStruct((b, t, d), F32),
        scratch_shapes=[pltpu.VMEM((tm, d), BF16)],
        compiler_params=_params("parallel", "arbitrary"),
        name="xattn_prompt",
    )(x3, g.reshape(1, d), wq, mem_kv3, mem_kv3, wo)


XS_HALVES = HD_X // LANES
XS_ROWS = XS_HALVES * H_X


def _xattn_sample_kernel(q_ref, mk_ref, mv_ref, o_ref):
    n_col = N_MEM * XS_ROWS
    row = lax.broadcasted_iota(jnp.int32, (SUBLANES, n_col), 0)
    col = lax.broadcasted_iota(jnp.int32, (SUBLANES, n_col), 1)
    valid = (col & (XS_ROWS - 1)) == row
    for i in range(q_ref.shape[0]):
        q8 = _row_select([jnp.broadcast_to(q_ref[i, :, (a % H_X) * HD_X + (a // H_X) * LANES:
                                                 (a % H_X) * HD_X + (a // H_X + 1) * LANES],
                                           (SUBLANES, LANES)) for a in range(XS_ROWS)])
        qb = jnp.concatenate([q8, q8], axis=0).astype(BF16)
        z = jnp.where(valid, _dot_nt(qb, mk_ref[i].astype(BF16))[:SUBLANES], 0.0)
        zr = pltpu.roll(z, H_X, axis=0)
        other = jnp.where(row < H_X, pltpu.roll(zr, n_col - H_X, axis=1),
                          pltpu.roll(zr, H_X, axis=1))
        s = (z + other) * (HD_X ** -0.5)
        m = jnp.max(jnp.where(valid, s, -jnp.inf), axis=-1, keepdims=True)
        p = jnp.where(valid, jnp.exp(s - m), 0.0)
        p = p / jnp.sum(p, axis=-1, keepdims=True)
        pb = jnp.concatenate([p, p], axis=0).astype(BF16)
        o_ref[i] = _dot(pb, mv_ref[i].astype(BF16))[:SUBLANES].astype(o_ref.dtype)


def _xattn_cache_view(cache):
    dp, b = cache.shape[:2]
    c = cache.reshape(dp, b, N_MEM, H_X, XS_HALVES, LANES)
    return jnp.swapaxes(c, 3, 4).reshape(dp, b, N_MEM * XS_ROWS, LANES)


def _xattn_sample(q3, mem_k, mem_v, l, bb):
    bs, _, d = q3.shape
    bb = min(bb, bs)
    ms = pl.BlockSpec((None, bb, N_MEM * XS_ROWS, LANES), lambda b: (l, b, 0, 0))
    o = pl.pallas_call(
        _xattn_sample_kernel,
        grid=(bs // bb,),
        in_specs=[pl.BlockSpec((bb, 1, d), lambda b: (b, 0, 0)), ms, ms],
        out_specs=pl.BlockSpec((bb, XS_ROWS, LANES), lambda b: (b, 0, 0)),
        out_shape=jax.ShapeDtypeStruct((bs, XS_ROWS, LANES), F32),
        compiler_params=_params("parallel"),
        name="xattn_sample",
    )(q3, mem_k, mem_v)
    return jnp.swapaxes(o.reshape(bs, XS_HALVES, H_X, LANES), 1, 2).reshape(bs, d)


FFN_TF = 256
HALO = SUBLANES


FFN_LOOKAHEAD = 2
FFN_ROW_CHUNKS = 4


def _ffn_prompt_kernel(x_ref, g_ref, wg_ref, wv_ref, dwg_ref, dwv_ref, bg_ref, bv_ref, wd_ref,
                       gout_ref, o_ref, hn_ref, tail_ref, *u_refs, out_norm, lookahead):
    t = pl.program_id(1)
    f = pl.program_id(2)
    tm = x_ref.shape[1]
    tf = wg_ref.shape[1]
    rc = tm // len(u_refs)

    @pl.when(f == 0)
    def _():
        hn_ref[...] = _rms(x_ref[0], g_ref[...]).astype(BF16)
        o_ref[0] = x_ref[0]

    def conv(u_ref, dw_ref, b_ref, sl):
        c = b_ref[...] + dw_ref[FFN_CONV_W - 1:FFN_CONV_W, :] * u_ref[HALO:, sl]
        for w in range(FFN_CONV_W - 1):
            off = HALO - (FFN_CONV_W - 1) + w
            c += dw_ref[w:w + 1, :] * u_ref[off:off + rc, sl]
        return c

    def up(k):
        hn = hn_ref[k * rc:(k + 1) * rc, :]
        u_refs[k][HALO:, :tf] = _dot(hn, wg_ref[...])
        u_refs[k][HALO:, tf:] = _dot(hn, wv_ref[...])

    for k in range(min(lookahead, len(u_refs))):
        up(k)
    for k, u_ref in enumerate(u_refs):
        if k + lookahead < len(u_refs):
            up(k + lookahead)
        if k == 0:
            u_ref[:HALO, :] = jnp.where(t == 0, 0.0, tail_ref[f])
        else:
            u_ref[:HALO, :] = u_refs[k - 1][rc:, :]
        gated = (_silu(conv(u_ref, dwg_ref, bg_ref, slice(0, tf)))
                 * conv(u_ref, dwv_ref, bv_ref, slice(tf, 2 * tf)))
        o_ref[0, k * rc:(k + 1) * rc, :] += _dot(gated.astype(BF16), wd_ref[...])
    tail_ref[f] = u_refs[-1][rc:, :]

    if out_norm:
        @pl.when(f == pl.num_programs(2) - 1)
        def _():
            o_ref[0] = _rms(o_ref[0], gout_ref[...])


def _ffn_prompt(x3, g, w_up, dw, dwb, w_down, layer, g_out=None, tm=2048,
                row_chunks=FFN_ROW_CHUNKS, lookahead=FFN_LOOKAHEAD):
    b, t, d = x3.shape
    tm = min(tm, t)
    nf = D_FF // FFN_TF
    assert D_FF % FFN_TF == 0 and t % tm == 0
    xs = pl.BlockSpec((1, tm, d), lambda bi, i, f: (bi, i, 0))
    dwb2 = dwb.reshape(1, -1)
    vec = pl.BlockSpec((1, d), lambda bi, i, f: (0, 0))
    return pl.pallas_call(
        functools.partial(_ffn_prompt_kernel, out_norm=g_out is not None, lookahead=lookahead),
        grid=(b, t // tm, nf),
        in_specs=[xs, pl.BlockSpec((1, d), lambda bi, i, f: (0, 0)),
                  pl.BlockSpec((None, d, FFN_TF), lambda bi, i, f: (layer, 0, f)),
                  pl.BlockSpec((None, d, FFN_TF), lambda bi, i, f: (layer, 0, f + nf)),
                  pl.BlockSpec((FFN_CONV_W, FFN_TF), lambda bi, i, f: (0, f)),
                  pl.BlockSpec((FFN_CONV_W, FFN_TF), lambda bi, i, f: (0, f + nf)),
                  pl.BlockSpec((1, FFN_TF), lambda bi, i, f: (0, f)),
                  pl.BlockSpec((1, FFN_TF), lambda bi, i, f: (0, f + nf)),
                  pl.BlockSpec((None, FFN_TF, d), lambda bi, i, f: (layer, f, 0)), vec],
        out_specs=xs,
        out_shape=jax.ShapeDtypeStruct((b, t, d), F32),
        scratch_shapes=[pltpu.VMEM((tm, d), BF16), pltpu.VMEM((nf, HALO, 2 * FFN_TF), F32)]
                       + [pltpu.VMEM((HALO + tm // row_chunks, 2 * FFN_TF), F32)] * row_chunks,
        compiler_params=_params("parallel", "arbitrary", "arbitrary"),
        name="ffn_prompt",
    )(x3, g.reshape(1, d), w_up, w_up, dw, dw, dwb2, dwb2, w_down,
      (g if g_out is None else g_out).reshape(1, d))


CONV_HALO = 32
CONV_RB = 32


def _conv_prompt_kernel(u_ref, x_ref, dw_ref, dwb_ref, lg_ref, lb_ref, w2_ref, b2_ref,
                        o_ref, s_ref, c_ref, dwt_ref):
    t = pl.program_id(1)
    tm = u_ref.shape[1]
    n = CONV_HALO + tm

    @pl.when(t == 0)
    def _():
        s_ref[0, :CONV_HALO, :] = jnp.zeros((CONV_HALO, D_MODEL), F32)

    @pl.when(t > 0)
    def _():
        s_ref[0, :CONV_HALO, :] = s_ref[0, tm:, :]

    s_ref[0, CONV_HALO:, :] = u_ref[0]
    for r in range(1, SUBLANES):
        s_ref[r] = pltpu.roll(s_ref[0], n - r, axis=0)
    base = CONV_HALO - (CONV_W - 1)

    @pl.when(t == 0)
    def _():
        for w in range(CONV_W):
            dwt_ref[w] = jnp.broadcast_to(dw_ref[w:w + 1, :], (SUBLANES, D_MODEL))

    n_sub = CONV_RB // SUBLANES

    def row_block(rb, carry):
        r0 = rb * CONV_RB
        bias = jnp.broadcast_to(dwb_ref[...], (SUBLANES, D_MODEL))
        accs = [bias] * n_sub
        for w in range(CONV_W):
            r = (base + w) % SUBLANES
            dwt = dwt_ref[w]
            for k in range(n_sub):
                start = pl.multiple_of(r0 + (base + w - r) + k * SUBLANES, SUBLANES)
                accs[k] = accs[k] + dwt * s_ref[r, pl.ds(start, SUBLANES), :]
        for k in range(n_sub):
            c_ref[pl.ds(pl.multiple_of(r0 + k * SUBLANES, SUBLANES), SUBLANES), :] = accs[k]
        return carry

    lax.fori_loop(0, tm // CONV_RB, row_block, 0)
    c = c_ref[...]
    mu = jnp.mean(c, axis=-1, keepdims=True)
    cc = c - mu
    var = jnp.mean(cc * cc, axis=-1, keepdims=True)
    y = _silu(cc * lax.rsqrt(var + NORM_EPS) * lg_ref[...] + lb_ref[...])
    o_ref[0] = x_ref[0] + _dot(y.astype(BF16), w2_ref[...]) + b2_ref[...]


def _conv_prompt(u3, x3, dw, dwb, ln_g, ln_b, w2, b2, tm=512):
    b, t, d = x3.shape
    tm = min(tm, t)
    xs = pl.BlockSpec((1, tm, d), lambda bi, i: (bi, i, 0))
    vec = pl.BlockSpec((1, d), lambda bi, i: (0, 0))
    return pl.pallas_call(
        _conv_prompt_kernel,
        grid=(b, t // tm),
        in_specs=[xs, xs, pl.BlockSpec((CONV_W, d), lambda bi, i: (0, 0)), vec, vec, vec,
                  pl.BlockSpec((d, d), lambda bi, i: (0, 0)), vec],
        out_specs=xs,
        out_shape=jax.ShapeDtypeStruct((b, t, d), F32),
        scratch_shapes=[pltpu.VMEM((SUBLANES, CONV_HALO + tm, d), F32), pltpu.VMEM((tm, d), F32),
                        pltpu.VMEM((CONV_W, SUBLANES, d), F32)],
        compiler_params=_params("parallel", "arbitrary"),
        name="conv_prompt",
    )(u3, x3, dw, dwb.reshape(1, d), ln_g.reshape(1, d), ln_b.reshape(1, d), w2, b2.reshape(1, d))


STEP_BB = 8


def _conv_step_kernel(st_ref, u_ref, dw_ref, dwb_ref, o_ref):
    w_taps = dw_ref.shape[0]
    for r in range(STEP_BB):
        acc = dwb_ref[...] + dw_ref[w_taps - 1:w_taps, :] * u_ref[r:r + 1, :]
        for w in range(w_taps - 1):
            acc += dw_ref[w:w + 1, :] * st_ref[r, w:w + 1, :]
        o_ref[r:r + 1, :] = acc


def _conv_step(state, l, u, dw, dwb):
    bs, c = u.shape
    w_taps = dw.shape[0]
    return pl.pallas_call(
        _conv_step_kernel,
        grid=(bs // STEP_BB,),
        in_specs=[pl.BlockSpec((None, STEP_BB, w_taps - 1, c), lambda i: (l, i, 0, 0)),
                  pl.BlockSpec((STEP_BB, c), lambda i: (i, 0)),
                  pl.BlockSpec((w_taps, c), lambda i: (0, 0)),
                  pl.BlockSpec((1, c), lambda i: (0, 0))],
        out_specs=pl.BlockSpec((STEP_BB, c), lambda i: (i, 0)),
        out_shape=jax.ShapeDtypeStruct((bs, c), F32),
        compiler_params=_params("parallel"),
        name="conv_step",
    )(state, u, dw, dwb.reshape(1, c))


def _conv_step_slots_kernel(st_ref, u_ref, dw_ref, dwb_ref, o_ref):
    w_taps = dw_ref.shape[0]
    acc = dwb_ref[...] + dw_ref[w_taps - 1:w_taps, :] * u_ref[...]
    for w in range(w_taps - 1):
        acc += dw_ref[w:w + 1, :] * st_ref[w]
    o_ref[...] = acc


def _conv_step_slots(state_t, l, u, dw, dwb):
    bs, c = u.shape
    w_taps = dw.shape[0]
    return pl.pallas_call(
        _conv_step_slots_kernel,
        grid=(bs // STEP_BB,),
        in_specs=[pl.BlockSpec((None, w_taps - 1, STEP_BB, c), lambda i: (l, 0, i, 0)),
                  pl.BlockSpec((STEP_BB, c), lambda i: (i, 0)),
                  pl.BlockSpec((w_taps, c), lambda i: (0, 0)),
                  pl.BlockSpec((1, c), lambda i: (0, 0))],
        out_specs=pl.BlockSpec((STEP_BB, c), lambda i: (i, 0)),
        out_shape=jax.ShapeDtypeStruct((bs, c), F32),
        compiler_params=_params("parallel"),
        name="conv_step_slots",
    )(state_t, u, dw, dwb.reshape(1, c))


def _shift_slots_kernel(*refs, n_layers):
    st_ref, u_refs, o_ref = refs[0], refs[1:1 + n_layers], refs[1 + n_layers]
    n_slots = st_ref.shape[0]
    o_ref[:n_slots - 1] = st_ref[1:]
    layer = pl.program_id(0)
    for n, u_ref in enumerate(u_refs):
        @pl.when(layer == n)
        def _(u_ref=u_ref):
            o_ref[n_slots - 1] = u_ref[...]


def _shift_slots(state_t, us):
    n_layers, n_slots, bs, c = state_t.shape

    def u_spec(n):
        return pl.BlockSpec((STEP_BB, c), lambda l, i: (jnp.where(l == n, i, 0), 0))

    blk = pl.BlockSpec((None, n_slots, STEP_BB, c), lambda l, i: (l, 0, i, 0))
    return pl.pallas_call(
        functools.partial(_shift_slots_kernel, n_layers=n_layers),
        grid=(n_layers, bs // STEP_BB),
        in_specs=[blk] + [u_spec(n) for n in range(n_layers)],
        out_specs=blk,
        out_shape=jax.ShapeDtypeStruct(state_t.shape, F32),
        compiler_params=_params("arbitrary", "arbitrary"),
        name="shift_slots",
    )(state_t, *us)


def _stack_rows_kernel(*refs, n_layers, col_starts):
    srcs, dst = refs[:n_layers], refs[n_layers]
    rows = srcs[0].shape[1]
    layer = pl.program_id(0)
    for n, src in enumerate(srcs):
        @pl.when(layer == n)
        def _(src=src):
            for a, c0 in enumerate(col_starts):
                dst[pl.ds(a, rows, stride=len(col_starts)), :] = src[0, :, c0:c0 + LANES]


def _stack_rows(srcs, col_blocks, width, col_starts, tm):
    n_layers = len(srcs)
    b, t, _ = srcs[0].shape
    tm = min(tm, t)
    nt = t // tm
    n_phase = len(col_starts)

    def src_spec(n):
        def idx(l, bi, i):
            on = l == n
            return (jnp.where(on, bi, 0), jnp.where(on, i, 0), col_blocks[n])
        return pl.BlockSpec((1, tm, width), idx)

    return pl.pallas_call(
        functools.partial(_stack_rows_kernel, n_layers=n_layers, col_starts=tuple(col_starts)),
        grid=(n_layers, b, nt),
        in_specs=[src_spec(n) for n in range(n_layers)],
        out_specs=pl.BlockSpec((None, None, tm * n_phase, LANES), lambda l, bi, i: (l, bi, i, 0)),
        out_shape=jax.ShapeDtypeStruct((n_layers, b, t * n_phase, LANES), F32),
        compiler_params=_params("arbitrary", "arbitrary", "arbitrary"),
        name="stack_rows",
    )(*srcs)


XS_BB = 8
STACK_TM = 2048


def kernel(x_prompt, x_sample, cache_sb_k, cache_sb_v, state_ret, state_conv, state_ffn_conv, cache_mem_k, cache_mem_v, page_table, mem_prompt, g_mix, w_in_ab, ret_gn_g, w_out_ab, sb_bias, cv_w1, cv_b1, cv_dw, cv_dwb, cv_ln_g, cv_ln_b, cv_w2, cv_b2, g_cross, xa_wq, xa_wk, xa_wv, xa_wo, g_ffn, ffn_w_up, ffn_dw, ffn_dwb, ffn_w_down, g_final):
    bp, t, d = x_prompt.shape
    bs = x_sample.shape[0]
    n_phys = cache_sb_k.shape[1]

    log_gamma = jnp.log1p(-jnp.exp2(-5.0 - jnp.arange(H_RET, dtype=F32)))
    ret_tabs = _ret_tables(log_gamma)
    cos_p, sin_p = _rope_tables(jnp.arange(t))
    cos_s, sin_s = _rope_tables(PAST_LEN + jnp.arange(1))
    uo = _suffix_matrix()
    w_up_all = ffn_w_up.astype(BF16)
    w_down_all = ffn_w_down.astype(BF16)
    conv_slots = jnp.swapaxes(state_conv, 1, 2)
    cache_k = cache_sb_k.reshape(cache_sb_k.shape[0], n_phys, PAGE_SIZE * H_SB, HD_SB)
    cache_v = cache_sb_v.reshape(cache_sb_v.shape[0], n_phys, PAGE_SIZE * H_SB, HD_SB)
    mem_k = _xattn_cache_view(cache_mem_k)
    mem_v = _xattn_cache_view(cache_mem_v)
    w_mem = jnp.concatenate([w[l] for l in range(DEPTH) for w in (xa_wk, xa_wv)], axis=1)
    mem_kv3 = _mm([mem_prompt.reshape(bp * N_MEM, d)], [w_mem.astype(BF16)], tm=2048, tn=d,
                  name="mem_kv").reshape(bp, N_MEM, 2 * DEPTH * d)

    xp = x_prompt.reshape(bp * t, d)
    xs = x_sample.reshape(bs, d)
    sbk_p, sbk_s, sbv_s, ret_p, ret_s = [], [], [], [], []
    cv_p, cv_s, ff_p, ff_s = [], [], [], []

    for l in range(DEPTH):
        if l % 2 == 0:
            e = l // 2
            w_in = w_in_ab[e].astype(BF16)
            w_out = w_out_ab[e].astype(BF16)
            w_out_parts = [w_out[:W_RK], w_out[W_RK:]]
            bias = sb_bias[e].astype(F32)
            bias_row = jnp.repeat(bias, SB_TQ).reshape(1, H_SB * SB_TQ)
            bias_rows = jnp.zeros((SUBLANES, PAGE_SIZE * H_SB), F32).at[:H_SB].set(
                jnp.broadcast_to(bias[:, None], (H_SB, PAGE_SIZE * H_SB)))
            proj = _mm([xp], [w_in], prologue="rms", pro=(g_mix[l],), tm=512,
                       tn=w_in.shape[1], name="proj_in")
            proj3 = proj.reshape(bp, t, -1)
            ret_o, s_p = _ret_prompt(proj3, cos_p, sin_p, ret_tabs, ret_gn_g[e],
                                     bb=RET_PROMPT_BB)
            sb_o = _sb_prompt(proj3, bias_row, uo)
            xp = _mm([ret_o.reshape(bp * t, W_RK), sb_o.reshape(bp * t, W_SB)], w_out_parts,
                     res=xp, tm=1024, tn=d, name="proj_out")
            sbk_p.append(proj3)
            ret_p.append(s_p)
            proj_s = _mm([xs], [w_in], prologue="rms", pro=(g_mix[l],), name="proj_in_s")
            ret_os, s_s = _ret_sample(proj_s, cos_s, sin_s, ret_gn_g[e], state_ret, e)
            sb_os = _sb_sample(proj_s.reshape(bs, 1, -1), bias_rows, uo, cache_k, cache_v,
                               page_table, e)
            xs = _mm([ret_os, sb_os.reshape(bs, W_SB)], w_out_parts, res=xs, tn=d,
                     name="proj_out_s")
            sbk_s.append(proj_s[:, 5 * W_SB:6 * W_SB].reshape(bs, 1, H_SB, HD_SB))
            sbv_s.append(proj_s[:, 6 * W_SB:7 * W_SB].reshape(bs, 1, H_SB, HD_SB))
            ret_s.append(s_s)
        else:
            o = l // 2
            w1 = cv_w1[o].astype(BF16)
            w2 = cv_w2[o].astype(BF16)
            u = _mm([xp], [w1], prologue="rms", pro=(g_mix[l],), bias=cv_b1[o], glu=True,
                    tm=512, tn=d, name="conv_glu")
            u3 = u.reshape(bp, t, d)
            xp = _conv_prompt(u3, xp.reshape(bp, t, d), cv_dw[o], cv_dwb[o], cv_ln_g[o],
                              cv_ln_b[o], w2, cv_b2[o]).reshape(bp * t, d)
            cv_p.append(u3[:, t - (CONV_W - 1):, :])
            u_s = _mm([xs], [w1], prologue="rms", pro=(g_mix[l],), bias=cv_b1[o], glu=True,
                      name="conv_glu_s")
            c_s = _conv_step_slots(conv_slots, o, u_s, cv_dw[o], cv_dwb[o])
            xs = _mm([c_s], [w2], prologue="ln_silu", pro=(cv_ln_g[o], cv_ln_b[o]), bias=cv_b2[o],
                     res=xs, tn=d, name="conv_out_s")
            cv_s.append(u_s)

        wq = xa_wq[l].astype(BF16)
        wo = xa_wo[l].astype(BF16)
        xp = _xattn_prompt(xp.reshape(bp, t, d), g_cross[l], wq, mem_kv3, 2 * l, 2 * l + 1, wo)
        q_s = _mm([xs], [wq], prologue="rms", pro=(g_cross[l],), tn=d, name="xattn_q_s")
        att_s = _xattn_sample(q_s.reshape(bs, 1, d), mem_k, mem_v, l, XS_BB)
        xs = _mm([att_s], [wo], res=xs, tn=d, name="xattn_out_s")

        tail_rows = xp[:, t - (FFN_CONV_W - 1):, :].reshape(bp * (FFN_CONV_W - 1), d)
        u_st = _mm([jnp.concatenate([xs, tail_rows], axis=0)], [(w_up_all, l)], prologue="rms",
                   pro=(g_ffn[l],), name="ffn_up_s")
        u_s = u_st[:bs]
        ff_p.append(u_st[bs:].reshape(bp, FFN_CONV_W - 1, 2 * D_FF))
        xp = _ffn_prompt(xp, g_ffn[l], w_up_all, ffn_dw[l], ffn_dwb[l], w_down_all, l,
                         g_out=g_final if l == DEPTH - 1 else None).reshape(bp * t, d)
        c_s = _conv_step(state_ffn_conv, l, u_s, ffn_dw[l], ffn_dwb[l])
        xs = _mm([c_s], [(w_down_all, l)], prologue="swiglu", res=xs, tn=d, name="ffn_down_s")
        ff_s.append(jnp.concatenate([state_ffn_conv[l][:, 1:], u_s[:, None, :]], axis=1))

    y_prompt = xp.reshape(bp, t, d)
    y_sample = _rmsnorm(xs, g_final).reshape(bs, 1, d)
    sb_cols = [h * HD_SB for h in range(H_SB)]
    sb_shape = (len(sbk_p), bp, t, H_SB, HD_SB)
    sb_k_prompt = _stack_rows(sbk_p, [5] * len(sbk_p), W_SB, sb_cols, STACK_TM).reshape(sb_shape)
    sb_v_prompt = _stack_rows(sbk_p, [6] * len(sbk_p), W_SB, sb_cols, STACK_TM).reshape(sb_shape)
    mem_cols = [(a % H_X) * HD_X + (a // H_X) * LANES for a in range(XS_ROWS)]

    def mem_out(first_block):
        o = _stack_rows([mem_kv3] * DEPTH, [2 * l + first_block for l in range(DEPTH)], d,
                        mem_cols, N_MEM)
        o = o.reshape(DEPTH, bp, N_MEM, XS_HALVES, H_X, LANES)
        return jnp.swapaxes(o, 3, 4).reshape(DEPTH, bp, N_MEM, H_X, HD_X)

    return (y_prompt, y_sample, sb_k_prompt, sb_v_prompt, jnp.stack(sbk_s),
            jnp.stack(sbv_s), jnp.stack(ret_p), jnp.stack(ret_s), jnp.stack(cv_p),
            jnp.swapaxes(_shift_slots(conv_slots, cv_s), 1, 2), jnp.stack(ff_p),
            jnp.stack(ff_s), mem_out(0), mem_out(1))
```
